```python
import math
import jax, jax.numpy as jnp
from jax import lax
import numpy as np

D_MODEL = 2048
BATCH = 4
SEQ = 2048
DEPTH = 1
DEC_BATCH = 128
DEC_SEQ = 8
PAST_LEN = 16384
PAGE_SIZE = 128

N_META = 16
D_FF = 5632
S5_WIDTH = D_MODEL // 2
S5_GROUP = 16
S5_GROUPS = S5_WIDTH // S5_GROUP
S5_STATE = 64
MLSTM_HEADS = 4
MLSTM_DQK = D_MODEL // 16
MLSTM_DV = D_MODEL // 8
QK_WIDTH = 2 * MLSTM_HEADS * MLSTM_DQK
V_WIDTH = MLSTM_HEADS * MLSTM_DV
CONV_W = 4
CHUNK = 64
N_BRANCH = 2
IN_WIDTH = S5_WIDTH + QK_WIDTH + 2 * V_WIDTH + 2 * MLSTM_HEADS + N_BRANCH * D_MODEL
EPS = 1e-6
DT_MIN = 1e-3
DT_MAX = 1e-1

kernel_name = 'hybrid_s5_mlstm_macaron_decode_step'


def rmsnorm(x, g):
    xf = x.astype(jnp.float32)
    y = xf * lax.rsqrt(jnp.mean(xf * xf, axis=-1, keepdims=True) + EPS)
    return (y * g.astype(jnp.float32)).astype(x.dtype)


def swiglu(x, wg, wu, wd):
    return (jax.nn.silu(x @ wg) * (x @ wu)) @ wd


def _lin_combine(e1, e2):
    a1, b1 = e1
    a2, b2 = e2
    return a1 * a2, a2 * b1 + b2


def s5_mixer(u, h0_re, h0_im, A_re, A_im, log_dt, B_re, B_im, C_re, C_im, d_skip, w_glu):
    n, L, _ = u.shape
    f32 = jnp.float32
    A = lax.complex(A_re.astype(f32), A_im.astype(f32))
    dt = jnp.exp(log_dt.astype(f32))[:, None]
    Abar = jnp.exp(A * dt)
    Bbar = ((Abar - 1.0) / A)[:, :, None] * lax.complex(B_re.astype(f32), B_im.astype(f32))
    ug = u.astype(f32).reshape(n, L, S5_GROUPS, S5_GROUP)
    Bu = jnp.einsum('nlgh,gph->nlgp', ug.astype(jnp.complex64), Bbar)
    h0 = lax.complex(h0_re.astype(f32), h0_im.astype(f32))[:, None]
    b = jnp.concatenate([h0, Bu], axis=1)
    a = jnp.broadcast_to(Abar, b.shape)
    _, hs = lax.associative_scan(_lin_combine, (a, b), axis=1)
    hs = hs[:, 1:]
    Cc = lax.complex(C_re.astype(f32), C_im.astype(f32))
    y = jnp.real(jnp.einsum('nlgp,ghp->nlgh', hs, Cc)) + d_skip.astype(f32).reshape(S5_GROUPS, S5_GROUP) * ug
    y = jax.nn.gelu(y.reshape(n, L, S5_WIDTH)).astype(u.dtype)
    y = y * jax.nn.sigmoid(y @ w_glu)
    h_last = hs[:, -1]
    return y, jnp.real(h_last).astype(h0_re.dtype), jnp.imag(h_last).astype(h0_re.dtype)


def causal_conv(x, buf, w, bias):
    L = x.shape[1]
    xp = jnp.concatenate([buf.astype(x.dtype), x], axis=1)
    out = sum(xp[:, j:j + L] * w[j] for j in range(CONV_W)) + bias
    return jax.nn.silu(out), xp[:, -(CONV_W - 1):]


def mlstm_chunk(carry, xs):
    C0, n0, m0 = carry
    q, k, v, ig, lf = xs
    L = q.shape[1]
    b = jnp.cumsum(lf, axis=1).transpose(0, 2, 1)
    i_ = ig.transpose(0, 2, 1)
    causal = jnp.tril(jnp.ones((L, L), dtype=bool))
    logw = jnp.where(causal, b[..., :, None] - b[..., None, :] + i_[..., None, :], -jnp.inf)
    g = b + m0[..., None]
    m = jnp.maximum(g, jnp.max(logw, axis=-1))
    w = jnp.exp(logw - m[..., None])
    inter = jnp.exp(g - m)
    s = jnp.einsum('nlhk,nshk->nhls', q, k) * w
    num = jnp.einsum('nhls,nshv->nlhv', s, v) + inter.transpose(0, 2, 1)[..., None] * jnp.einsum('nhvk,nlhk->nlhv', C0, q)
    den = jnp.sum(s, axis=-1) + inter * jnp.einsum('nhk,nlhk->nhl', n0, q)
    denom = jnp.maximum(jnp.abs(den), jnp.exp(-m))
    h = num / denom.transpose(0, 2, 1)[..., None]
    bL = b[..., -1]
    m_new = m[..., -1]
    decay = jnp.exp(bL + m0 - m_new)
    w_end = jnp.exp(bL[..., None] - b + i_ - m_new[..., None])
    C = decay[..., None, None] * C0 + jnp.einsum('nhs,nshv,nshk->nhvk', w_end, v, k)
    n = decay[..., None] * n0 + jnp.einsum('nhs,nshk->nhk', w_end, k)
    return (C, n, m_new), h


def mlstm_run(q, k, v, ig, lf, carry, lead):
    N, L = q.shape[0], q.shape[1]
    xs_all = (q, k, v, ig, lf)
    outs = []
    if lead > 0:
        carry, hl = mlstm_chunk(carry, tuple(t[:, :lead] for t in xs_all))
        outs.append(hl)
    nf, rem = divmod(L - lead, CHUNK)
    if nf > 0:
        def to_chunks(t):
            t = t[:, lead:lead + nf * CHUNK]
            return jnp.moveaxis(t.reshape((N, nf, CHUNK) + t.shape[2:]), 1, 0)
        carry, hc = lax.scan(mlstm_chunk, carry, tuple(to_chunks(t) for t in xs_all))
        hc = jnp.moveaxis(hc, 0, 1)
        outs.append(hc.reshape((N, nf * CHUNK) + hc.shape[3:]))
    if rem > 0:
        carry, ht = mlstm_chunk(carry, tuple(t[:, L - rem:] for t in xs_all))
        outs.append(ht)
    return jnp.concatenate(outs, axis=1), carry


def mlstm_mixer(qk_pre, v_pre, o_pre, i_pre, f_pre, conv_buf, C0, n0, m0, conv_w, conv_b, b_i, b_f, norm_g, lead):
    N, L, _ = qk_pre.shape
    f32 = jnp.float32
    qk, new_buf = causal_conv(qk_pre, conv_buf, conv_w, conv_b)
    q = qk[..., :QK_WIDTH // 2].reshape(N, L, MLSTM_HEADS, MLSTM_DQK).astype(f32) * (MLSTM_DQK ** -0.5)
    k = qk[..., QK_WIDTH // 2:].reshape(N, L, MLSTM_HEADS, MLSTM_DQK).astype(f32)
    v = v_pre.reshape(N, L, MLSTM_HEADS, MLSTM_DV).astype(f32)
    ig = (i_pre + b_i).astype(f32)
    lf = jax.nn.log_sigmoid((f_pre + b_f).astype(f32))
    carry = (C0.astype(f32), n0.astype(f32), m0.astype(f32))
    h, (C, n, m) = mlstm_run(q, k, v, ig, lf, carry, lead)
    h = h * lax.rsqrt(jnp.mean(h * h, axis=-1, keepdims=True) + EPS)
    h = h.reshape(N, L, V_WIDTH) * norm_g.astype(f32)
    h = (h * jax.nn.sigmoid(o_pre.astype(f32))).astype(qk_pre.dtype)
    dt = C0.dtype
    return h, C.astype(dt), n.astype(dt), m.astype(dt), new_buf.astype(conv_buf.dtype)


def layer(h, st, p, lead):
    s5_re, s5_im, C0, n0, m0, buf = st
    h = h + 0.5 * swiglu(rmsnorm(h, p['ffn1_norm']), p['ffn1_w_gate'], p['ffn1_w_up'], p['ffn1_w_down'])
    u = rmsnorm(h, p['mix_norm'])
    z = u @ p['w_in']
    o0 = S5_WIDTH
    o1 = o0 + QK_WIDTH
    o2 = o1 + V_WIDTH
    o3 = o2 + V_WIDTH
    o4 = o3 + MLSTM_HEADS
    o5 = o4 + MLSTM_HEADS
    y_s5, ns_re, ns_im = s5_mixer(z[..., :o0], s5_re, s5_im, p['s5_A_re'], p['s5_A_im'], p['s5_log_dt'],
                                  p['s5_B_re'], p['s5_B_im'], p['s5_C_re'], p['s5_C_im'], p['s5_D'], p['s5_w_glu'])
    y_ml, nC, nn_, nm, nbuf = mlstm_mixer(z[..., o0:o1], z[..., o1:o2], z[..., o2:o3], z[..., o3:o4], z[..., o4:o5],
                                          buf, C0, n0, m0, p['mlstm_conv_w'], p['mlstm_conv_b'],
                                          p['mlstm_b_i'], p['mlstm_b_f'], p['mlstm_norm'], lead)
    gates = jax.nn.sigmoid(z[..., o5:])
    merged = gates[..., :D_MODEL] * (y_s5 @ p['w_branch_s5']) + gates[..., D_MODEL:] * (y_ml @ p['w_branch_mlstm'])
    h = h + merged @ p['w_out']
    h = h + 0.5 * swiglu(rmsnorm(h, p['ffn2_norm']), p['ffn2_w_gate'], p['ffn2_w_up'], p['ffn2_w_down'])
    return h, (ns_re, ns_im, nC, nn_, nm, nbuf)


def setup_inputs(seed: int = 0) -> dict:
    key = jax.random.key(seed)
    ks = iter(jax.random.split(key, 64))

    def nrm(shape, scale=1.0):
        return jax.random.normal(next(ks), shape, jnp.float32) * scale

    def gain(shape):
        return 1.0 + nrm(shape, 0.02)

    Ld = DEPTH
    G, P, Hc = S5_GROUPS, S5_STATE, S5_GROUP
    out = {}
    out['x_prompt'] = nrm((BATCH, SEQ, D_MODEL))
    out['x_sample'] = nrm((DEC_BATCH, DEC_SEQ, D_MODEL))
    out['state_s5_re'] = nrm((Ld, DEC_BATCH, G, P), 0.1)
    out['state_s5_im'] = nrm((Ld, DEC_BATCH, G, P), 0.1)
    out['state_mlstm_C'] = nrm((Ld, DEC_BATCH, MLSTM_HEADS, MLSTM_DV, MLSTM_DQK), 0.1)
    out['state_mlstm_n'] = jnp.abs(nrm((Ld, DEC_BATCH, MLSTM_HEADS, MLSTM_DQK), 0.1))
    out['state_mlstm_m'] = nrm((Ld, DEC_BATCH, MLSTM_HEADS), 0.5)
    out['state_mlstm_conv'] = nrm((Ld, DEC_BATCH, CONV_W - 1, QK_WIDTH))
    out['meta_tokens'] = nrm((N_META, D_MODEL))
    out['ffn1_norm'] = gain((Ld, D_MODEL))
    out['ffn1_w_gate'] = nrm((Ld, D_MODEL, D_FF), D_MODEL ** -0.5)
    out['ffn1_w_up'] = nrm((Ld, D_MODEL, D_FF), D_MODEL ** -0.5)
    out['ffn1_w_down'] = nrm((Ld, D_FF, D_MODEL), D_FF ** -0.5)
    out['mix_norm'] = gain((Ld, D_MODEL))
    out['w_in'] = nrm((Ld, D_MODEL, IN_WIDTH), D_MODEL ** -0.5)
    out['s5_A_re'] = -0.5 + nrm((Ld, G, P), 0.01)
    out['s5_A_im'] = jnp.pi * jnp.arange(P, dtype=jnp.float32) + nrm((Ld, G, P), 0.01)
    out['s5_log_dt'] = jax.random.uniform(next(ks), (Ld, G), jnp.float32, math.log(DT_MIN), math.log(DT_MAX))
    out['s5_B_re'] = nrm((Ld, G, P, Hc), (2 * Hc) ** -0.5)
    out['s5_B_im'] = nrm((Ld, G, P, Hc), (2 * Hc) ** -0.5)
    out['s5_C_re'] = nrm((Ld, G, Hc, P), P ** -0.5)
    out['s5_C_im'] = nrm((Ld, G, Hc, P), P ** -0.5)
    out['s5_D'] = nrm((Ld, S5_WIDTH))
    out['s5_w_glu'] = nrm((Ld, S5_WIDTH, S5_WIDTH), S5_WIDTH ** -0.5)
    out['mlstm_conv_w'] = nrm((Ld, CONV_W, QK_WIDTH), CONV_W ** -0.5)
    out['mlstm_conv_b'] = nrm((Ld, QK_WIDTH), 0.01)
    out['mlstm_b_i'] = nrm((Ld, MLSTM_HEADS), 0.1)
    out['mlstm_b_f'] = jnp.linspace(3.0, 6.0, MLSTM_HEADS, dtype=jnp.float32) + nrm((Ld, MLSTM_HEADS), 0.1)
    out['mlstm_norm'] = gain((Ld, V_WIDTH))
    out['w_branch_s5'] = nrm((Ld, S5_WIDTH, D_MODEL), S5_WIDTH ** -0.5)
    out['w_branch_mlstm'] = nrm((Ld, V_WIDTH, D_MODEL), V_WIDTH ** -0.5)
    out['w_out'] = nrm((Ld, D_MODEL, D_MODEL), D_MODEL ** -0.5)
    out['ffn2_norm'] = gain((Ld, D_MODEL))
    out['ffn2_w_gate'] = nrm((Ld, D_MODEL, D_FF), D_MODEL ** -0.5)
    out['ffn2_w_up'] = nrm((Ld, D_MODEL, D_FF), D_MODEL ** -0.5)
    out['ffn2_w_down'] = nrm((Ld, D_FF, D_MODEL), D_FF ** -0.5)
    out['final_norm'] = gain((D_MODEL,))
    return out


def reference(x_prompt, x_sample, state_s5_re, state_s5_im, state_mlstm_C, state_mlstm_n, state_mlstm_m,
              state_mlstm_conv, meta_tokens, ffn1_norm, ffn1_w_gate, ffn1_w_up, ffn1_w_down, mix_norm, w_in,
              s5_A_re, s5_A_im, s5_log_dt, s5_B_re, s5_B_im, s5_C_re, s5_C_im, s5_D, s5_w_glu,
              mlstm_conv_w, mlstm_conv_b, mlstm_b_i, mlstm_b_f, mlstm_norm, w_branch_s5, w_branch_mlstm,
              w_out, ffn2_norm, ffn2_w_gate, ffn2_w_up, ffn2_w_down, final_norm):
    nb = x_prompt.shape[0]
    meta = jnp.broadcast_to(meta_tokens.astype(x_prompt.dtype)[None], (nb, N_META, D_MODEL))
    hp = jnp.concatenate([meta, x_prompt], axis=1)
    hs = x_sample
    sdt = state_mlstm_C.dtype
    zero_state = (jnp.zeros((nb, S5_GROUPS, S5_STATE), state_s5_re.dtype),
                  jnp.zeros((nb, S5_GROUPS, S5_STATE), state_s5_im.dtype),
                  jnp.zeros((nb, MLSTM_HEADS, MLSTM_DV, MLSTM_DQK), sdt),
                  jnp.zeros((nb, MLSTM_HEADS, MLSTM_DQK), sdt),
                  jnp.zeros((nb, MLSTM_HEADS), sdt),
                  jnp.zeros((nb, CONV_W - 1, QK_WIDTH), state_mlstm_conv.dtype))
    p_states = []
    s_states = []
    for l in range(DEPTH):
        p = {'ffn1_norm': ffn1_norm[l], 'ffn1_w_gate': ffn1_w_gate[l], 'ffn1_w_up': ffn1_w_up[l],
             'ffn1_w_down': ffn1_w_down[l], 'mix_norm': mix_norm[l], 'w_in': w_in[l],
             's5_A_re': s5_A_re[l], 's5_A_im': s5_A_im[l], 's5_log_dt': s5_log_dt[l],
             's5_B_re': s5_B_re[l], 's5_B_im': s5_B_im[l], 's5_C_re': s5_C_re[l], 's5_C_im': s5_C_im[l],
             's5_D': s5_D[l], 's5_w_glu': s5_w_glu[l], 'mlstm_conv_w': mlstm_conv_w[l],
             'mlstm_conv_b': mlstm_conv_b[l], 'mlstm_b_i': mlstm_b_i[l], 'mlstm_b_f': mlstm_b_f[l],
             'mlstm_norm': mlstm_norm[l], 'w_branch_s5': w_branch_s5[l], 'w_branch_mlstm': w_branch_mlstm[l],
             'w_out': w_out[l], 'ffn2_norm': ffn2_norm[l], 'ffn2_w_gate': ffn2_w_gate[l],
             'ffn2_w_up': ffn2_w_up[l], 'ffn2_w_down': ffn2_w_down[l]}
        hp, stp = layer(hp, zero_state, p, N_META)
        st_in = (state_s5_re[l], state_s5_im[l], state_mlstm_C[l], state_mlstm_n[l], state_mlstm_m[l],
                 state_mlstm_conv[l])
        hs, sts = layer(hs, st_in, p, 0)
        p_states.append(stp)
        s_states.append(sts)
    y_prompt = rmsnorm(hp, final_norm)[:, N_META:]
    y_sample = rmsnorm(hs, final_norm)
    p_s5_re = jnp.stack([s[0] for s in p_states])
    p_s5_im = jnp.stack([s[1] for s in p_states])
    p_C = jnp.stack([s[2] for s in p_states])
    p_n = jnp.stack([s[3] for s in p_states])
    p_m = jnp.stack([s[4] for s in p_states])
    p_conv = jnp.stack([s[5] for s in p_states])
    s_s5_re = jnp.stack([s[0] for s in s_states])
    s_s5_im = jnp.stack([s[1] for s in s_states])
    s_C = jnp.stack([s[2] for s in s_states])
    s_n = jnp.stack([s[3] for s in s_states])
    s_m = jnp.stack([s[4] for s in s_states])
    s_conv = jnp.stack([s[5] for s in s_states])
    return (y_prompt, y_sample, p_s5_re, p_s5_im, p_C, p_n, p_m, p_conv,
            s_s5_re, s_s5_im, s_C, s_n, s_m, s_conv)
```

```python
import functools

import jax
import jax.numpy as jnp
from jax import lax
from jax.experimental import pallas as pl
from jax.experimental.pallas import tpu as pltpu

F32 = jnp.float32
BF16 = jnp.bfloat16

D_MODEL = 2048
D_FF = 5632
N_META = 16
S5_WIDTH = 1024
S5_GROUP = 16
S5_GROUPS = 64
S5_STATE = 64
HEADS = 4
DQK = 128
DV = 256
QK_WIDTH = 1024
V_WIDTH = 1024
CONV_W = 4
EPS = 1e-6

LANES = 128
S5_T = 8
S5_GPB = LANES // S5_GROUP
S5_NBLK = S5_WIDTH // LANES
S5_SW = S5_GPB * S5_STATE
VMEM_LIMIT = 56 * 1024 * 1024


def _cparams(sem):
    return pltpu.CompilerParams(dimension_semantics=sem, vmem_limit_bytes=VMEM_LIMIT)


def _rmsnorm(x, g):
    ms = jnp.mean(x * x, axis=-1, keepdims=True)
    return (x * lax.rsqrt(ms + EPS)) * g


def _dot(a, b):
    return jnp.dot(a, b, preferred_element_type=F32)


def _dot_nt(a, b):
    return lax.dot_general(a, b, (((1,), (1,)), ((), ())), preferred_element_type=F32)


def _dot_tn(a, b):
    return lax.dot_general(a, b, (((0,), (0,)), ((), ())), preferred_element_type=F32)


def _split3(x):
    hi = x.astype(BF16)
    r = x - hi.astype(F32)
    mid = r.astype(BF16)
    lo = (r - mid.astype(F32)).astype(BF16)
    return hi, mid, lo


def _ffn_kernel(x_ref, g_ref, wg_ref, wu_ref, wd_ref, *rest, final_norm):
    if final_norm:
        fg_ref, o_ref, xn_ref = rest
    else:
        o_ref, xn_ref = rest
    j = pl.program_id(1)

    @pl.when(j == 0)
    def _():
        xn_ref[...] = _rmsnorm(x_ref[...], g_ref[...]).astype(BF16)

    xn = xn_ref[...]
    gt = _dot(xn, wg_ref[...])
    up = _dot(xn, wu_ref[...])
    act = (gt * jax.nn.sigmoid(gt) * up).astype(BF16)
    part = _dot(act, wd_ref[...])

    @pl.when(j == 0)
    def _():
        o_ref[...] = part

    @pl.when(j > 0)
    def _():
        o_ref[...] += part

    @pl.when(j == pl.num_programs(1) - 1)
    def _():
        h = x_ref[...] + 0.5 * o_ref[...]
        if final_norm:
            h = _rmsnorm(h, fg_ref[...])
        o_ref[...] = h


def _ffn(x, g, wg, wu, wd, final_g=None, *, tm, tf=512):
    n = x.shape[0]
    in_specs = [
        pl.BlockSpec((tm, D_MODEL), lambda i, j: (i, 0)),
        pl.BlockSpec((1, D_MODEL), lambda i, j: (0, 0)),
        pl.BlockSpec((D_MODEL, tf), lambda i, j: (0, j)),
        pl.BlockSpec((D_MODEL, tf), lambda i, j: (0, j)),
        pl.BlockSpec((tf, D_MODEL), lambda i, j: (j, 0)),
    ]
    args = [x, g, wg, wu, wd]
    if final_g is not None:
        in_specs.append(pl.BlockSpec((1, D_MODEL), lambda i, j: (0, 0)))
        args.append(final_g)
    return pl.pallas_call(
        functools.partial(_ffn_kernel, final_norm=final_g is not None),
        grid=(n // tm, D_FF // tf),
        in_specs=in_specs,
        out_specs=pl.BlockSpec((tm, D_MODEL), lambda i, j: (i, 0)),
        out_shape=jax.ShapeDtypeStruct((n, D_MODEL), F32),
        scratch_shapes=[pltpu.VMEM((tm, D_MODEL), BF16)],
        compiler_params=_cparams(("parallel", "arbitrary")),
        name="ffn",
    )(*args)


def _win_kernel(h_ref, g_ref, w_ref, wif_ref, z_ref, zif_ref, un_ref):
    j = pl.program_id(1)

    @pl.when(j == 0)
    def _():
        un = _rmsnorm(h_ref[...], g_ref[...]).astype(BF16)
        un_ref[...] = un
        zif_ref[...] = _dot(un, wif_ref[...])

    z_ref[...] = _dot(un_ref[...], w_ref[...])


def _win(h, g, w, wif, *, tm, tn=1024):
    n = h.shape[0]
    nw = w.shape[1]
    return pl.pallas_call(
        _win_kernel,
        grid=(n // tm, nw // tn),
        in_specs=[
            pl.BlockSpec((tm, D_MODEL), lambda i, j: (i, 0)),
            pl.BlockSpec((1, D_MODEL), lambda i, j: (0, 0)),
            pl.BlockSpec((D_MODEL, tn), lambda i, j: (0, j)),
            pl.BlockSpec((D_MODEL, LANES), lambda i, j: (0, 0)),
        ],
        out_specs=[
            pl.BlockSpec((tm, tn), lambda i, j: (i, j)),
            pl.BlockSpec((tm, LANES), lambda i, j: (i, 0)),
        ],
        out_shape=[
            jax.ShapeDtypeStruct((n, nw), F32),
            jax.ShapeDtypeStruct((n, LANES), F32),
        ],
        scratch_shapes=[pltpu.VMEM((tm, D_MODEL), BF16)],
        compiler_params=_cparams(("parallel", "arbitrary")),
        name="w_in",
    )(h, g, w, wif)


def _s5_prep_kernel(f_ref, ct_ref, w_ref):
    w_ref[...] = jnp.zeros(w_ref.shape, w_ref.dtype)
    ct = ct_ref[0]
    c_hi, c_mid, _ = _split3(ct)
    for lag in range(S5_T):
        r0 = (S5_T - 1 - lag) * LANES
        x = f_ref[0, r0:r0 + LANES, :]
        x_hi, x_mid, _ = _split3(x)
        k = (_dot(x_hi, c_hi) + _dot(x_mid, c_hi) + _dot(x_hi, c_mid)).astype(BF16)
        for s in range(S5_T - lag):
            t = s + lag
            w_ref[0, s * LANES:(s + 1) * LANES, t * LANES:(t + 1) * LANES] = k


def _s5_prep(f_mat, ct_mat):
    n = S5_T * LANES
    return pl.pallas_call(
        _s5_prep_kernel,
        grid=(S5_NBLK,),
        in_specs=[
            pl.BlockSpec((1, n, 2 * S5_SW), lambda j: (j, 0, 0)),
            pl.BlockSpec((1, 2 * S5_SW, LANES), lambda j: (j, 0, 0)),
        ],
        out_specs=pl.BlockSpec((1, n, n), lambda j: (j, 0, 0)),
        out_shape=jax.ShapeDtypeStruct((S5_NBLK, n, n), BF16),
        compiler_params=_cparams(("parallel",)),
        name="s5_prep",
    )(f_mat, ct_mat)


def _s5_kernel(x_ref, w_ref, f_ref, e_ref, lam_ref, d_ref, h0re_ref, h0im_ref,
               y_ref, ore_ref, oim_ref, yi_ref, s_ref, hin_ref, *, nseq, nblk, rt):
    rows = nseq * nblk
    ar = lam_ref[0, 0:1, :]
    ai = lam_ref[0, 1:2, :]

    def load_u(r0):
        return jnp.concatenate(
            [x_ref[pl.ds(r0 * S5_T + s, rt, stride=S5_T), :] for s in range(S5_T)], axis=1)

    for r0 in range(0, rows, rt):
        ub = load_u(r0).astype(BF16)
        yi_ref[r0:r0 + rt, :] = _dot(ub, w_ref[0])
        s_ref[r0:r0 + rt, :] = _dot(ub, f_ref[0])

    if nblk == 1:
        hre = h0re_ref[...]
        him = h0im_ref[...]
        hin_ref[:, :S5_SW] = hre
        hin_ref[:, S5_SW:] = him
        ore_ref[...] = ar * hre - ai * him + s_ref[:, :S5_SW]
        oim_ref[...] = ar * him + ai * hre + s_ref[:, S5_SW:]
    else:
        def body(c, carry):
            new = []
            for b in range(nseq):
                hre, him = carry[2 * b], carry[2 * b + 1]
                row = b * nblk + c
                hin_ref[pl.ds(row, 1), :S5_SW] = hre
                hin_ref[pl.ds(row, 1), S5_SW:] = him
                sre = s_ref[pl.ds(row, 1), :S5_SW]
                sim = s_ref[pl.ds(row, 1), S5_SW:]
                new.append(ar * hre - ai * him + sre)
                new.append(ar * him + ai * hre + sim)
            return tuple(new)

        init = []
        for b in range(nseq):
            init.append(h0re_ref[b:b + 1, :])
            init.append(h0im_ref[b:b + 1, :])
        fin = lax.fori_loop(0, nblk, body, tuple(init))
        for b in range(nseq):
            ore_ref[b:b + 1, :] = fin[2 * b]
            oim_ref[b:b + 1, :] = fin[2 * b + 1]

    dt = jnp.concatenate([d_ref[...]] * S5_T, axis=1)
    for r0 in range(0, rows, rt):
        yo = _dot(hin_ref[r0:r0 + rt, :].astype(BF16), e_ref[0])
        y = jax.nn.gelu(yi_ref[r0:r0 + rt, :] + yo + load_u(r0) * dt)
        for t in range(S5_T):
            y_ref[pl.ds(r0 * S5_T + t, rt, stride=S5_T), :] = y[:, t * LANES:(t + 1) * LANES]


def _s5(z, w, f, e, lam, d, h0re, h0im, *, nseq, seqlen):
    nblk = seqlen // S5_T
    rows = nseq * nblk
    rt = min(rows, 256)
    n = S5_T * LANES
    tok = nseq * seqlen
    kern = functools.partial(_s5_kernel, nseq=nseq, nblk=nblk, rt=rt)
    return pl.pallas_call(
        kern,
        grid=(S5_NBLK,),
        in_specs=[
            pl.BlockSpec((tok, LANES), lambda j: (0, j)),
            pl.BlockSpec((1, n, n), lambda j: (j, 0, 0)),
            pl.BlockSpec((1, n, 2 * S5_SW), lambda j: (j, 0, 0)),
            pl.BlockSpec((1, 2 * S5_SW, n), lambda j: (j, 0, 0)),
            pl.BlockSpec((1, 2, S5_SW), lambda j: (j, 0, 0)),
            pl.BlockSpec((1, LANES), lambda j: (0, j)),
            pl.BlockSpec((nseq, S5_SW), lambda j: (0, j)),
            pl.BlockSpec((nseq, S5_SW), lambda j: (0, j)),
        ],
        out_specs=[
            pl.BlockSpec((tok, LANES), lambda j: (0, j)),
            pl.BlockSpec((nseq, S5_SW), lambda j: (0, j)),
            pl.BlockSpec((nseq, S5_SW), lambda j: (0, j)),
        ],
        out_shape=[
            jax.ShapeDtypeStruct((tok, S5_WIDTH), F32),
            jax.ShapeDtypeStruct((nseq, S5_GROUPS * S5_STATE), F32),
            jax.ShapeDtypeStruct((nseq, S5_GROUPS * S5_STATE), F32),
        ],
        scratch_shapes=[
            pltpu.VMEM((rows, n), F32),
            pltpu.VMEM((rows, 2 * S5_SW), F32),
            pltpu.VMEM((rows, 2 * S5_SW), F32),
        ],
        compiler_params=_cparams(("parallel",)),
        name="s5",
    )(z, w, f, e, lam, d, h0re, h0im)


def _s5_operators(a_re, a_im, log_dt, b_re, b_im, c_re, c_im):
    dt = jnp.exp(log_dt)[:, None]
    mag = jnp.exp(a_re * dt)
    abar_re = mag * jnp.cos(a_im * dt)
    abar_im = mag * jnp.sin(a_im * dt)
    nr, ni = abar_re - 1.0, abar_im
    den = a_re * a_re + a_im * a_im
    q_re = (nr * a_re + ni * a_im) / den
    q_im = (ni * a_re - nr * a_im) / den
    bb_re = q_re[:, :, None] * b_re - q_im[:, :, None] * b_im
    bb_im = q_re[:, :, None] * b_im + q_im[:, :, None] * b_re
    taus = jnp.arange(S5_T + 1, dtype=F32)[:, None, None]
    pmag = jnp.exp(taus * (a_re * dt)[None])
    pw_re = pmag * jnp.cos(taus * (a_im * dt)[None])
    pw_im = pmag * jnp.sin(taus * (a_im * dt)[None])
    eye = jnp.eye(S5_GPB, dtype=F32)

    def blockdiag(x_gab, order):
        lead = x_gab.shape[:-3]
        x = x_gab.reshape(lead + (S5_NBLK, S5_GPB) + x_gab.shape[-2:])
        return jnp.einsum(order, x, eye)

    lag = S5_T - 1 - jnp.arange(S5_T)
    fb_re = pw_re[lag][:, :, :, None] * bb_re[None] - pw_im[lag][:, :, :, None] * bb_im[None]
    fb_im = pw_re[lag][:, :, :, None] * bb_im[None] + pw_im[lag][:, :, :, None] * bb_re[None]
    f_re = blockdiag(jnp.swapaxes(fb_re, -1, -2), "sjgab,gk->jsgakb")
    f_im = blockdiag(jnp.swapaxes(fb_im, -1, -2), "sjgab,gk->jsgakb")
    n = S5_T * LANES
    f_mat = jnp.concatenate([f_re.reshape(S5_NBLK, n, S5_SW), f_im.reshape(S5_NBLK, n, S5_SW)], axis=2)

    tt = 1 + jnp.arange(S5_T)
    cp = jnp.swapaxes(c_re, -1, -2), jnp.swapaxes(c_im, -1, -2)
    ce_re = cp[0][None] * pw_re[tt][:, :, :, None] - cp[1][None] * pw_im[tt][:, :, :, None]
    ce_im = cp[0][None] * pw_im[tt][:, :, :, None] + cp[1][None] * pw_re[tt][:, :, :, None]
    e_re = blockdiag(ce_re, "tjgab,gk->jgatkb").reshape(S5_NBLK, S5_SW, n)
    e_im = blockdiag(-ce_im, "tjgab,gk->jgatkb").reshape(S5_NBLK, S5_SW, n)
    e_mat = jnp.concatenate([e_re, e_im], axis=1)

    ct_re = blockdiag(cp[0], "jgab,gk->jgakb").reshape(S5_NBLK, S5_SW, LANES)
    ct_im = blockdiag(-cp[1], "jgab,gk->jgakb").reshape(S5_NBLK, S5_SW, LANES)
    ct_mat = jnp.concatenate([ct_re, ct_im], axis=1)

    lam = jnp.stack([pw_re[S5_T].reshape(S5_NBLK, S5_SW), pw_im[S5_T].reshape(S5_NBLK, S5_SW)], axis=1)
    return f_mat, e_mat, ct_mat, lam


def _mlstm_kernel(qk_ref, v_ref, o_ref, if_ref, cw_ref, cb_ref, bif_ref, ng_ref,
                  buf0_ref, c0_ref, n0_ref, m0_ref,
                  y_ref, bufo_ref, co_ref, no_ref, mo_ref, xp_ref, *, nb, lc):
    @pl.when(pl.program_id(1) == 0)
    def _():
        bufo_ref[...] = buf0_ref[...]
        co_ref[...] = c0_ref[...]
        no_ref[...] = n0_ref[...]
        mo_ref[...] = m0_ref[...]

    lp = max(lc, LANES)
    row = lax.broadcasted_iota(jnp.int32, (lc, lc), 0)
    col = lax.broadcasted_iota(jnp.int32, (lc, lc), 1)
    causal = row >= col
    tril = jnp.where(causal, 1.0, 0.0).astype(BF16)
    lane = lax.broadcasted_iota(jnp.int32, (lc, LANES), 1)
    is_i = lane < HEADS
    is_f = jnp.logical_and(lane >= HEADS, lane < 2 * HEADS)

    for s in range(nb):
        r0 = s * lc
        xp_ref[5:8, :] = bufo_ref[s]
        xp_ref[8:8 + lc, :] = qk_ref[r0:r0 + lc, :]
        conv = cb_ref[...] + sum(xp_ref[5 + j:5 + j + lc, :] * cw_ref[j:j + 1, :] for j in range(CONV_W))
        bufo_ref[s] = xp_ref[5 + lc:8 + lc, :]
        qk = conv * jax.nn.sigmoid(conv)
        q = qk[:, :QK_WIDTH // 2] * (DQK ** -0.5)
        k = qk[:, QK_WIDTH // 2:]

        gate = if_ref[r0:r0 + lc, :] + bif_ref[...]
        lf = jnp.where(is_f, jax.nn.log_sigmoid(gate), 0.0)
        parts = _split3(lf)
        bcum = _dot(tril, parts[0]) + _dot(tril, parts[1]) + _dot(tril, parts[2])
        pc = jnp.where(is_i, gate, bcum)
        if lc < lp:
            pc_t = jnp.concatenate([pc, jnp.zeros((lp - lc, LANES), F32)], axis=0).T[:, :lc]
        else:
            pc_t = pc.T

        for h in range(HEADS):
            i_col = pc[:, h:h + 1]
            b_col = pc[:, HEADS + h:HEADS + h + 1]
            i_row = pc_t[h:h + 1, :]
            b_row = pc_t[HEADS + h:HEADS + h + 1, :]
            m0 = mo_ref[s, :, h:h + 1]
            logw = jnp.where(causal, b_col - b_row + i_row, -jnp.inf)
            g = b_col + m0
            m = jnp.maximum(g, jnp.max(logw, axis=-1, keepdims=True))
            w = jnp.exp(logw - m)
            inter = jnp.exp(g - m)
            qh = q[:, h * DQK:(h + 1) * DQK]
            kh = k[:, h * DQK:(h + 1) * DQK]
            vh = v_ref[r0:r0 + lc, h * DV:(h + 1) * DV]
            qb = qh.astype(BF16)
            vb = vh.astype(BF16)
            c_prev = co_ref[s, h]
            n_prev = no_ref[s, h:h + 1, :]
            sc = _dot_nt(qb, kh.astype(BF16)) * w
            num = _dot(sc.astype(BF16), vb) + inter * _dot_nt(qb, c_prev.astype(BF16))
            den = jnp.sum(sc, axis=-1, keepdims=True) + inter * jnp.sum(qh * n_prev, axis=-1, keepdims=True)
            hh = num / jnp.maximum(jnp.abs(den), jnp.exp(-m))
            hh = hh * lax.rsqrt(jnp.mean(hh * hh, axis=-1, keepdims=True) + EPS)
            og = jax.nn.sigmoid(o_ref[r0:r0 + lc, h * DV:(h + 1) * DV])
            y_ref[r0:r0 + lc, h * DV:(h + 1) * DV] = hh * ng_ref[:, h * DV:(h + 1) * DV] * og

            b_last = b_col[lc - 1:lc, :]
            m_new = m[lc - 1:lc, :]
            decay = jnp.exp(b_last + m0 - m_new)
            kw = kh * jnp.exp(b_last - b_col + i_col - m_new)
            co_ref[s, h] = decay * c_prev + _dot_tn(vb, kw.astype(BF16))
            no_ref[s, h:h + 1, :] = decay * n_prev + jnp.sum(kw, axis=0, keepdims=True)
            mo_ref[s, :, h:h + 1] = m_new


def _mlstm(z, zif, cw, cb, bif, ng, buf0, c0, n0, m0, *, nseq, seqlen, nb, lc):
    tok = nseq * seqlen
    nchunk = seqlen // lc
    rows = nb * lc
    tok_map = lambda col: (lambda i, c: (i * nchunk + c, col))
    const2 = lambda i, c: (0, 0)
    st3 = lambda i, c: (i, 0, 0)
    st4 = lambda i, c: (i, 0, 0, 0)
    kern = functools.partial(_mlstm_kernel, nb=nb, lc=lc)
    return pl.pallas_call(
        kern,
        grid=(nseq // nb, nchunk),
        in_specs=[
            pl.BlockSpec((rows, QK_WIDTH), tok_map(1)),
            pl.BlockSpec((rows, V_WIDTH), tok_map(2)),
            pl.BlockSpec((rows, V_WIDTH), tok_map(3)),
            pl.BlockSpec((rows, LANES), tok_map(0)),
            pl.BlockSpec((CONV_W, QK_WIDTH), const2),
            pl.BlockSpec((1, QK_WIDTH), const2),
            pl.BlockSpec((1, LANES), const2),
            pl.BlockSpec((1, V_WIDTH), const2),
            pl.BlockSpec((nb, CONV_W - 1, QK_WIDTH), st3),
            pl.BlockSpec((nb, HEADS, DV, DQK), st4),
            pl.BlockSpec((nb, HEADS, DQK), st3),
            pl.BlockSpec((nb, 1, HEADS), st3),
        ],
        out_specs=[
            pl.BlockSpec((rows, V_WIDTH), tok_map(0)),
            pl.BlockSpec((nb, CONV_W - 1, QK_WIDTH), st3),
            pl.BlockSpec((nb, HEADS, DV, DQK), st4),
            pl.BlockSpec((nb, HEADS, DQK), st3),
            pl.BlockSpec((nb, 1, HEADS), st3),
        ],
        out_shape=[
            jax.ShapeDtypeStruct((tok, V_WIDTH), F32),
            jax.ShapeDtypeStruct((nseq, CONV_W - 1, QK_WIDTH), F32),
            jax.ShapeDtypeStruct((nseq, HEADS, DV, DQK), F32),
            jax.ShapeDtypeStruct((nseq, HEADS, DQK), F32),
            jax.ShapeDtypeStruct((nseq, 1, HEADS), F32),
        ],
        scratch_shapes=[pltpu.VMEM((lc + 8, QK_WIDTH), F32)],
        compiler_params=_cparams(("parallel", "arbitrary")),
        name="mlstm",
    )(z, z, z, zif, cw, cb, bif, ng, buf0, c0, n0, m0)


def _merge_kernel(h_ref, ys_ref, ym_ref, g1_ref, g2_ref, wglu_ref, wbs_ref, wbm_ref, wo_ref, o_ref):
    ys = ys_ref[...]
    glu = ys * jax.nn.sigmoid(_dot(ys.astype(BF16), wglu_ref[...]))
    a = _dot(glu.astype(BF16), wbs_ref[...])
    b = _dot(ym_ref[...].astype(BF16), wbm_ref[...])
    merged = jax.nn.sigmoid(g1_ref[...]) * a + jax.nn.sigmoid(g2_ref[...]) * b
    o_ref[...] = h_ref[...] + _dot(merged.astype(BF16), wo_ref[...])


def _merge(h, ys, ym, z, wglu, wbs, wbm, wo, *, tm):
    n = h.shape[0]
    row = lambda i: (i, 0)
    const = lambda i: (0, 0)
    resident = lambda shape: pl.BlockSpec(shape, const, pipeline_mode=pl.Buffered(1))
    return pl.pallas_call(
        _merge_kernel,
        grid=(n // tm,),
        in_specs=[
            pl.BlockSpec((tm, D_MODEL), row),
            pl.BlockSpec((tm, S5_WIDTH), row),
            pl.BlockSpec((tm, V_WIDTH), row),
            pl.BlockSpec((tm, D_MODEL), lambda i: (i, 2)),
            pl.BlockSpec((tm, D_MODEL), lambda i: (i, 3)),
            resident((S5_WIDTH, S5_WIDTH)),
            resident((S5_WIDTH, D_MODEL)),
            resident((V_WIDTH, D_MODEL)),
            resident((D_MODEL, D_MODEL)),
        ],
        out_specs=pl.BlockSpec((tm, D_MODEL), row),
        out_shape=jax.ShapeDtypeStruct((n, D_MODEL), F32),
        compiler_params=_cparams(("parallel",)),
        name="merge",
    )(h, ys, ym, z, z, wglu, wbs, wbm, wo)


def kernel(x_prompt, x_sample, state_s5_re, state_s5_im, state_mlstm_C, state_mlstm_n, state_mlstm_m,
           state_mlstm_conv, meta_tokens, ffn1_norm, ffn1_w_gate, ffn1_w_up, ffn1_w_down, mix_norm, w_in,
           s5_A_re, s5_A_im, s5_log_dt, s5_B_re, s5_B_im, s5_C_re, s5_C_im, s5_D, s5_w_glu,
           mlstm_conv_w, mlstm_conv_b, mlstm_b_i, mlstm_b_f, mlstm_norm, w_branch_s5, w_branch_mlstm,
           w_out, ffn2_norm, ffn2_w_gate, ffn2_w_up, ffn2_w_down, final_norm):
    nbatch, seq, _ = x_prompt.shape
    nsamp, sseq, _ = x_sample.shape
    l = 0

    w1g, w1u, w1d = (w[l].astype(BF16) for w in (ffn1_w_gate, ffn1_w_up, ffn1_w_down))
    w2g, w2u, w2d = (w[l].astype(BF16) for w in (ffn2_w_gate, ffn2_w_up, ffn2_w_down))
    o_if = S5_WIDTH + QK_WIDTH + 2 * V_WIDTH
    o_gate = o_if + 2 * HEADS
    win = w_in[l]
    w_wide = jnp.concatenate([win[:, :o_if], win[:, o_gate:]], axis=1).astype(BF16)
    w_if = jnp.pad(win[:, o_if:o_gate], ((0, 0), (0, LANES - 2 * HEADS))).astype(BF16)
    wglu, wbs, wbm, wo = (w[l].astype(BF16) for w in (s5_w_glu, w_branch_s5, w_branch_mlstm, w_out))
    g1 = ffn1_norm[l][None]
    gm = mix_norm[l][None]
    g2 = ffn2_norm[l][None]
    gf = final_norm[None]
    bif = jnp.pad(jnp.concatenate([mlstm_b_i[l], mlstm_b_f[l]]), (0, LANES - 2 * HEADS))[None]
    cw = mlstm_conv_w[l]
    cb = mlstm_conv_b[l][None]
    ng = mlstm_norm[l][None]
    d_skip = s5_D[l][None]

    f_mat, e_mat, ct_mat, lam = _s5_operators(s5_A_re[l], s5_A_im[l], s5_log_dt[l], s5_B_re[l], s5_B_im[l],
                                               s5_C_re[l], s5_C_im[l])
    w_toe = _s5_prep(f_mat, ct_mat)
    f_bf = f_mat.astype(BF16)
    e_bf = e_mat.astype(BF16)

    def front(x, tm):
        h1 = _ffn(x, g1, w1g, w1u, w1d, tm=tm)
        z, zif = _win(h1, gm, w_wide, w_if, tm=tm)
        return h1, z, zif

    def mixers(z, zif, s5_state, ml_state, *, nseq, seqlen, nb, lc):
        ys, sre, sim = _s5(z, w_toe, f_bf, e_bf, lam, d_skip, s5_state[0], s5_state[1], nseq=nseq, seqlen=seqlen)
        buf0, c0, n0, m0 = ml_state
        ym, buf, c, n, m = _mlstm(z, zif, cw, cb, bif, ng, buf0, c0, n0, m0.reshape(nseq, 1, HEADS),
                                  nseq=nseq, seqlen=seqlen, nb=nb, lc=lc)
        return ys, ym, (sre, sim), (buf, c, n, m.reshape(nseq, HEADS))

    def back(h1, ys, ym, z, tm):
        h2 = _merge(h1, ys, ym, z, wglu, wbs, wbm, wo, tm=tm // 2)
        return _ffn(h2, g2, w2g, w2u, w2d, gf, tm=tm)

    x_m = jnp.tile(meta_tokens, (nbatch, 1))
    _, z_m, zif_m = front(x_m, nbatch * N_META)
    zeros = lambda *s: jnp.zeros((nbatch,) + s, F32)
    _, _, s5_m, ml_m = mixers(
        z_m, zif_m, (zeros(S5_GROUPS * S5_STATE), zeros(S5_GROUPS * S5_STATE)),
        (zeros(CONV_W - 1, QK_WIDTH), zeros(HEADS, DV, DQK), zeros(HEADS, DQK), zeros(HEADS)),
        nseq=nbatch, seqlen=N_META, nb=1, lc=N_META)

    h1_p, z_p, zif_p = front(x_prompt.reshape(nbatch * seq, D_MODEL), 512)
    ys_p, ym_p, s5_p, ml_p = mixers(z_p, zif_p, s5_m, ml_m, nseq=nbatch, seqlen=seq, nb=1, lc=256)
    y_p = back(h1_p, ys_p, ym_p, z_p, 512)

    h1_s, z_s, zif_s = front(x_sample.reshape(nsamp * sseq, D_MODEL), 512)
    ys_s, ym_s, s5_s, ml_s = mixers(
        z_s, zif_s,
        (state_s5_re[l].reshape(nsamp, -1), state_s5_im[l].reshape(nsamp, -1)),
        (state_mlstm_conv[l], state_mlstm_C[l], state_mlstm_n[l], state_mlstm_m[l]),
        nseq=nsamp, seqlen=sseq, nb=8, lc=sseq)
    y_s = back(h1_s, ys_s, ym_s, z_s, 512)

    def pack(n, s5_st, ml_st):
        buf, c, nn, m = ml_st
        return (s5_st[0].reshape(1, n, S5_GROUPS, S5_STATE), s5_st[1].reshape(1, n, S5_GROUPS, S5_STATE),
                c[None], nn[None], m[None], buf[None])

    return ((y_p.reshape(nbatch, seq, D_MODEL), y_s.reshape(nsamp, sseq, D_MODEL))
            + pack(nbatch, s5_p, ml_p) + pack(nsamp, s5_s, ml_s))
```

```python
import functools

import jax
import jax.numpy as jnp
from jax import lax
from jax.experimental import pallas as pl
from jax.experimental.pallas import tpu as pltpu

F32 = jnp.float32
BF16 = jnp.bfloat16

D_MODEL = 2048
D_FF = 5632
N_META = 16
S5_WIDTH = 1024
S5_GROUP = 16
S5_GROUPS = 64
S5_STATE = 64
HEADS = 4
DQK = 128
DV = 256
QK_WIDTH = 1024
V_WIDTH = 1024
CONV_W = 4
EPS = 1e-6

LANES = 128
S5_T = 8
S5_GPB = LANES // S5_GROUP
S5_NBLK = S5_WIDTH // LANES
S5_SW = S5_GPB * S5_STATE
VMEM_LIMIT = 56 * 1024 * 1024


def _cparams(sem):
    return pltpu.CompilerParams(dimension_semantics=sem, vmem_limit_bytes=VMEM_LIMIT)


def _rmsnorm(x, g):
    ms = jnp.mean(x * x, axis=-1, keepdims=True)
    return (x * lax.rsqrt(ms + EPS)) * g


def _dot(a, b):
    return jnp.dot(a, b, preferred_element_type=F32)


def _dot_nt(a, b):
    return lax.dot_general(a, b, (((1,), (1,)), ((), ())), preferred_element_type=F32)


def _dot_tn(a, b):
    return lax.dot_general(a, b, (((0,), (0,)), ((), ())), preferred_element_type=F32)


def _split3(x):
    hi = x.astype(BF16)
    r = x - hi.astype(F32)
    mid = r.astype(BF16)
    lo = (r - mid.astype(F32)).astype(BF16)
    return hi, mid, lo


def _ffn_kernel(x_ref, g_ref, wg_ref, wu_ref, wd_ref, *rest, final_norm):
    if final_norm:
        fg_ref, o_ref, xn_ref = rest
    else:
        o_ref, xn_ref = rest
    j = pl.program_id(1)

    @pl.when(j == 0)
    def _():
        xn_ref[...] = _rmsnorm(x_ref[...], g_ref[...]).astype(BF16)
        o_ref[...] = jnp.zeros(o_ref.shape, F32)

    xn = xn_ref[...]
    gt = _dot(xn, wg_ref[...])
    up = _dot(xn, wu_ref[...])
    act = (gt * jax.nn.sigmoid(gt) * up).astype(BF16)
    o_ref[...] += _dot(act, wd_ref[...])

    @pl.when(j == pl.num_programs(1) - 1)
    def _():
        h = x_ref[...] + 0.5 * o_ref[...]
        if final_norm:
            h = _rmsnorm(h, fg_ref[...])
        o_ref[...] = h


def _ffn(x, g, wg, wu, wd, final_g=None, *, tm, tf=512):
    n = x.shape[0]
    in_specs = [
        pl.BlockSpec((tm, D_MODEL), lambda i, j: (i, 0)),
        pl.BlockSpec((1, D_MODEL), lambda i, j: (0, 0)),
        pl.BlockSpec((D_MODEL, tf), lambda i, j: (0, j)),
        pl.BlockSpec((D_MODEL, tf), lambda i, j: (0, j)),
        pl.BlockSpec((tf, D_MODEL), lambda i, j: (j, 0)),
    ]
    args = [x, g, wg, wu, wd]
    if final_g is not None:
        in_specs.append(pl.BlockSpec((1, D_MODEL), lambda i, j: (0, 0)))
        args.append(final_g)
    return pl.pallas_call(
        functools.partial(_ffn_kernel, final_norm=final_g is not None),
        grid=(n // tm, D_FF // tf),
        in_specs=in_specs,
        out_specs=pl.BlockSpec((tm, D_MODEL), lambda i, j: (i, 0)),
        out_shape=jax.ShapeDtypeStruct((n, D_MODEL), F32),
        scratch_shapes=[pltpu.VMEM((tm, D_MODEL), BF16)],
        compiler_params=_cparams(("parallel", "arbitrary")),
        name="ffn",
    )(*args)


def _win_kernel(h_ref, g_ref, wa_ref, wb_ref, wif_ref, z_ref, zif_ref, un_ref, *, na):
    j = pl.program_id(1)

    @pl.when(j == 0)
    def _():
        un = _rmsnorm(h_ref[...], g_ref[...]).astype(BF16)
        un_ref[...] = un
        zif_ref[...] = _dot(un, wif_ref[...])

    @pl.when(j < na)
    def _():
        z_ref[...] = _dot(un_ref[...], wa_ref[...])

    @pl.when(j >= na)
    def _():
        z_ref[...] = _dot(un_ref[...], wb_ref[...])


def _win(h, g, wa, wb, wif, *, tm, tn=512):
    n = h.shape[0]
    na = wa.shape[1] // tn
    nb = wb.shape[1] // tn
    return pl.pallas_call(
        functools.partial(_win_kernel, na=na),
        grid=(n // tm, na + nb),
        in_specs=[
            pl.BlockSpec((tm, D_MODEL), lambda i, j: (i, 0)),
            pl.BlockSpec((1, D_MODEL), lambda i, j: (0, 0)),
            pl.BlockSpec((D_MODEL, tn), lambda i, j: (0, jnp.minimum(j, na - 1))),
            pl.BlockSpec((D_MODEL, tn), lambda i, j: (0, jnp.maximum(j - na, 0))),
            pl.BlockSpec((D_MODEL, LANES), lambda i, j: (0, 0)),
        ],
        out_specs=[
            pl.BlockSpec((tm, tn), lambda i, j: (i, j)),
            pl.BlockSpec((tm, LANES), lambda i, j: (i, 0)),
        ],
        out_shape=[
            jax.ShapeDtypeStruct((n, (na + nb) * tn), F32),
            jax.ShapeDtypeStruct((n, LANES), F32),
        ],
        scratch_shapes=[pltpu.VMEM((tm, D_MODEL), BF16)],
        compiler_params=_cparams(("parallel", "arbitrary")),
        name="w_in",
    )(h, g, wa, wb, wif)


def _s5_prep_kernel(fc_ref, ec_ref, cc_ref, w_ref, f_ref, e_ref):
    n = S5_T * LANES
    sw2 = 2 * S5_SW
    i32 = jnp.int32
    lg_h = S5_GROUP.bit_length() - 1
    lg_p = S5_STATE.bit_length() - 1
    lg_sw = S5_SW.bit_length() - 1

    def tile_mat(k, c, src_col):
        kk = lax.broadcasted_iota(i32, (k, c), 0)
        cc = lax.broadcasted_iota(i32, (k, c), 1)
        return jnp.where(kk == src_col(cc), 1.0, 0.0).astype(BF16)

    def group_mask(r, c, row_group, col_group):
        rr = lax.broadcasted_iota(i32, (r, c), 0)
        cc = lax.broadcasted_iota(i32, (r, c), 1)
        return row_group(rr) == col_group(cc)

    gmask = S5_GPB - 1
    rf = tile_mat(2 * S5_STATE, sw2, lambda c: ((c >> lg_sw) << lg_p) + (c & (S5_STATE - 1)))
    mf = group_mask(n, sw2, lambda r: (r >> lg_h) & gmask, lambda c: (c >> lg_p) & gmask)
    f_hi, f_mid, _ = _split3(fc_ref[0])
    x_hi = jnp.where(mf, _dot(f_hi, rf), 0.0).astype(BF16)
    x_mid = jnp.where(mf, _dot(f_mid, rf), 0.0).astype(BF16)
    f_ref[0] = x_hi
    lg_l = LANES.bit_length() - 1
    re = tile_mat(LANES, n, lambda c: ((c >> lg_l) << lg_h) + (c & (S5_GROUP - 1)))
    me = group_mask(sw2, n, lambda r: (r >> lg_p) & gmask, lambda c: (c >> lg_h) & gmask)
    e_ref[0] = jnp.where(me, _dot(ec_ref[0].astype(BF16), re), 0.0).astype(BF16)
    rc = tile_mat(S5_GROUP, LANES, lambda c: c & (S5_GROUP - 1))
    mc = group_mask(sw2, LANES, lambda r: (r >> lg_p) & gmask, lambda c: c >> lg_h)
    c_hi, c_mid, _ = _split3(cc_ref[0])
    ct_hi = jnp.where(mc, _dot(c_hi, rc), 0.0).astype(BF16)
    ct_mid = jnp.where(mc, _dot(c_mid, rc), 0.0).astype(BF16)
    w_ref[...] = jnp.zeros(w_ref.shape, w_ref.dtype)
    for lag in range(S5_T):
        r0 = (S5_T - 1 - lag) * LANES
        xh = x_hi[r0:r0 + LANES, :]
        xm = x_mid[r0:r0 + LANES, :]
        k = (_dot(xh, ct_hi) + _dot(xm, ct_hi) + _dot(xh, ct_mid)).astype(BF16)
        for s in range(S5_T - lag):
            t = s + lag
            w_ref[0, s * LANES:(s + 1) * LANES, t * LANES:(t + 1) * LANES] = k


def _s5_prep(fc, ec, cc):
    n = S5_T * LANES
    blk = lambda r, c: pl.BlockSpec((1, r, c), lambda j: (j, 0, 0))
    return pl.pallas_call(
        _s5_prep_kernel,
        grid=(S5_NBLK,),
        in_specs=[blk(n, 2 * S5_STATE), blk(2 * S5_SW, LANES), blk(2 * S5_SW, S5_GROUP)],
        out_specs=[blk(n, n), blk(n, 2 * S5_SW), blk(2 * S5_SW, n)],
        out_shape=[
            jax.ShapeDtypeStruct((S5_NBLK, n, n), BF16),
            jax.ShapeDtypeStruct((S5_NBLK, n, 2 * S5_SW), BF16),
            jax.ShapeDtypeStruct((S5_NBLK, 2 * S5_SW, n), BF16),
        ],
        compiler_params=_cparams(("parallel",)),
        name="s5_prep",
    )(fc, ec, cc)


def _s5_kernel(x_ref, w_ref, f_ref, e_ref, lam_ref, d_ref, h0re_ref, h0im_ref,
               y_ref, ore_ref, oim_ref, yi_ref, s_ref, hin_ref, *, nseq, nblk, rt):
    rows = nseq * nblk
    ar = lam_ref[0, 0:1, :]
    ai = lam_ref[0, 1:2, :]

    def load_u(r0):
        return jnp.concatenate(
            [x_ref[pl.ds(r0 * S5_T + s, rt, stride=S5_T), :] for s in range(S5_T)], axis=1)

    for r0 in range(0, rows, rt):
        ub = load_u(r0).astype(BF16)
        yi_ref[r0:r0 + rt, :] = _dot(ub, w_ref[0])
        s_ref[r0:r0 + rt, :] = _dot(ub, f_ref[0])

    if nblk == 1:
        hre = h0re_ref[...]
        him = h0im_ref[...]
        hin_ref[:, :S5_SW] = hre
        hin_ref[:, S5_SW:] = him
        ore_ref[...] = ar * hre - ai * him + s_ref[:, :S5_SW]
        oim_ref[...] = ar * him + ai * hre + s_ref[:, S5_SW:]
    else:
        def body(c, carry):
            new = []
            for b in range(nseq):
                hre, him = carry[2 * b], carry[2 * b + 1]
                row = b * nblk + c
                hin_ref[pl.ds(row, 1), :S5_SW] = hre
                hin_ref[pl.ds(row, 1), S5_SW:] = him
                sre = s_ref[pl.ds(row, 1), :S5_SW]
                sim = s_ref[pl.ds(row, 1), S5_SW:]
                new.append(ar * hre - ai * him + sre)
                new.append(ar * him + ai * hre + sim)
            return tuple(new)

        init = []
        for b in range(nseq):
            init.append(h0re_ref[b:b + 1, :])
            init.append(h0im_ref[b:b + 1, :])
        fin = lax.fori_loop(0, nblk, body, tuple(init))
        for b in range(nseq):
            ore_ref[b:b + 1, :] = fin[2 * b]
            oim_ref[b:b + 1, :] = fin[2 * b + 1]

    dt = jnp.concatenate([d_ref[...]] * S5_T, axis=1)
    for r0 in range(0, rows, rt):
        yo = _dot(hin_ref[r0:r0 + rt, :].astype(BF16), e_ref[0])
        y = jax.nn.gelu(yi_ref[r0:r0 + rt, :] + yo + load_u(r0) * dt)
        for t in range(S5_T):
            y_ref[pl.ds(r0 * S5_T + t, rt, stride=S5_T), :] = y[:, t * LANES:(t + 1) * LANES]


def _s5(z, w, f, e, lam, d, h0re, h0im, *, nseq, seqlen):
    nblk = seqlen // S5_T
    rows = nseq * nblk
    rt = min(rows, 256)
    n = S5_T * LANES
    tok = nseq * seqlen
    kern = functools.partial(_s5_kernel, nseq=nseq, nblk=nblk, rt=rt)
    return pl.pallas_call(
        kern,
        grid=(S5_NBLK,),
        in_specs=[
            pl.BlockSpec((tok, LANES), lambda j: (0, j)),
            pl.BlockSpec((1, n, n), lambda j: (j, 0, 0)),
            pl.BlockSpec((1, n, 2 * S5_SW), lambda j: (j, 0, 0)),
            pl.BlockSpec((1, 2 * S5_SW, n), lambda j: (j, 0, 0)),
            pl.BlockSpec((1, 2, S5_SW), lambda j: (j, 0, 0)),
            pl.BlockSpec((1, LANES), lambda j: (0, j)),
            pl.BlockSpec((nseq, S5_SW), lambda j: (0, j)),
            pl.BlockSpec((nseq, S5_SW), lambda j: (0, j)),
        ],
        out_specs=[
            pl.BlockSpec((tok, LANES), lambda j: (0, j)),
            pl.BlockSpec((nseq, S5_SW), lambda j: (0, j)),
            pl.BlockSpec((nseq, S5_SW), lambda j: (0, j)),
        ],
        out_shape=[
            jax.ShapeDtypeStruct((tok, S5_WIDTH), F32),
            jax.ShapeDtypeStruct((nseq, S5_GROUPS * S5_STATE), F32),
            jax.ShapeDtypeStruct((nseq, S5_GROUPS * S5_STATE), F32),
        ],
        scratch_shapes=[
            pltpu.VMEM((rows, n), F32),
            pltpu.VMEM((rows, 2 * S5_SW), F32),
            pltpu.VMEM((rows, 2 * S5_SW), F32),
        ],
        compiler_params=_cparams(("parallel",)),
        name="s5",
    )(z, w, f, e, lam, d, h0re, h0im)


def _s5_operators(a_re, a_im, log_dt, b_re, b_im, c_re, c_im):
    dt = jnp.exp(log_dt)[:, None]
    mag = jnp.exp(a_re * dt)
    abar_re = mag * jnp.cos(a_im * dt)
    abar_im = mag * jnp.sin(a_im * dt)
    nr, ni = abar_re - 1.0, abar_im
    den = a_re * a_re + a_im * a_im
    q_re = (nr * a_re + ni * a_im) / den
    q_im = (ni * a_re - nr * a_im) / den
    bb_re = q_re[:, :, None] * b_re - q_im[:, :, None] * b_im
    bb_im = q_re[:, :, None] * b_im + q_im[:, :, None] * b_re
    taus = jnp.arange(S5_T + 1, dtype=F32)[:, None, None]
    pmag = jnp.exp(taus * (a_re * dt)[None])
    pw_re = pmag * jnp.cos(taus * (a_im * dt)[None])
    pw_im = pmag * jnp.sin(taus * (a_im * dt)[None])
    n = S5_T * LANES

    lags = (S5_T - 1) - jnp.arange(S5_T, dtype=F32)[:, None, None]
    lmag = jnp.exp(lags * (a_re * dt)[None])
    lr = (lmag * jnp.cos(lags * (a_im * dt)[None]))[:, :, :, None]
    li = (lmag * jnp.sin(lags * (a_im * dt)[None]))[:, :, :, None]
    fb = jnp.stack([lr * bb_re[None] - li * bb_im[None], lr * bb_im[None] + li * bb_re[None]])
    fc = fb.reshape(2, S5_T, S5_NBLK, S5_GPB, S5_STATE, S5_GROUP).transpose(2, 1, 3, 5, 0, 4)
    fc = fc.reshape(S5_NBLK, n, 2 * S5_STATE)

    cp_re = jnp.swapaxes(c_re, -1, -2)[None]
    cp_im = jnp.swapaxes(c_im, -1, -2)[None]
    tr = pw_re[1:][:, :, :, None]
    ti = pw_im[1:][:, :, :, None]
    ce = jnp.stack([cp_re * tr - cp_im * ti, -(cp_re * ti + cp_im * tr)])
    ec = ce.reshape(2, S5_T, S5_NBLK, S5_GPB, S5_STATE, S5_GROUP).transpose(2, 0, 3, 4, 1, 5)
    ec = ec.reshape(S5_NBLK, 2 * S5_SW, LANES)

    cc = jnp.stack([cp_re[0], -cp_im[0]]).reshape(2, S5_NBLK, S5_GPB, S5_STATE, S5_GROUP)
    cc = cc.transpose(1, 0, 2, 3, 4).reshape(S5_NBLK, 2 * S5_SW, S5_GROUP)

    lam = jnp.stack([pw_re[S5_T].reshape(S5_NBLK, S5_SW), pw_im[S5_T].reshape(S5_NBLK, S5_SW)], axis=1)
    return fc, ec, cc, lam


def _mlstm_kernel(qk_ref, v_ref, o_ref, if_ref, cw_ref, cb_ref, bif_ref, ng_ref,
                  buf0_ref, c0_ref, n0_ref, m0_ref,
                  y_ref, bufo_ref, co_ref, no_ref, mo_ref, xp_ref, *, nb, lc):
    @pl.when(pl.program_id(1) == 0)
    def _():
        bufo_ref[...] = buf0_ref[...]
        co_ref[...] = c0_ref[...]
        no_ref[...] = n0_ref[...]
        mo_ref[...] = m0_ref[...]

    lp = max(lc, LANES)
    row = lax.broadcasted_iota(jnp.int32, (lc, lc), 0)
    col = lax.broadcasted_iota(jnp.int32, (lc, lc), 1)
    causal = row >= col
    tril = jnp.where(causal, 1.0, 0.0).astype(BF16)
    lane = lax.broadcasted_iota(jnp.int32, (lc, LANES), 1)
    is_i = lane < HEADS
    is_f = jnp.logical_and(lane >= HEADS, lane < 2 * HEADS)

    for s in range(nb):
        r0 = s * lc
        xp_ref[5:8, :] = bufo_ref[s]
        xp_ref[8:8 + lc, :] = qk_ref[r0:r0 + lc, :]
        conv = cb_ref[...] + sum(xp_ref[5 + j:5 + j + lc, :] * cw_ref[j:j + 1, :] for j in range(CONV_W))
        bufo_ref[s] = xp_ref[5 + lc:8 + lc, :]
        qk = conv * jax.nn.sigmoid(conv)
        q = qk[:, :QK_WIDTH // 2] * (DQK ** -0.5)
        k = qk[:, QK_WIDTH // 2:]

        gate = if_ref[r0:r0 + lc, :] + bif_ref[...]
        lf = jnp.where(is_f, jax.nn.log_sigmoid(gate), 0.0)
        parts = _split3(lf)
        bcum = _dot(tril, parts[0]) + _dot(tril, parts[1]) + _dot(tril, parts[2])
        pc = jnp.where(is_i, gate, bcum)
        if lc < lp:
            pc_t = jnp.concatenate([pc, jnp.zeros((lp - lc, LANES), F32)], axis=0).T[:, :lc]
        else:
            pc_t = pc.T

        for h in range(HEADS):
            i_col = pc[:, h:h + 1]
            b_col = pc[:, HEADS + h:HEADS + h + 1]
            i_row = pc_t[h:h + 1, :]
            b_row = pc_t[HEADS + h:HEADS + h + 1, :]
            m0 = mo_ref[s, :, h:h + 1]
            logw = jnp.where(causal, b_col - b_row + i_row, -jnp.inf)
            g = b_col + m0
            m = jnp.maximum(g, jnp.max(logw, axis=-1, keepdims=True))
            w = jnp.exp(logw - m)
            inter = jnp.exp(g - m)
            qh = q[:, h * DQK:(h + 1) * DQK]
            kh = k[:, h * DQK:(h + 1) * DQK]
            vh = v_ref[r0:r0 + lc, h * DV:(h + 1) * DV]
            qb = qh.astype(BF16)
            vb = vh.astype(BF16)
            c_prev = co_ref[s, h]
            n_prev = no_ref[s, h:h + 1, :]
            sc = _dot_nt(qb, kh.astype(BF16)) * w
            num = _dot(sc.astype(BF16), vb) + inter * _dot_nt(qb, c_prev.astype(BF16))
            den = jnp.sum(sc, axis=-1, keepdims=True) + inter * jnp.sum(qh * n_prev, axis=-1, keepdims=True)
            hh = num / jnp.maximum(jnp.abs(den), jnp.exp(-m))
            hh = hh * lax.rsqrt(jnp.mean(hh * hh, axis=-1, keepdims=True) + EPS)
            og = jax.nn.sigmoid(o_ref[r0:r0 + lc, h * DV:(h + 1) * DV])
            y_ref[r0:r0 + lc, h * DV:(h + 1) * DV] = hh * ng_ref[:, h * DV:(h + 1) * DV] * og

            b_last = b_col[lc - 1:lc, :]
            m_new = m[lc - 1:lc, :]
            decay = jnp.exp(b_last + m0 - m_new)
            kw = kh * jnp.exp(b_last - b_col + i_col - m_new)
            co_ref[s, h] = decay * c_prev + _dot_tn(vb, kw.astype(BF16))
            no_ref[s, h:h + 1, :] = decay * n_prev + jnp.sum(kw, axis=0, keepdims=True)
            mo_ref[s, :, h:h + 1] = m_new


def _mlstm(z, zif, cw, cb, bif, ng, buf0, c0, n0, m0, *, nseq, seqlen, nb, lc):
    tok = nseq * seqlen
    nchunk = seqlen // lc
    rows = nb * lc
    tok_map = lambda col: (lambda i, c: (i * nchunk + c, col))
    const2 = lambda i, c: (0, 0)
    st3 = lambda i, c: (i, 0, 0)
    st4 = lambda i, c: (i, 0, 0, 0)
    kern = functools.partial(_mlstm_kernel, nb=nb, lc=lc)
    return pl.pallas_call(
        kern,
        grid=(nseq // nb, nchunk),
        in_specs=[
            pl.BlockSpec((rows, QK_WIDTH), tok_map(1)),
            pl.BlockSpec((rows, V_WIDTH), tok_map(2)),
            pl.BlockSpec((rows, V_WIDTH), tok_map(3)),
            pl.BlockSpec((rows, LANES), tok_map(0)),
            pl.BlockSpec((CONV_W, QK_WIDTH), const2),
            pl.BlockSpec((1, QK_WIDTH), const2),
            pl.BlockSpec((1, LANES), const2),
            pl.BlockSpec((1, V_WIDTH), const2),
            pl.BlockSpec((nb, CONV_W - 1, QK_WIDTH), st3),
            pl.BlockSpec((nb, HEADS, DV, DQK), st4),
            pl.BlockSpec((nb, HEADS, DQK), st3),
            pl.BlockSpec((nb, 1, HEADS), st3),
        ],
        out_specs=[
            pl.BlockSpec((rows, V_WIDTH), tok_map(0)),
            pl.BlockSpec((nb, CONV_W - 1, QK_WIDTH), st3),
            pl.BlockSpec((nb, HEADS, DV, DQK), st4),
            pl.BlockSpec((nb, HEADS, DQK), st3),
            pl.BlockSpec((nb, 1, HEADS), st3),
        ],
        out_shape=[
            jax.ShapeDtypeStruct((tok, V_WIDTH), F32),
            jax.ShapeDtypeStruct((nseq, CONV_W - 1, QK_WIDTH), F32),
            jax.ShapeDtypeStruct((nseq, HEADS, DV, DQK), F32),
            jax.ShapeDtypeStruct((nseq, HEADS, DQK), F32),
            jax.ShapeDtypeStruct((nseq, 1, HEADS), F32),
        ],
        scratch_shapes=[pltpu.VMEM((lc + 8, QK_WIDTH), F32)],
        compiler_params=_cparams(("parallel", "arbitrary")),
        name="mlstm",
    )(z, z, z, zif, cw, cb, bif, ng, buf0, c0, n0, m0)


def _merge_kernel(h_ref, ys_ref, ym_ref, g1_ref, g2_ref, wglu_ref, wbs_ref, wbm_ref, wo_ref, o_ref):
    ys = ys_ref[...]
    glu = ys * jax.nn.sigmoid(_dot(ys.astype(BF16), wglu_ref[...]))
    a = _dot(glu.astype(BF16), wbs_ref[...])
    b = _dot(ym_ref[...].astype(BF16), wbm_ref[...])
    merged = jax.nn.sigmoid(g1_ref[...]) * a + jax.nn.sigmoid(g2_ref[...]) * b
    o_ref[...] = h_ref[...] + _dot(merged.astype(BF16), wo_ref[...])


def _merge(h, ys, ym, z, wglu, wbs, wbm, wo, *, tm):
    n = h.shape[0]
    row = lambda i: (i, 0)
    const = lambda i: (0, 0)
    resident = lambda shape: pl.BlockSpec(shape, const, pipeline_mode=pl.Buffered(1))
    return pl.pallas_call(
        _merge_kernel,
        grid=(n // tm,),
        in_specs=[
            pl.BlockSpec((tm, D_MODEL), row),
            pl.BlockSpec((tm, S5_WIDTH), row),
            pl.BlockSpec((tm, V_WIDTH), row),
            pl.BlockSpec((tm, D_MODEL), lambda i: (i, 2)),
            pl.BlockSpec((tm, D_MODEL), lambda i: (i, 3)),
            resident((S5_WIDTH, S5_WIDTH)),
            resident((S5_WIDTH, D_MODEL)),
            resident((V_WIDTH, D_MODEL)),
            resident((D_MODEL, D_MODEL)),
        ],
        out_specs=pl.BlockSpec((tm, D_MODEL), row),
        out_shape=jax.ShapeDtypeStruct((n, D_MODEL), F32),
        compiler_params=_cparams(("parallel",)),
        name="merge",
    )(h, ys, ym, z, z, wglu, wbs, wbm, wo)


def kernel(x_prompt, x_sample, state_s5_re, state_s5_im, state_mlstm_C, state_mlstm_n, state_mlstm_m,
           state_mlstm_conv, meta_tokens, ffn1_norm, ffn1_w_gate, ffn1_w_up, ffn1_w_down, mix_norm, w_in,
           s5_A_re, s5_A_im, s5_log_dt, s5_B_re, s5_B_im, s5_C_re, s5_C_im, s5_D, s5_w_glu,
           mlstm_conv_w, mlstm_conv_b, mlstm_b_i, mlstm_b_f, mlstm_norm, w_branch_s5, w_branch_mlstm,
           w_out, ffn2_norm, ffn2_w_gate, ffn2_w_up, ffn2_w_down, final_norm):
    nbatch, seq, _ = x_prompt.shape
    nsamp, sseq, _ = x_sample.shape
    l = 0

    w1g, w1u, w1d = (w[l].astype(BF16) for w in (ffn1_w_gate, ffn1_w_up, ffn1_w_down))
    w2g, w2u, w2d = (w[l].astype(BF16) for w in (ffn2_w_gate, ffn2_w_up, ffn2_w_down))
    o_if = S5_WIDTH + QK_WIDTH + 2 * V_WIDTH
    o_gate = o_if + 2 * HEADS
    win = w_in[l]
    w_wide = win[:, :o_if].astype(BF16)
    w_gates = win[:, o_gate:].astype(BF16)
    w_if =jnp.pad(win[:, o_if:o_gate], ((0, 0), (0, LANES - 2 * HEADS))).astype(BF16)
    wglu, wbs, wbm, wo = (w[l].astype(BF16) for w in (s5_w_glu, w_branch_s5, w_branch_mlstm, w_out))
    g1 = ffn1_norm[l][None]
    gm = mix_norm[l][None]
    g2 = ffn2_norm[l][None]
    gf = final_norm[None]
    bif = jnp.pad(jnp.concatenate([mlstm_b_i[l], mlstm_b_f[l]]), (0, LANES - 2 * HEADS))[None]
    cw = mlstm_conv_w[l]
    cb = mlstm_conv_b[l][None]
    ng = mlstm_norm[l][None]
    d_skip = s5_D[l][None]

    fc, ec, cc, lam = _s5_operators(s5_A_re[l], s5_A_im[l], s5_log_dt[l], s5_B_re[l], s5_B_im[l],
                                    s5_C_re[l], s5_C_im[l])
    w_toe, f_bf, e_bf = _s5_prep(fc, ec, cc)

    def front(x, tm, tm_in):
        h1 = _ffn(x, g1, w1g, w1u, w1d, tm=tm)
        z, zif = _win(h1, gm, w_wide, w_gates, w_if, tm=tm_in)
        return h1, z, zif

    def mixers(z, zif, s5_state, ml_state, *, nseq, seqlen, nb, lc):
        ys, sre, sim = _s5(z, w_toe, f_bf, e_bf, lam, d_skip, s5_state[0], s5_state[1], nseq=nseq, seqlen=seqlen)
        buf0, c0, n0, m0 = ml_state
        ym, buf, c, n, m = _mlstm(z, zif, cw, cb, bif, ng, buf0, c0, n0, m0.reshape(nseq, 1, HEADS),
                                  nseq=nseq, seqlen=seqlen, nb=nb, lc=lc)
        return ys, ym, (sre, sim), (buf, c, n, m.reshape(nseq, HEADS))

    def back(h1, ys, ym, z, tm):
        h2 = _merge(h1, ys, ym, z, wglu, wbs, wbm, wo, tm=tm // 2)
        return _ffn(h2, g2, w2g, w2u, w2d, gf, tm=tm)

    x_m = jnp.tile(meta_tokens, (nbatch, 1))
    _, z_m, zif_m = front(x_m, nbatch * N_META, nbatch * N_META)
    zeros = lambda *s: jnp.zeros((nbatch,) + s, F32)
    _, _, s5_m, ml_m = mixers(
        z_m, zif_m, (zeros(S5_GROUPS * S5_STATE), zeros(S5_GROUPS * S5_STATE)),
        (zeros(CONV_W - 1, QK_WIDTH), zeros(HEADS, DV, DQK), zeros(HEADS, DQK), zeros(HEADS)),
        nseq=nbatch, seqlen=N_META, nb=1, lc=N_META)

    h1_p, z_p, zif_p = front(x_prompt.reshape(nbatch * seq, D_MODEL), 512, 1024)
    ys_p, ym_p, s5_p, ml_p = mixers(z_p, zif_p, s5_m, ml_m, nseq=nbatch, seqlen=seq, nb=1, lc=256)
    y_p = back(h1_p, ys_p, ym_p, z_p, 512)

    h1_s, z_s, zif_s = front(x_sample.reshape(nsamp * sseq, D_MODEL), 512, 1024)
    ys_s, ym_s, s5_s, ml_s = mixers(
        z_s, zif_s,
        (state_s5_re[l].reshape(nsamp, -1), state_s5_im[l].reshape(nsamp, -1)),
        (state_mlstm_conv[l], state_mlstm_C[l], state_mlstm_n[l], state_mlstm_m[l]),
        nseq=nsamp, seqlen=sseq, nb=8, lc=sseq)
    y_s = back(h1_s, ys_s, ym_s, z_s, 512)

    def pack(n, s5_st, ml_st):
        buf, c, nn, m = ml_st
        return (s5_st[0].reshape(1, n, S5_GROUPS, S5_STATE), s5_st[1].reshape(1, n, S5_GROUPS, S5_STATE),
                c[None], nn[None], m[None], buf[None])

    return ((y_p.reshape(nbatch, seq, D_MODEL), y_s.reshape(nsamp, sseq, D_MODEL))
            + pack(nbatch, s5_p, ml_p) + pack(nsamp, s5_s, ml_s))
```

```python
import functools

import jax
import jax.numpy as jnp
from jax import lax
from jax.experimental import pallas as pl
from jax.experimental.pallas import tpu as pltpu

F32 = jnp.float32
BF16 = jnp.bfloat16

D_MODEL = 2048
D_FF = 5632
N_META = 16
S5_WIDTH = 1024
S5_GROUP = 16
S5_GROUPS = 64
S5_STATE = 64
HEADS = 4
DQK = 128
DV = 256
QK_WIDTH = 1024
V_WIDTH = 1024
CONV_W = 4
EPS = 1e-6

LANES = 128
S5_T = 8
S5_GPB = LANES // S5_GROUP
S5_NBLK = S5_WIDTH // LANES
S5_SW = S5_GPB * S5_STATE
VMEM_LIMIT = 58 * 1024 * 1024
FFN_TM = 1024
MERGE_TM = 256


def _cparams(sem):
    return pltpu.CompilerParams(dimension_semantics=sem, vmem_limit_bytes=VMEM_LIMIT)


def _rmsnorm(x, g):
    ms = jnp.mean(x * x, axis=-1, keepdims=True)
    return (x * lax.rsqrt(ms + EPS)) * g


def _dot(a, b):
    return jnp.dot(a, b, preferred_element_type=F32)


def _dot_nt(a, b):
    return lax.dot_general(a, b, (((1,), (1,)), ((), ())), preferred_element_type=F32)


def _dot_tn(a, b):
    return lax.dot_general(a, b, (((0,), (0,)), ((), ())), preferred_element_type=F32)


def _split3(x):
    hi = x.astype(BF16)
    r = x - hi.astype(F32)
    mid = r.astype(BF16)
    lo = (r - mid.astype(F32)).astype(BF16)
    return hi, mid, lo


def _ffn_kernel(x_ref, g_ref, wg_ref, wu_ref, wd_ref, *rest, final_norm):
    if final_norm:
        fg_ref, o_ref, xn_ref = rest
    else:
        o_ref, xn_ref = rest
    j = pl.program_id(1)

    @pl.when(j == 0)
    def _():
        xn_ref[...] = _rmsnorm(x_ref[...], g_ref[...]).astype(BF16)
        o_ref[...] = jnp.zeros(o_ref.shape, F32)

    xn = xn_ref[...]
    gt = _dot(xn, wg_ref[...].astype(BF16))
    up = _dot(xn, wu_ref[...].astype(BF16))
    act = (gt * jax.nn.sigmoid(gt) * up).astype(BF16)
    o_ref[...] += _dot(act, wd_ref[...].astype(BF16))

    @pl.when(j == pl.num_programs(1) - 1)
    def _():
        h = x_ref[...] + 0.5 * o_ref[...]
        if final_norm:
            h = _rmsnorm(h, fg_ref[...])
        o_ref[...] = h


def _ffn(x, g, wg, wu, wd, final_g=None, *, n, tm, tf=256):
    in_specs = [
        pl.BlockSpec((tm, D_MODEL), lambda i, j: (i, 0), pipeline_mode=pl.Buffered(1)),
        pl.BlockSpec((1, D_MODEL), lambda i, j: (0, 0)),
        pl.BlockSpec((D_MODEL, tf), lambda i, j: (0, j)),
        pl.BlockSpec((D_MODEL, tf), lambda i, j: (0, j)),
        pl.BlockSpec((tf, D_MODEL), lambda i, j: (j, 0)),
    ]
    args = [x, g, wg, wu, wd]
    if final_g is not None:
        in_specs.append(pl.BlockSpec((1, D_MODEL), lambda i, j: (0, 0)))
        args.append(final_g)
    return pl.pallas_call(
        functools.partial(_ffn_kernel, final_norm=final_g is not None),
        grid=(n // tm, D_FF // tf),
        in_specs=in_specs,
        out_specs=pl.BlockSpec((tm, D_MODEL), lambda i, j: (i, 0)),
        out_shape=jax.ShapeDtypeStruct((n, D_MODEL), F32),
        scratch_shapes=[pltpu.VMEM((tm, D_MODEL), BF16)],
        compiler_params=_cparams(("parallel", "arbitrary")),
        name="ffn",
    )(*args)


def _win_kernel(h_ref, g_ref, wa_ref, wb_ref, wif_ref, z_ref, zif_ref, un_ref, *, na):
    j = pl.program_id(1)

    @pl.when(j == 0)
    def _():
        un = _rmsnorm(h_ref[...], g_ref[...]).astype(BF16)
        un_ref[...] = un
        zif_ref[...] = _dot(un, wif_ref[...])

    @pl.when(j < na)
    def _():
        z_ref[...] = _dot(un_ref[...], wa_ref[...])

    @pl.when(j >= na)
    def _():
        z_ref[...] = _dot(un_ref[...], wb_ref[...])


def _win(h, g, wa, wb, wif, *, tm, tn=512):
    n = h.shape[0]
    na = wa.shape[1] // tn
    nb = wb.shape[1] // tn
    return pl.pallas_call(
        functools.partial(_win_kernel, na=na),
        grid=(n // tm, na + nb),
        in_specs=[
            pl.BlockSpec((tm, D_MODEL), lambda i, j: (i, 0)),
            pl.BlockSpec((1, D_MODEL), lambda i, j: (0, 0)),
            pl.BlockSpec((D_MODEL, tn), lambda i, j: (0, jnp.minimum(j, na - 1))),
            pl.BlockSpec((D_MODEL, tn), lambda i, j: (0, jnp.maximum(j - na, 0))),
            pl.BlockSpec((D_MODEL, LANES), lambda i, j: (0, 0)),
        ],
        out_specs=[
            pl.BlockSpec((tm, tn), lambda i, j: (i, j)),
            pl.BlockSpec((tm, LANES), lambda i, j: (i, 0)),
        ],
        out_shape=[
            jax.ShapeDtypeStruct((n, (na + nb) * tn), F32),
            jax.ShapeDtypeStruct((n, LANES), F32),
        ],
        scratch_shapes=[pltpu.VMEM((tm, D_MODEL), BF16)],
        compiler_params=_cparams(("parallel", "arbitrary")),
        name="w_in",
    )(h, g, wa, wb, wif)


def _s5_prep_kernel(fc_ref, ec_ref, cc_ref, w_ref, f_ref, e_ref):
    n = S5_T * LANES
    sw2 = 2 * S5_SW
    i32 = jnp.int32
    lg_h = S5_GROUP.bit_length() - 1
    lg_p = S5_STATE.bit_length() - 1
    lg_sw = S5_SW.bit_length() - 1

    def tile_mat(k, c, src_col):
        kk = lax.broadcasted_iota(i32, (k, c), 0)
        cc = lax.broadcasted_iota(i32, (k, c), 1)
        return jnp.where(kk == src_col(cc), 1.0, 0.0).astype(BF16)

    def group_mask(r, c, row_group, col_group):
        rr = lax.broadcasted_iota(i32, (r, c), 0)
        cc = lax.broadcasted_iota(i32, (r, c), 1)
        return row_group(rr) == col_group(cc)

    gmask = S5_GPB - 1
    rf = tile_mat(2 * S5_STATE, sw2, lambda c: ((c >> lg_sw) << lg_p) + (c & (S5_STATE - 1)))
    mf = group_mask(n, sw2, lambda r: (r >> lg_h) & gmask, lambda c: (c >> lg_p) & gmask)
    f_hi, f_mid, _ = _split3(fc_ref[0])
    x_hi = jnp.where(mf, _dot(f_hi, rf), 0.0).astype(BF16)
    x_mid = jnp.where(mf, _dot(f_mid, rf), 0.0).astype(BF16)
    f_ref[0] = x_hi
    lg_l = LANES.bit_length() - 1
    re = tile_mat(LANES, n, lambda c: ((c >> lg_l) << lg_h) + (c & (S5_GROUP - 1)))
    me = group_mask(sw2, n, lambda r: (r >> lg_p) & gmask, lambda c: (c >> lg_h) & gmask)
    e_ref[0] = jnp.where(me, _dot(ec_ref[0].astype(BF16), re), 0.0).astype(BF16)
    rc = tile_mat(S5_GROUP, LANES, lambda c: c & (S5_GROUP - 1))
    mc = group_mask(sw2, LANES, lambda r: (r >> lg_p) & gmask, lambda c: c >> lg_h)
    c_hi, c_mid, _ = _split3(cc_ref[0])
    ct_hi = jnp.where(mc, _dot(c_hi, rc), 0.0).astype(BF16)
    ct_mid = jnp.where(mc, _dot(c_mid, rc), 0.0).astype(BF16)
    w_ref[...] = jnp.zeros(w_ref.shape, w_ref.dtype)
    for lag in range(S5_T):
        r0 = (S5_T - 1 - lag) * LANES
        xh = x_hi[r0:r0 + LANES, :]
        xm = x_mid[r0:r0 + LANES, :]
        k = (_dot(xh, ct_hi) + _dot(xm, ct_hi) + _dot(xh, ct_mid)).astype(BF16)
        for s in range(S5_T - lag):
            t = s + lag
            w_ref[0, s * LANES:(s + 1) * LANES, t * LANES:(t + 1) * LANES] = k


def _s5_prep(fc, ec, cc):
    n = S5_T * LANES
    blk = lambda r, c: pl.BlockSpec((1, r, c), lambda j: (j, 0, 0))
    return pl.pallas_call(
        _s5_prep_kernel,
        grid=(S5_NBLK,),
        in_specs=[blk(n, 2 * S5_STATE), blk(2 * S5_SW, LANES), blk(2 * S5_SW, S5_GROUP)],
        out_specs=[blk(n, n), blk(n, 2 * S5_SW), blk(2 * S5_SW, n)],
        out_shape=[
            jax.ShapeDtypeStruct((S5_NBLK, n, n), BF16),
            jax.ShapeDtypeStruct((S5_NBLK, n, 2 * S5_SW), BF16),
            jax.ShapeDtypeStruct((S5_NBLK, 2 * S5_SW, n), BF16),
        ],
        compiler_params=_cparams(("parallel",)),
        name="s5_prep",
    )(fc, ec, cc)


def _s5_kernel(x_ref, w_ref, f_ref, e_ref, lam_ref, d_ref, h0re_ref, h0im_ref,
               y_ref, ore_ref, oim_ref, yi_ref, s_ref, hin_ref, *, nseq, nblk, rt):
    rows = nseq * nblk
    ar = lam_ref[0, 0:1, :]
    ai = lam_ref[0, 1:2, :]

    def load_u(r0):
        return jnp.concatenate(
            [x_ref[pl.ds(r0 * S5_T + s, rt, stride=S5_T), :] for s in range(S5_T)], axis=1)

    for r0 in range(0, rows, rt):
        ub = load_u(r0).astype(BF16)
        yi_ref[r0:r0 + rt, :] = _dot(ub, w_ref[0])
        s_ref[r0:r0 + rt, :] = _dot(ub, f_ref[0])

    if nblk == 1:
        hre = h0re_ref[...]
        him = h0im_ref[...]
        hin_ref[:, :S5_SW] = hre
        hin_ref[:, S5_SW:] = him
        ore_ref[...] = ar * hre - ai * him + s_ref[:, :S5_SW]
        oim_ref[...] = ar * him + ai * hre + s_ref[:, S5_SW:]
    else:
        def body(c, carry):
            new = []
            for b in range(nseq):
                hre, him = carry[2 * b], carry[2 * b + 1]
                row = b * nblk + c
                hin_ref[pl.ds(row, 1), :S5_SW] = hre
                hin_ref[pl.ds(row, 1), S5_SW:] = him
                sre = s_ref[pl.ds(row, 1), :S5_SW]
                sim = s_ref[pl.ds(row, 1), S5_SW:]
                new.append(ar * hre - ai * him + sre)
                new.append(ar * him + ai * hre + sim)
            return tuple(new)

        init = []
        for b in range(nseq):
            init.append(h0re_ref[b:b + 1, :])
            init.append(h0im_ref[b:b + 1, :])
        fin = lax.fori_loop(0, nblk, body, tuple(init))
        for b in range(nseq):
            ore_ref[b:b + 1, :] = fin[2 * b]
            oim_ref[b:b + 1, :] = fin[2 * b + 1]

    dt = jnp.concatenate([d_ref[...]] * S5_T, axis=1)
    for r0 in range(0, rows, rt):
        yo = _dot(hin_ref[r0:r0 + rt, :].astype(BF16), e_ref[0])
        y = jax.nn.gelu(yi_ref[r0:r0 + rt, :] + yo + load_u(r0) * dt)
        for t in range(S5_T):
            y_ref[pl.ds(r0 * S5_T + t, rt, stride=S5_T), :] = y[:, t * LANES:(t + 1) * LANES]


def _s5(z, w, f, e, lam, d, h0re, h0im, *, nseq, seqlen):
    nblk = seqlen // S5_T
    rows = nseq * nblk
    rt = min(rows, 256)
    n = S5_T * LANES
    tok = nseq * seqlen
    kern = functools.partial(_s5_kernel, nseq=nseq, nblk=nblk, rt=rt)
    return pl.pallas_call(
        kern,
        grid=(S5_NBLK,),
        in_specs=[
            pl.BlockSpec((tok, LANES), lambda j: (0, j)),
            pl.BlockSpec((1, n, n), lambda j: (j, 0, 0)),
            pl.BlockSpec((1, n, 2 * S5_SW), lambda j: (j, 0, 0)),
            pl.BlockSpec((1, 2 * S5_SW, n), lambda j: (j, 0, 0)),
            pl.BlockSpec((1, 2, S5_SW), lambda j: (j, 0, 0)),
            pl.BlockSpec((1, LANES), lambda j: (0, j)),
            pl.BlockSpec((nseq, S5_SW), lambda j: (0, j)),
            pl.BlockSpec((nseq, S5_SW), lambda j: (0, j)),
        ],
        out_specs=[
            pl.BlockSpec((tok, LANES), lambda j: (0, j)),
            pl.BlockSpec((nseq, S5_SW), lambda j: (0, j)),
            pl.BlockSpec((nseq, S5_SW), lambda j: (0, j)),
        ],
        out_shape=[
            jax.ShapeDtypeStruct((tok, S5_WIDTH), F32),
            jax.ShapeDtypeStruct((nseq, S5_GROUPS * S5_STATE), F32),
            jax.ShapeDtypeStruct((nseq, S5_GROUPS * S5_STATE), F32),
        ],
        scratch_shapes=[
            pltpu.VMEM((rows, n), F32),
            pltpu.VMEM((rows, 2 * S5_SW), F32),
            pltpu.VMEM((rows, 2 * S5_SW), F32),
        ],
        compiler_params=_cparams(("parallel",)),
        name="s5",
    )(z, w, f, e, lam, d, h0re, h0im)


def _s5_operators(a_re, a_im, log_dt, b_re, b_im, c_re, c_im):
    dt = jnp.exp(log_dt)[:, None]
    mag = jnp.exp(a_re * dt)
    abar_re = mag * jnp.cos(a_im * dt)
    abar_im = mag * jnp.sin(a_im * dt)
    nr, ni = abar_re - 1.0, abar_im
    den = a_re * a_re + a_im * a_im
    q_re = (nr * a_re + ni * a_im) / den
    q_im = (ni * a_re - nr * a_im) / den
    bb_re = q_re[:, :, None] * b_re - q_im[:, :, None] * b_im
    bb_im = q_re[:, :, None] * b_im + q_im[:, :, None] * b_re
    taus = jnp.arange(S5_T + 1, dtype=F32)[:, None, None]
    pmag = jnp.exp(taus * (a_re * dt)[None])
    pw_re = pmag * jnp.cos(taus * (a_im * dt)[None])
    pw_im = pmag * jnp.sin(taus * (a_im * dt)[None])
    n = S5_T * LANES

    lags = (S5_T - 1) - jnp.arange(S5_T, dtype=F32)[:, None, None]
    lmag = jnp.exp(lags * (a_re * dt)[None])
    lr = (lmag * jnp.cos(lags * (a_im * dt)[None]))[:, :, :, None]
    li = (lmag * jnp.sin(lags * (a_im * dt)[None]))[:, :, :, None]
    fb = jnp.stack([lr * bb_re[None] - li * bb_im[None], lr * bb_im[None] + li * bb_re[None]])
    fc = fb.reshape(2, S5_T, S5_NBLK, S5_GPB, S5_STATE, S5_GROUP).transpose(2, 1, 3, 5, 0, 4)
    fc = fc.reshape(S5_NBLK, n, 2 * S5_STATE)

    cp_re = jnp.swapaxes(c_re, -1, -2)[None]
    cp_im = jnp.swapaxes(c_im, -1, -2)[None]
    tr = pw_re[1:][:, :, :, None]
    ti = pw_im[1:][:, :, :, None]
    ce = jnp.stack([cp_re * tr - cp_im * ti, -(cp_re * ti + cp_im * tr)])
    ec = ce.reshape(2, S5_T, S5_NBLK, S5_GPB, S5_STATE, S5_GROUP).transpose(2, 0, 3, 4, 1, 5)
    ec = ec.reshape(S5_NBLK, 2 * S5_SW, LANES)

    cc = jnp.stack([cp_re[0], -cp_im[0]]).reshape(2, S5_NBLK, S5_GPB, S5_STATE, S5_GROUP)
    cc = cc.transpose(1, 0, 2, 3, 4).reshape(S5_NBLK, 2 * S5_SW, S5_GROUP)

    lam = jnp.stack([pw_re[S5_T].reshape(S5_NBLK, S5_SW), pw_im[S5_T].reshape(S5_NBLK, S5_SW)], axis=1)
    return fc, ec, cc, lam


def _mlstm_kernel(qk_ref, v_ref, o_ref, if_ref, cw_ref, cb_ref, bif_ref, ng_ref,
                  buf0_ref, c0_ref, n0_ref, m0_ref,
                  y_ref, bufo_ref, co_ref, no_ref, mo_ref, xp_ref, *, nb, lc, carry):
    if carry:
        @pl.when(pl.program_id(1) == 0)
        def _():
            bufo_ref[...] = buf0_ref[...]
            co_ref[...] = c0_ref[...]
            no_ref[...] = n0_ref[...]
            mo_ref[...] = m0_ref[...]
        bufs_ref, cs_ref, ns_ref, ms_ref = bufo_ref, co_ref, no_ref, mo_ref
    else:
        bufs_ref, cs_ref, ns_ref, ms_ref = buf0_ref, c0_ref, n0_ref, m0_ref

    lp = max(lc, LANES)
    head_row = lax.broadcasted_iota(jnp.int32, (HEADS, DQK), 0)
    head_lane = lax.broadcasted_iota(jnp.int32, (1, HEADS), 1)
    row = lax.broadcasted_iota(jnp.int32, (lc, lc), 0)
    col = lax.broadcasted_iota(jnp.int32, (lc, lc), 1)
    causal = row >= col
    tril = jnp.where(causal, 1.0, 0.0).astype(BF16)
    lane = lax.broadcasted_iota(jnp.int32, (lc, LANES), 1)
    is_i = lane < HEADS
    is_f = jnp.logical_and(lane >= HEADS, lane < 2 * HEADS)

    for s in range(nb):
        r0 = s * lc
        xp_ref[s, 5:8, :] = bufs_ref[s]
        xp_ref[s, 8:8 + lc, :] = qk_ref[r0:r0 + lc, :]
        conv = cb_ref[...] + sum(xp_ref[s, 5 + j:5 + j + lc, :] * cw_ref[j:j + 1, :] for j in range(CONV_W))
        bufo_ref[s] = xp_ref[s, 5 + lc:8 + lc, :]
        n_all = ns_ref[s]
        m_all = ms_ref[s]
        n_new_all = jnp.zeros((HEADS, DQK), F32)
        m_new_all = jnp.zeros((1, HEADS), F32)
        qk = conv * jax.nn.sigmoid(conv)
        q = qk[:, :QK_WIDTH // 2] * (DQK ** -0.5)
        k = qk[:, QK_WIDTH // 2:]

        gate = if_ref[r0:r0 + lc, :] + bif_ref[...]
        lf = jnp.where(is_f, jax.nn.log_sigmoid(gate), 0.0)
        parts = _split3(lf)
        bcum = _dot(tril, parts[0]) + _dot(tril, parts[1]) + _dot(tril, parts[2])
        pc = jnp.where(is_i, gate, bcum)
        if lc < lp:
            pc_t = jnp.concatenate([pc, jnp.zeros((lp - lc, LANES), F32)], axis=0).T[:, :lc]
        else:
            pc_t = pc.T

        for h in range(HEADS):
            i_col = pc[:, h:h + 1]
            b_col = pc[:, HEADS + h:HEADS + h + 1]
            i_row = pc_t[h:h + 1, :]
            b_row = pc_t[HEADS + h:HEADS + h + 1, :]
            m0 = m_all[:, h:h + 1]
            logw = jnp.where(causal, b_col - b_row + i_row, -jnp.inf)
            g = b_col + m0
            m = jnp.maximum(g, jnp.max(logw, axis=-1, keepdims=True))
            w = jnp.exp(logw - m)
            inter = jnp.exp(g - m)
            qh = q[:, h * DQK:(h + 1) * DQK]
            kh = k[:, h * DQK:(h + 1) * DQK]
            vh = v_ref[r0:r0 + lc, h * DV:(h + 1) * DV]
            qb = qh.astype(BF16)
            vb = vh.astype(BF16)
            c_prev = cs_ref[s, h]
            n_prev = n_all[h:h + 1, :]
            sc = _dot_nt(qb, kh.astype(BF16)) * w
            num = _dot(sc.astype(BF16), vb) + inter * _dot_nt(qb, c_prev.astype(BF16))
            den = jnp.sum(sc, axis=-1, keepdims=True) + inter * jnp.sum(qh * n_prev, axis=-1, keepdims=True)
            hh = num / jnp.maximum(jnp.abs(den), jnp.exp(-m))
            hh = hh * lax.rsqrt(jnp.mean(hh * hh, axis=-1, keepdims=True) + EPS)
            og = jax.nn.sigmoid(o_ref[r0:r0 + lc, h * DV:(h + 1) * DV])
            y_ref[r0:r0 + lc, h * DV:(h + 1) * DV] = hh * ng_ref[:, h * DV:(h + 1) * DV] * og

            b_last = b_col[lc - 1:lc, :]
            m_new = m[lc - 1:lc, :]
            decay = jnp.exp(b_last + m0 - m_new)
            kw = kh * jnp.exp(b_last - b_col + i_col - m_new)
            co_ref[s, h] = decay * c_prev + _dot_tn(vb, kw.astype(BF16))
            n_new = decay * n_prev + jnp.sum(kw, axis=0, keepdims=True)
            n_new_all = jnp.where(head_row == h, n_new, n_new_all)
            m_new_all = jnp.where(head_lane == h, m_new, m_new_all)
        no_ref[s] = n_new_all
        mo_ref[s] = m_new_all


def _mlstm(z, zif, cw, cb, bif, ng, buf0, c0, n0, m0, *, nseq, seqlen, nb, lc):
    tok = nseq * seqlen
    nchunk = seqlen // lc
    rows = nb * lc
    tok_map = lambda col: (lambda i, c: (i * nchunk + c, col))
    const2 = lambda i, c: (0, 0)
    st3 = lambda i, c: (i, 0, 0)
    st4 = lambda i, c: (i, 0, 0, 0)
    kern = functools.partial(_mlstm_kernel, nb=nb, lc=lc, carry=nchunk > 1)
    return pl.pallas_call(
        kern,
        grid=(nseq // nb, nchunk),
        in_specs=[
            pl.BlockSpec((rows, QK_WIDTH), tok_map(1)),
            pl.BlockSpec((rows, V_WIDTH), tok_map(2)),
            pl.BlockSpec((rows, V_WIDTH), tok_map(3)),
            pl.BlockSpec((rows, LANES), tok_map(0)),
            pl.BlockSpec((CONV_W, QK_WIDTH), const2),
            pl.BlockSpec((1, QK_WIDTH), const2),
            pl.BlockSpec((1, LANES), const2),
            pl.BlockSpec((1, V_WIDTH), const2),
            pl.BlockSpec((nb, CONV_W - 1, QK_WIDTH), st3),
            pl.BlockSpec((nb, HEADS, DV, DQK), st4),
            pl.BlockSpec((nb, HEADS, DQK), st3),
            pl.BlockSpec((nb, 1, HEADS), st3),
        ],
        out_specs=[
            pl.BlockSpec((rows, V_WIDTH), tok_map(0)),
            pl.BlockSpec((nb, CONV_W - 1, QK_WIDTH), st3),
            pl.BlockSpec((nb, HEADS, DV, DQK), st4),
            pl.BlockSpec((nb, HEADS, DQK), st3),
            pl.BlockSpec((nb, 1, HEADS), st3),
        ],
        out_shape=[
            jax.ShapeDtypeStruct((tok, V_WIDTH), F32),
            jax.ShapeDtypeStruct((nseq, CONV_W - 1, QK_WIDTH), F32),
            jax.ShapeDtypeStruct((nseq, HEADS, DV, DQK), F32),
            jax.ShapeDtypeStruct((nseq, HEADS, DQK), F32),
            jax.ShapeDtypeStruct((nseq, 1, HEADS), F32),
        ],
        scratch_shapes=[pltpu.VMEM((nb, lc + 8, QK_WIDTH), F32)],
        compiler_params=_cparams(("parallel", "arbitrary")),
        name="mlstm",
    )(z, z, z, zif, cw, cb, bif, ng, buf0, c0, n0, m0)


def _merge_kernel(h_ref, ys_ref, ym_ref, g1_ref, g2_ref, wglu_ref, wbs_ref, wbm_ref, wo_ref, o_ref):
    ys = ys_ref[...]
    glu = ys * jax.nn.sigmoid(_dot(ys.astype(BF16), wglu_ref[...]))
    a = _dot(glu.astype(BF16), wbs_ref[...])
    b = _dot(ym_ref[...].astype(BF16), wbm_ref[...])
    merged = jax.nn.sigmoid(g1_ref[...]) * a + jax.nn.sigmoid(g2_ref[...]) * b
    o_ref[...] = h_ref[...] + _dot(merged.astype(BF16), wo_ref[...])


def _merge(h, ys, ym, z, wglu, wbs, wbm, wo, *, n, tm):
    row = lambda i: (i, 0)
    const = lambda i: (0, 0)
    resident = lambda shape: pl.BlockSpec(shape, const, pipeline_mode=pl.Buffered(1))
    return pl.pallas_call(
        _merge_kernel,
        grid=(n // tm,),
        in_specs=[
            pl.BlockSpec((tm, D_MODEL), row),
            pl.BlockSpec((tm, S5_WIDTH), row),
            pl.BlockSpec((tm, V_WIDTH), row),
            pl.BlockSpec((tm, D_MODEL), lambda i: (i, 2)),
            pl.BlockSpec((tm, D_MODEL), lambda i: (i, 3)),
            resident((S5_WIDTH, S5_WIDTH)),
            resident((S5_WIDTH, D_MODEL)),
            resident((V_WIDTH, D_MODEL)),
            resident((D_MODEL, D_MODEL)),
        ],
        out_specs=pl.BlockSpec((tm, D_MODEL), row),
        out_shape=jax.ShapeDtypeStruct((n, D_MODEL), F32),
        compiler_params=_cparams(("parallel",)),
        name="merge",
    )(h, ys, ym, z, z, wglu, wbs, wbm, wo)


def kernel(x_prompt, x_sample, state_s5_re, state_s5_im, state_mlstm_C, state_mlstm_n, state_mlstm_m,
           state_mlstm_conv, meta_tokens, ffn1_norm, ffn1_w_gate, ffn1_w_up, ffn1_w_down, mix_norm, w_in,
           s5_A_re, s5_A_im, s5_log_dt, s5_B_re, s5_B_im, s5_C_re, s5_C_im, s5_D, s5_w_glu,
           mlstm_conv_w, mlstm_conv_b, mlstm_b_i, mlstm_b_f, mlstm_norm, w_branch_s5, w_branch_mlstm,
           w_out, ffn2_norm, ffn2_w_gate, ffn2_w_up, ffn2_w_down, final_norm):
    nbatch, seq, _ = x_prompt.shape
    nsamp, sseq, _ = x_sample.shape
    l = 0

    w1g, w1u, w1d = ffn1_w_gate[l], ffn1_w_up[l], ffn1_w_down[l]
    w2g, w2u, w2d = ffn2_w_gate[l], ffn2_w_up[l], ffn2_w_down[l]
    o_if = S5_WIDTH + QK_WIDTH + 2 * V_WIDTH
    o_gate = o_if + 2 * HEADS
    win = w_in[l]
    w_wide = win[:, :o_if].astype(BF16)
    w_gates = win[:, o_gate:].astype(BF16)
    w_if = jnp.pad(win[:, o_if:o_gate], ((0, 0), (0, LANES - 2 * HEADS))).astype(BF16)
    wglu, wbs, wbm, wo = (w[l].astype(BF16) for w in (s5_w_glu, w_branch_s5, w_branch_mlstm, w_out))
    g1 = ffn1_norm[l][None]
    gm = mix_norm[l][None]
    g2 = ffn2_norm[l][None]
    gf = final_norm[None]
    bif = jnp.pad(jnp.concatenate([mlstm_b_i[l], mlstm_b_f[l]]), (0, LANES - 2 * HEADS))[None]
    cw = mlstm_conv_w[l]
    cb = mlstm_conv_b[l][None]
    ng = mlstm_norm[l][None]
    d_skip = s5_D[l][None]

    fc, ec, cc, lam = _s5_operators(s5_A_re[l], s5_A_im[l], s5_log_dt[l], s5_B_re[l], s5_B_im[l],
                                    s5_C_re[l], s5_C_im[l])
    w_toe, f_bf, e_bf = _s5_prep(fc, ec, cc)

    def front(x, tm):
        n = x.shape[0]
        h1 = _ffn(x, g1, w1g, w1u, w1d, n=n, tm=tm)
        z, zif = _win(h1, gm, w_wide, w_gates, w_if, tm=tm)
        return h1, z, zif

    def mixers(z, zif, s5_state, ml_state, *, nseq, seqlen, nb, lc):
        ys, sre, sim = _s5(z, w_toe, f_bf, e_bf, lam, d_skip, s5_state[0], s5_state[1], nseq=nseq, seqlen=seqlen)
        buf0, c0, n0, m0 = ml_state
        ym, buf, c, n, m = _mlstm(z, zif, cw, cb, bif, ng, buf0, c0, n0, m0.reshape(nseq, 1, HEADS),
                                  nseq=nseq, seqlen=seqlen, nb=nb, lc=lc)
        return ys, ym, (sre, sim), (buf, c, n, m.reshape(nseq, HEADS))

    def back(h1, ys, ym, z, n):
        h2 = _merge(h1, ys, ym, z, wglu, wbs, wbm, wo, n=n, tm=MERGE_TM)
        return _ffn(h2, g2, w2g, w2u, w2d, gf, n=n, tm=FFN_TM)

    ntok_s = nsamp * sseq
    x_sm = jnp.concatenate([x_sample.reshape(ntok_s, D_MODEL), meta_tokens], axis=0)
    h1_sm, z_sm, zif_sm = front(x_sm, ntok_s + N_META)

    z_m = jnp.tile(z_sm[ntok_s:], (nbatch, 1))
    zif_m = jnp.tile(zif_sm[ntok_s:], (nbatch, 1))
    zeros = lambda *s: jnp.zeros((nbatch,) + s, F32)
    _, _, s5_m, ml_m = mixers(
        z_m, zif_m, (zeros(S5_GROUPS * S5_STATE), zeros(S5_GROUPS * S5_STATE)),
        (zeros(CONV_W - 1, QK_WIDTH), zeros(HEADS, DV, DQK), zeros(HEADS, DQK), zeros(HEADS)),
        nseq=nbatch, seqlen=N_META, nb=1, lc=N_META)

    ntok_p = nbatch * seq
    h1_p, z_p, zif_p = front(x_prompt.reshape(ntok_p, D_MODEL), FFN_TM)
    ys_p, ym_p, s5_p, ml_p = mixers(z_p, zif_p, s5_m, ml_m, nseq=nbatch, seqlen=seq, nb=1, lc=256)
    y_p = back(h1_p, ys_p, ym_p, z_p, ntok_p)

    ys_s, ym_s, s5_s, ml_s = mixers(
        z_sm, zif_sm,
        (state_s5_re[l].reshape(nsamp, -1), state_s5_im[l].reshape(nsamp, -1)),
        (state_mlstm_conv[l], state_mlstm_C[l], state_mlstm_n[l], state_mlstm_m[l]),
        nseq=nsamp, seqlen=sseq, nb=8, lc=sseq)
    y_s = back(h1_sm, ys_s, ym_s, z_sm, ntok_s)

    def pack(n, s5_st, ml_st):
        buf, c, nn, m = ml_st
        return (s5_st[0].reshape(1, n, S5_GROUPS, S5_STATE), s5_st[1].reshape(1, n, S5_GROUPS, S5_STATE),
                c[None], nn[None], m[None], buf[None])

    return ((y_p.reshape(nbatch, seq, D_MODEL), y_s.reshape(nsamp, sseq, D_MODEL))
            + pack(nbatch, s5_p, ml_p) + pack(nsamp, s5_s, ml_s))
```

```python
import functools

import jax
import jax.numpy as jnp
from jax import lax
from jax.experimental import pallas as pl
from jax.experimental.pallas import tpu as pltpu

F32 = jnp.float32
BF16 = jnp.bfloat16

D_MODEL = 2048
D_FF = 5632
N_META = 16
S5_WIDTH = 1024
S5_GROUP = 16
S5_GROUPS = 64
S5_STATE = 64
HEADS = 4
DQK = 128
DV = 256
QK_WIDTH = 1024
V_WIDTH = 1024
CONV_W = 4
EPS = 1e-6

LANES = 128
S5_T = 8
S5_GPB = LANES // S5_GROUP
S5_NBLK = S5_WIDTH // LANES
S5_SW = S5_GPB * S5_STATE
VMEM_LIMIT = 58 * 1024 * 1024
FFN_TM = 1024
MERGE_TM = 256
STEP_NB = 16


def _cparams(sem):
    return pltpu.CompilerParams(dimension_semantics=sem, vmem_limit_bytes=VMEM_LIMIT)


def _rmsnorm(x, g):
    ms = jnp.mean(x * x, axis=-1, keepdims=True)
    return (x * lax.rsqrt(ms + EPS)) * g


def _dot(a, b):
    return jnp.dot(a, b, preferred_element_type=F32)


def _dot_nt(a, b):
    return lax.dot_general(a, b, (((1,), (1,)), ((), ())), preferred_element_type=F32)


def _dot_tn(a, b):
    return lax.dot_general(a, b, (((0,), (0,)), ((), ())), preferred_element_type=F32)


def _split3(x):
    hi = x.astype(BF16)
    r = x - hi.astype(F32)
    mid = r.astype(BF16)
    lo = (r - mid.astype(F32)).astype(BF16)
    return hi, mid, lo


def _ffn_kernel(x_ref, g_ref, wg_ref, wu_ref, wd_ref, *rest, final_norm):
    if final_norm:
        fg_ref, o_ref, xn_ref = rest
    else:
        o_ref, xn_ref = rest
    j = pl.program_id(1)

    @pl.when(j == 0)
    def _():
        xn_ref[...] = _rmsnorm(x_ref[...], g_ref[...]).astype(BF16)
        o_ref[...] = jnp.zeros(o_ref.shape, F32)

    xn = xn_ref[...]
    gt = _dot(xn, wg_ref[...].astype(BF16))
    up = _dot(xn, wu_ref[...].astype(BF16))
    act = (gt * jax.nn.sigmoid(gt) * up).astype(BF16)
    o_ref[...] += _dot(act, wd_ref[...].astype(BF16))

    @pl.when(j == pl.num_programs(1) - 1)
    def _():
        h = x_ref[...] + 0.5 * o_ref[...]
        if final_norm:
            h = _rmsnorm(h, fg_ref[...])
        o_ref[...] = h


def _ffn(x, g, wg, wu, wd, final_g=None, *, n, tm, tf=256, x_buffers=1):
    in_specs = [
        pl.BlockSpec((tm, D_MODEL), lambda i, j: (i, 0), pipeline_mode=pl.Buffered(x_buffers)),
        pl.BlockSpec((1, D_MODEL), lambda i, j: (0, 0)),
        pl.BlockSpec((D_MODEL, tf), lambda i, j: (0, j)),
        pl.BlockSpec((D_MODEL, tf), lambda i, j: (0, j)),
        pl.BlockSpec((tf, D_MODEL), lambda i, j: (j, 0)),
    ]
    args = [x, g, wg, wu, wd]
    if final_g is not None:
        in_specs.append(pl.BlockSpec((1, D_MODEL), lambda i, j: (0, 0)))
        args.append(final_g)
    return pl.pallas_call(
        functools.partial(_ffn_kernel, final_norm=final_g is not None),
        grid=(n // tm, D_FF // tf),
        in_specs=in_specs,
        out_specs=pl.BlockSpec((tm, D_MODEL), lambda i, j: (i, 0)),
        out_shape=jax.ShapeDtypeStruct((n, D_MODEL), F32),
        scratch_shapes=[pltpu.VMEM((tm, D_MODEL), BF16)],
        compiler_params=_cparams(("parallel", "arbitrary")),
        name="ffn",
    )(*args)


def _win_kernel(h_ref, g_ref, wa_ref, wb_ref, wif_ref, z_ref, zif_ref, un_ref, *, na):
    j = pl.program_id(1)

    @pl.when(j == 0)
    def _():
        un = _rmsnorm(h_ref[...], g_ref[...]).astype(BF16)
        un_ref[...] = un
        zif_ref[...] = _dot(un, wif_ref[...])

    @pl.when(j < na)
    def _():
        z_ref[...] = _dot(un_ref[...], wa_ref[...])

    @pl.when(j >= na)
    def _():
        z_ref[...] = _dot(un_ref[...], wb_ref[...])


def _win(h, g, wa, wb, wif, *, tm, tn=512):
    n = h.shape[0]
    na = wa.shape[1] // tn
    nb = wb.shape[1] // tn
    return pl.pallas_call(
        functools.partial(_win_kernel, na=na),
        grid=(n // tm, na + nb),
        in_specs=[
            pl.BlockSpec((tm, D_MODEL), lambda i, j: (i, 0)),
            pl.BlockSpec((1, D_MODEL), lambda i, j: (0, 0)),
            pl.BlockSpec((D_MODEL, tn), lambda i, j: (0, jnp.minimum(j, na - 1))),
            pl.BlockSpec((D_MODEL, tn), lambda i, j: (0, jnp.maximum(j - na, 0))),
            pl.BlockSpec((D_MODEL, LANES), lambda i, j: (0, 0)),
        ],
        out_specs=[
            pl.BlockSpec((tm, tn), lambda i, j: (i, j)),
            pl.BlockSpec((tm, LANES), lambda i, j: (i, 0)),
        ],
        out_shape=[
            jax.ShapeDtypeStruct((n, (na + nb) * tn), F32),
            jax.ShapeDtypeStruct((n, LANES), F32),
        ],
        scratch_shapes=[pltpu.VMEM((tm, D_MODEL), BF16)],
        compiler_params=_cparams(("parallel", "arbitrary")),
        name="w_in",
    )(h, g, wa, wb, wif)


def _s5_prep_kernel(fc_ref, ec_ref, cc_ref, w_ref, f_ref, e_ref):
    n = S5_T * LANES
    sw2 = 2 * S5_SW
    i32 = jnp.int32
    lg_h = S5_GROUP.bit_length() - 1
    lg_p = S5_STATE.bit_length() - 1
    lg_sw = S5_SW.bit_length() - 1

    def tile_mat(k, c, src_col):
        kk = lax.broadcasted_iota(i32, (k, c), 0)
        cc = lax.broadcasted_iota(i32, (k, c), 1)
        return jnp.where(kk == src_col(cc), 1.0, 0.0).astype(BF16)

    def group_mask(r, c, row_group, col_group):
        rr = lax.broadcasted_iota(i32, (r, c), 0)
        cc = lax.broadcasted_iota(i32, (r, c), 1)
        return row_group(rr) == col_group(cc)

    gmask = S5_GPB - 1
    rf = tile_mat(2 * S5_STATE, sw2, lambda c: ((c >> lg_sw) << lg_p) + (c & (S5_STATE - 1)))
    mf = group_mask(n, sw2, lambda r: (r >> lg_h) & gmask, lambda c: (c >> lg_p) & gmask)
    f_hi, f_mid, _ = _split3(fc_ref[0])
    x_hi = jnp.where(mf, _dot(f_hi, rf), 0.0).astype(BF16)
    x_mid = jnp.where(mf, _dot(f_mid, rf), 0.0).astype(BF16)
    f_ref[0] = x_hi
    lg_l = LANES.bit_length() - 1
    re = tile_mat(LANES, n, lambda c: ((c >> lg_l) << lg_h) + (c & (S5_GROUP - 1)))
    me = group_mask(sw2, n, lambda r: (r >> lg_p) & gmask, lambda c: (c >> lg_h) & gmask)
    e_ref[0] = jnp.where(me, _dot(ec_ref[0].astype(BF16), re), 0.0).astype(BF16)
    rc = tile_mat(S5_GROUP, LANES, lambda c: c & (S5_GROUP - 1))
    mc = group_mask(sw2, LANES, lambda r: (r >> lg_p) & gmask, lambda c: c >> lg_h)
    c_hi, c_mid, _ = _split3(cc_ref[0])
    ct_hi = jnp.where(mc, _dot(c_hi, rc), 0.0).astype(BF16)
    ct_mid = jnp.where(mc, _dot(c_mid, rc), 0.0).astype(BF16)
    w_ref[...] = jnp.zeros(w_ref.shape, w_ref.dtype)
    for lag in range(S5_T):
        r0 = (S5_T - 1 - lag) * LANES
        xh = x_hi[r0:r0 + LANES, :]
        xm = x_mid[r0:r0 + LANES, :]
        k = (_dot(xh, ct_hi) + _dot(xm, ct_hi) + _dot(xh, ct_mid)).astype(BF16)
        for s in range(S5_T - lag):
            t = s + lag
            w_ref[0, s * LANES:(s + 1) * LANES, t * LANES:(t + 1) * LANES] = k


def _s5_prep(fc, ec, cc):
    n = S5_T * LANES
    blk = lambda r, c: pl.BlockSpec((1, r, c), lambda j: (j, 0, 0))
    return pl.pallas_call(
        _s5_prep_kernel,
        grid=(S5_NBLK,),
        in_specs=[blk(n, 2 * S5_STATE), blk(2 * S5_SW, LANES), blk(2 * S5_SW, S5_GROUP)],
        out_specs=[blk(n, n), blk(n, 2 * S5_SW), blk(2 * S5_SW, n)],
        out_shape=[
            jax.ShapeDtypeStruct((S5_NBLK, n, n), BF16),
            jax.ShapeDtypeStruct((S5_NBLK, n, 2 * S5_SW), BF16),
            jax.ShapeDtypeStruct((S5_NBLK, 2 * S5_SW, n), BF16),
        ],
        compiler_params=_cparams(("parallel",)),
        name="s5_prep",
    )(fc, ec, cc)


def _s5_kernel(x_ref, w_ref, f_ref, e_ref, lam_ref, d_ref, h0re_ref, h0im_ref,
               y_ref, ore_ref, oim_ref, yi_ref, s_ref, hin_ref, *, nseq, nblk, rt):
    rows = nseq * nblk
    ar = lam_ref[0, 0:1, :]
    ai = lam_ref[0, 1:2, :]

    def load_u(r0):
        return jnp.concatenate(
            [x_ref[pl.ds(r0 * S5_T + s, rt, stride=S5_T), :] for s in range(S5_T)], axis=1)

    for r0 in range(0, rows, rt):
        ub = load_u(r0).astype(BF16)
        yi_ref[r0:r0 + rt, :] = _dot(ub, w_ref[0])
        s_ref[r0:r0 + rt, :] = _dot(ub, f_ref[0])

    if nblk == 1:
        hre = h0re_ref[...]
        him = h0im_ref[...]
        hin_ref[:, :S5_SW] = hre
        hin_ref[:, S5_SW:] = him
        ore_ref[...] = ar * hre - ai * him + s_ref[:, :S5_SW]
        oim_ref[...] = ar * him + ai * hre + s_ref[:, S5_SW:]
    else:
        def body(c, carry):
            new = []
            for b in range(nseq):
                hre, him = carry[2 * b], carry[2 * b + 1]
                row = b * nblk + c
                hin_ref[pl.ds(row, 1), :S5_SW] = hre
                hin_ref[pl.ds(row, 1), S5_SW:] = him
                sre = s_ref[pl.ds(row, 1), :S5_SW]
                sim = s_ref[pl.ds(row, 1), S5_SW:]
                new.append(ar * hre - ai * him + sre)
                new.append(ar * him + ai * hre + sim)
            return tuple(new)

        init = []
        for b in range(nseq):
            init.append(h0re_ref[b:b + 1, :])
            init.append(h0im_ref[b:b + 1, :])
        fin = lax.fori_loop(0, nblk, body, tuple(init))
        for b in range(nseq):
            ore_ref[b:b + 1, :] = fin[2 * b]
            oim_ref[b:b + 1, :] = fin[2 * b + 1]

    dt = jnp.concatenate([d_ref[...]] * S5_T, axis=1)
    for r0 in range(0, rows, rt):
        yo = _dot(hin_ref[r0:r0 + rt, :].astype(BF16), e_ref[0])
        y = jax.nn.gelu(yi_ref[r0:r0 + rt, :] + yo + load_u(r0) * dt)
        for t in range(S5_T):
            y_ref[pl.ds(r0 * S5_T + t, rt, stride=S5_T), :] = y[:, t * LANES:(t + 1) * LANES]


def _s5(z, w, f, e, lam, d, h0re, h0im, *, nseq, seqlen):
    nblk = seqlen // S5_T
    rows = nseq * nblk
    rt = min(rows, 256)
    n = S5_T * LANES
    tok = nseq * seqlen
    kern = functools.partial(_s5_kernel, nseq=nseq, nblk=nblk, rt=rt)
    return pl.pallas_call(
        kern,
        grid=(S5_NBLK,),
        in_specs=[
            pl.BlockSpec((tok, LANES), lambda j: (0, j)),
            pl.BlockSpec((1, n, n), lambda j: (j, 0, 0)),
            pl.BlockSpec((1, n, 2 * S5_SW), lambda j: (j, 0, 0)),
            pl.BlockSpec((1, 2 * S5_SW, n), lambda j: (j, 0, 0)),
            pl.BlockSpec((1, 2, S5_SW), lambda j: (j, 0, 0)),
            pl.BlockSpec((1, LANES), lambda j: (0, j)),
            pl.BlockSpec((nseq, S5_SW), lambda j: (0, j)),
            pl.BlockSpec((nseq, S5_SW), lambda j: (0, j)),
        ],
        out_specs=[
            pl.BlockSpec((tok, LANES), lambda j: (0, j)),
            pl.BlockSpec((nseq, S5_SW), lambda j: (0, j)),
            pl.BlockSpec((nseq, S5_SW), lambda j: (0, j)),
        ],
        out_shape=[
            jax.ShapeDtypeStruct((tok, S5_WIDTH), F32),
            jax.ShapeDtypeStruct((nseq, S5_GROUPS * S5_STATE), F32),
            jax.ShapeDtypeStruct((nseq, S5_GROUPS * S5_STATE), F32),
        ],
        scratch_shapes=[
            pltpu.VMEM((rows, n), F32),
            pltpu.VMEM((rows, 2 * S5_SW), F32),
            pltpu.VMEM((rows, 2 * S5_SW), F32),
        ],
        compiler_params=_cparams(("parallel",)),
        name="s5",
    )(z, w, f, e, lam, d, h0re, h0im)


def _s5_operators(a_re, a_im, log_dt, b_re, b_im, c_re, c_im):
    dt = jnp.exp(log_dt)[:, None]
    mag = jnp.exp(a_re * dt)
    abar_re = mag * jnp.cos(a_im * dt)
    abar_im = mag * jnp.sin(a_im * dt)
    nr, ni = abar_re - 1.0, abar_im
    den = a_re * a_re + a_im * a_im
    q_re = (nr * a_re + ni * a_im) / den
    q_im = (ni * a_re - nr * a_im) / den
    bb_re = q_re[:, :, None] * b_re - q_im[:, :, None] * b_im
    bb_im = q_re[:, :, None] * b_im + q_im[:, :, None] * b_re
    taus = jnp.arange(S5_T + 1, dtype=F32)[:, None, None]
    pmag = jnp.exp(taus * (a_re * dt)[None])
    pw_re = pmag * jnp.cos(taus * (a_im * dt)[None])
    pw_im = pmag * jnp.sin(taus * (a_im * dt)[None])
    n = S5_T * LANES

    lags = (S5_T - 1) - jnp.arange(S5_T, dtype=F32)[:, None, None]
    lmag = jnp.exp(lags * (a_re * dt)[None])
    lr = (lmag * jnp.cos(lags * (a_im * dt)[None]))[:, :, :, None]
    li = (lmag * jnp.sin(lags * (a_im * dt)[None]))[:, :, :, None]
    fb = jnp.stack([lr * bb_re[None] - li * bb_im[None], lr * bb_im[None] + li * bb_re[None]])
    fc = fb.reshape(2, S5_T, S5_NBLK, S5_GPB, S5_STATE, S5_GROUP).transpose(2, 1, 3, 5, 0, 4)
    fc = fc.reshape(S5_NBLK, n, 2 * S5_STATE)

    cp_re = jnp.swapaxes(c_re, -1, -2)[None]
    cp_im = jnp.swapaxes(c_im, -1, -2)[None]
    tr = pw_re[1:][:, :, :, None]
    ti = pw_im[1:][:, :, :, None]
    ce = jnp.stack([cp_re * tr - cp_im * ti, -(cp_re * ti + cp_im * tr)])
    ec = ce.reshape(2, S5_T, S5_NBLK, S5_GPB, S5_STATE, S5_GROUP).transpose(2, 0, 3, 4, 1, 5)
    ec = ec.reshape(S5_NBLK, 2 * S5_SW, LANES)

    cc = jnp.stack([cp_re[0], -cp_im[0]]).reshape(2, S5_NBLK, S5_GPB, S5_STATE, S5_GROUP)
    cc = cc.transpose(1, 0, 2, 3, 4).reshape(S5_NBLK, 2 * S5_SW, S5_GROUP)

    lam = jnp.stack([pw_re[S5_T].reshape(S5_NBLK, S5_SW), pw_im[S5_T].reshape(S5_NBLK, S5_SW)], axis=1)
    return fc, ec, cc, lam


def _mlstm_kernel(qk_ref, v_ref, o_ref, if_ref, cw_ref, cb_ref, bif_ref, ng_ref,
                  buf0_ref, c0_ref, n0_ref, m0_ref,
                  y_ref, bufo_ref, co_ref, no_ref, mo_ref, xp_ref, *, nb, lc, carry):
    if carry:
        @pl.when(pl.program_id(1) == 0)
        def _():
            bufo_ref[...] = buf0_ref[...]
            co_ref[...] = c0_ref[...]
            no_ref[...] = n0_ref[...]
            mo_ref[...] = m0_ref[...]
        bufs_ref, cs_ref, ns_ref, ms_ref = bufo_ref, co_ref, no_ref, mo_ref
    else:
        bufs_ref, cs_ref, ns_ref, ms_ref = buf0_ref, c0_ref, n0_ref, m0_ref

    lp = max(lc, LANES)
    head_row = lax.broadcasted_iota(jnp.int32, (HEADS, DQK), 0)
    head_lane = lax.broadcasted_iota(jnp.int32, (1, HEADS), 1)
    row = lax.broadcasted_iota(jnp.int32, (lc, lc), 0)
    col = lax.broadcasted_iota(jnp.int32, (lc, lc), 1)
    causal = row >= col
    tril = jnp.where(causal, 1.0, 0.0).astype(BF16)
    lane = lax.broadcasted_iota(jnp.int32, (lc, LANES), 1)
    is_i = lane < HEADS
    is_f = jnp.logical_and(lane >= HEADS, lane < 2 * HEADS)

    for s in range(nb):
        r0 = s * lc
        xp_ref[s, 5:8, :] = bufs_ref[s]
        xp_ref[s, 8:8 + lc, :] = qk_ref[r0:r0 + lc, :]
        conv = cb_ref[...] + sum(xp_ref[s, 5 + j:5 + j + lc, :] * cw_ref[j:j + 1, :] for j in range(CONV_W))
        bufo_ref[s] = xp_ref[s, 5 + lc:8 + lc, :]
        n_all = ns_ref[s]
        m_all = ms_ref[s]
        n_new_all = jnp.zeros((HEADS, DQK), F32)
        m_new_all = jnp.zeros((1, HEADS), F32)
        qk = conv * jax.nn.sigmoid(conv)
        q = qk[:, :QK_WIDTH // 2] * (DQK ** -0.5)
        k = qk[:, QK_WIDTH // 2:]

        gate = if_ref[r0:r0 + lc, :] + bif_ref[...]
        lf = jnp.where(is_f, jax.nn.log_sigmoid(gate), 0.0)
        parts = _split3(lf)
        bcum = _dot(tril, parts[0]) + _dot(tril, parts[1]) + _dot(tril, parts[2])
        pc = jnp.where(is_i, gate, bcum)
        if lc < lp:
            pc_t = jnp.concatenate([pc, jnp.zeros((lp - lc, LANES), F32)], axis=0).T[:, :lc]
        else:
            pc_t = pc.T

        for h in range(HEADS):
            i_col = pc[:, h:h + 1]
            b_col = pc[:, HEADS + h:HEADS + h + 1]
            i_row = pc_t[h:h + 1, :]
            b_row = pc_t[HEADS + h:HEADS + h + 1, :]
            m0 = m_all[:, h:h + 1]
            logw = jnp.where(causal, b_col - b_row + i_row, -jnp.inf)
            g = b_col + m0
            m = jnp.maximum(g, jnp.max(logw, axis=-1, keepdims=True))
            w = jnp.exp(logw - m)
            inter = jnp.exp(g - m)
            qh = q[:, h * DQK:(h + 1) * DQK]
            kh = k[:, h * DQK:(h + 1) * DQK]
            vh = v_ref[r0:r0 + lc, h * DV:(h + 1) * DV]
            qb = qh.astype(BF16)
            vb = vh.astype(BF16)
            c_prev = cs_ref[s, h]
            n_prev = n_all[h:h + 1, :]
            sc = _dot_nt(qb, kh.astype(BF16)) * w
            num = _dot(sc.astype(BF16), vb) + inter * _dot_nt(qb, c_prev.astype(BF16))
            den = jnp.sum(sc, axis=-1, keepdims=True) + inter * jnp.sum(qh * n_prev, axis=-1, keepdims=True)
            hh = num / jnp.maximum(jnp.abs(den), jnp.exp(-m))
            hh = hh * lax.rsqrt(jnp.mean(hh * hh, axis=-1, keepdims=True) + EPS)
            og = jax.nn.sigmoid(o_ref[r0:r0 + lc, h * DV:(h + 1) * DV])
            y_ref[r0:r0 + lc, h * DV:(h + 1) * DV] = hh * ng_ref[:, h * DV:(h + 1) * DV] * og

            b_last = b_col[lc - 1:lc, :]
            m_new = m[lc - 1:lc, :]
            decay = jnp.exp(b_last + m0 - m_new)
            kw = kh * jnp.exp(b_last - b_col + i_col - m_new)
            co_ref[s, h] = decay * c_prev + _dot_tn(vb, kw.astype(BF16))
            n_new = decay * n_prev + jnp.sum(kw, axis=0, keepdims=True)
            n_new_all = jnp.where(head_row == h, n_new, n_new_all)
            m_new_all = jnp.where(head_lane == h, m_new, m_new_all)
        no_ref[s] = n_new_all
        mo_ref[s] = m_new_all


def _mlstm(z, zif, cw, cb, bif, ng, buf0, c0, n0, m0, *, nseq, seqlen, nb, lc):
    tok = nseq * seqlen
    nchunk = seqlen // lc
    rows = nb * lc
    tok_map = lambda col: (lambda i, c: (i * nchunk + c, col))
    const2 = lambda i, c: (0, 0)
    st3 = lambda i, c: (i, 0, 0)
    st4 = lambda i, c: (i, 0, 0, 0)
    kern = functools.partial(_mlstm_kernel, nb=nb, lc=lc, carry=nchunk > 1)
    return pl.pallas_call(
        kern,
        grid=(nseq // nb, nchunk),
        in_specs=[
            pl.BlockSpec((rows, QK_WIDTH), tok_map(1)),
            pl.BlockSpec((rows, V_WIDTH), tok_map(2)),
            pl.BlockSpec((rows, V_WIDTH), tok_map(3)),
            pl.BlockSpec((rows, LANES), tok_map(0)),
            pl.BlockSpec((CONV_W, QK_WIDTH), const2),
            pl.BlockSpec((1, QK_WIDTH), const2),
            pl.BlockSpec((1, LANES), const2),
            pl.BlockSpec((1, V_WIDTH), const2),
            pl.BlockSpec((nb, CONV_W - 1, QK_WIDTH), st3),
            pl.BlockSpec((nb, HEADS, DV, DQK), st4),
            pl.BlockSpec((nb, HEADS, DQK), st3),
            pl.BlockSpec((nb, 1, HEADS), st3),
        ],
        out_specs=[
            pl.BlockSpec((rows, V_WIDTH), tok_map(0)),
            pl.BlockSpec((nb, CONV_W - 1, QK_WIDTH), st3),
            pl.BlockSpec((nb, HEADS, DV, DQK), st4),
            pl.BlockSpec((nb, HEADS, DQK), st3),
            pl.BlockSpec((nb, 1, HEADS), st3),
        ],
        out_shape=[
            jax.ShapeDtypeStruct((tok, V_WIDTH), F32),
            jax.ShapeDtypeStruct((nseq, CONV_W - 1, QK_WIDTH), F32),
            jax.ShapeDtypeStruct((nseq, HEADS, DV, DQK), F32),
            jax.ShapeDtypeStruct((nseq, HEADS, DQK), F32),
            jax.ShapeDtypeStruct((nseq, 1, HEADS), F32),
        ],
        scratch_shapes=[pltpu.VMEM((nb, lc + 8, QK_WIDTH), F32)],
        compiler_params=_cparams(("parallel", "arbitrary")),
        name="mlstm",
    )(z, z, z, zif, cw, cb, bif, ng, buf0, c0, n0, m0)


def _mlstm_step_kernel(qk_ref, v_ref, o_ref, if_ref, cw_ref, cb_ref, bif_ref, ng_ref,
                       buf0_ref, c0_ref, n0x_ref, m0x_ref,
                       y_ref, bufo_ref, co_ref, nox_ref, mox_ref, xp_ref, conv_ref, *, nb, lc):
    rows = nb * lc
    lg = lc.bit_length() - 1
    i32 = jnp.int32
    row = lax.broadcasted_iota(i32, (rows, rows), 0)
    col = lax.broadcasted_iota(i32, (rows, rows), 1)
    same = (row >> lg) == (col >> lg)
    causal = jnp.logical_and(same, row >= col)
    same_b = jnp.where(same, 1.0, 0.0).astype(BF16)
    tril_b = jnp.where(causal, 1.0, 0.0).astype(BF16)
    lane = lax.broadcasted_iota(i32, (rows, LANES), 1)
    is_i = lane < HEADS
    is_f = jnp.logical_and(lane >= HEADS, lane < 2 * HEADS)

    def seg_dot(mat, x):
        hi, mid, lo = _split3(x)
        return _dot(mat, hi) + _dot(mat, mid) + _dot(mat, lo)

    for s in range(nb):
        xp_ref[s, 5:8, :] = buf0_ref[s]
        xp_ref[s, 8:8 + lc, :] = qk_ref[s * lc:(s + 1) * lc, :]
        conv_ref[s * lc:(s + 1) * lc, :] = cb_ref[...] + sum(
            xp_ref[s, 5 + j:5 + j + lc, :] * cw_ref[j:j + 1, :] for j in range(CONV_W))
        bufo_ref[s] = xp_ref[s, 5 + lc:8 + lc, :]
    conv = conv_ref[...]
    qk = conv * jax.nn.sigmoid(conv)
    q = qk[:, :QK_WIDTH // 2] * (DQK ** -0.5)
    k = qk[:, QK_WIDTH // 2:]

    gate = if_ref[...] + bif_ref[...]
    lf = jnp.where(is_f, jax.nn.log_sigmoid(gate), 0.0)
    bcum = seg_dot(tril_b, lf)
    btot = seg_dot(same_b, lf)
    pc = jnp.where(is_i, gate, bcum)
    if rows < LANES:
        pc_t = jnp.concatenate([pc, jnp.zeros((LANES - rows, LANES), F32)], axis=0).T[:, :rows]
    else:
        pc_t = pc.T
    m0x = m0x_ref[...]
    mox = jnp.zeros((rows, LANES), F32)

    for h in range(HEADS):
        i_col = pc[:, h:h + 1]
        b_col = pc[:, HEADS + h:HEADS + h + 1]
        i_row = pc_t[h:h + 1, :]
        b_row = pc_t[HEADS + h:HEADS + h + 1, :]
        b_last = btot[:, HEADS + h:HEADS + h + 1]
        m0 = m0x[:, h:h + 1]
        lw = b_col - b_row + i_row
        logw = jnp.where(causal, lw, -jnp.inf)
        g = b_col + m0
        m = jnp.maximum(g, jnp.max(logw, axis=-1, keepdims=True))
        w = jnp.exp(logw - m)
        inter = jnp.exp(g - m)
        lw_end = jnp.where(same, b_last - b_row + i_row, -jnp.inf)
        m_new = jnp.maximum(b_last + m0, jnp.max(lw_end, axis=-1, keepdims=True))
        decay = jnp.exp(b_last + m0 - m_new)
        qh = q[:, h * DQK:(h + 1) * DQK]
        kh = k[:, h * DQK:(h + 1) * DQK]
        qb = qh.astype(BF16)
        vb = v_ref[:, h * DV:(h + 1) * DV].astype(BF16)
        n_prev = n0x_ref[:, h * DQK:(h + 1) * DQK]
        sc = _dot_nt(qb, kh.astype(BF16)) * w
        carried = jnp.concatenate(
            [_dot_nt(qb[s * lc:(s + 1) * lc], c0_ref[s, h].astype(BF16)) for s in range(nb)], axis=0)
        num = _dot(sc.astype(BF16), vb) + inter * carried
        den = jnp.sum(sc, axis=-1, keepdims=True) + inter * jnp.sum(qh * n_prev, axis=-1, keepdims=True)
        hh = num / jnp.maximum(jnp.abs(den), jnp.exp(-m))
        hh = hh * lax.rsqrt(jnp.mean(hh * hh, axis=-1, keepdims=True) + EPS)
        og = jax.nn.sigmoid(o_ref[:, h * DV:(h + 1) * DV])
        y_ref[:, h * DV:(h + 1) * DV] = hh * ng_ref[:, h * DV:(h + 1) * DV] * og

        kw = kh * jnp.exp(b_last - b_col + i_col - m_new)
        kwb = kw.astype(BF16)
        for s in range(nb):
            r0 = s * lc
            co_ref[s, h] = decay[r0:r0 + 1, :] * c0_ref[s, h] + _dot_tn(vb[r0:r0 + lc], kwb[r0:r0 + lc])
        nox_ref[:, h * DQK:(h + 1) * DQK] = decay * n_prev + seg_dot(same_b, kw)
        mox = jnp.where(lane == h, m_new, mox)
    mox_ref[...] = mox


def _mlstm_step(z, zif, cw, cb, bif, ng, buf0, c0, n0, m0, *, nseq, seqlen, nb):
    lc = seqlen
    assert lc & (lc - 1) == 0 and lc >= CONV_W - 1
    tok = nseq * seqlen
    rows = nb * lc
    n0x = jnp.repeat(n0.reshape(nseq, HEADS * DQK), lc, axis=0)
    m0x = jnp.repeat(jnp.pad(m0, ((0, 0), (0, LANES - HEADS))), lc, axis=0)
    tok_map = lambda col: (lambda i: (i, col))
    const2 = lambda i: (0, 0)
    st3 = lambda i: (i, 0, 0)
    st4 = lambda i: (i, 0, 0, 0)
    y, buf, c, nox, mox = pl.pallas_call(
        functools.partial(_mlstm_step_kernel, nb=nb, lc=lc),
        grid=(nseq // nb,),
        in_specs=[
            pl.BlockSpec((rows, QK_WIDTH), tok_map(1)),
            pl.BlockSpec((rows, V_WIDTH), tok_map(2)),
            pl.BlockSpec((rows, V_WIDTH), tok_map(3)),
            pl.BlockSpec((rows, LANES), tok_map(0)),
            pl.BlockSpec((CONV_W, QK_WIDTH), const2),
            pl.BlockSpec((1, QK_WIDTH), const2),
            pl.BlockSpec((1, LANES), const2),
            pl.BlockSpec((1, V_WIDTH), const2),
            pl.BlockSpec((nb, CONV_W - 1, QK_WIDTH), st3),
            pl.BlockSpec((nb, HEADS, DV, DQK), st4),
            pl.BlockSpec((rows, HEADS * DQK), tok_map(0)),
            pl.BlockSpec((rows, LANES), tok_map(0)),
        ],
        out_specs=[
            pl.BlockSpec((rows, V_WIDTH), tok_map(0)),
            pl.BlockSpec((nb, CONV_W - 1, QK_WIDTH), st3),
            pl.BlockSpec((nb, HEADS, DV, DQK), st4),
            pl.BlockSpec((rows, HEADS * DQK), tok_map(0)),
            pl.BlockSpec((rows, LANES), tok_map(0)),
        ],
        out_shape=[
            jax.ShapeDtypeStruct((tok, V_WIDTH), F32),
            jax.ShapeDtypeStruct((nseq, CONV_W - 1, QK_WIDTH), F32),
            jax.ShapeDtypeStruct((nseq, HEADS, DV, DQK), F32),
            jax.ShapeDtypeStruct((tok, HEADS * DQK), F32),
            jax.ShapeDtypeStruct((tok, LANES), F32),
        ],
        scratch_shapes=[pltpu.VMEM((nb, lc + 8, QK_WIDTH), F32), pltpu.VMEM((rows, QK_WIDTH), F32)],
        compiler_params=_cparams(("parallel",)),
        name="mlstm_step",
    )(z, z, z, zif, cw, cb, bif, ng, buf0, c0, n0x, m0x)
    n = nox[::lc].reshape(nseq, HEADS, DQK)
    m = mox[::lc, :HEADS]
    return y, buf, c, n, m


def _merge_kernel(h_ref, ys_ref, ym_ref, g1_ref, g2_ref, wglu_ref, wbs_ref, wbm_ref, wo_ref, o_ref):
    ys = ys_ref[...]
    glu = ys * jax.nn.sigmoid(_dot(ys.astype(BF16), wglu_ref[...]))
    a = _dot(glu.astype(BF16), wbs_ref[...])
    b = _dot(ym_ref[...].astype(BF16), wbm_ref[...])
    merged = jax.nn.sigmoid(g1_ref[...]) * a + jax.nn.sigmoid(g2_ref[...]) * b
    o_ref[...] = h_ref[...] + _dot(merged.astype(BF16), wo_ref[...])


def _merge(h, ys, ym, z, wglu, wbs, wbm, wo, *, n, tm):
    row = lambda i: (i, 0)
    const = lambda i: (0, 0)
    resident = lambda shape: pl.BlockSpec(shape, const, pipeline_mode=pl.Buffered(1))
    return pl.pallas_call(
        _merge_kernel,
        grid=(n // tm,),
        in_specs=[
            pl.BlockSpec((tm, D_MODEL), row),
            pl.BlockSpec((tm, S5_WIDTH), row),
            pl.BlockSpec((tm, V_WIDTH), row),
            pl.BlockSpec((tm, D_MODEL), lambda i: (i, 2)),
            pl.BlockSpec((tm, D_MODEL), lambda i: (i, 3)),
            resident((S5_WIDTH, S5_WIDTH)),
            resident((S5_WIDTH, D_MODEL)),
            resident((V_WIDTH, D_MODEL)),
            resident((D_MODEL, D_MODEL)),
        ],
        out_specs=pl.BlockSpec((tm, D_MODEL), row),
        out_shape=jax.ShapeDtypeStruct((n, D_MODEL), F32),
        compiler_params=_cparams(("parallel",)),
        name="merge",
    )(h, ys, ym, z, z, wglu, wbs, wbm, wo)


def kernel(x_prompt, x_sample, state_s5_re, state_s5_im, state_mlstm_C, state_mlstm_n, state_mlstm_m,
           state_mlstm_conv, meta_tokens, ffn1_norm, ffn1_w_gate, ffn1_w_up, ffn1_w_down, mix_norm, w_in,
           s5_A_re, s5_A_im, s5_log_dt, s5_B_re, s5_B_im, s5_C_re, s5_C_im, s5_D, s5_w_glu,
           mlstm_conv_w, mlstm_conv_b, mlstm_b_i, mlstm_b_f, mlstm_norm, w_branch_s5, w_branch_mlstm,
           w_out, ffn2_norm, ffn2_w_gate, ffn2_w_up, ffn2_w_down, final_norm):
    nbatch, seq, _ = x_prompt.shape
    nsamp, sseq, _ = x_sample.shape
    l = 0

    w1g, w1u, w1d = ffn1_w_gate[l], ffn1_w_up[l], ffn1_w_down[l]
    w2g, w2u, w2d = ffn2_w_gate[l], ffn2_w_up[l], ffn2_w_down[l]
    o_if = S5_WIDTH + QK_WIDTH + 2 * V_WIDTH
    o_gate = o_if + 2 * HEADS
    win = w_in[l]
    w_wide = win[:, :o_if].astype(BF16)
    w_gates = win[:, o_gate:].astype(BF16)
    w_if = jnp.pad(win[:, o_if:o_gate], ((0, 0), (0, LANES - 2 * HEADS))).astype(BF16)
    wglu, wbs, wbm, wo = (w[l].astype(BF16) for w in (s5_w_glu, w_branch_s5, w_branch_mlstm, w_out))
    g1 = ffn1_norm[l][None]
    gm = mix_norm[l][None]
    g2 = ffn2_norm[l][None]
    gf = final_norm[None]
    bif = jnp.pad(jnp.concatenate([mlstm_b_i[l], mlstm_b_f[l]]), (0, LANES - 2 * HEADS))[None]
    cw = mlstm_conv_w[l]
    cb = mlstm_conv_b[l][None]
    ng = mlstm_norm[l][None]
    d_skip = s5_D[l][None]

    fc, ec, cc, lam = _s5_operators(s5_A_re[l], s5_A_im[l], s5_log_dt[l], s5_B_re[l], s5_B_im[l],
                                    s5_C_re[l], s5_C_im[l])
    w_toe, f_bf, e_bf = _s5_prep(fc, ec, cc)

    def front(x, tm):
        n = x.shape[0]
        h1 = _ffn(x, g1, w1g, w1u, w1d, n=n, tm=tm, x_buffers=2)
        z, zif = _win(h1, gm, w_wide, w_gates, w_if, tm=tm)
        return h1, z, zif

    def mixers(z, zif, s5_state, ml_state, *, nseq, seqlen, nb, lc):
        ys, sre, sim = _s5(z, w_toe, f_bf, e_bf, lam, d_skip, s5_state[0], s5_state[1], nseq=nseq, seqlen=seqlen)
        buf0, c0, n0, m0 = ml_state
        if lc == seqlen and nb > 1:
            ym, buf, c, n, m = _mlstm_step(z, zif, cw, cb, bif, ng, buf0, c0, n0, m0,
                                           nseq=nseq, seqlen=seqlen, nb=nb)
        else:
            ym, buf, c, n, m = _mlstm(z, zif, cw, cb, bif, ng, buf0, c0, n0, m0.reshape(nseq, 1, HEADS),
                                      nseq=nseq, seqlen=seqlen, nb=nb, lc=lc)
            m = m.reshape(nseq, HEADS)
        return ys, ym, (sre, sim), (buf, c, n, m)

    def back(h1, ys, ym, z, n):
        h2 = _merge(h1, ys, ym, z, wglu, wbs, wbm, wo, n=n, tm=MERGE_TM)
        return _ffn(h2, g2, w2g, w2u, w2d, gf, n=n, tm=FFN_TM)

    ntok_s = nsamp * sseq
    x_sm = jnp.concatenate([x_sample.reshape(ntok_s, D_MODEL), meta_tokens], axis=0)
    h1_sm, z_sm, zif_sm = front(x_sm, ntok_s + N_META)

    z_m = jnp.tile(z_sm[ntok_s:], (nbatch, 1))
    zif_m = jnp.tile(zif_sm[ntok_s:], (nbatch, 1))
    zeros = lambda *s: jnp.zeros((nbatch,) + s, F32)
    _, _, s5_m, ml_m = mixers(
        z_m, zif_m, (zeros(S5_GROUPS * S5_STATE), zeros(S5_GROUPS * S5_STATE)),
        (zeros(CONV_W - 1, QK_WIDTH), zeros(HEADS, DV, DQK), zeros(HEADS, DQK), zeros(HEADS)),
        nseq=nbatch, seqlen=N_META, nb=1, lc=N_META)

    ntok_p = nbatch * seq
    h1_p, z_p, zif_p = front(x_prompt.reshape(ntok_p, D_MODEL), FFN_TM)
    ys_p, ym_p, s5_p, ml_p = mixers(z_p, zif_p, s5_m, ml_m, nseq=nbatch, seqlen=seq, nb=1, lc=256)
    y_p = back(h1_p, ys_p, ym_p, z_p, ntok_p)

    ys_s, ym_s, s5_s, ml_s = mixers(
        z_sm, zif_sm,
        (state_s5_re[l].reshape(nsamp, -1), state_s5_im[l].reshape(nsamp, -1)),
        (state_mlstm_conv[l], state_mlstm_C[l], state_mlstm_n[l], state_mlstm_m[l]),
        nseq=nsamp, seqlen=sseq, nb=STEP_NB, lc=sseq)
    y_s = back(h1_sm, ys_s, ym_s, z_sm, ntok_s)

    def pack(n, s5_st, ml_st):
        buf, c, nn, m = ml_st
        return (s5_st[0].reshape(1, n, S5_GROUPS, S5_STATE), s5_st[1].reshape(1, n, S5_GROUPS, S5_STATE),
                c[None], nn[None], m[None], buf[None])

    return ((y_p.reshape(nbatch, seq, D_MODEL), y_s.reshape(nsamp, sseq, D_MODEL))
            + pack(nbatch, s5_p, ml_p) + pack(nsamp, s5_s, ml_s))
```

```python
import functools

import jax
import jax.numpy as jnp
from jax import lax
from jax.experimental import pallas as pl
from jax.experimental.pallas import tpu as pltpu

F32 = jnp.float32
BF16 = jnp.bfloat16

D_MODEL = 2048
D_FF = 5632
N_META = 16
S5_WIDTH = 1024
S5_GROUP = 16
S5_GROUPS = 64
S5_STATE = 64
HEADS = 4
DQK = 128
DV = 256
QK_WIDTH = 1024
V_WIDTH = 1024
CONV_W = 4
EPS = 1e-6

LANES = 128
S5_T = 8
S5_GPB = LANES // S5_GROUP
S5_NBLK = S5_WIDTH // LANES
S5_SW = S5_GPB * S5_STATE
VMEM_LIMIT = 58 * 1024 * 1024
FFN_TM = 1024
MERGE_TM = 256
STEP_NB = 16


def _cparams(sem):
    return pltpu.CompilerParams(dimension_semantics=sem, vmem_limit_bytes=VMEM_LIMIT)


def _rmsnorm(x, g):
    ms = jnp.mean(x * x, axis=-1, keepdims=True)
    return (x * lax.rsqrt(ms + EPS)) * g


def _dot(a, b):
    return jnp.dot(a, b, preferred_element_type=F32)


def _dot_nt(a, b):
    return lax.dot_general(a, b, (((1,), (1,)), ((), ())), preferred_element_type=F32)


def _dot_tn(a, b):
    return lax.dot_general(a, b, (((0,), (0,)), ((), ())), preferred_element_type=F32)


def _split3(x):
    hi = x.astype(BF16)
    r = x - hi.astype(F32)
    mid = r.astype(BF16)
    lo = (r - mid.astype(F32)).astype(BF16)
    return hi, mid, lo


def _ffn_kernel(x_ref, g_ref, wg_ref, wu_ref, wd_ref, *rest, final_norm):
    if final_norm:
        fg_ref, o_ref, xn_ref = rest
    else:
        o_ref, xn_ref = rest
    j = pl.program_id(1)

    @pl.when(j == 0)
    def _():
        xn_ref[...] = _rmsnorm(x_ref[...], g_ref[...]).astype(BF16)
        o_ref[...] = jnp.zeros(o_ref.shape, F32)

    xn = xn_ref[...]
    gt = _dot(xn, wg_ref[...].astype(BF16))
    up = _dot(xn, wu_ref[...].astype(BF16))
    act = (gt * jax.nn.sigmoid(gt) * up).astype(BF16)
    o_ref[...] += _dot(act, wd_ref[...].astype(BF16))

    @pl.when(j == pl.num_programs(1) - 1)
    def _():
        h = x_ref[...] + 0.5 * o_ref[...]
        if final_norm:
            h = _rmsnorm(h, fg_ref[...])
        o_ref[...] = h


def _ffn(x, g, wg, wu, wd, final_g=None, *, n, tm, tf=256, x_buffers=2):
    in_specs = [
        pl.BlockSpec((tm, D_MODEL), lambda i, j: (i, 0), pipeline_mode=pl.Buffered(x_buffers)),
        pl.BlockSpec((1, D_MODEL), lambda i, j: (0, 0)),
        pl.BlockSpec((D_MODEL, tf), lambda i, j: (0, j)),
        pl.BlockSpec((D_MODEL, tf), lambda i, j: (0, j)),
        pl.BlockSpec((tf, D_MODEL), lambda i, j: (j, 0)),
    ]
    args = [x, g, wg, wu, wd]
    if final_g is not None:
        in_specs.append(pl.BlockSpec((1, D_MODEL), lambda i, j: (0, 0)))
        args.append(final_g)
    return pl.pallas_call(
        functools.partial(_ffn_kernel, final_norm=final_g is not None),
        grid=(n // tm, D_FF // tf),
        in_specs=in_specs,
        out_specs=pl.BlockSpec((tm, D_MODEL), lambda i, j: (i, 0)),
        out_shape=jax.ShapeDtypeStruct((n, D_MODEL), F32),
        scratch_shapes=[pltpu.VMEM((tm, D_MODEL), BF16)],
        compiler_params=_cparams(("parallel", "arbitrary")),
        name="ffn",
    )(*args)


def _win_split_kernel(w_ref, wide_ref, gates_ref, wif_ref, *, o_if, o_gate):
    wide_ref[...] = w_ref[:, :o_if].astype(BF16)
    wif_ref[...] = w_ref[:, o_if:o_if + LANES].astype(BF16)
    gates_ref[...] = w_ref[:, o_gate:].astype(BF16)


def _win_split(win, o_if, o_gate, *, tr=256):
    k, width = win.shape
    ngate = width - o_gate
    return pl.pallas_call(
        functools.partial(_win_split_kernel, o_if=o_if, o_gate=o_gate),
        grid=(k // tr,),
        in_specs=[pl.BlockSpec((tr, width), lambda i: (i, 0))],
        out_specs=[
            pl.BlockSpec((tr, o_if), lambda i: (i, 0)),
            pl.BlockSpec((tr, ngate), lambda i: (i, 0)),
            pl.BlockSpec((tr, LANES), lambda i: (i, 0)),
        ],
        out_shape=[
            jax.ShapeDtypeStruct((k, o_if), BF16),
            jax.ShapeDtypeStruct((k, ngate), BF16),
            jax.ShapeDtypeStruct((k, LANES), BF16),
        ],
        compiler_params=_cparams(("parallel",)),
        name="w_in_split",
    )(win)


def _win_kernel(h_ref, g_ref, wa_ref, wb_ref, wif_ref, z_ref, zif_ref, un_ref, *, na):
    j = pl.program_id(1)

    @pl.when(j == 0)
    def _():
        un = _rmsnorm(h_ref[...], g_ref[...]).astype(BF16)
        un_ref[...] = un
        zif_ref[...] = _dot(un, wif_ref[...])

    @pl.when(j < na)
    def _():
        z_ref[...] = _dot(un_ref[...], wa_ref[...])

    @pl.when(j >= na)
    def _():
        z_ref[...] = _dot(un_ref[...], wb_ref[...])


def _win(h, g, wa, wb, wif, *, tm, tn=1024):
    n = h.shape[0]
    na = wa.shape[1] // tn
    nb = wb.shape[1] // tn
    return pl.pallas_call(
        functools.partial(_win_kernel, na=na),
        grid=(n // tm, na + nb),
        in_specs=[
            pl.BlockSpec((tm, D_MODEL), lambda i, j: (i, 0)),
            pl.BlockSpec((1, D_MODEL), lambda i, j: (0, 0)),
            pl.BlockSpec((D_MODEL, tn), lambda i, j: (0, jnp.minimum(j, na - 1))),
            pl.BlockSpec((D_MODEL, tn), lambda i, j: (0, jnp.maximum(j - na, 0))),
            pl.BlockSpec((D_MODEL, LANES), lambda i, j: (0, 0)),
        ],
        out_specs=[
            pl.BlockSpec((tm, tn), lambda i, j: (i, j)),
            pl.BlockSpec((tm, LANES), lambda i, j: (i, 0)),
        ],
        out_shape=[
            jax.ShapeDtypeStruct((n, (na + nb) * tn), F32),
            jax.ShapeDtypeStruct((n, LANES), F32),
        ],
        scratch_shapes=[pltpu.VMEM((tm, D_MODEL), BF16)],
        compiler_params=_cparams(("parallel", "arbitrary")),
        name="w_in",
    )(h, g, wa, wb, wif)


def _s5_prep_kernel(fc_ref, ec_ref, cc_ref, w_ref, f_ref, e_ref):
    n = S5_T * LANES
    sw2 = 2 * S5_SW
    i32 = jnp.int32
    lg_h = S5_GROUP.bit_length() - 1
    lg_p = S5_STATE.bit_length() - 1
    lg_sw = S5_SW.bit_length() - 1

    def tile_mat(k, c, src_col):
        kk = lax.broadcasted_iota(i32, (k, c), 0)
        cc = lax.broadcasted_iota(i32, (k, c), 1)
        return jnp.where(kk == src_col(cc), 1.0, 0.0).astype(BF16)

    def group_mask(r, c, row_group, col_group):
        rr = lax.broadcasted_iota(i32, (r, c), 0)
        cc = lax.broadcasted_iota(i32, (r, c), 1)
        return row_group(rr) == col_group(cc)

    gmask = S5_GPB - 1
    rf = tile_mat(2 * S5_STATE, sw2, lambda c: ((c >> lg_sw) << lg_p) + (c & (S5_STATE - 1)))
    mf = group_mask(n, sw2, lambda r: (r >> lg_h) & gmask, lambda c: (c >> lg_p) & gmask)
    f_hi, f_mid, _ = _split3(fc_ref[0])
    x_hi = jnp.where(mf, _dot(f_hi, rf), 0.0).astype(BF16)
    x_mid = jnp.where(mf, _dot(f_mid, rf), 0.0).astype(BF16)
    f_ref[0] = x_hi
    lg_l = LANES.bit_length() - 1
    re = tile_mat(LANES, n, lambda c: ((c >> lg_l) << lg_h) + (c & (S5_GROUP - 1)))
    me = group_mask(sw2, n, lambda r: (r >> lg_p) & gmask, lambda c: (c >> lg_h) & gmask)
    e_ref[0] = jnp.where(me, _dot(ec_ref[0].astype(BF16), re), 0.0).astype(BF16)
    rc = tile_mat(S5_GROUP, LANES, lambda c: c & (S5_GROUP - 1))
    mc = group_mask(sw2, LANES, lambda r: (r >> lg_p) & gmask, lambda c: c >> lg_h)
    c_hi, c_mid, _ = _split3(cc_ref[0])
    ct_hi = jnp.where(mc, _dot(c_hi, rc), 0.0).astype(BF16)
    ct_mid = jnp.where(mc, _dot(c_mid, rc), 0.0).astype(BF16)
    w_ref[...] = jnp.zeros(w_ref.shape, w_ref.dtype)
    for lag in range(S5_T):
        r0 = (S5_T - 1 - lag) * LANES
        xh = x_hi[r0:r0 + LANES, :]
        xm = x_mid[r0:r0 + LANES, :]
        k = (_dot(xh, ct_hi) + _dot(xm, ct_hi) + _dot(xh, ct_mid)).astype(BF16)
        for s in range(S5_T - lag):
            t = s + lag
            w_ref[0, s * LANES:(s + 1) * LANES, t * LANES:(t + 1) * LANES] = k


def _s5_prep(fc, ec, cc):
    n = S5_T * LANES
    blk = lambda r, c: pl.BlockSpec((1, r, c), lambda j: (j, 0, 0))
    return pl.pallas_call(
        _s5_prep_kernel,
        grid=(S5_NBLK,),
        in_specs=[blk(n, 2 * S5_STATE), blk(2 * S5_SW, LANES), blk(2 * S5_SW, S5_GROUP)],
        out_specs=[blk(n, n), blk(n, 2 * S5_SW), blk(2 * S5_SW, n)],
        out_shape=[
            jax.ShapeDtypeStruct((S5_NBLK, n, n), BF16),
            jax.ShapeDtypeStruct((S5_NBLK, n, 2 * S5_SW), BF16),
            jax.ShapeDtypeStruct((S5_NBLK, 2 * S5_SW, n), BF16),
        ],
        compiler_params=_cparams(("parallel",)),
        name="s5_prep",
    )(fc, ec, cc)


def _s5_kernel(x_ref, w_ref, f_ref, e_ref, lam_ref, d_ref, h0re_ref, h0im_ref,
               y_ref, ore_ref, oim_ref, yi_ref, s_ref, hin_ref, *, nseq, nblk, rt):
    rows = nseq * nblk
    ar = lam_ref[0, 0:1, :]
    ai = lam_ref[0, 1:2, :]

    def load_u(r0):
        return jnp.concatenate(
            [x_ref[pl.ds(r0 * S5_T + s, rt, stride=S5_T), :] for s in range(S5_T)], axis=1)

    for r0 in range(0, rows, rt):
        ub = load_u(r0).astype(BF16)
        yi_ref[r0:r0 + rt, :] = _dot(ub, w_ref[0])
        s_ref[r0:r0 + rt, :] = _dot(ub, f_ref[0])

    if nblk == 1:
        hre = h0re_ref[...]
        him = h0im_ref[...]
        hin_ref[:, :S5_SW] = hre
        hin_ref[:, S5_SW:] = him
        ore_ref[...] = ar * hre - ai * him + s_ref[:, :S5_SW]
        oim_ref[...] = ar * him + ai * hre + s_ref[:, S5_SW:]
    else:
        def body(c, carry):
            new = []
            for b in range(nseq):
                hre, him = carry[2 * b], carry[2 * b + 1]
                row = b * nblk + c
                hin_ref[pl.ds(row, 1), :S5_SW] = hre
                hin_ref[pl.ds(row, 1), S5_SW:] = him
                sre = s_ref[pl.ds(row, 1), :S5_SW]
                sim = s_ref[pl.ds(row, 1), S5_SW:]
                new.append(ar * hre - ai * him + sre)
                new.append(ar * him + ai * hre + sim)
            return tuple(new)

        init = []
        for b in range(nseq):
            init.append(h0re_ref[b:b + 1, :])
            init.append(h0im_ref[b:b + 1, :])
        fin = lax.fori_loop(0, nblk, body, tuple(init))
        for b in range(nseq):
            ore_ref[b:b + 1, :] = fin[2 * b]
            oim_ref[b:b + 1, :] = fin[2 * b + 1]

    dt = jnp.concatenate([d_ref[...]] * S5_T, axis=1)
    for r0 in range(0, rows, rt):
        yo = _dot(hin_ref[r0:r0 + rt, :].astype(BF16), e_ref[0])
        y = jax.nn.gelu(yi_ref[r0:r0 + rt, :] + yo + load_u(r0) * dt)
        for t in range(S5_T):
            y_ref[pl.ds(r0 * S5_T + t, rt, stride=S5_T), :] = y[:, t * LANES:(t + 1) * LANES]


def _s5(z, w, f, e, lam, d, h0re, h0im, *, nseq, seqlen):
    nblk = seqlen // S5_T
    rows = nseq * nblk
    rt = min(rows, 256)
    n = S5_T * LANES
    tok = nseq * seqlen
    kern = functools.partial(_s5_kernel, nseq=nseq, nblk=nblk, rt=rt)
    return pl.pallas_call(
        kern,
        grid=(S5_NBLK,),
        in_specs=[
            pl.BlockSpec((tok, LANES), lambda j: (0, j)),
            pl.BlockSpec((1, n, n), lambda j: (j, 0, 0)),
            pl.BlockSpec((1, n, 2 * S5_SW), lambda j: (j, 0, 0)),
            pl.BlockSpec((1, 2 * S5_SW, n), lambda j: (j, 0, 0)),
            pl.BlockSpec((1, 2, S5_SW), lambda j: (j, 0, 0)),
            pl.BlockSpec((1, LANES), lambda j: (0, j)),
            pl.BlockSpec((nseq, S5_SW), lambda j: (0, j)),
            pl.BlockSpec((nseq, S5_SW), lambda j: (0, j)),
        ],
        out_specs=[
            pl.BlockSpec((tok, LANES), lambda j: (0, j)),
            pl.BlockSpec((nseq, S5_SW), lambda j: (0, j)),
            pl.BlockSpec((nseq, S5_SW), lambda j: (0, j)),
        ],
        out_shape=[
            jax.ShapeDtypeStruct((tok, S5_WIDTH), F32),
            jax.ShapeDtypeStruct((nseq, S5_GROUPS * S5_STATE), F32),
            jax.ShapeDtypeStruct((nseq, S5_GROUPS * S5_STATE), F32),
        ],
        scratch_shapes=[
            pltpu.VMEM((rows, n), F32),
            pltpu.VMEM((rows, 2 * S5_SW), F32),
            pltpu.VMEM((rows, 2 * S5_SW), F32),
        ],
        compiler_params=_cparams(("parallel",)),
        name="s5",
    )(z, w, f, e, lam, d, h0re, h0im)


def _s5_operators(a_re, a_im, log_dt, b_re, b_im, c_re, c_im):
    dt = jnp.exp(log_dt)[:, None]
    mag = jnp.exp(a_re * dt)
    abar_re = mag * jnp.cos(a_im * dt)
    abar_im = mag * jnp.sin(a_im * dt)
    nr, ni = abar_re - 1.0, abar_im
    den = a_re * a_re + a_im * a_im
    q_re = (nr * a_re + ni * a_im) / den
    q_im = (ni * a_re - nr * a_im) / den
    bb_re = q_re[:, :, None] * b_re - q_im[:, :, None] * b_im
    bb_im = q_re[:, :, None] * b_im + q_im[:, :, None] * b_re
    taus = jnp.arange(S5_T + 1, dtype=F32)[:, None, None]
    pmag = jnp.exp(taus * (a_re * dt)[None])
    pw_re = pmag * jnp.cos(taus * (a_im * dt)[None])
    pw_im = pmag * jnp.sin(taus * (a_im * dt)[None])
    n = S5_T * LANES

    lags = (S5_T - 1) - jnp.arange(S5_T, dtype=F32)[:, None, None]
    lmag = jnp.exp(lags * (a_re * dt)[None])
    lr = (lmag * jnp.cos(lags * (a_im * dt)[None]))[:, :, :, None]
    li = (lmag * jnp.sin(lags * (a_im * dt)[None]))[:, :, :, None]
    fb = jnp.stack([lr * bb_re[None] - li * bb_im[None], lr * bb_im[None] + li * bb_re[None]])
    fc = fb.reshape(2, S5_T, S5_NBLK, S5_GPB, S5_STATE, S5_GROUP).transpose(2, 1, 3, 5, 0, 4)
    fc = fc.reshape(S5_NBLK, n, 2 * S5_STATE)

    cp_re = jnp.swapaxes(c_re, -1, -2)[None]
    cp_im = jnp.swapaxes(c_im, -1, -2)[None]
    tr = pw_re[1:][:, :, :, None]
    ti = pw_im[1:][:, :, :, None]
    ce = jnp.stack([cp_re * tr - cp_im * ti, -(cp_re * ti + cp_im * tr)])
    ec = ce.reshape(2, S5_T, S5_NBLK, S5_GPB, S5_STATE, S5_GROUP).transpose(2, 0, 3, 4, 1, 5)
    ec = ec.reshape(S5_NBLK, 2 * S5_SW, LANES)

    cc = jnp.stack([cp_re[0], -cp_im[0]]).reshape(2, S5_NBLK, S5_GPB, S5_STATE, S5_GROUP)
    cc = cc.transpose(1, 0, 2, 3, 4).reshape(S5_NBLK, 2 * S5_SW, S5_GROUP)

    lam = jnp.stack([pw_re[S5_T].reshape(S5_NBLK, S5_SW), pw_im[S5_T].reshape(S5_NBLK, S5_SW)], axis=1)
    return fc, ec, cc, lam


def _mlstm_kernel(qk_ref, v_ref, o_ref, if_ref, cw_ref, cb_ref, bif_ref, ng_ref,
                  buf0_ref, c0_ref, n0_ref, m0_ref,
                  y_ref, bufo_ref, co_ref, no_ref, mo_ref, xp_ref, *, nb, lc, carry):
    if carry:
        @pl.when(pl.program_id(1) == 0)
        def _():
            bufo_ref[...] = buf0_ref[...]
            co_ref[...] = c0_ref[...]
            no_ref[...] = n0_ref[...]
            mo_ref[...] = m0_ref[...]
        bufs_ref, cs_ref, ns_ref, ms_ref = bufo_ref, co_ref, no_ref, mo_ref
    else:
        bufs_ref, cs_ref, ns_ref, ms_ref = buf0_ref, c0_ref, n0_ref, m0_ref

    lp = max(lc, LANES)
    head_row = lax.broadcasted_iota(jnp.int32, (HEADS, DQK), 0)
    head_lane = lax.broadcasted_iota(jnp.int32, (1, HEADS), 1)
    row = lax.broadcasted_iota(jnp.int32, (lc, lc), 0)
    col = lax.broadcasted_iota(jnp.int32, (lc, lc), 1)
    causal = row >= col
    tril = jnp.where(causal, 1.0, 0.0).astype(BF16)
    lane = lax.broadcasted_iota(jnp.int32, (lc, LANES), 1)
    is_i = lane < HEADS
    is_f = jnp.logical_and(lane >= HEADS, lane < 2 * HEADS)

    for s in range(nb):
        r0 = s * lc
        xp_ref[s, 5:8, :] = bufs_ref[s]
        xp_ref[s, 8:8 + lc, :] = qk_ref[r0:r0 + lc, :]
        conv = cb_ref[...] + sum(xp_ref[s, 5 + j:5 + j + lc, :] * cw_ref[j:j + 1, :] for j in range(CONV_W))
        bufo_ref[s] = xp_ref[s, 5 + lc:8 + lc, :]
        n_all = ns_ref[s]
        m_all = ms_ref[s]
        n_new_all = jnp.zeros((HEADS, DQK), F32)
        m_new_all = jnp.zeros((1, HEADS), F32)
        qk = conv * jax.nn.sigmoid(conv)
        q = qk[:, :QK_WIDTH // 2] * (DQK ** -0.5)
        k = qk[:, QK_WIDTH // 2:]

        gate = if_ref[r0:r0 + lc, :] + bif_ref[...]
        lf = jnp.where(is_f, jax.nn.log_sigmoid(gate), 0.0)
        parts = _split3(lf)
        bcum = _dot(tril, parts[0]) + _dot(tril, parts[1]) + _dot(tril, parts[2])
        pc = jnp.where(is_i, gate, bcum)
        if lc < lp:
            pc_t = jnp.concatenate([pc, jnp.zeros((lp - lc, LANES), F32)], axis=0).T[:, :lc]
        else:
            pc_t = pc.T

        for h in range(HEADS):
            i_col = pc[:, h:h + 1]
            b_col = pc[:, HEADS + h:HEADS + h + 1]
            i_row = pc_t[h:h + 1, :]
            b_row = pc_t[HEADS + h:HEADS + h + 1, :]
            m0 = m_all[:, h:h + 1]
            logw = jnp.where(causal, b_col - b_row + i_row, -jnp.inf)
            g = b_col + m0
            m = jnp.maximum(g, jnp.max(logw, axis=-1, keepdims=True))
            w = jnp.exp(logw - m)
            inter = jnp.exp(g - m)
            qh = q[:, h * DQK:(h + 1) * DQK]
            kh = k[:, h * DQK:(h + 1) * DQK]
            vh = v_ref[r0:r0 + lc, h * DV:(h + 1) * DV]
            qb = qh.astype(BF16)
            vb = vh.astype(BF16)
            c_prev = cs_ref[s, h]
            n_prev = n_all[h:h + 1, :]
            sc = _dot_nt(qb, kh.astype(BF16)) * w
            num = _dot(sc.astype(BF16), vb) + inter * _dot_nt(qb, c_prev.astype(BF16))
            den = jnp.sum(sc, axis=-1, keepdims=True) + inter * jnp.sum(qh * n_prev, axis=-1, keepdims=True)
            hh = num / jnp.maximum(jnp.abs(den), jnp.exp(-m))
            hh = hh * lax.rsqrt(jnp.mean(hh * hh, axis=-1, keepdims=True) + EPS)
            og = jax.nn.sigmoid(o_ref[r0:r0 + lc, h * DV:(h + 1) * DV])
            y_ref[r0:r0 + lc, h * DV:(h + 1) * DV] = hh * ng_ref[:, h * DV:(h + 1) * DV] * og

            b_last = b_col[lc - 1:lc, :]
            m_new = m[lc - 1:lc, :]
            decay = jnp.exp(b_last + m0 - m_new)
            kw = kh * jnp.exp(b_last - b_col + i_col - m_new)
            co_ref[s, h] = decay * c_prev + _dot_tn(vb, kw.astype(BF16))
            n_new = decay * n_prev + jnp.sum(kw, axis=0, keepdims=True)
            n_new_all = jnp.where(head_row == h, n_new, n_new_all)
            m_new_all = jnp.where(head_lane == h, m_new, m_new_all)
        no_ref[s] = n_new_all
        mo_ref[s] = m_new_all


def _mlstm(z, zif, cw, cb, bif, ng, buf0, c0, n0, m0, *, nseq, seqlen, nb, lc):
    tok = nseq * seqlen
    nchunk = seqlen // lc
    rows = nb * lc
    tok_map = lambda col: (lambda i, c: (i * nchunk + c, col))
    const2 = lambda i, c: (0, 0)
    st3 = lambda i, c: (i, 0, 0)
    st4 = lambda i, c: (i, 0, 0, 0)
    kern = functools.partial(_mlstm_kernel, nb=nb, lc=lc, carry=nchunk > 1)
    return pl.pallas_call(
        kern,
        grid=(nseq // nb, nchunk),
        in_specs=[
            pl.BlockSpec((rows, QK_WIDTH), tok_map(1)),
            pl.BlockSpec((rows, V_WIDTH), tok_map(2)),
            pl.BlockSpec((rows, V_WIDTH), tok_map(3)),
            pl.BlockSpec((rows, LANES), tok_map(0)),
            pl.BlockSpec((CONV_W, QK_WIDTH), const2),
            pl.BlockSpec((1, QK_WIDTH), const2),
            pl.BlockSpec((1, LANES), const2),
            pl.BlockSpec((1, V_WIDTH), const2),
            pl.BlockSpec((nb, CONV_W - 1, QK_WIDTH), st3),
            pl.BlockSpec((nb, HEADS, DV, DQK), st4),
            pl.BlockSpec((nb, HEADS, DQK), st3),
            pl.BlockSpec((nb, 1, HEADS), st3),
        ],
        out_specs=[
            pl.BlockSpec((rows, V_WIDTH), tok_map(0)),
            pl.BlockSpec((nb, CONV_W - 1, QK_WIDTH), st3),
            pl.BlockSpec((nb, HEADS, DV, DQK), st4),
            pl.BlockSpec((nb, HEADS, DQK), st3),
            pl.BlockSpec((nb, 1, HEADS), st3),
        ],
        out_shape=[
            jax.ShapeDtypeStruct((tok, V_WIDTH), F32),
            jax.ShapeDtypeStruct((nseq, CONV_W - 1, QK_WIDTH), F32),
            jax.ShapeDtypeStruct((nseq, HEADS, DV, DQK), F32),
            jax.ShapeDtypeStruct((nseq, HEADS, DQK), F32),
            jax.ShapeDtypeStruct((nseq, 1, HEADS), F32),
        ],
        scratch_shapes=[pltpu.VMEM((nb, lc + 8, QK_WIDTH), F32)],
        compiler_params=_cparams(("parallel", "arbitrary")),
        name="mlstm",
    )(z, z, z, zif, cw, cb, bif, ng, buf0, c0, n0, m0)


def _mlstm_step_kernel(qk_ref, v_ref, o_ref, if_ref, cw_ref, cb_ref, bif_ref, ng_ref,
                       buf0_ref, c0_ref, n0x_ref, m0x_ref,
                       y_ref, bufo_ref, co_ref, nox_ref, mox_ref, xp_ref, conv_ref, *, nb, lc):
    rows = nb * lc
    lg = lc.bit_length() - 1
    i32 = jnp.int32
    row = lax.broadcasted_iota(i32, (rows, rows), 0)
    col = lax.broadcasted_iota(i32, (rows, rows), 1)
    same = (row >> lg) == (col >> lg)
    causal = jnp.logical_and(same, row >= col)
    same_b = jnp.where(same, 1.0, 0.0).astype(BF16)
    tril_b = jnp.where(causal, 1.0, 0.0).astype(BF16)
    lane = lax.broadcasted_iota(i32, (rows, LANES), 1)
    is_i = lane < HEADS
    is_f = jnp.logical_and(lane >= HEADS, lane < 2 * HEADS)

    def seg_dot(mat, x):
        hi, mid, lo = _split3(x)
        return _dot(mat, hi) + _dot(mat, mid) + _dot(mat, lo)

    for s in range(nb):
        xp_ref[s, 5:8, :] = buf0_ref[s]
        xp_ref[s, 8:8 + lc, :] = qk_ref[s * lc:(s + 1) * lc, :]
        conv_ref[s * lc:(s + 1) * lc, :] = cb_ref[...] + sum(
            xp_ref[s, 5 + j:5 + j + lc, :] * cw_ref[j:j + 1, :] for j in range(CONV_W))
        bufo_ref[s] = xp_ref[s, 5 + lc:8 + lc, :]
    conv = conv_ref[...]
    qk = conv * jax.nn.sigmoid(conv)
    q = qk[:, :QK_WIDTH // 2] * (DQK ** -0.5)
    k = qk[:, QK_WIDTH // 2:]

    gate = if_ref[...] + bif_ref[...]
    lf = jnp.where(is_f, jax.nn.log_sigmoid(gate), 0.0)
    bcum = seg_dot(tril_b, lf)
    btot = seg_dot(same_b, lf)
    pc = jnp.where(is_i, gate, bcum)
    if rows < LANES:
        pc_t = jnp.concatenate([pc, jnp.zeros((LANES - rows, LANES), F32)], axis=0).T[:, :rows]
    else:
        pc_t = pc.T
    m0x = m0x_ref[...]
    mox = jnp.zeros((rows, LANES), F32)

    for h in range(HEADS):
        i_col = pc[:, h:h + 1]
        b_col = pc[:, HEADS + h:HEADS + h + 1]
        i_row = pc_t[h:h + 1, :]
        b_row = pc_t[HEADS + h:HEADS + h + 1, :]
        b_last = btot[:, HEADS + h:HEADS + h + 1]
        m0 = m0x[:, h:h + 1]
        lw = b_col - b_row + i_row
        logw = jnp.where(causal, lw, -jnp.inf)
        g = b_col + m0
        m = jnp.maximum(g, jnp.max(logw, axis=-1, keepdims=True))
        w = jnp.exp(logw - m)
        inter = jnp.exp(g - m)
        lw_end = jnp.where(same, b_last - b_row + i_row, -jnp.inf)
        m_new = jnp.maximum(b_last + m0, jnp.max(lw_end, axis=-1, keepdims=True))
        decay = jnp.exp(b_last + m0 - m_new)
        qh = q[:, h * DQK:(h + 1) * DQK]
        kh = k[:, h * DQK:(h + 1) * DQK]
        qb = qh.astype(BF16)
        vb = v_ref[:, h * DV:(h + 1) * DV].astype(BF16)
        n_prev = n0x_ref[:, h * DQK:(h + 1) * DQK]
        sc = _dot_nt(qb, kh.astype(BF16)) * w
        carried = jnp.concatenate(
            [_dot_nt(qb[s * lc:(s + 1) * lc], c0_ref[s, h].astype(BF16)) for s in range(nb)], axis=0)
        num = _dot(sc.astype(BF16), vb) + inter * carried
        den = jnp.sum(sc, axis=-1, keepdims=True) + inter * jnp.sum(qh * n_prev, axis=-1, keepdims=True)
        hh = num / jnp.maximum(jnp.abs(den), jnp.exp(-m))
        hh = hh * lax.rsqrt(jnp.mean(hh * hh, axis=-1, keepdims=True) + EPS)
        og = jax.nn.sigmoid(o_ref[:, h * DV:(h + 1) * DV])
        y_ref[:, h * DV:(h + 1) * DV] = hh * ng_ref[:, h * DV:(h + 1) * DV] * og

        kw = kh * jnp.exp(b_last - b_col + i_col - m_new)
        kwb = kw.astype(BF16)
        for s in range(nb):
            r0 = s * lc
            co_ref[s, h] = decay[r0:r0 + 1, :] * c0_ref[s, h] + _dot_tn(vb[r0:r0 + lc], kwb[r0:r0 + lc])
        nox_ref[:, h * DQK:(h + 1) * DQK] = decay * n_prev + seg_dot(same_b, kw)
        mox = jnp.where(lane == h, m_new, mox)
    mox_ref[...] = mox


def _mlstm_step(z, zif, cw, cb, bif, ng, buf0, c0, n0, m0, *, nseq, seqlen, nb):
    lc = seqlen
    assert lc & (lc - 1) == 0 and lc >= CONV_W - 1
    tok = nseq * seqlen
    rows = nb * lc
    n0x = jnp.repeat(n0.reshape(nseq, HEADS * DQK), lc, axis=0)
    m0x = jnp.repeat(jnp.pad(m0, ((0, 0), (0, LANES - HEADS))), lc, axis=0)
    tok_map = lambda col: (lambda i: (i, col))
    const2 = lambda i: (0, 0)
    st3 = lambda i: (i, 0, 0)
    st4 = lambda i: (i, 0, 0, 0)
    y, buf, c, nox, mox = pl.pallas_call(
        functools.partial(_mlstm_step_kernel, nb=nb, lc=lc),
        grid=(nseq // nb,),
        in_specs=[
            pl.BlockSpec((rows, QK_WIDTH), tok_map(1)),
            pl.BlockSpec((rows, V_WIDTH), tok_map(2)),
            pl.BlockSpec((rows, V_WIDTH), tok_map(3)),
            pl.BlockSpec((rows, LANES), tok_map(0)),
            pl.BlockSpec((CONV_W, QK_WIDTH), const2),
            pl.BlockSpec((1, QK_WIDTH), const2),
            pl.BlockSpec((1, LANES), const2),
            pl.BlockSpec((1, V_WIDTH), const2),
            pl.BlockSpec((nb, CONV_W - 1, QK_WIDTH), st3),
            pl.BlockSpec((nb, HEADS, DV, DQK), st4),
            pl.BlockSpec((rows, HEADS * DQK), tok_map(0)),
            pl.BlockSpec((rows, LANES), tok_map(0)),
        ],
        out_specs=[
            pl.BlockSpec((rows, V_WIDTH), tok_map(0)),
            pl.BlockSpec((nb, CONV_W - 1, QK_WIDTH), st3),
            pl.BlockSpec((nb, HEADS, DV, DQK), st4),
            pl.BlockSpec((rows, HEADS * DQK), tok_map(0)),
            pl.BlockSpec((rows, LANES), tok_map(0)),
        ],
        out_shape=[
            jax.ShapeDtypeStruct((tok, V_WIDTH), F32),
            jax.ShapeDtypeStruct((nseq, CONV_W - 1, QK_WIDTH), F32),
            jax.ShapeDtypeStruct((nseq, HEADS, DV, DQK), F32),
            jax.ShapeDtypeStruct((tok, HEADS * DQK), F32),
            jax.ShapeDtypeStruct((tok, LANES), F32),
        ],
        scratch_shapes=[pltpu.VMEM((nb, lc + 8, QK_WIDTH), F32), pltpu.VMEM((rows, QK_WIDTH), F32)],
        compiler_params=_cparams(("parallel",)),
        name="mlstm_step",
    )(z, z, z, zif, cw, cb, bif, ng, buf0, c0, n0x, m0x)
    n = nox[::lc].reshape(nseq, HEADS, DQK)
    m = mox[::lc, :HEADS]
    return y, buf, c, n, m


def _merge_kernel(h_ref, ys_ref, ym_ref, g1_ref, g2_ref, wglu_ref, wbs_ref, wbm_ref, wo_ref, o_ref):
    ys = ys_ref[...]
    glu = ys * jax.nn.sigmoid(_dot(ys.astype(BF16), wglu_ref[...]))
    a = _dot(glu.astype(BF16), wbs_ref[...])
    b = _dot(ym_ref[...].astype(BF16), wbm_ref[...])
    merged = jax.nn.sigmoid(g1_ref[...]) * a + jax.nn.sigmoid(g2_ref[...]) * b
    o_ref[...] = h_ref[...] + _dot(merged.astype(BF16), wo_ref[...])


def _merge(h, ys, ym, z, wglu, wbs, wbm, wo, *, n, tm):
    row = lambda i: (i, 0)
    const = lambda i: (0, 0)
    resident = lambda shape: pl.BlockSpec(shape, const, pipeline_mode=pl.Buffered(1))
    return pl.pallas_call(
        _merge_kernel,
        grid=(n // tm,),
        in_specs=[
            pl.BlockSpec((tm, D_MODEL), row),
            pl.BlockSpec((tm, S5_WIDTH), row),
            pl.BlockSpec((tm, V_WIDTH), row),
            pl.BlockSpec((tm, D_MODEL), lambda i: (i, 2)),
            pl.BlockSpec((tm, D_MODEL), lambda i: (i, 3)),
            resident((S5_WIDTH, S5_WIDTH)),
            resident((S5_WIDTH, D_MODEL)),
            resident((V_WIDTH, D_MODEL)),
            resident((D_MODEL, D_MODEL)),
        ],
        out_specs=pl.BlockSpec((tm, D_MODEL), row),
        out_shape=jax.ShapeDtypeStruct((n, D_MODEL), F32),
        compiler_params=_cparams(("parallel",)),
        name="merge",
    )(h, ys, ym, z, z, wglu, wbs, wbm, wo)


def kernel(x_prompt, x_sample, state_s5_re, state_s5_im, state_mlstm_C, state_mlstm_n, state_mlstm_m,
           state_mlstm_conv, meta_tokens, ffn1_norm, ffn1_w_gate, ffn1_w_up, ffn1_w_down, mix_norm, w_in,
           s5_A_re, s5_A_im, s5_log_dt, s5_B_re, s5_B_im, s5_C_re, s5_C_im, s5_D, s5_w_glu,
           mlstm_conv_w, mlstm_conv_b, mlstm_b_i, mlstm_b_f, mlstm_norm, w_branch_s5, w_branch_mlstm,
           w_out, ffn2_norm, ffn2_w_gate, ffn2_w_up, ffn2_w_down, final_norm):
    nbatch, seq, _ = x_prompt.shape
    nsamp, sseq, _ = x_sample.shape
    l = 0

    w1g, w1u, w1d = ffn1_w_gate[l], ffn1_w_up[l], ffn1_w_down[l]
    w2g, w2u, w2d = ffn2_w_gate[l], ffn2_w_up[l], ffn2_w_down[l]
    o_if = S5_WIDTH + QK_WIDTH + 2 * V_WIDTH
    o_gate = o_if + 2 * HEADS
    w_wide, w_gates, w_if = _win_split(w_in[l], o_if, o_gate)
    wglu, wbs, wbm, wo = (w[l].astype(BF16) for w in (s5_w_glu, w_branch_s5, w_branch_mlstm, w_out))
    g1 = ffn1_norm[l][None]
    gm = mix_norm[l][None]
    g2 = ffn2_norm[l][None]
    gf = final_norm[None]
    bif = jnp.pad(jnp.concatenate([mlstm_b_i[l], mlstm_b_f[l]]), (0, LANES - 2 * HEADS))[None]
    cw = mlstm_conv_w[l]
    cb = mlstm_conv_b[l][None]
    ng = mlstm_norm[l][None]
    d_skip = s5_D[l][None]

    fc, ec, cc, lam = _s5_operators(s5_A_re[l], s5_A_im[l], s5_log_dt[l], s5_B_re[l], s5_B_im[l],
                                    s5_C_re[l], s5_C_im[l])
    w_toe, f_bf, e_bf = _s5_prep(fc, ec, cc)

    def front(x, tm):
        n = x.shape[0]
        h1 = _ffn(x, g1, w1g, w1u, w1d, n=n, tm=tm)
        z, zif = _win(h1, gm, w_wide, w_gates, w_if, tm=tm)
        return h1, z, zif

    def mixers(z, zif, s5_state, ml_state, *, nseq, seqlen, nb, lc):
        ys, sre, sim = _s5(z, w_toe, f_bf, e_bf, lam, d_skip, s5_state[0], s5_state[1], nseq=nseq, seqlen=seqlen)
        buf0, c0, n0, m0 = ml_state
        if lc == seqlen and nb > 1:
            ym, buf, c, n, m = _mlstm_step(z, zif, cw, cb, bif, ng, buf0, c0, n0, m0,
                                           nseq=nseq, seqlen=seqlen, nb=nb)
        else:
            ym, buf, c, n, m = _mlstm(z, zif, cw, cb, bif, ng, buf0, c0, n0, m0.reshape(nseq, 1, HEADS),
                                      nseq=nseq, seqlen=seqlen, nb=nb, lc=lc)
            m = m.reshape(nseq, HEADS)
        return ys, ym, (sre, sim), (buf, c, n, m)

    def back(h1, ys, ym, z, n):
        h2 = _merge(h1, ys, ym, z, wglu, wbs, wbm, wo, n=n, tm=MERGE_TM)
        return _ffn(h2, g2, w2g, w2u, w2d, gf, n=n, tm=FFN_TM)

    ntok_s = nsamp * sseq
    x_sm = jnp.concatenate([x_sample.reshape(ntok_s, D_MODEL), meta_tokens], axis=0)
    h1_sm, z_sm, zif_sm = front(x_sm, ntok_s + N_META)

    z_m = jnp.tile(z_sm[ntok_s:], (nbatch, 1))
    zif_m = jnp.tile(zif_sm[ntok_s:], (nbatch, 1))
    zeros = lambda *s: jnp.zeros((nbatch,) + s, F32)
    _, _, s5_m, ml_m = mixers(
        z_m, zif_m, (zeros(S5_GROUPS * S5_STATE), zeros(S5_GROUPS * S5_STATE)),
        (zeros(CONV_W - 1, QK_WIDTH), zeros(HEADS, DV, DQK), zeros(HEADS, DQK), zeros(HEADS)),
        nseq=nbatch, seqlen=N_META, nb=1, lc=N_META)

    ntok_p = nbatch * seq
    h1_p, z_p, zif_p = front(x_prompt.reshape(ntok_p, D_MODEL), FFN_TM)
    ys_p, ym_p, s5_p, ml_p = mixers(z_p, zif_p, s5_m, ml_m, nseq=nbatch, seqlen=seq, nb=1, lc=256)
    y_p = back(h1_p, ys_p, ym_p, z_p, ntok_p)

    ys_s, ym_s, s5_s, ml_s = mixers(
        z_sm, zif_sm,
        (state_s5_re[l].reshape(nsamp, -1), state_s5_im[l].reshape(nsamp, -1)),
        (state_mlstm_conv[l], state_mlstm_C[l], state_mlstm_n[l], state_mlstm_m[l]),
        nseq=nsamp, seqlen=sseq, nb=STEP_NB, lc=sseq)
    y_s = back(h1_sm, ys_s, ym_s, z_sm, ntok_s)

    def pack(n, s5_st, ml_st):
        buf, c, nn, m = ml_st
        return (s5_st[0].reshape(1, n, S5_GROUPS, S5_STATE), s5_st[1].reshape(1, n, S5_GROUPS, S5_STATE),
                c[None], nn[None], m[None], buf[None])

    return ((y_p.reshape(nbatch, seq, D_MODEL), y_s.reshape(nsamp, sseq, D_MODEL))
            + pack(nbatch, s5_p, ml_p) + pack(nsamp, s5_s, ml_s))
```

```python
import functools

import jax
import jax.numpy as jnp
from jax import lax
from jax.experimental import pallas as pl
from jax.experimental.pallas import tpu as pltpu

F32 = jnp.float32
BF16 = jnp.bfloat16

D_MODEL = 2048
D_FF = 5632
N_META = 16
S5_WIDTH = 1024
S5_GROUP = 16
S5_GROUPS = 64
S5_STATE = 64
HEADS = 4
DQK = 128
DV = 256
QK_WIDTH = 1024
V_WIDTH = 1024
CONV_W = 4
N_BRANCH = 2
EPS = 1e-6

LANES = 128
S5_T = 8
S5_GPB = LANES // S5_GROUP
S5_NBLK = S5_WIDTH // LANES
S5_SW = S5_GPB * S5_STATE
VMEM_LIMIT = 58 * 1024 * 1024
FFN_TM = 1024
MERGE_TM = 256
WIN_CAST_ROWS = 456
STEP_NB = 16


def _cparams(sem):
    return pltpu.CompilerParams(dimension_semantics=sem, vmem_limit_bytes=VMEM_LIMIT)


def _rmsnorm(x, g):
    ms = jnp.mean(x * x, axis=-1, keepdims=True)
    return (x * lax.rsqrt(ms + EPS)) * g


def _dot(a, b):
    return jnp.dot(a, b, preferred_element_type=F32)


def _dot_nt(a, b):
    return lax.dot_general(a, b, (((1,), (1,)), ((), ())), preferred_element_type=F32)


def _dot_tn(a, b):
    return lax.dot_general(a, b, (((0,), (0,)), ((), ())), preferred_element_type=F32)


def _split3(x):
    hi = x.astype(BF16)
    r = x - hi.astype(F32)
    mid = r.astype(BF16)
    lo = (r - mid.astype(F32)).astype(BF16)
    return hi, mid, lo


def _ffn_kernel(x_ref, g_ref, wg_ref, wu_ref, wd_ref, *rest, final_norm):
    if final_norm:
        fg_ref, o_ref, xn_ref = rest
    else:
        o_ref, xn_ref = rest
    j = pl.program_id(1)

    @pl.when(j == 0)
    def _():
        xn_ref[...] = _rmsnorm(x_ref[...], g_ref[...]).astype(BF16)
        o_ref[...] = jnp.zeros(o_ref.shape, F32)

    xn = xn_ref[...]
    gt = _dot(xn, wg_ref[...].astype(BF16))
    up = _dot(xn, wu_ref[...].astype(BF16))
    act = (gt * jax.nn.sigmoid(gt) * up).astype(BF16)
    o_ref[...] += _dot(act, wd_ref[...].astype(BF16))

    @pl.when(j == pl.num_programs(1) - 1)
    def _():
        h = x_ref[...] + 0.5 * o_ref[...]
        if final_norm:
            h = _rmsnorm(h, fg_ref[...])
        o_ref[...] = h


def _ffn(x, g, wg, wu, wd, final_g=None, *, n, tm, tf=256, x_buffers=2):
    in_specs = [
        pl.BlockSpec((tm, D_MODEL), lambda i, j: (i, 0), pipeline_mode=pl.Buffered(x_buffers)),
        pl.BlockSpec((1, D_MODEL), lambda i, j: (0, 0)),
        pl.BlockSpec((D_MODEL, tf), lambda i, j: (0, j)),
        pl.BlockSpec((D_MODEL, tf), lambda i, j: (0, j)),
        pl.BlockSpec((tf, D_MODEL), lambda i, j: (j, 0)),
    ]
    args = [x, g, wg, wu, wd]
    if final_g is not None:
        in_specs.append(pl.BlockSpec((1, D_MODEL), lambda i, j: (0, 0)))
        args.append(final_g)
    return pl.pallas_call(
        functools.partial(_ffn_kernel, final_norm=final_g is not None),
        grid=(n // tm, D_FF // tf),
        in_specs=in_specs,
        out_specs=pl.BlockSpec((tm, D_MODEL), lambda i, j: (i, 0)),
        out_shape=jax.ShapeDtypeStruct((n, D_MODEL), F32),
        scratch_shapes=[pltpu.VMEM((tm, D_MODEL), BF16)],
        compiler_params=_cparams(("parallel", "arbitrary")),
        name="ffn",
    )(*args)


def _cast_rows_kernel(w_ref, o_ref):
    o_ref[0] = w_ref[...].astype(BF16)


def _win_halves(wt, o_gate, *, tr):
    rows, k = wt.shape
    per = o_gate // tr
    assert per * tr == o_gate and tr % 8 == 0 and rows <= 2 * o_gate
    return pl.pallas_call(
        _cast_rows_kernel,
        grid=(2 * per,),
        in_specs=[pl.BlockSpec((tr, k), lambda i: (i, 0))],
        out_specs=pl.BlockSpec((1, tr, k), lambda i: (i // per, i % per, 0)),
        out_shape=jax.ShapeDtypeStruct((2, o_gate, k), BF16),
        compiler_params=_cparams(("parallel",)),
        name="w_in_cast",
    )(wt)


def _win_kernel(h_ref, g_ref, wa_ref, wb_ref, wif_ref, z_ref, zif_ref, un_ref, *, na):
    j = pl.program_id(1)

    @pl.when(j == 0)
    def _():
        un = _rmsnorm(h_ref[...], g_ref[...]).astype(BF16)
        un_ref[...] = un
        zif_ref[...] = _dot_nt(un, wif_ref[...].astype(BF16))

    @pl.when(j < na)
    def _():
        z_ref[...] = _dot_nt(un_ref[...], wa_ref[0])

    @pl.when(j >= na)
    def _():
        z_ref[...] = _dot_nt(un_ref[...], wb_ref[0])


def _win(h, g, w2, wt, *, o_if, ngate, tm, tn=1024):
    n = h.shape[0]
    na = o_if // tn
    nb = ngate // tn
    return pl.pallas_call(
        functools.partial(_win_kernel, na=na),
        grid=(n // tm, na + nb),
        in_specs=[
            pl.BlockSpec((tm, D_MODEL), lambda i, j: (i, 0)),
            pl.BlockSpec((1, D_MODEL), lambda i, j: (0, 0)),
            pl.BlockSpec((1, tn, D_MODEL), lambda i, j: (0, jnp.minimum(j, na - 1), 0)),
            pl.BlockSpec((1, tn, D_MODEL), lambda i, j: (1, jnp.maximum(j - na, 0), 0)),
            pl.BlockSpec((LANES, D_MODEL), lambda i, j: (o_if // LANES, 0)),
        ],
        out_specs=[
            pl.BlockSpec((tm, tn), lambda i, j: (i, j)),
            pl.BlockSpec((tm, LANES), lambda i, j: (i, 0)),
        ],
        out_shape=[
            jax.ShapeDtypeStruct((n, (na + nb) * tn), F32),
            jax.ShapeDtypeStruct((n, LANES), F32),
        ],
        scratch_shapes=[pltpu.VMEM((tm, D_MODEL), BF16)],
        compiler_params=_cparams(("parallel", "arbitrary")),
        name="w_in",
    )(h, g, w2, w2, wt)


def _s5_prep_kernel(fc_ref, ec_ref, cc_ref, w_ref, f_ref, e_ref):
    n = S5_T * LANES
    sw2 = 2 * S5_SW
    i32 = jnp.int32
    lg_h = S5_GROUP.bit_length() - 1
    lg_p = S5_STATE.bit_length() - 1
    lg_sw = S5_SW.bit_length() - 1

    def tile_mat(k, c, src_col):
        kk = lax.broadcasted_iota(i32, (k, c), 0)
        cc = lax.broadcasted_iota(i32, (k, c), 1)
        return jnp.where(kk == src_col(cc), 1.0, 0.0).astype(BF16)

    def group_mask(r, c, row_group, col_group):
        rr = lax.broadcasted_iota(i32, (r, c), 0)
        cc = lax.broadcasted_iota(i32, (r, c), 1)
        return row_group(rr) == col_group(cc)

    gmask = S5_GPB - 1
    rf = tile_mat(2 * S5_STATE, sw2, lambda c: ((c >> lg_sw) << lg_p) + (c & (S5_STATE - 1)))
    mf = group_mask(n, sw2, lambda r: (r >> lg_h) & gmask, lambda c: (c >> lg_p) & gmask)
    f_hi, f_mid, _ = _split3(fc_ref[0])
    x_hi = jnp.where(mf, _dot(f_hi, rf), 0.0).astype(BF16)
    x_mid = jnp.where(mf, _dot(f_mid, rf), 0.0).astype(BF16)
    f_ref[0] = x_hi
    lg_l = LANES.bit_length() - 1
    re = tile_mat(LANES, n, lambda c: ((c >> lg_l) << lg_h) + (c & (S5_GROUP - 1)))
    me = group_mask(sw2, n, lambda r: (r >> lg_p) & gmask, lambda c: (c >> lg_h) & gmask)
    e_ref[0] = jnp.where(me, _dot(ec_ref[0].astype(BF16), re), 0.0).astype(BF16)
    rc = tile_mat(S5_GROUP, LANES, lambda c: c & (S5_GROUP - 1))
    mc = group_mask(sw2, LANES, lambda r: (r >> lg_p) & gmask, lambda c: c >> lg_h)
    c_hi, c_mid, _ = _split3(cc_ref[0])
    ct_hi = jnp.where(mc, _dot(c_hi, rc), 0.0).astype(BF16)
    ct_mid = jnp.where(mc, _dot(c_mid, rc), 0.0).astype(BF16)
    w_ref[...] = jnp.zeros(w_ref.shape, w_ref.dtype)
    for lag in range(S5_T):
        r0 = (S5_T - 1 - lag) * LANES
        xh = x_hi[r0:r0 + LANES, :]
        xm = x_mid[r0:r0 + LANES, :]
        k = (_dot(xh, ct_hi) + _dot(xm, ct_hi) + _dot(xh, ct_mid)).astype(BF16)
        for s in range(S5_T - lag):
            t = s + lag
            w_ref[0, s * LANES:(s + 1) * LANES, t * LANES:(t + 1) * LANES] = k


def _s5_prep(fc, ec, cc):
    n = S5_T * LANES
    blk = lambda r, c: pl.BlockSpec((1, r, c), lambda j: (j, 0, 0))
    return pl.pallas_call(
        _s5_prep_kernel,
        grid=(S5_NBLK,),
        in_specs=[blk(n, 2 * S5_STATE), blk(2 * S5_SW, LANES), blk(2 * S5_SW, S5_GROUP)],
        out_specs=[blk(n, n), blk(n, 2 * S5_SW), blk(2 * S5_SW, n)],
        out_shape=[
            jax.ShapeDtypeStruct((S5_NBLK, n, n), BF16),
            jax.ShapeDtypeStruct((S5_NBLK, n, 2 * S5_SW), BF16),
            jax.ShapeDtypeStruct((S5_NBLK, 2 * S5_SW, n), BF16),
        ],
        compiler_params=_cparams(("parallel",)),
        name="s5_prep",
    )(fc, ec, cc)


def _s5_kernel(x_ref, w_ref, f_ref, e_ref, lam_ref, d_ref, h0re_ref, h0im_ref,
               y_ref, ore_ref, oim_ref, yi_ref, s_ref, hin_ref, *, nseq, nblk, rt):
    rows = nseq * nblk
    ar = lam_ref[0, 0:1, :]
    ai = lam_ref[0, 1:2, :]

    def load_u(r0):
        return jnp.concatenate(
            [x_ref[pl.ds(r0 * S5_T + s, rt, stride=S5_T), :] for s in range(S5_T)], axis=1)

    for r0 in range(0, rows, rt):
        ub = load_u(r0).astype(BF16)
        yi_ref[r0:r0 + rt, :] = _dot(ub, w_ref[0])
        s_ref[r0:r0 + rt, :] = _dot(ub, f_ref[0])

    if nblk == 1:
        hre = h0re_ref[...]
        him = h0im_ref[...]
        hin_ref[:, :S5_SW] = hre
        hin_ref[:, S5_SW:] = him
        ore_ref[...] = ar * hre - ai * him + s_ref[:, :S5_SW]
        oim_ref[...] = ar * him + ai * hre + s_ref[:, S5_SW:]
    else:
        def body(c, carry):
            new = []
            for b in range(nseq):
                hre, him = carry[2 * b], carry[2 * b + 1]
                row = b * nblk + c
                hin_ref[pl.ds(row, 1), :S5_SW] = hre
                hin_ref[pl.ds(row, 1), S5_SW:] = him
                sre = s_ref[pl.ds(row, 1), :S5_SW]
                sim = s_ref[pl.ds(row, 1), S5_SW:]
                new.append(ar * hre - ai * him + sre)
                new.append(ar * him + ai * hre + sim)
            return tuple(new)

        init = []
        for b in range(nseq):
            init.append(h0re_ref[b:b + 1, :])
            init.append(h0im_ref[b:b + 1, :])
        fin = lax.fori_loop(0, nblk, body, tuple(init))
        for b in range(nseq):
            ore_ref[b:b + 1, :] = fin[2 * b]
            oim_ref[b:b + 1, :] = fin[2 * b + 1]

    dt = jnp.concatenate([d_ref[...]] * S5_T, axis=1)
    for r0 in range(0, rows, rt):
        yo = _dot(hin_ref[r0:r0 + rt, :].astype(BF16), e_ref[0])
        y = jax.nn.gelu(yi_ref[r0:r0 + rt, :] + yo + load_u(r0) * dt)
        for t in range(S5_T):
            y_ref[pl.ds(r0 * S5_T + t, rt, stride=S5_T), :] = y[:, t * LANES:(t + 1) * LANES]


def _s5(z, w, f, e, lam, d, h0re, h0im, *, nseq, seqlen):
    nblk = seqlen // S5_T
    rows = nseq * nblk
    rt = min(rows, 256)
    n = S5_T * LANES
    tok = nseq * seqlen
    kern = functools.partial(_s5_kernel, nseq=nseq, nblk=nblk, rt=rt)
    return pl.pallas_call(
        kern,
        grid=(S5_NBLK,),
        in_specs=[
            pl.BlockSpec((tok, LANES), lambda j: (0, j)),
            pl.BlockSpec((1, n, n), lambda j: (j, 0, 0)),
            pl.BlockSpec((1, n, 2 * S5_SW), lambda j: (j, 0, 0)),
            pl.BlockSpec((1, 2 * S5_SW, n), lambda j: (j, 0, 0)),
            pl.BlockSpec((1, 2, S5_SW), lambda j: (j, 0, 0)),
            pl.BlockSpec((1, LANES), lambda j: (0, j)),
            pl.BlockSpec((nseq, S5_SW), lambda j: (0, j)),
            pl.BlockSpec((nseq, S5_SW), lambda j: (0, j)),
        ],
        out_specs=[
            pl.BlockSpec((tok, LANES), lambda j: (0, j)),
            pl.BlockSpec((nseq, S5_SW), lambda j: (0, j)),
            pl.BlockSpec((nseq, S5_SW), lambda j: (0, j)),
        ],
        out_shape=[
            jax.ShapeDtypeStruct((tok, S5_WIDTH), F32),
            jax.ShapeDtypeStruct((nseq, S5_GROUPS * S5_STATE), F32),
            jax.ShapeDtypeStruct((nseq, S5_GROUPS * S5_STATE), F32),
        ],
        scratch_shapes=[
            pltpu.VMEM((rows, n), F32),
            pltpu.VMEM((rows, 2 * S5_SW), F32),
            pltpu.VMEM((rows, 2 * S5_SW), F32),
        ],
        compiler_params=_cparams(("parallel",)),
        name="s5",
    )(z, w, f, e, lam, d, h0re, h0im)


def _s5_operators(a_re, a_im, log_dt, b_re, b_im, c_re, c_im):
    dt = jnp.exp(log_dt)[:, None]
    mag = jnp.exp(a_re * dt)
    abar_re = mag * jnp.cos(a_im * dt)
    abar_im = mag * jnp.sin(a_im * dt)
    nr, ni = abar_re - 1.0, abar_im
    den = a_re * a_re + a_im * a_im
    q_re = (nr * a_re + ni * a_im) / den
    q_im = (ni * a_re - nr * a_im) / den
    bb_re = q_re[:, :, None] * b_re - q_im[:, :, None] * b_im
    bb_im = q_re[:, :, None] * b_im + q_im[:, :, None] * b_re
    taus = jnp.arange(S5_T + 1, dtype=F32)[:, None, None]
    pmag = jnp.exp(taus * (a_re * dt)[None])
    pw_re = pmag * jnp.cos(taus * (a_im * dt)[None])
    pw_im = pmag * jnp.sin(taus * (a_im * dt)[None])
    n = S5_T * LANES

    lags = (S5_T - 1) - jnp.arange(S5_T, dtype=F32)[:, None, None]
    lmag = jnp.exp(lags * (a_re * dt)[None])
    lr = (lmag * jnp.cos(lags * (a_im * dt)[None]))[:, :, :, None]
    li = (lmag * jnp.sin(lags * (a_im * dt)[None]))[:, :, :, None]
    fb = jnp.stack([lr * bb_re[None] - li * bb_im[None], lr * bb_im[None] + li * bb_re[None]])
    fc = fb.reshape(2, S5_T, S5_NBLK, S5_GPB, S5_STATE, S5_GROUP).transpose(2, 1, 3, 5, 0, 4)
    fc = fc.reshape(S5_NBLK, n, 2 * S5_STATE)

    cp_re = jnp.swapaxes(c_re, -1, -2)[None]
    cp_im = jnp.swapaxes(c_im, -1, -2)[None]
    tr = pw_re[1:][:, :, :, None]
    ti = pw_im[1:][:, :, :, None]
    ce = jnp.stack([cp_re * tr - cp_im * ti, -(cp_re * ti + cp_im * tr)])
    ec = ce.reshape(2, S5_T, S5_NBLK, S5_GPB, S5_STATE, S5_GROUP).transpose(2, 0, 3, 4, 1, 5)
    ec = ec.reshape(S5_NBLK, 2 * S5_SW, LANES)

    cc = jnp.stack([cp_re[0], -cp_im[0]]).reshape(2, S5_NBLK, S5_GPB, S5_STATE, S5_GROUP)
    cc = cc.transpose(1, 0, 2, 3, 4).reshape(S5_NBLK, 2 * S5_SW, S5_GROUP)

    lam = jnp.stack([pw_re[S5_T].reshape(S5_NBLK, S5_SW), pw_im[S5_T].reshape(S5_NBLK, S5_SW)], axis=1)
    return fc, ec, cc, lam


def _mlstm_kernel(qk_ref, v_ref, o_ref, if_ref, cw_ref, cb_ref, bif_ref, ng_ref,
                  buf0_ref, c0_ref, n0_ref, m0_ref,
                  y_ref, bufo_ref, co_ref, no_ref, mo_ref, xp_ref, *, nb, lc, carry):
    if carry:
        @pl.when(pl.program_id(1) == 0)
        def _():
            bufo_ref[...] = buf0_ref[...]
            co_ref[...] = c0_ref[...]
            no_ref[...] = n0_ref[...]
            mo_ref[...] = m0_ref[...]
        bufs_ref, cs_ref, ns_ref, ms_ref = bufo_ref, co_ref, no_ref, mo_ref
    else:
        bufs_ref, cs_ref, ns_ref, ms_ref = buf0_ref, c0_ref, n0_ref, m0_ref

    lp = max(lc, LANES)
    head_row = lax.broadcasted_iota(jnp.int32, (HEADS, DQK), 0)
    head_lane = lax.broadcasted_iota(jnp.int32, (1, HEADS), 1)
    row = lax.broadcasted_iota(jnp.int32, (lc, lc), 0)
    col = lax.broadcasted_iota(jnp.int32, (lc, lc), 1)
    causal = row >= col
    tril = jnp.where(causal, 1.0, 0.0).astype(BF16)
    lane = lax.broadcasted_iota(jnp.int32, (lc, LANES), 1)
    is_i = lane < HEADS
    is_f = jnp.logical_and(lane >= HEADS, lane < 2 * HEADS)

    for s in range(nb):
        r0 = s * lc
        xp_ref[s, 5:8, :] = bufs_ref[s]
        xp_ref[s, 8:8 + lc, :] = qk_ref[r0:r0 + lc, :]
        conv = cb_ref[...] + sum(xp_ref[s, 5 + j:5 + j + lc, :] * cw_ref[j:j + 1, :] for j in range(CONV_W))
        bufo_ref[s] = xp_ref[s, 5 + lc:8 + lc, :]
        n_all = ns_ref[s]
        m_all = ms_ref[s]
        n_new_all = jnp.zeros((HEADS, DQK), F32)
        m_new_all = jnp.zeros((1, HEADS), F32)
        qk = conv * jax.nn.sigmoid(conv)
        q = qk[:, :QK_WIDTH // 2] * (DQK ** -0.5)
        k = qk[:, QK_WIDTH // 2:]

        gate = if_ref[r0:r0 + lc, :] + bif_ref[...]
        lf = jnp.where(is_f, jax.nn.log_sigmoid(gate), 0.0)
        parts = _split3(lf)
        bcum = _dot(tril, parts[0]) + _dot(tril, parts[1]) + _dot(tril, parts[2])
        pc = jnp.where(is_i, gate, bcum)
        if lc < lp:
            pc_t = jnp.concatenate([pc, jnp.zeros((lp - lc, LANES), F32)], axis=0).T[:, :lc]
        else:
            pc_t = pc.T

        for h in range(HEADS):
            i_col = pc[:, h:h + 1]
            b_col = pc[:, HEADS + h:HEADS + h + 1]
            i_row = pc_t[h:h + 1, :]
            b_row = pc_t[HEADS + h:HEADS + h + 1, :]
            m0 = m_all[:, h:h + 1]
            logw = jnp.where(causal, b_col - b_row + i_row, -jnp.inf)
            g = b_col + m0
            m = jnp.maximum(g, jnp.max(logw, axis=-1, keepdims=True))
            w = jnp.exp(logw - m)
            inter = jnp.exp(g - m)
            qh = q[:, h * DQK:(h + 1) * DQK]
            kh = k[:, h * DQK:(h + 1) * DQK]
            vh = v_ref[r0:r0 + lc, h * DV:(h + 1) * DV]
            qb = qh.astype(BF16)
            vb = vh.astype(BF16)
            c_prev = cs_ref[s, h]
            n_prev = n_all[h:h + 1, :]
            sc = _dot_nt(qb, kh.astype(BF16)) * w
            num = _dot(sc.astype(BF16), vb) + inter * _dot_nt(qb, c_prev.astype(BF16))
            den = jnp.sum(sc, axis=-1, keepdims=True) + inter * jnp.sum(qh * n_prev, axis=-1, keepdims=True)
            hh = num / jnp.maximum(jnp.abs(den), jnp.exp(-m))
            hh = hh * lax.rsqrt(jnp.mean(hh * hh, axis=-1, keepdims=True) + EPS)
            og = jax.nn.sigmoid(o_ref[r0:r0 + lc, h * DV:(h + 1) * DV])
            y_ref[r0:r0 + lc, h * DV:(h + 1) * DV] = hh * ng_ref[:, h * DV:(h + 1) * DV] * og

            b_last = b_col[lc - 1:lc, :]
            m_new = m[lc - 1:lc, :]
            decay = jnp.exp(b_last + m0 - m_new)
            kw = kh * jnp.exp(b_last - b_col + i_col - m_new)
            co_ref[s, h] = decay * c_prev + _dot_tn(vb, kw.astype(BF16))
            n_new = decay * n_prev + jnp.sum(kw, axis=0, keepdims=True)
            n_new_all = jnp.where(head_row == h, n_new, n_new_all)
            m_new_all = jnp.where(head_lane == h, m_new, m_new_all)
        no_ref[s] = n_new_all
        mo_ref[s] = m_new_all


def _mlstm(z, zif, cw, cb, bif, ng, buf0, c0, n0, m0, *, nseq, seqlen, nb, lc):
    tok = nseq * seqlen
    nchunk = seqlen // lc
    rows = nb * lc
    tok_map = lambda col: (lambda i, c: (i * nchunk + c, col))
    const2 = lambda i, c: (0, 0)
    st3 = lambda i, c: (i, 0, 0)
    st4 = lambda i, c: (i, 0, 0, 0)
    kern = functools.partial(_mlstm_kernel, nb=nb, lc=lc, carry=nchunk > 1)
    return pl.pallas_call(
        kern,
        grid=(nseq // nb, nchunk),
        in_specs=[
            pl.BlockSpec((rows, QK_WIDTH), tok_map(1)),
            pl.BlockSpec((rows, V_WIDTH), tok_map(2)),
            pl.BlockSpec((rows, V_WIDTH), tok_map(3)),
            pl.BlockSpec((rows, LANES), tok_map(0)),
            pl.BlockSpec((CONV_W, QK_WIDTH), const2),
            pl.BlockSpec((1, QK_WIDTH), const2),
            pl.BlockSpec((1, LANES), const2),
            pl.BlockSpec((1, V_WIDTH), const2),
            pl.BlockSpec((nb, CONV_W - 1, QK_WIDTH), st3),
            pl.BlockSpec((nb, HEADS, DV, DQK), st4),
            pl.BlockSpec((nb, HEADS, DQK), st3),
            pl.BlockSpec((nb, 1, HEADS), st3),
        ],
        out_specs=[
            pl.BlockSpec((rows, V_WIDTH), tok_map(0)),
            pl.BlockSpec((nb, CONV_W - 1, QK_WIDTH), st3),
            pl.BlockSpec((nb, HEADS, DV, DQK), st4),
            pl.BlockSpec((nb, HEADS, DQK), st3),
            pl.BlockSpec((nb, 1, HEADS), st3),
        ],
        out_shape=[
            jax.ShapeDtypeStruct((tok, V_WIDTH), F32),
            jax.ShapeDtypeStruct((nseq, CONV_W - 1, QK_WIDTH), F32),
            jax.ShapeDtypeStruct((nseq, HEADS, DV, DQK), F32),
            jax.ShapeDtypeStruct((nseq, HEADS, DQK), F32),
            jax.ShapeDtypeStruct((nseq, 1, HEADS), F32),
        ],
        scratch_shapes=[pltpu.VMEM((nb, lc + 8, QK_WIDTH), F32)],
        compiler_params=_cparams(("parallel", "arbitrary")),
        name="mlstm",
    )(z, z, z, zif, cw, cb, bif, ng, buf0, c0, n0, m0)


def _mlstm_step_kernel(qk_ref, v_ref, o_ref, if_ref, cw_ref, cb_ref, bif_ref, ng_ref,
                       buf0_ref, c0_ref, n0x_ref, m0x_ref,
                       y_ref, bufo_ref, co_ref, nox_ref, mox_ref, xp_ref, conv_ref, *, nb, lc):
    rows = nb * lc
    lg = lc.bit_length() - 1
    i32 = jnp.int32
    row = lax.broadcasted_iota(i32, (rows, rows), 0)
    col = lax.broadcasted_iota(i32, (rows, rows), 1)
    same = (row >> lg) == (col >> lg)
    causal = jnp.logical_and(same, row >= col)
    same_b = jnp.where(same, 1.0, 0.0).astype(BF16)
    tril_b = jnp.where(causal, 1.0, 0.0).astype(BF16)
    lane = lax.broadcasted_iota(i32, (rows, LANES), 1)
    is_i = lane < HEADS
    is_f = jnp.logical_and(lane >= HEADS, lane < 2 * HEADS)

    def seg_dot(mat, x):
        hi, mid, lo = _split3(x)
        return _dot(mat, hi) + _dot(mat, mid) + _dot(mat, lo)

    for s in range(nb):
        xp_ref[s, 5:8, :] = buf0_ref[s]
        xp_ref[s, 8:8 + lc, :] = qk_ref[s * lc:(s + 1) * lc, :]
        conv_ref[s * lc:(s + 1) * lc, :] = cb_ref[...] + sum(
            xp_ref[s, 5 + j:5 + j + lc, :] * cw_ref[j:j + 1, :] for j in range(CONV_W))
        bufo_ref[s] = xp_ref[s, 5 + lc:8 + lc, :]
    conv = conv_ref[...]
    qk = conv * jax.nn.sigmoid(conv)
    q = qk[:, :QK_WIDTH // 2] * (DQK ** -0.5)
    k = qk[:, QK_WIDTH // 2:]

    gate = if_ref[...] + bif_ref[...]
    lf = jnp.where(is_f, jax.nn.log_sigmoid(gate), 0.0)
    bcum = seg_dot(tril_b, lf)
    btot = seg_dot(same_b, lf)
    pc = jnp.where(is_i, gate, bcum)
    if rows < LANES:
        pc_t = jnp.concatenate([pc, jnp.zeros((LANES - rows, LANES), F32)], axis=0).T[:, :rows]
    else:
        pc_t = pc.T
    m0x = m0x_ref[...]
    mox = jnp.zeros((rows, LANES), F32)

    for h in range(HEADS):
        i_col = pc[:, h:h + 1]
        b_col = pc[:, HEADS + h:HEADS + h + 1]
        i_row = pc_t[h:h + 1, :]
        b_row = pc_t[HEADS + h:HEADS + h + 1, :]
        b_last = btot[:, HEADS + h:HEADS + h + 1]
        m0 = m0x[:, h:h + 1]
        lw = b_col - b_row + i_row
        logw = jnp.where(causal, lw, -jnp.inf)
        g = b_col + m0
        m = jnp.maximum(g, jnp.max(logw, axis=-1, keepdims=True))
        w = jnp.exp(logw - m)
        inter = jnp.exp(g - m)
        lw_end = jnp.where(same, b_last - b_row + i_row, -jnp.inf)
        m_new = jnp.maximum(b_last + m0, jnp.max(lw_end, axis=-1, keepdims=True))
        decay = jnp.exp(b_last + m0 - m_new)
        qh = q[:, h * DQK:(h + 1) * DQK]
        kh = k[:, h * DQK:(h + 1) * DQK]
        qb = qh.astype(BF16)
        vb = v_ref[:, h * DV:(h + 1) * DV].astype(BF16)
        n_prev = n0x_ref[:, h * DQK:(h + 1) * DQK]
        sc = _dot_nt(qb, kh.astype(BF16)) * w
        carried = jnp.concatenate(
            [_dot_nt(qb[s * lc:(s + 1) * lc], c0_ref[s, h].astype(BF16)) for s in range(nb)], axis=0)
        num = _dot(sc.astype(BF16), vb) + inter * carried
        den = jnp.sum(sc, axis=-1, keepdims=True) + inter * jnp.sum(qh * n_prev, axis=-1, keepdims=True)
        hh = num / jnp.maximum(jnp.abs(den), jnp.exp(-m))
        hh = hh * lax.rsqrt(jnp.mean(hh * hh, axis=-1, keepdims=True) + EPS)
        og = jax.nn.sigmoid(o_ref[:, h * DV:(h + 1) * DV])
        y_ref[:, h * DV:(h + 1) * DV] = hh * ng_ref[:, h * DV:(h + 1) * DV] * og

        kw = kh * jnp.exp(b_last - b_col + i_col - m_new)
        kwb = kw.astype(BF16)
        for s in range(nb):
            r0 = s * lc
            co_ref[s, h] = decay[r0:r0 + 1, :] * c0_ref[s, h] + _dot_tn(vb[r0:r0 + lc], kwb[r0:r0 + lc])
        nox_ref[:, h * DQK:(h + 1) * DQK] = decay * n_prev + seg_dot(same_b, kw)
        mox = jnp.where(lane == h, m_new, mox)
    mox_ref[...] = mox


def _mlstm_step(z, zif, cw, cb, bif, ng, buf0, c0, n0, m0, *, nseq, seqlen, nb):
    lc = seqlen
    assert lc & (lc - 1) == 0 and lc >= CONV_W - 1
    tok = nseq * seqlen
    rows = nb * lc
    n0x = jnp.repeat(n0.reshape(nseq, HEADS * DQK), lc, axis=0)
    m0x = jnp.repeat(jnp.pad(m0, ((0, 0), (0, LANES - HEADS))), lc, axis=0)
    tok_map = lambda col: (lambda i: (i, col))
    const2 = lambda i: (0, 0)
    st3 = lambda i: (i, 0, 0)
    st4 = lambda i: (i, 0, 0, 0)
    y, buf, c, nox, mox = pl.pallas_call(
        functools.partial(_mlstm_step_kernel, nb=nb, lc=lc),
        grid=(nseq // nb,),
        in_specs=[
            pl.BlockSpec((rows, QK_WIDTH), tok_map(1)),
            pl.BlockSpec((rows, V_WIDTH), tok_map(2)),
            pl.BlockSpec((rows, V_WIDTH), tok_map(3)),
            pl.BlockSpec((rows, LANES), tok_map(0)),
            pl.BlockSpec((CONV_W, QK_WIDTH), const2),
            pl.BlockSpec((1, QK_WIDTH), const2),
            pl.BlockSpec((1, LANES), const2),
            pl.BlockSpec((1, V_WIDTH), const2),
            pl.BlockSpec((nb, CONV_W - 1, QK_WIDTH), st3),
            pl.BlockSpec((nb, HEADS, DV, DQK), st4),
            pl.BlockSpec((rows, HEADS * DQK), tok_map(0)),
            pl.BlockSpec((rows, LANES), tok_map(0)),
        ],
        out_specs=[
            pl.BlockSpec((rows, V_WIDTH), tok_map(0)),
            pl.BlockSpec((nb, CONV_W - 1, QK_WIDTH), st3),
            pl.BlockSpec((nb, HEADS, DV, DQK), st4),
            pl.BlockSpec((rows, HEADS * DQK), tok_map(0)),
            pl.BlockSpec((rows, LANES), tok_map(0)),
        ],
        out_shape=[
            jax.ShapeDtypeStruct((tok, V_WIDTH), F32),
            jax.ShapeDtypeStruct((nseq, CONV_W - 1, QK_WIDTH), F32),
            jax.ShapeDtypeStruct((nseq, HEADS, DV, DQK), F32),
            jax.ShapeDtypeStruct((tok, HEADS * DQK), F32),
            jax.ShapeDtypeStruct((tok, LANES), F32),
        ],
        scratch_shapes=[pltpu.VMEM((nb, lc + 8, QK_WIDTH), F32), pltpu.VMEM((rows, QK_WIDTH), F32)],
        compiler_params=_cparams(("parallel",)),
        name="mlstm_step",
    )(z, z, z, zif, cw, cb, bif, ng, buf0, c0, n0x, m0x)
    n = nox[::lc].reshape(nseq, HEADS, DQK)
    m = mox[::lc, :HEADS]
    return y, buf, c, n, m


def _merge_kernel(h_ref, ys_ref, ym_ref, g1_ref, g2_ref, wglu_ref, wbs_ref, wbm_ref, wo_ref, o_ref):
    ys = ys_ref[...]
    glu = ys * jax.nn.sigmoid(_dot(ys.astype(BF16), wglu_ref[...]))
    a = _dot(glu.astype(BF16), wbs_ref[...])
    b = _dot(ym_ref[...].astype(BF16), wbm_ref[...])
    merged = jax.nn.sigmoid(g1_ref[...]) * a + jax.nn.sigmoid(g2_ref[...]) * b
    o_ref[...] = h_ref[...] + _dot(merged.astype(BF16), wo_ref[...])


def _merge(h, ys, ym, z, wglu, wbs, wbm, wo, *, n, tm):
    row = lambda i: (i, 0)
    const = lambda i: (0, 0)
    resident = lambda shape: pl.BlockSpec(shape, const, pipeline_mode=pl.Buffered(1))
    return pl.pallas_call(
        _merge_kernel,
        grid=(n // tm,),
        in_specs=[
            pl.BlockSpec((tm, D_MODEL), row),
            pl.BlockSpec((tm, S5_WIDTH), row),
            pl.BlockSpec((tm, V_WIDTH), row),
            pl.BlockSpec((tm, D_MODEL), lambda i: (i, 2)),
            pl.BlockSpec((tm, D_MODEL), lambda i: (i, 3)),
            resident((S5_WIDTH, S5_WIDTH)),
            resident((S5_WIDTH, D_MODEL)),
            resident((V_WIDTH, D_MODEL)),
            resident((D_MODEL, D_MODEL)),
        ],
        out_specs=pl.BlockSpec((tm, D_MODEL), row),
        out_shape=jax.ShapeDtypeStruct((n, D_MODEL), F32),
        compiler_params=_cparams(("parallel",)),
        name="merge",
    )(h, ys, ym, z, z, wglu, wbs, wbm, wo)


def kernel(x_prompt, x_sample, state_s5_re, state_s5_im, state_mlstm_C, state_mlstm_n, state_mlstm_m,
           state_mlstm_conv, meta_tokens, ffn1_norm, ffn1_w_gate, ffn1_w_up, ffn1_w_down, mix_norm, w_in,
           s5_A_re, s5_A_im, s5_log_dt, s5_B_re, s5_B_im, s5_C_re, s5_C_im, s5_D, s5_w_glu,
           mlstm_conv_w, mlstm_conv_b, mlstm_b_i, mlstm_b_f, mlstm_norm, w_branch_s5, w_branch_mlstm,
           w_out, ffn2_norm, ffn2_w_gate, ffn2_w_up, ffn2_w_down, final_norm):
    nbatch, seq, _ = x_prompt.shape
    nsamp, sseq, _ = x_sample.shape
    l = 0

    w1g, w1u, w1d = ffn1_w_gate[l], ffn1_w_up[l], ffn1_w_down[l]
    w2g, w2u, w2d = ffn2_w_gate[l], ffn2_w_up[l], ffn2_w_down[l]
    o_if = S5_WIDTH + QK_WIDTH + 2 * V_WIDTH
    o_gate = o_if + 2 * HEADS
    wt = w_in[l].T
    w2 = _win_halves(wt, o_gate, tr=WIN_CAST_ROWS)
    wglu, wbs, wbm, wo = (w[l].astype(BF16) for w in (s5_w_glu, w_branch_s5, w_branch_mlstm, w_out))
    g1 = ffn1_norm[l][None]
    gm = mix_norm[l][None]
    g2 = ffn2_norm[l][None]
    gf = final_norm[None]
    bif = jnp.pad(jnp.concatenate([mlstm_b_i[l], mlstm_b_f[l]]), (0, LANES - 2 * HEADS))[None]
    cw = mlstm_conv_w[l]
    cb = mlstm_conv_b[l][None]
    ng = mlstm_norm[l][None]
    d_skip = s5_D[l][None]

    fc, ec, cc, lam = _s5_operators(s5_A_re[l], s5_A_im[l], s5_log_dt[l], s5_B_re[l], s5_B_im[l],
                                    s5_C_re[l], s5_C_im[l])
    w_toe, f_bf, e_bf = _s5_prep(fc, ec, cc)

    def front(x, tm):
        n = x.shape[0]
        h1 = _ffn(x, g1, w1g, w1u, w1d, n=n, tm=tm)
        z, zif = _win(h1, gm, w2, wt, o_if=o_if, ngate=N_BRANCH * D_MODEL, tm=tm)
        return h1, z, zif

    def mixers(z, zif, s5_state, ml_state, *, nseq, seqlen, nb, lc):
        ys, sre, sim = _s5(z, w_toe, f_bf, e_bf, lam, d_skip, s5_state[0], s5_state[1], nseq=nseq, seqlen=seqlen)
        buf0, c0, n0, m0 = ml_state
        if lc == seqlen and nb > 1:
            ym, buf, c, n, m = _mlstm_step(z, zif, cw, cb, bif, ng, buf0, c0, n0, m0,
                                           nseq=nseq, seqlen=seqlen, nb=nb)
        else:
            ym, buf, c, n, m = _mlstm(z, zif, cw, cb, bif, ng, buf0, c0, n0, m0.reshape(nseq, 1, HEADS),
                                      nseq=nseq, seqlen=seqlen, nb=nb, lc=lc)
            m = m.reshape(nseq, HEADS)
        return ys, ym, (sre, sim), (buf, c, n, m)

    def back(h1, ys, ym, z, n):
        h2 = _merge(h1, ys, ym, z, wglu, wbs, wbm, wo, n=n, tm=MERGE_TM)
        return _ffn(h2, g2, w2g, w2u, w2d, gf, n=n, tm=FFN_TM)

    ntok_s = nsamp * sseq
    x_sm = jnp.concatenate([x_sample.reshape(ntok_s, D_MODEL), meta_tokens], axis=0)
    h1_sm, z_sm, zif_sm = front(x_sm, ntok_s + N_META)

    z_m = jnp.tile(z_sm[ntok_s:], (nbatch, 1))
    zif_m = jnp.tile(zif_sm[ntok_s:], (nbatch, 1))
    zeros = lambda *s: jnp.zeros((nbatch,) + s, F32)
    _, _, s5_m, ml_m = mixers(
        z_m, zif_m, (zeros(S5_GROUPS * S5_STATE), zeros(S5_GROUPS * S5_STATE)),
        (zeros(CONV_W - 1, QK_WIDTH), zeros(HEADS, DV, DQK), zeros(HEADS, DQK), zeros(HEADS)),
        nseq=nbatch, seqlen=N_META, nb=1, lc=N_META)

    ntok_p = nbatch * seq
    h1_p, z_p, zif_p = front(x_prompt.reshape(ntok_p, D_MODEL), FFN_TM)
    ys_p, ym_p, s5_p, ml_p = mixers(z_p, zif_p, s5_m, ml_m, nseq=nbatch, seqlen=seq, nb=1, lc=256)
    y_p = back(h1_p, ys_p, ym_p, z_p, ntok_p)

    ys_s, ym_s, s5_s, ml_s = mixers(
        z_sm, zif_sm,
        (state_s5_re[l].reshape(nsamp, -1), state_s5_im[l].reshape(nsamp, -1)),
        (state_mlstm_conv[l], state_mlstm_C[l], state_mlstm_n[l], state_mlstm_m[l]),
        nseq=nsamp, seqlen=sseq, nb=STEP_NB, lc=sseq)
    y_s = back(h1_sm, ys_s, ym_s, z_sm, ntok_s)

    def pack(n, s5_st, ml_st):
        buf, c, nn, m = ml_st
        return (s5_st[0].reshape(1, n, S5_GROUPS, S5_STATE), s5_st[1].reshape(1, n, S5_GROUPS, S5_STATE),
                c[None], nn[None], m[None], buf[None])

    return ((y_p.reshape(nbatch, seq, D_MODEL), y_s.reshape(nsamp, sseq, D_MODEL))
            + pack(nbatch, s5_p, ml_p) + pack(nsamp, s5_s, ml_s))
```

```python
import functools

import jax
import jax.numpy as jnp
from jax import lax
from jax.experimental import pallas as pl
from jax.experimental.pallas import tpu as pltpu

F32 = jnp.float32
BF16 = jnp.bfloat16

D_MODEL = 2048
D_FF = 5632
N_META = 16
S5_WIDTH = 1024
S5_GROUP = 16
S5_GROUPS = 64
S5_STATE = 64
HEADS = 4
DQK = 128
DV = 256
QK_WIDTH = 1024
V_WIDTH = 1024
CONV_W = 4
N_BRANCH = 2
EPS = 1e-6

LANES = 128
S5_T = 8
S5_GPB = LANES // S5_GROUP
S5_NBLK = S5_WIDTH // LANES
S5_SW = S5_GPB * S5_STATE
VMEM_LIMIT = 58 * 1024 * 1024
FFN_TM = 1024
MERGE_TM = 256
WIN_CAST_ROWS = 456
STEP_NB = 16


def _cparams(sem):
    return pltpu.CompilerParams(dimension_semantics=sem, vmem_limit_bytes=VMEM_LIMIT)


def _rmsnorm(x, g):
    ms = jnp.mean(x * x, axis=-1, keepdims=True)
    return (x * lax.rsqrt(ms + EPS)) * g


def _dot(a, b):
    return jnp.dot(a, b, preferred_element_type=F32)


def _dot_nt(a, b):
    return lax.dot_general(a, b, (((1,), (1,)), ((), ())), preferred_element_type=F32)


def _dot_tn(a, b):
    return lax.dot_general(a, b, (((0,), (0,)), ((), ())), preferred_element_type=F32)


def _split3(x):
    hi = x.astype(BF16)
    r = x - hi.astype(F32)
    mid = r.astype(BF16)
    lo = (r - mid.astype(F32)).astype(BF16)
    return hi, mid, lo


def _ffn_kernel(x_ref, g_ref, wg_ref, wu_ref, wd_ref, *rest, final_norm):
    if final_norm:
        fg_ref, o_ref, xn_ref = rest
    else:
        o_ref, xn_ref = rest
    j = pl.program_id(1)
    last = pl.num_programs(1) - 1

    def down(xn):
        gt = _dot(xn, wg_ref[...].astype(BF16))
        up = _dot(xn, wu_ref[...].astype(BF16))
        act = (gt * jax.nn.sigmoid(gt) * up).astype(BF16)
        return _dot(act, wd_ref[...].astype(BF16))

    @pl.when(j == 0)
    def _():
        xn = _rmsnorm(x_ref[...], g_ref[...]).astype(BF16)
        xn_ref[...] = xn
        o_ref[...] = down(xn)

    @pl.when(jnp.logical_and(j > 0, j < last))
    def _():
        o_ref[...] += down(xn_ref[...])

    @pl.when(j == last)
    def _():
        h = x_ref[...] + 0.5 * (o_ref[...] + down(xn_ref[...]))
        if final_norm:
            h = _rmsnorm(h, fg_ref[...])
        o_ref[...] = h


def _ffn(x, g, wg, wu, wd, final_g=None, *, n, tm, tf=256, x_buffers=2):
    in_specs = [
        pl.BlockSpec((tm, D_MODEL), lambda i, j: (i, 0), pipeline_mode=pl.Buffered(x_buffers)),
        pl.BlockSpec((1, D_MODEL), lambda i, j: (0, 0)),
        pl.BlockSpec((D_MODEL, tf), lambda i, j: (0, j)),
        pl.BlockSpec((D_MODEL, tf), lambda i, j: (0, j)),
        pl.BlockSpec((tf, D_MODEL), lambda i, j: (j, 0)),
    ]
    args = [x, g, wg, wu, wd]
    if final_g is not None:
        in_specs.append(pl.BlockSpec((1, D_MODEL), lambda i, j: (0, 0)))
        args.append(final_g)
    return pl.pallas_call(
        functools.partial(_ffn_kernel, final_norm=final_g is not None),
        grid=(n // tm, D_FF // tf),
        in_specs=in_specs,
        out_specs=pl.BlockSpec((tm, D_MODEL), lambda i, j: (i, 0)),
        out_shape=jax.ShapeDtypeStruct((n, D_MODEL), F32),
        scratch_shapes=[pltpu.VMEM((tm, D_MODEL), BF16)],
        compiler_params=_cparams(("parallel", "arbitrary")),
        name="ffn",
    )(*args)


def _cast_rows_kernel(w_ref, o_ref, *, tail):
    last = pl.num_programs(0) - 1

    @pl.when(pl.program_id(0) < last)
    def _():
        o_ref[0] = w_ref[...].astype(BF16)

    @pl.when(pl.program_id(0) == last)
    def _():
        o_ref[0, :tail] = w_ref[:tail].astype(BF16)
        o_ref[0, tail:] = jnp.zeros((o_ref.shape[1] - tail, o_ref.shape[2]), BF16)


def _win_halves(wt, o_gate, *, tr):
    rows, k = wt.shape
    per = o_gate // tr
    tail = rows - (2 * per - 1) * tr
    assert per * tr == o_gate and tr % 8 == 0 and 0 < tail < tr and tail % 8 == 0
    return pl.pallas_call(
        functools.partial(_cast_rows_kernel, tail=tail),
        grid=(2 * per,),
        in_specs=[pl.BlockSpec((tr, k), lambda i: (i, 0))],
        out_specs=pl.BlockSpec((1, tr, k), lambda i: (i // per, i % per, 0)),
        out_shape=jax.ShapeDtypeStruct((2, o_gate, k), BF16),
        compiler_params=_cparams(("parallel",)),
        name="w_in_cast",
    )(wt)


def _win_kernel(h_ref, g_ref, wa_ref, wb_ref, wif_ref, z_ref, zif_ref, un_ref, *, na):
    j = pl.program_id(1)

    @pl.when(j == 0)
    def _():
        un = _rmsnorm(h_ref[...], g_ref[...]).astype(BF16)
        un_ref[...] = un
        zif_ref[...] = _dot_nt(un, wif_ref[...].astype(BF16))

    @pl.when(j < na)
    def _():
        z_ref[...] = _dot_nt(un_ref[...], wa_ref[0])

    @pl.when(j >= na)
    def _():
        z_ref[...] = _dot_nt(un_ref[...], wb_ref[0])


def _win(h, g, w2, wt, *, o_if, ngate, tm, tn=1024):
    n = h.shape[0]
    na = o_if // tn
    nb = ngate // tn
    return pl.pallas_call(
        functools.partial(_win_kernel, na=na),
        grid=(n // tm, na + nb),
        in_specs=[
            pl.BlockSpec((tm, D_MODEL), lambda i, j: (i, 0)),
            pl.BlockSpec((1, D_MODEL), lambda i, j: (0, 0)),
            pl.BlockSpec((1, tn, D_MODEL), lambda i, j: (0, jnp.minimum(j, na - 1), 0)),
            pl.BlockSpec((1, tn, D_MODEL), lambda i, j: (1, jnp.maximum(j - na, 0), 0)),
            pl.BlockSpec((LANES, D_MODEL), lambda i, j: (o_if // LANES, 0)),
        ],
        out_specs=[
            pl.BlockSpec((tm, tn), lambda i, j: (i, j)),
            pl.BlockSpec((tm, LANES), lambda i, j: (i, 0)),
        ],
        out_shape=[
            jax.ShapeDtypeStruct((n, (na + nb) * tn), F32),
            jax.ShapeDtypeStruct((n, LANES), F32),
        ],
        scratch_shapes=[pltpu.VMEM((tm, D_MODEL), BF16)],
        compiler_params=_cparams(("parallel", "arbitrary")),
        name="w_in",
    )(h, g, w2, w2, wt)


def _s5_prep_kernel(fc_ref, ec_ref, cc_ref, w_ref, f_ref, e_ref):
    n = S5_T * LANES
    sw2 = 2 * S5_SW
    i32 = jnp.int32
    lg_h = S5_GROUP.bit_length() - 1
    lg_p = S5_STATE.bit_length() - 1
    lg_sw = S5_SW.bit_length() - 1

    def tile_mat(k, c, src_col):
        kk = lax.broadcasted_iota(i32, (k, c), 0)
        cc = lax.broadcasted_iota(i32, (k, c), 1)
        return jnp.where(kk == src_col(cc), 1.0, 0.0).astype(BF16)

    def group_mask(r, c, row_group, col_group):
        rr = lax.broadcasted_iota(i32, (r, c), 0)
        cc = lax.broadcasted_iota(i32, (r, c), 1)
        return row_group(rr) == col_group(cc)

    gmask = S5_GPB - 1
    rf = tile_mat(2 * S5_STATE, sw2, lambda c: ((c >> lg_sw) << lg_p) + (c & (S5_STATE - 1)))
    mf = group_mask(n, sw2, lambda r: (r >> lg_h) & gmask, lambda c: (c >> lg_p) & gmask)
    f_hi, f_mid, _ = _split3(fc_ref[0])
    x_hi = jnp.where(mf, _dot(f_hi, rf), 0.0).astype(BF16)
    x_mid = jnp.where(mf, _dot(f_mid, rf), 0.0).astype(BF16)
    f_ref[0] = x_hi
    lg_l = LANES.bit_length() - 1
    re = tile_mat(LANES, n, lambda c: ((c >> lg_l) << lg_h) + (c & (S5_GROUP - 1)))
    me = group_mask(sw2, n, lambda r: (r >> lg_p) & gmask, lambda c: (c >> lg_h) & gmask)
    e_ref[0] = jnp.where(me, _dot(ec_ref[0].astype(BF16), re), 0.0).astype(BF16)
    rc = tile_mat(S5_GROUP, LANES, lambda c: c & (S5_GROUP - 1))
    mc = group_mask(sw2, LANES, lambda r: (r >> lg_p) & gmask, lambda c: c >> lg_h)
    c_hi, c_mid, _ = _split3(cc_ref[0])
    ct_hi = jnp.where(mc, _dot(c_hi, rc), 0.0).astype(BF16)
    ct_mid = jnp.where(mc, _dot(c_mid, rc), 0.0).astype(BF16)
    w_ref[...] = jnp.zeros(w_ref.shape, w_ref.dtype)
    for lag in range(S5_T):
        r0 = (S5_T - 1 - lag) * LANES
        xh = x_hi[r0:r0 + LANES, :]
        xm = x_mid[r0:r0 + LANES, :]
        k = (_dot(xh, ct_hi) + _dot(xm, ct_hi) + _dot(xh, ct_mid)).astype(BF16)
        for s in range(S5_T - lag):
            t = s + lag
            w_ref[0, s * LANES:(s + 1) * LANES, t * LANES:(t + 1) * LANES] = k


def _s5_prep(fc, ec, cc):
    n = S5_T * LANES
    blk = lambda r, c: pl.BlockSpec((1, r, c), lambda j: (j, 0, 0))
    return pl.pallas_call(
        _s5_prep_kernel,
        grid=(S5_NBLK,),
        in_specs=[blk(n, 2 * S5_STATE), blk(2 * S5_SW, LANES), blk(2 * S5_SW, S5_GROUP)],
        out_specs=[blk(n, n), blk(n, 2 * S5_SW), blk(2 * S5_SW, n)],
        out_shape=[
            jax.ShapeDtypeStruct((S5_NBLK, n, n), BF16),
            jax.ShapeDtypeStruct((S5_NBLK, n, 2 * S5_SW), BF16),
            jax.ShapeDtypeStruct((S5_NBLK, 2 * S5_SW, n), BF16),
        ],
        compiler_params=_cparams(("parallel",)),
        name="s5_prep",
    )(fc, ec, cc)


def _s5_kernel(x_ref, w_ref, f_ref, e_ref, lam_ref, d_ref, h0re_ref, h0im_ref,
               y_ref, ore_ref, oim_ref, yi_ref, s_ref, hin_ref, *, nseq, nblk, rt):
    rows = nseq * nblk
    ar = lam_ref[0, 0:1, :]
    ai = lam_ref[0, 1:2, :]

    def load_u(r0):
        return jnp.concatenate(
            [x_ref[pl.ds(r0 * S5_T + s, rt, stride=S5_T), :] for s in range(S5_T)], axis=1)

    for r0 in range(0, rows, rt):
        ub = load_u(r0).astype(BF16)
        yi_ref[r0:r0 + rt, :] = _dot(ub, w_ref[0])
        s_ref[r0:r0 + rt, :] = _dot(ub, f_ref[0])

    if nblk == 1:
        hre = h0re_ref[...]
        him = h0im_ref[...]
        hin_ref[:, :S5_SW] = hre
        hin_ref[:, S5_SW:] = him
        ore_ref[...] = ar * hre - ai * him + s_ref[:, :S5_SW]
        oim_ref[...] = ar * him + ai * hre + s_ref[:, S5_SW:]
    else:
        def body(c, carry):
            new = []
            for b in range(nseq):
                hre, him = carry[2 * b], carry[2 * b + 1]
                row = b * nblk + c
                hin_ref[pl.ds(row, 1), :S5_SW] = hre
                hin_ref[pl.ds(row, 1), S5_SW:] = him
                sre = s_ref[pl.ds(row, 1), :S5_SW]
                sim = s_ref[pl.ds(row, 1), S5_SW:]
                new.append(ar * hre - ai * him + sre)
                new.append(ar * him + ai * hre + sim)
            return tuple(new)

        init = []
        for b in range(nseq):
            init.append(h0re_ref[b:b + 1, :])
            init.append(h0im_ref[b:b + 1, :])
        fin = lax.fori_loop(0, nblk, body, tuple(init))
        for b in range(nseq):
            ore_ref[b:b + 1, :] = fin[2 * b]
            oim_ref[b:b + 1, :] = fin[2 * b + 1]

    dt = jnp.concatenate([d_ref[...]] * S5_T, axis=1)
    for r0 in range(0, rows, rt):
        yo = _dot(hin_ref[r0:r0 + rt, :].astype(BF16), e_ref[0])
        y = jax.nn.gelu(yi_ref[r0:r0 + rt, :] + yo + load_u(r0) * dt)
        for t in range(S5_T):
            y_ref[pl.ds(r0 * S5_T + t, rt, stride=S5_T), :] = y[:, t * LANES:(t + 1) * LANES]


def _s5(z, w, f, e, lam, d, h0re, h0im, *, nseq, seqlen):
    nblk = seqlen // S5_T
    rows = nseq * nblk
    rt = min(rows, 256)
    n = S5_T * LANES
    tok = nseq * seqlen
    kern = functools.partial(_s5_kernel, nseq=nseq, nblk=nblk, rt=rt)
    return pl.pallas_call(
        kern,
        grid=(S5_NBLK,),
        in_specs=[
            pl.BlockSpec((tok, LANES), lambda j: (0, j)),
            pl.BlockSpec((1, n, n), lambda j: (j, 0, 0)),
            pl.BlockSpec((1, n, 2 * S5_SW), lambda j: (j, 0, 0)),
            pl.BlockSpec((1, 2 * S5_SW, n), lambda j: (j, 0, 0)),
            pl.BlockSpec((1, 2, S5_SW), lambda j: (j, 0, 0)),
            pl.BlockSpec((1, LANES), lambda j: (0, j)),
            pl.BlockSpec((nseq, S5_SW), lambda j: (0, j)),
            pl.BlockSpec((nseq, S5_SW), lambda j: (0, j)),
        ],
        out_specs=[
            pl.BlockSpec((tok, LANES), lambda j: (0, j)),
            pl.BlockSpec((nseq, S5_SW), lambda j: (0, j)),
            pl.BlockSpec((nseq, S5_SW), lambda j: (0, j)),
        ],
        out_shape=[
            jax.ShapeDtypeStruct((tok, S5_WIDTH), F32),
            jax.ShapeDtypeStruct((nseq, S5_GROUPS * S5_STATE), F32),
            jax.ShapeDtypeStruct((nseq, S5_GROUPS * S5_STATE), F32),
        ],
        scratch_shapes=[
            pltpu.VMEM((rows, n), F32),
            pltpu.VMEM((rows, 2 * S5_SW), F32),
            pltpu.VMEM((rows, 2 * S5_SW), F32),
        ],
        compiler_params=_cparams(("parallel",)),
        name="s5",
    )(z, w, f, e, lam, d, h0re, h0im)


def _s5_operators(a_re, a_im, log_dt, b_re, b_im, c_re, c_im):
    dt = jnp.exp(log_dt)[:, None]
    mag = jnp.exp(a_re * dt)
    abar_re = mag * jnp.cos(a_im * dt)
    abar_im = mag * jnp.sin(a_im * dt)
    nr, ni = abar_re - 1.0, abar_im
    den = a_re * a_re + a_im * a_im
    q_re = (nr * a_re + ni * a_im) / den
    q_im = (ni * a_re - nr * a_im) / den
    bb_re = q_re[:, :, None] * b_re - q_im[:, :, None] * b_im
    bb_im = q_re[:, :, None] * b_im + q_im[:, :, None] * b_re
    taus = jnp.arange(S5_T + 1, dtype=F32)[:, None, None]
    pmag = jnp.exp(taus * (a_re * dt)[None])
    pw_re = pmag * jnp.cos(taus * (a_im * dt)[None])
    pw_im = pmag * jnp.sin(taus * (a_im * dt)[None])
    n = S5_T * LANES

    lags = (S5_T - 1) - jnp.arange(S5_T, dtype=F32)[:, None, None]
    lmag = jnp.exp(lags * (a_re * dt)[None])
    lr = (lmag * jnp.cos(lags * (a_im * dt)[None]))[:, :, :, None]
    li = (lmag * jnp.sin(lags * (a_im * dt)[None]))[:, :, :, None]
    fb = jnp.stack([lr * bb_re[None] - li * bb_im[None], lr * bb_im[None] + li * bb_re[None]])
    fc = fb.reshape(2, S5_T, S5_NBLK, S5_GPB, S5_STATE, S5_GROUP).transpose(2, 1, 3, 5, 0, 4)
    fc = fc.reshape(S5_NBLK, n, 2 * S5_STATE)

    cp_re = jnp.swapaxes(c_re, -1, -2)[None]
    cp_im = jnp.swapaxes(c_im, -1, -2)[None]
    tr = pw_re[1:][:, :, :, None]
    ti = pw_im[1:][:, :, :, None]
    ce = jnp.stack([cp_re * tr - cp_im * ti, -(cp_re * ti + cp_im * tr)])
    ec = ce.reshape(2, S5_T, S5_NBLK, S5_GPB, S5_STATE, S5_GROUP).transpose(2, 0, 3, 4, 1, 5)
    ec = ec.reshape(S5_NBLK, 2 * S5_SW, LANES)

    cc = jnp.stack([cp_re[0], -cp_im[0]]).reshape(2, S5_NBLK, S5_GPB, S5_STATE, S5_GROUP)
    cc = cc.transpose(1, 0, 2, 3, 4).reshape(S5_NBLK, 2 * S5_SW, S5_GROUP)

    lam = jnp.stack([pw_re[S5_T].reshape(S5_NBLK, S5_SW), pw_im[S5_T].reshape(S5_NBLK, S5_SW)], axis=1)
    return fc, ec, cc, lam


def _mlstm_kernel(qk_ref, v_ref, o_ref, if_ref, cw_ref, cb_ref, bif_ref, ng_ref,
                  buf0_ref, c0_ref, n0_ref, m0_ref,
                  y_ref, bufo_ref, co_ref, no_ref, mo_ref, xp_ref, *, nb, lc, carry):
    if carry:
        @pl.when(pl.program_id(1) == 0)
        def _():
            bufo_ref[...] = buf0_ref[...]
            co_ref[...] = c0_ref[...]
            no_ref[...] = n0_ref[...]
            mo_ref[...] = m0_ref[...]
        bufs_ref, cs_ref, ns_ref, ms_ref = bufo_ref, co_ref, no_ref, mo_ref
    else:
        bufs_ref, cs_ref, ns_ref, ms_ref = buf0_ref, c0_ref, n0_ref, m0_ref

    lp = max(lc, LANES)
    head_row = lax.broadcasted_iota(jnp.int32, (HEADS, DQK), 0)
    head_lane = lax.broadcasted_iota(jnp.int32, (1, HEADS), 1)
    row = lax.broadcasted_iota(jnp.int32, (lc, lc), 0)
    col = lax.broadcasted_iota(jnp.int32, (lc, lc), 1)
    causal = row >= col
    tril = jnp.where(causal, 1.0, 0.0).astype(BF16)
    lane = lax.broadcasted_iota(jnp.int32, (lc, LANES), 1)
    is_i = lane < HEADS
    is_f = jnp.logical_and(lane >= HEADS, lane < 2 * HEADS)

    for s in range(nb):
        r0 = s * lc
        xp_ref[s, 5:8, :] = bufs_ref[s]
        xp_ref[s, 8:8 + lc, :] = qk_ref[r0:r0 + lc, :]
        conv = cb_ref[...] + sum(xp_ref[s, 5 + j:5 + j + lc, :] * cw_ref[j:j + 1, :] for j in range(CONV_W))
        bufo_ref[s] = xp_ref[s, 5 + lc:8 + lc, :]
        n_all = ns_ref[s]
        m_all = ms_ref[s]
        n_new_all = jnp.zeros((HEADS, DQK), F32)
        m_new_all = jnp.zeros((1, HEADS), F32)
        qk = conv * jax.nn.sigmoid(conv)
        q = qk[:, :QK_WIDTH // 2] * (DQK ** -0.5)
        k = qk[:, QK_WIDTH // 2:]

        gate = if_ref[r0:r0 + lc, :] + bif_ref[...]
        lf = jnp.where(is_f, jax.nn.log_sigmoid(gate), 0.0)
        parts = _split3(lf)
        bcum = _dot(tril, parts[0]) + _dot(tril, parts[1]) + _dot(tril, parts[2])
        pc = jnp.where(is_i, gate, bcum)
        if lc < lp:
            pc_t = jnp.concatenate([pc, jnp.zeros((lp - lc, LANES), F32)], axis=0).T[:, :lc]
        else:
            pc_t = pc.T

        for h in range(HEADS):
            i_col = pc[:, h:h + 1]
            b_col = pc[:, HEADS + h:HEADS + h + 1]
            i_row = pc_t[h:h + 1, :]
            b_row = pc_t[HEADS + h:HEADS + h + 1, :]
            m0 = m_all[:, h:h + 1]
            logw = jnp.where(causal, b_col - b_row + i_row, -jnp.inf)
            g = b_col + m0
            m = jnp.maximum(g, jnp.max(logw, axis=-1, keepdims=True))
            w = jnp.exp(logw - m)
            inter = jnp.exp(g - m)
            qh = q[:, h * DQK:(h + 1) * DQK]
            kh = k[:, h * DQK:(h + 1) * DQK]
            vh = v_ref[r0:r0 + lc, h * DV:(h + 1) * DV]
            qb = qh.astype(BF16)
            vb = vh.astype(BF16)
            c_prev = cs_ref[s, h]
            n_prev = n_all[h:h + 1, :]
            sc = _dot_nt(qb, kh.astype(BF16)) * w
            num = _dot(sc.astype(BF16), vb) + inter * _dot_nt(qb, c_prev.astype(BF16))
            den = jnp.sum(sc, axis=-1, keepdims=True) + inter * jnp.sum(qh * n_prev, axis=-1, keepdims=True)
            hh = num / jnp.maximum(jnp.abs(den), jnp.exp(-m))
            hh = hh * lax.rsqrt(jnp.mean(hh * hh, axis=-1, keepdims=True) + EPS)
            og = jax.nn.sigmoid(o_ref[r0:r0 + lc, h * DV:(h + 1) * DV])
            y_ref[r0:r0 + lc, h * DV:(h + 1) * DV] = hh * ng_ref[:, h * DV:(h + 1) * DV] * og

            b_last = b_col[lc - 1:lc, :]
            m_new = m[lc - 1:lc, :]
            decay = jnp.exp(b_last + m0 - m_new)
            kw = kh * jnp.exp(b_last - b_col + i_col - m_new)
            co_ref[s, h] = decay * c_prev + _dot_tn(vb, kw.astype(BF16))
            n_new = decay * n_prev + jnp.sum(kw, axis=0, keepdims=True)
            n_new_all = jnp.where(head_row == h, n_new, n_new_all)
            m_new_all = jnp.where(head_lane == h, m_new, m_new_all)
        no_ref[s] = n_new_all
        mo_ref[s] = m_new_all


def _mlstm(z, zif, cw, cb, bif, ng, buf0, c0, n0, m0, *, nseq, seqlen, nb, lc):
    tok = nseq * seqlen
    nchunk = seqlen // lc
    rows = nb * lc
    tok_map = lambda col: (lambda i, c: (i * nchunk + c, col))
    const2 = lambda i, c: (0, 0)
    st3 = lambda i, c: (i, 0, 0)
    st4 = lambda i, c: (i, 0, 0, 0)
    kern = functools.partial(_mlstm_kernel, nb=nb, lc=lc, carry=nchunk > 1)
    return pl.pallas_call(
        kern,
        grid=(nseq // nb, nchunk),
        in_specs=[
            pl.BlockSpec((rows, QK_WIDTH), tok_map(1)),
            pl.BlockSpec((rows, V_WIDTH), tok_map(2)),
            pl.BlockSpec((rows, V_WIDTH), tok_map(3)),
            pl.BlockSpec((rows, LANES), tok_map(0)),
            pl.BlockSpec((CONV_W, QK_WIDTH), const2),
            pl.BlockSpec((1, QK_WIDTH), const2),
            pl.BlockSpec((1, LANES), const2),
            pl.BlockSpec((1, V_WIDTH), const2),
            pl.BlockSpec((nb, CONV_W - 1, QK_WIDTH), st3),
            pl.BlockSpec((nb, HEADS, DV, DQK), st4),
            pl.BlockSpec((nb, HEADS, DQK), st3),
            pl.BlockSpec((nb, 1, HEADS), st3),
        ],
        out_specs=[
            pl.BlockSpec((rows, V_WIDTH), tok_map(0)),
            pl.BlockSpec((nb, CONV_W - 1, QK_WIDTH), st3),
            pl.BlockSpec((nb, HEADS, DV, DQK), st4),
            pl.BlockSpec((nb, HEADS, DQK), st3),
            pl.BlockSpec((nb, 1, HEADS), st3),
        ],
        out_shape=[
            jax.ShapeDtypeStruct((tok, V_WIDTH), F32),
            jax.ShapeDtypeStruct((nseq, CONV_W - 1, QK_WIDTH), F32),
            jax.ShapeDtypeStruct((nseq, HEADS, DV, DQK), F32),
            jax.ShapeDtypeStruct((nseq, HEADS, DQK), F32),
            jax.ShapeDtypeStruct((nseq, 1, HEADS), F32),
        ],
        scratch_shapes=[pltpu.VMEM((nb, lc + 8, QK_WIDTH), F32)],
        compiler_params=_cparams(("parallel", "arbitrary")),
        name="mlstm",
    )(z, z, z, zif, cw, cb, bif, ng, buf0, c0, n0, m0)


def _mlstm_step_kernel(qk_ref, v_ref, o_ref, if_ref, cw_ref, cb_ref, bif_ref, ng_ref,
                       buf0_ref, c0_ref, n0x_ref, m0x_ref,
                       y_ref, bufo_ref, co_ref, nox_ref, mox_ref, xp_ref, conv_ref, *, nb, lc):
    rows = nb * lc
    lg = lc.bit_length() - 1
    i32 = jnp.int32
    row = lax.broadcasted_iota(i32, (rows, rows), 0)
    col = lax.broadcasted_iota(i32, (rows, rows), 1)
    same = (row >> lg) == (col >> lg)
    causal = jnp.logical_and(same, row >= col)
    same_b = jnp.where(same, 1.0, 0.0).astype(BF16)
    tril_b = jnp.where(causal, 1.0, 0.0).astype(BF16)
    lane = lax.broadcasted_iota(i32, (rows, LANES), 1)
    is_i = lane < HEADS
    is_f = jnp.logical_and(lane >= HEADS, lane < 2 * HEADS)

    def seg_dot(mat, x):
        hi, mid, lo = _split3(x)
        return _dot(mat, hi) + _dot(mat, mid) + _dot(mat, lo)

    for s in range(nb):
        xp_ref[s, 5:8, :] = buf0_ref[s]
        xp_ref[s, 8:8 + lc, :] = qk_ref[s * lc:(s + 1) * lc, :]
        conv_ref[s * lc:(s + 1) * lc, :] = cb_ref[...] + sum(
            xp_ref[s, 5 + j:5 + j + lc, :] * cw_ref[j:j + 1, :] for j in range(CONV_W))
        bufo_ref[s] = xp_ref[s, 5 + lc:8 + lc, :]
    conv = conv_ref[...]
    qk = conv * jax.nn.sigmoid(conv)
    q = qk[:, :QK_WIDTH // 2] * (DQK ** -0.5)
    k = qk[:, QK_WIDTH // 2:]

    gate = if_ref[...] + bif_ref[...]
    lf = jnp.where(is_f, jax.nn.log_sigmoid(gate), 0.0)
    bcum = seg_dot(tril_b, lf)
    btot = seg_dot(same_b, lf)
    pc = jnp.where(is_i, gate, bcum)
    if rows < LANES:
        pc_t = jnp.concatenate([pc, jnp.zeros((LANES - rows, LANES), F32)], axis=0).T[:, :rows]
    else:
        pc_t = pc.T
    m0x = m0x_ref[...]
    mox = jnp.zeros((rows, LANES), F32)

    for h in range(HEADS):
        i_col = pc[:, h:h + 1]
        b_col = pc[:, HEADS + h:HEADS + h + 1]
        i_row = pc_t[h:h + 1, :]
        b_row = pc_t[HEADS + h:HEADS + h + 1, :]
        b_last = btot[:, HEADS + h:HEADS + h + 1]
        m0 = m0x[:, h:h + 1]
        lw = b_col - b_row + i_row
        logw = jnp.where(causal, lw, -jnp.inf)
        g = b_col + m0
        m = jnp.maximum(g, jnp.max(logw, axis=-1, keepdims=True))
        w = jnp.exp(logw - m)
        inter = jnp.exp(g - m)
        lw_end = jnp.where(same, b_last - b_row + i_row, -jnp.inf)
        m_new = jnp.maximum(b_last + m0, jnp.max(lw_end, axis=-1, keepdims=True))
        decay = jnp.exp(b_last + m0 - m_new)
        qh = q[:, h * DQK:(h + 1) * DQK]
        kh = k[:, h * DQK:(h + 1) * DQK]
        qb = qh.astype(BF16)
        vb = v_ref[:, h * DV:(h + 1) * DV].astype(BF16)
        n_prev = n0x_ref[:, h * DQK:(h + 1) * DQK]
        sc = _dot_nt(qb, kh.astype(BF16)) * w
        carried = jnp.concatenate(
            [_dot_nt(qb[s * lc:(s + 1) * lc], c0_ref[s, h].astype(BF16)) for s in range(nb)], axis=0)
        num = _dot(sc.astype(BF16), vb) + inter * carried
        den = jnp.sum(sc, axis=-1, keepdims=True) + inter * jnp.sum(qh * n_prev, axis=-1, keepdims=True)
        hh = num / jnp.maximum(jnp.abs(den), jnp.exp(-m))
        hh = hh * lax.rsqrt(jnp.mean(hh * hh, axis=-1, keepdims=True) + EPS)
        og = jax.nn.sigmoid(o_ref[:, h * DV:(h + 1) * DV])
        y_ref[:, h * DV:(h + 1) * DV] = hh * ng_ref[:, h * DV:(h + 1) * DV] * og

        kw = kh * jnp.exp(b_last - b_col + i_col - m_new)
        kwb = kw.astype(BF16)
        for s in range(nb):
            r0 = s * lc
            co_ref[s, h] = decay[r0:r0 + 1, :] * c0_ref[s, h] + _dot_tn(vb[r0:r0 + lc], kwb[r0:r0 + lc])
        nox_ref[:, h * DQK:(h + 1) * DQK] = decay * n_prev + seg_dot(same_b, kw)
        mox = jnp.where(lane == h, m_new, mox)
    mox_ref[...] = mox


def _mlstm_step(z, zif, cw, cb, bif, ng, buf0, c0, n0, m0, *, nseq, seqlen, nb):
    lc = seqlen
    assert lc & (lc - 1) == 0 and lc >= CONV_W - 1
    tok = nseq * seqlen
    rows = nb * lc
    n0x = jnp.repeat(n0.reshape(nseq, HEADS * DQK), lc, axis=0)
    m0x = jnp.repeat(jnp.pad(m0, ((0, 0), (0, LANES - HEADS))), lc, axis=0)
    tok_map = lambda col: (lambda i: (i, col))
    const2 = lambda i: (0, 0)
    st3 = lambda i: (i, 0, 0)
    st4 = lambda i: (i, 0, 0, 0)
    y, buf, c, nox, mox = pl.pallas_call(
        functools.partial(_mlstm_step_kernel, nb=nb, lc=lc),
        grid=(nseq // nb,),
        in_specs=[
            pl.BlockSpec((rows, QK_WIDTH), tok_map(1)),
            pl.BlockSpec((rows, V_WIDTH), tok_map(2)),
            pl.BlockSpec((rows, V_WIDTH), tok_map(3)),
            pl.BlockSpec((rows, LANES), tok_map(0)),
            pl.BlockSpec((CONV_W, QK_WIDTH), const2),
            pl.BlockSpec((1, QK_WIDTH), const2),
            pl.BlockSpec((1, LANES), const2),
            pl.BlockSpec((1, V_WIDTH), const2),
            pl.BlockSpec((nb, CONV_W - 1, QK_WIDTH), st3),
            pl.BlockSpec((nb, HEADS, DV, DQK), st4),
            pl.BlockSpec((rows, HEADS * DQK), tok_map(0)),
            pl.BlockSpec((rows, LANES), tok_map(0)),
        ],
        out_specs=[
            pl.BlockSpec((rows, V_WIDTH), tok_map(0)),
            pl.BlockSpec((nb, CONV_W - 1, QK_WIDTH), st3),
            pl.BlockSpec((nb, HEADS, DV, DQK), st4),
            pl.BlockSpec((rows, HEADS * DQK), tok_map(0)),
            pl.BlockSpec((rows, LANES), tok_map(0)),
        ],
        out_shape=[
            jax.ShapeDtypeStruct((tok, V_WIDTH), F32),
            jax.ShapeDtypeStruct((nseq, CONV_W - 1, QK_WIDTH), F32),
            jax.ShapeDtypeStruct((nseq, HEADS, DV, DQK), F32),
            jax.ShapeDtypeStruct((tok, HEADS * DQK), F32),
            jax.ShapeDtypeStruct((tok, LANES), F32),
        ],
        scratch_shapes=[pltpu.VMEM((nb, lc + 8, QK_WIDTH), F32), pltpu.VMEM((rows, QK_WIDTH), F32)],
        compiler_params=_cparams(("parallel",)),
        name="mlstm_step",
    )(z, z, z, zif, cw, cb, bif, ng, buf0, c0, n0x, m0x)
    n = nox[::lc].reshape(nseq, HEADS, DQK)
    m = mox[::lc, :HEADS]
    return y, buf, c, n, m


def _merge_kernel(h_ref, ys_ref, ym_ref, g1_ref, g2_ref, wglu_ref, wbs_ref, wbm_ref, wo_ref, o_ref):
    ys = ys_ref[...]
    glu = ys * jax.nn.sigmoid(_dot(ys.astype(BF16), wglu_ref[...]))
    a = _dot(glu.astype(BF16), wbs_ref[...])
    b = _dot(ym_ref[...].astype(BF16), wbm_ref[...])
    merged = jax.nn.sigmoid(g1_ref[...]) * a + jax.nn.sigmoid(g2_ref[...]) * b
    o_ref[...] = h_ref[...] + _dot(merged.astype(BF16), wo_ref[...])


def _merge(h, ys, ym, z, wglu, wbs, wbm, wo, *, n, tm):
    row = lambda i: (i, 0)
    const = lambda i: (0, 0)
    resident = lambda shape: pl.BlockSpec(shape, const, pipeline_mode=pl.Buffered(1))
    return pl.pallas_call(
        _merge_kernel,
        grid=(n // tm,),
        in_specs=[
            pl.BlockSpec((tm, D_MODEL), row),
            pl.BlockSpec((tm, S5_WIDTH), row),
            pl.BlockSpec((tm, V_WIDTH), row),
            pl.BlockSpec((tm, D_MODEL), lambda i: (i, 2)),
            pl.BlockSpec((tm, D_MODEL), lambda i: (i, 3)),
            resident((S5_WIDTH, S5_WIDTH)),
            resident((S5_WIDTH, D_MODEL)),
            resident((V_WIDTH, D_MODEL)),
            resident((D_MODEL, D_MODEL)),
        ],
        out_specs=pl.BlockSpec((tm, D_MODEL), row),
        out_shape=jax.ShapeDtypeStruct((n, D_MODEL), F32),
        compiler_params=_cparams(("parallel",)),
        name="merge",
    )(h, ys, ym, z, z, wglu, wbs, wbm, wo)


def kernel(x_prompt, x_sample, state_s5_re, state_s5_im, state_mlstm_C, state_mlstm_n, state_mlstm_m,
           state_mlstm_conv, meta_tokens, ffn1_norm, ffn1_w_gate, ffn1_w_up, ffn1_w_down, mix_norm, w_in,
           s5_A_re, s5_A_im, s5_log_dt, s5_B_re, s5_B_im, s5_C_re, s5_C_im, s5_D, s5_w_glu,
           mlstm_conv_w, mlstm_conv_b, mlstm_b_i, mlstm_b_f, mlstm_norm, w_branch_s5, w_branch_mlstm,
           w_out, ffn2_norm, ffn2_w_gate, ffn2_w_up, ffn2_w_down, final_norm):
    nbatch, seq, _ = x_prompt.shape
    nsamp, sseq, _ = x_sample.shape
    l = 0

    w1g, w1u, w1d = ffn1_w_gate[l], ffn1_w_up[l], ffn1_w_down[l]
    w2g, w2u, w2d = ffn2_w_gate[l], ffn2_w_up[l], ffn2_w_down[l]
    o_if = S5_WIDTH + QK_WIDTH + 2 * V_WIDTH
    o_gate = o_if + 2 * HEADS
    wt = w_in[l].T
    w2 = _win_halves(wt, o_gate, tr=WIN_CAST_ROWS)
    wglu, wbs, wbm, wo = (w[l].astype(BF16) for w in (s5_w_glu, w_branch_s5, w_branch_mlstm, w_out))
    g1 = ffn1_norm[l][None]
    gm = mix_norm[l][None]
    g2 = ffn2_norm[l][None]
    gf = final_norm[None]
    bif = jnp.pad(jnp.concatenate([mlstm_b_i[l], mlstm_b_f[l]]), (0, LANES - 2 * HEADS))[None]
    cw = mlstm_conv_w[l]
    cb = mlstm_conv_b[l][None]
    ng = mlstm_norm[l][None]
    d_skip = s5_D[l][None]

    fc, ec, cc, lam = _s5_operators(s5_A_re[l], s5_A_im[l], s5_log_dt[l], s5_B_re[l], s5_B_im[l],
                                    s5_C_re[l], s5_C_im[l])
    w_toe, f_bf, e_bf = _s5_prep(fc, ec, cc)

    def front(x, tm):
        n = x.shape[0]
        h1 = _ffn(x, g1, w1g, w1u, w1d, n=n, tm=tm)
        z, zif = _win(h1, gm, w2, wt, o_if=o_if, ngate=N_BRANCH * D_MODEL, tm=tm)
        return h1, z, zif

    def mixers(z, zif, s5_state, ml_state, *, nseq, seqlen, nb, lc):
        ys, sre, sim = _s5(z, w_toe, f_bf, e_bf, lam, d_skip, s5_state[0], s5_state[1], nseq=nseq, seqlen=seqlen)
        buf0, c0, n0, m0 = ml_state
        if lc == seqlen and nb > 1:
            ym, buf, c, n, m = _mlstm_step(z, zif, cw, cb, bif, ng, buf0, c0, n0, m0,
                                           nseq=nseq, seqlen=seqlen, nb=nb)
        else:
            ym, buf, c, n, m = _mlstm(z, zif, cw, cb, bif, ng, buf0, c0, n0, m0.reshape(nseq, 1, HEADS),
                                      nseq=nseq, seqlen=seqlen, nb=nb, lc=lc)
            m = m.reshape(nseq, HEADS)
        return ys, ym, (sre, sim), (buf, c, n, m)

    def back(h1, ys, ym, z, n):
        h2 = _merge(h1, ys, ym, z, wglu, wbs, wbm, wo, n=n, tm=MERGE_TM)
        return _ffn(h2, g2, w2g, w2u, w2d, gf, n=n, tm=FFN_TM)

    ntok_s = nsamp * sseq
    x_sm = jnp.concatenate([x_sample.reshape(ntok_s, D_MODEL), meta_tokens], axis=0)
    h1_sm, z_sm, zif_sm = front(x_sm, ntok_s + N_META)

    z_m = jnp.tile(z_sm[ntok_s:], (nbatch, 1))
    zif_m = jnp.tile(zif_sm[ntok_s:], (nbatch, 1))
    zeros = lambda *s: jnp.zeros((nbatch,) + s, F32)
    _, _, s5_m, ml_m = mixers(
        z_m, zif_m, (zeros(S5_GROUPS * S5_STATE), zeros(S5_GROUPS * S5_STATE)),
        (zeros(CONV_W - 1, QK_WIDTH), zeros(HEADS, DV, DQK), zeros(HEADS, DQK), zeros(HEADS)),
        nseq=nbatch, seqlen=N_META, nb=1, lc=N_META)

    ntok_p = nbatch * seq
    h1_p, z_p, zif_p = front(x_prompt.reshape(ntok_p, D_MODEL), FFN_TM)
    ys_p, ym_p, s5_p, ml_p = mixers(z_p, zif_p, s5_m, ml_m, nseq=nbatch, seqlen=seq, nb=1, lc=256)
    y_p = back(h1_p, ys_p, ym_p, z_p, ntok_p)

    ys_s, ym_s, s5_s, ml_s = mixers(
        z_sm, zif_sm,
        (state_s5_re[l].reshape(nsamp, -1), state_s5_im[l].reshape(nsamp, -1)),
        (state_mlstm_conv[l], state_mlstm_C[l], state_mlstm_n[l], state_mlstm_m[l]),
        nseq=nsamp, seqlen=sseq, nb=STEP_NB, lc=sseq)
    y_s = back(h1_sm, ys_s, ym_s, z_sm, ntok_s)

    def pack(n, s5_st, ml_st):
        buf, c, nn, m = ml_st
        return (s5_st[0].reshape(1, n, S5_GROUPS, S5_STATE), s5_st[1].reshape(1, n, S5_GROUPS, S5_STATE),
                c[None], nn[None], m[None], buf[None])

    return ((y_p.reshape(nbatch, seq, D_MODEL), y_s.reshape(nsamp, sseq, D_MODEL))
            + pack(nbatch, s5_p, ml_p) + pack(nsamp, s5_s, ml_s))
```

```python
import functools

import jax
import jax.numpy as jnp
from jax import lax
from jax.experimental import pallas as pl
from jax.experimental.pallas import tpu as pltpu

F32 = jnp.float32
BF16 = jnp.bfloat16

D_MODEL = 2048
D_FF = 5632
N_META = 16
S5_WIDTH = 1024
S5_GROUP = 16
S5_GROUPS = 64
S5_STATE = 64
HEADS = 4
DQK = 128
DV = 256
QK_WIDTH = 1024
V_WIDTH = 1024
CONV_W = 4
N_BRANCH = 2
EPS = 1e-6

LANES = 128
S5_T = 8
S5_GPB = LANES // S5_GROUP
S5_NBLK = S5_WIDTH // LANES
S5_SW = S5_GPB * S5_STATE
VMEM_LIMIT = 58 * 1024 * 1024
FFN_TM = 1024
MERGE_TM = 256
WIN_CAST_ROWS = 456
STEP_NB = 16


def _cparams(sem):
    return pltpu.CompilerParams(dimension_semantics=sem, vmem_limit_bytes=VMEM_LIMIT)


def _rmsnorm(x, g):
    ms = jnp.mean(x * x, axis=-1, keepdims=True)
    return (x * lax.rsqrt(ms + EPS)) * g


def _dot(a, b):
    return jnp.dot(a, b, preferred_element_type=F32)


def _dot_nt(a, b):
    return lax.dot_general(a, b, (((1,), (1,)), ((), ())), preferred_element_type=F32)


def _dot_tn(a, b):
    return lax.dot_general(a, b, (((0,), (0,)), ((), ())), preferred_element_type=F32)


def _split3(x):
    hi = x.astype(BF16)
    r = x - hi.astype(F32)
    mid = r.astype(BF16)
    lo = (r - mid.astype(F32)).astype(BF16)
    return hi, mid, lo


def _ffn_kernel(x_ref, g_ref, *rest, final_norm, ksplit):
    wgu_refs, rest = rest[:2 * ksplit], rest[2 * ksplit:]
    wd_ref, rest = rest[0], rest[1:]
    if final_norm:
        fg_ref, o_ref, xn_ref = rest
    else:
        o_ref, xn_ref = rest
    j = pl.program_id(1)
    last = pl.num_programs(1) - 1
    kc = D_MODEL // ksplit

    def down(xn):
        gt = sum(_dot(xn[:, c * kc:(c + 1) * kc], wgu_refs[c][...].astype(BF16)) for c in range(ksplit))
        up = sum(_dot(xn[:, c * kc:(c + 1) * kc], wgu_refs[ksplit + c][...].astype(BF16)) for c in range(ksplit))
        act = (gt * jax.nn.sigmoid(gt) * up).astype(BF16)
        return _dot(act, wd_ref[...].astype(BF16))

    @pl.when(j == 0)
    def _():
        xn = _rmsnorm(x_ref[...], g_ref[...]).astype(BF16)
        xn_ref[...] = xn
        o_ref[...] = down(xn)

    @pl.when(jnp.logical_and(j > 0, j < last))
    def _():
        o_ref[...] += down(xn_ref[...])

    @pl.when(j == last)
    def _():
        h = x_ref[...] + 0.5 * (o_ref[...] + down(xn_ref[...]))
        if final_norm:
            h = _rmsnorm(h, fg_ref[...])
        o_ref[...] = h


def _ffn(x, g, wg, wu, wd, final_g=None, *, n, tm, tf=256, x_buffers=2, ksplit=1):
    kc = D_MODEL // ksplit
    slab = lambda c: pl.BlockSpec((kc, tf), lambda i, j: (c, j))
    in_specs = [
        pl.BlockSpec((tm, D_MODEL), lambda i, j: (i, 0), pipeline_mode=pl.Buffered(x_buffers)),
        pl.BlockSpec((1, D_MODEL), lambda i, j: (0, 0)),
        *[slab(c) for c in range(ksplit)],
        *[slab(c) for c in range(ksplit)],
        pl.BlockSpec((tf, D_MODEL), lambda i, j: (j, 0)),
    ]
    args = [x, g] + [wg] * ksplit + [wu] * ksplit + [wd]
    if final_g is not None:
        in_specs.append(pl.BlockSpec((1, D_MODEL), lambda i, j: (0, 0)))
        args.append(final_g)
    return pl.pallas_call(
        functools.partial(_ffn_kernel, final_norm=final_g is not None, ksplit=ksplit),
        grid=(n // tm, D_FF // tf),
        in_specs=in_specs,
        out_specs=pl.BlockSpec((tm, D_MODEL), lambda i, j: (i, 0)),
        out_shape=jax.ShapeDtypeStruct((n, D_MODEL), F32),
        scratch_shapes=[pltpu.VMEM((tm, D_MODEL), BF16)],
        compiler_params=_cparams(("parallel", "arbitrary")),
        name="ffn",
    )(*args)


def _cast_rows_kernel(w_ref, o_ref, *, tail):
    last = pl.num_programs(0) - 1

    @pl.when(pl.program_id(0) < last)
    def _():
        o_ref[0] = w_ref[...].astype(BF16)

    @pl.when(pl.program_id(0) == last)
    def _():
        o_ref[0, :tail] = w_ref[:tail].astype(BF16)
        o_ref[0, tail:] = jnp.zeros((o_ref.shape[1] - tail, o_ref.shape[2]), BF16)


def _win_halves(wt, o_gate, *, tr):
    rows, k = wt.shape
    per = o_gate // tr
    tail = rows - (2 * per - 1) * tr
    assert per * tr == o_gate and tr % 8 == 0 and 0 < tail < tr and tail % 8 == 0
    return pl.pallas_call(
        functools.partial(_cast_rows_kernel, tail=tail),
        grid=(2 * per,),
        in_specs=[pl.BlockSpec((tr, k), lambda i: (i, 0))],
        out_specs=pl.BlockSpec((1, tr, k), lambda i: (i // per, i % per, 0)),
        out_shape=jax.ShapeDtypeStruct((2, o_gate, k), BF16),
        compiler_params=_cparams(("parallel",)),
        name="w_in_cast",
    )(wt)


def _win_kernel(h_ref, g_ref, wa_ref, wb_ref, wif_ref, z_ref, zif_ref, un_ref, *, na):
    j = pl.program_id(1)

    @pl.when(j == 0)
    def _():
        un = _rmsnorm(h_ref[...], g_ref[...]).astype(BF16)
        un_ref[...] = un
        zif_ref[...] = _dot_nt(un, wif_ref[...].astype(BF16))

    @pl.when(j < na)
    def _():
        z_ref[...] = _dot_nt(un_ref[...], wa_ref[0])

    @pl.when(j >= na)
    def _():
        z_ref[...] = _dot_nt(un_ref[...], wb_ref[0])


def _win(h, g, w2, wt, *, o_if, ngate, tm, tn=1024):
    n = h.shape[0]
    na = o_if // tn
    nb = ngate // tn
    return pl.pallas_call(
        functools.partial(_win_kernel, na=na),
        grid=(n // tm, na + nb),
        in_specs=[
            pl.BlockSpec((tm, D_MODEL), lambda i, j: (i, 0)),
            pl.BlockSpec((1, D_MODEL), lambda i, j: (0, 0)),
            pl.BlockSpec((1, tn, D_MODEL), lambda i, j: (0, jnp.minimum(j, na - 1), 0)),
            pl.BlockSpec((1, tn, D_MODEL), lambda i, j: (1, jnp.maximum(j - na, 0), 0)),
            pl.BlockSpec((LANES, D_MODEL), lambda i, j: (o_if // LANES, 0)),
        ],
        out_specs=[
            pl.BlockSpec((tm, tn), lambda i, j: (i, j)),
            pl.BlockSpec((tm, LANES), lambda i, j: (i, 0)),
        ],
        out_shape=[
            jax.ShapeDtypeStruct((n, (na + nb) * tn), F32),
            jax.ShapeDtypeStruct((n, LANES), F32),
        ],
        scratch_shapes=[pltpu.VMEM((tm, D_MODEL), BF16)],
        compiler_params=_cparams(("parallel", "arbitrary")),
        name="w_in",
    )(h, g, w2, w2, wt)


def _s5_prep_kernel(fc_ref, ec_ref, cc_ref, w_ref, f_ref, e_ref):
    n = S5_T * LANES
    sw2 = 2 * S5_SW
    i32 = jnp.int32
    lg_h = S5_GROUP.bit_length() - 1
    lg_p = S5_STATE.bit_length() - 1
    lg_sw = S5_SW.bit_length() - 1

    def tile_mat(k, c, src_col):
        kk = lax.broadcasted_iota(i32, (k, c), 0)
        cc = lax.broadcasted_iota(i32, (k, c), 1)
        return jnp.where(kk == src_col(cc), 1.0, 0.0).astype(BF16)

    def group_mask(r, c, row_group, col_group):
        rr = lax.broadcasted_iota(i32, (r, c), 0)
        cc = lax.broadcasted_iota(i32, (r, c), 1)
        return row_group(rr) == col_group(cc)

    gmask = S5_GPB - 1
    rf = tile_mat(2 * S5_STATE, sw2, lambda c: ((c >> lg_sw) << lg_p) + (c & (S5_STATE - 1)))
    mf = group_mask(n, sw2, lambda r: (r >> lg_h) & gmask, lambda c: (c >> lg_p) & gmask)
    f_hi, f_mid, _ = _split3(fc_ref[0])
    x_hi = jnp.where(mf, _dot(f_hi, rf), 0.0).astype(BF16)
    x_mid = jnp.where(mf, _dot(f_mid, rf), 0.0).astype(BF16)
    f_ref[0] = x_hi
    lg_l = LANES.bit_length() - 1
    re = tile_mat(LANES, n, lambda c: ((c >> lg_l) << lg_h) + (c & (S5_GROUP - 1)))
    me = group_mask(sw2, n, lambda r: (r >> lg_p) & gmask, lambda c: (c >> lg_h) & gmask)
    e_ref[0] = jnp.where(me, _dot(ec_ref[0].astype(BF16), re), 0.0).astype(BF16)
    rc = tile_mat(S5_GROUP, LANES, lambda c: c & (S5_GROUP - 1))
    mc = group_mask(sw2, LANES, lambda r: (r >> lg_p) & gmask, lambda c: c >> lg_h)
    c_hi, c_mid, _ = _split3(cc_ref[0])
    ct_hi = jnp.where(mc, _dot(c_hi, rc), 0.0).astype(BF16)
    ct_mid = jnp.where(mc, _dot(c_mid, rc), 0.0).astype(BF16)
    w_ref[...] = jnp.zeros(w_ref.shape, w_ref.dtype)
    for lag in range(S5_T):
        r0 = (S5_T - 1 - lag) * LANES
        xh = x_hi[r0:r0 + LANES, :]
        xm = x_mid[r0:r0 + LANES, :]
        k = (_dot(xh, ct_hi) + _dot(xm, ct_hi) + _dot(xh, ct_mid)).astype(BF16)
        for s in range(S5_T - lag):
            t = s + lag
            w_ref[0, s * LANES:(s + 1) * LANES, t * LANES:(t + 1) * LANES] = k


def _s5_prep(fc, ec, cc):
    n = S5_T * LANES
    blk = lambda r, c: pl.BlockSpec((1, r, c), lambda j: (j, 0, 0))
    return pl.pallas_call(
        _s5_prep_kernel,
        grid=(S5_NBLK,),
        in_specs=[blk(n, 2 * S5_STATE), blk(2 * S5_SW, LANES), blk(2 * S5_SW, S5_GROUP)],
        out_specs=[blk(n, n), blk(n, 2 * S5_SW), blk(2 * S5_SW, n)],
        out_shape=[
            jax.ShapeDtypeStruct((S5_NBLK, n, n), BF16),
            jax.ShapeDtypeStruct((S5_NBLK, n, 2 * S5_SW), BF16),
            jax.ShapeDtypeStruct((S5_NBLK, 2 * S5_SW, n), BF16),
        ],
        compiler_params=_cparams(("parallel",)),
        name="s5_prep",
    )(fc, ec, cc)


def _s5_kernel(x_ref, w_ref, f_ref, e_ref, lam_ref, d_ref, h0re_ref, h0im_ref,
               y_ref, ore_ref, oim_ref, yi_ref, s_ref, hin_ref, *, nseq, nblk, rt):
    rows = nseq * nblk
    ar = lam_ref[0, 0:1, :]
    ai = lam_ref[0, 1:2, :]

    def load_u(r0):
        return jnp.concatenate(
            [x_ref[pl.ds(r0 * S5_T + s, rt, stride=S5_T), :] for s in range(S5_T)], axis=1)

    for r0 in range(0, rows, rt):
        ub = load_u(r0).astype(BF16)
        yi_ref[r0:r0 + rt, :] = _dot(ub, w_ref[0])
        s_ref[r0:r0 + rt, :] = _dot(ub, f_ref[0])

    if nblk == 1:
        hre = h0re_ref[...]
        him = h0im_ref[...]
        hin_ref[:, :S5_SW] = hre
        hin_ref[:, S5_SW:] = him
        ore_ref[...] = ar * hre - ai * him + s_ref[:, :S5_SW]
        oim_ref[...] = ar * him + ai * hre + s_ref[:, S5_SW:]
    else:
        def body(c, carry):
            new = []
            for b in range(nseq):
                hre, him = carry[2 * b], carry[2 * b + 1]
                row = b * nblk + c
                hin_ref[pl.ds(row, 1), :S5_SW] = hre
                hin_ref[pl.ds(row, 1), S5_SW:] = him
                sre = s_ref[pl.ds(row, 1), :S5_SW]
                sim = s_ref[pl.ds(row, 1), S5_SW:]
                new.append(ar * hre - ai * him + sre)
                new.append(ar * him + ai * hre + sim)
            return tuple(new)

        init = []
        for b in range(nseq):
            init.append(h0re_ref[b:b + 1, :])
            init.append(h0im_ref[b:b + 1, :])
        fin = lax.fori_loop(0, nblk, body, tuple(init))
        for b in range(nseq):
            ore_ref[b:b + 1, :] = fin[2 * b]
            oim_ref[b:b + 1, :] = fin[2 * b + 1]

    dt = jnp.concatenate([d_ref[...]] * S5_T, axis=1)
    for r0 in range(0, rows, rt):
        yo = _dot(hin_ref[r0:r0 + rt, :].astype(BF16), e_ref[0])
        y = jax.nn.gelu(yi_ref[r0:r0 + rt, :] + yo + load_u(r0) * dt)
        for t in range(S5_T):
            y_ref[pl.ds(r0 * S5_T + t, rt, stride=S5_T), :] = y[:, t * LANES:(t + 1) * LANES]


def _s5(z, w, f, e, lam, d, h0re, h0im, *, nseq, seqlen):
    nblk = seqlen // S5_T
    rows = nseq * nblk
    rt = min(rows, 256)
    n = S5_T * LANES
    tok = nseq * seqlen
    kern = functools.partial(_s5_kernel, nseq=nseq, nblk=nblk, rt=rt)
    return pl.pallas_call(
        kern,
        grid=(S5_NBLK,),
        in_specs=[
            pl.BlockSpec((tok, LANES), lambda j: (0, j)),
            pl.BlockSpec((1, n, n), lambda j: (j, 0, 0)),
            pl.BlockSpec((1, n, 2 * S5_SW), lambda j: (j, 0, 0)),
            pl.BlockSpec((1, 2 * S5_SW, n), lambda j: (j, 0, 0)),
            pl.BlockSpec((1, 2, S5_SW), lambda j: (j, 0, 0)),
            pl.BlockSpec((1, LANES), lambda j: (0, j)),
            pl.BlockSpec((nseq, S5_SW), lambda j: (0, j)),
            pl.BlockSpec((nseq, S5_SW), lambda j: (0, j)),
        ],
        out_specs=[
            pl.BlockSpec((tok, LANES), lambda j: (0, j)),
            pl.BlockSpec((nseq, S5_SW), lambda j: (0, j)),
            pl.BlockSpec((nseq, S5_SW), lambda j: (0, j)),
        ],
        out_shape=[
            jax.ShapeDtypeStruct((tok, S5_WIDTH), F32),
            jax.ShapeDtypeStruct((nseq, S5_GROUPS * S5_STATE), F32),
            jax.ShapeDtypeStruct((nseq, S5_GROUPS * S5_STATE), F32),
        ],
        scratch_shapes=[
            pltpu.VMEM((rows, n), F32),
            pltpu.VMEM((rows, 2 * S5_SW), F32),
            pltpu.VMEM((rows, 2 * S5_SW), F32),
        ],
        compiler_params=_cparams(("parallel",)),
        name="s5",
    )(z, w, f, e, lam, d, h0re, h0im)


def _s5_operators(a_re, a_im, log_dt, b_re, b_im, c_re, c_im):
    dt = jnp.exp(log_dt)[:, None]
    mag = jnp.exp(a_re * dt)
    abar_re = mag * jnp.cos(a_im * dt)
    abar_im = mag * jnp.sin(a_im * dt)
    nr, ni = abar_re - 1.0, abar_im
    den = a_re * a_re + a_im * a_im
    q_re = (nr * a_re + ni * a_im) / den
    q_im = (ni * a_re - nr * a_im) / den
    bb_re = q_re[:, :, None] * b_re - q_im[:, :, None] * b_im
    bb_im = q_re[:, :, None] * b_im + q_im[:, :, None] * b_re
    taus = jnp.arange(S5_T + 1, dtype=F32)[:, None, None]
    pmag = jnp.exp(taus * (a_re * dt)[None])
    pw_re = pmag * jnp.cos(taus * (a_im * dt)[None])
    pw_im = pmag * jnp.sin(taus * (a_im * dt)[None])
    n = S5_T * LANES

    lags = (S5_T - 1) - jnp.arange(S5_T, dtype=F32)[:, None, None]
    lmag = jnp.exp(lags * (a_re * dt)[None])
    lr = (lmag * jnp.cos(lags * (a_im * dt)[None]))[:, :, :, None]
    li = (lmag * jnp.sin(lags * (a_im * dt)[None]))[:, :, :, None]
    fb = jnp.stack([lr * bb_re[None] - li * bb_im[None], lr * bb_im[None] + li * bb_re[None]])
    fc = fb.reshape(2, S5_T, S5_NBLK, S5_GPB, S5_STATE, S5_GROUP).transpose(2, 1, 3, 5, 0, 4)
    fc = fc.reshape(S5_NBLK, n, 2 * S5_STATE)

    cp_re = jnp.swapaxes(c_re, -1, -2)[None]
    cp_im = jnp.swapaxes(c_im, -1, -2)[None]
    tr = pw_re[1:][:, :, :, None]
    ti = pw_im[1:][:, :, :, None]
    ce = jnp.stack([cp_re * tr - cp_im * ti, -(cp_re * ti + cp_im * tr)])
    ec = ce.reshape(2, S5_T, S5_NBLK, S5_GPB, S5_STATE, S5_GROUP).transpose(2, 0, 3, 4, 1, 5)
    ec = ec.reshape(S5_NBLK, 2 * S5_SW, LANES)

    cc = jnp.stack([cp_re[0], -cp_im[0]]).reshape(2, S5_NBLK, S5_GPB, S5_STATE, S5_GROUP)
    cc = cc.transpose(1, 0, 2, 3, 4).reshape(S5_NBLK, 2 * S5_SW, S5_GROUP)

    lam = jnp.stack([pw_re[S5_T].reshape(S5_NBLK, S5_SW), pw_im[S5_T].reshape(S5_NBLK, S5_SW)], axis=1)
    return fc, ec, cc, lam


def _mlstm_kernel(qk_ref, v_ref, o_ref, if_ref, cw_ref, cb_ref, bif_ref, ng_ref,
                  buf0_ref, c0_ref, n0_ref, m0_ref,
                  y_ref, bufo_ref, co_ref, no_ref, mo_ref, xp_ref, *, nb, lc, carry):
    if carry:
        @pl.when(pl.program_id(1) == 0)
        def _():
            bufo_ref[...] = buf0_ref[...]
            co_ref[...] = c0_ref[...]
            no_ref[...] = n0_ref[...]
            mo_ref[...] = m0_ref[...]
        bufs_ref, cs_ref, ns_ref, ms_ref = bufo_ref, co_ref, no_ref, mo_ref
    else:
        bufs_ref, cs_ref, ns_ref, ms_ref = buf0_ref, c0_ref, n0_ref, m0_ref

    lp = max(lc, LANES)
    head_row = lax.broadcasted_iota(jnp.int32, (HEADS, DQK), 0)
    head_lane = lax.broadcasted_iota(jnp.int32, (1, HEADS), 1)
    row = lax.broadcasted_iota(jnp.int32, (lc, lc), 0)
    col = lax.broadcasted_iota(jnp.int32, (lc, lc), 1)
    causal = row >= col
    tril = jnp.where(causal, 1.0, 0.0).astype(BF16)
    lane = lax.broadcasted_iota(jnp.int32, (lc, LANES), 1)
    is_i = lane < HEADS
    is_f = jnp.logical_and(lane >= HEADS, lane < 2 * HEADS)

    for s in range(nb):
        r0 = s * lc
        xp_ref[s, 5:8, :] = bufs_ref[s]
        xp_ref[s, 8:8 + lc, :] = qk_ref[r0:r0 + lc, :]
        conv = cb_ref[...] + sum(xp_ref[s, 5 + j:5 + j + lc, :] * cw_ref[j:j + 1, :] for j in range(CONV_W))
        bufo_ref[s] = xp_ref[s, 5 + lc:8 + lc, :]
        n_all = ns_ref[s]
        m_all = ms_ref[s]
        n_new_all = jnp.zeros((HEADS, DQK), F32)
        m_new_all = jnp.zeros((1, HEADS), F32)
        qk = conv * jax.nn.sigmoid(conv)
        q = qk[:, :QK_WIDTH // 2] * (DQK ** -0.5)
        k = qk[:, QK_WIDTH // 2:]

        gate = if_ref[r0:r0 + lc, :] + bif_ref[...]
        lf = jnp.where(is_f, jax.nn.log_sigmoid(gate), 0.0)
        parts = _split3(lf)
        bcum = _dot(tril, parts[0]) + _dot(tril, parts[1]) + _dot(tril, parts[2])
        pc = jnp.where(is_i, gate, bcum)
        if lc < lp:
            pc_t = jnp.concatenate([pc, jnp.zeros((lp - lc, LANES), F32)], axis=0).T[:, :lc]
        else:
            pc_t = pc.T

        for h in range(HEADS):
            i_col = pc[:, h:h + 1]
            b_col = pc[:, HEADS + h:HEADS + h + 1]
            i_row = pc_t[h:h + 1, :]
            b_row = pc_t[HEADS + h:HEADS + h + 1, :]
            m0 = m_all[:, h:h + 1]
            logw = jnp.where(causal, b_col - b_row + i_row, -jnp.inf)
            g = b_col + m0
            m = jnp.maximum(g, jnp.max(logw, axis=-1, keepdims=True))
            w = jnp.exp(logw - m)
            inter = jnp.exp(g - m)
            qh = q[:, h * DQK:(h + 1) * DQK]
            kh = k[:, h * DQK:(h + 1) * DQK]
            vh = v_ref[r0:r0 + lc, h * DV:(h + 1) * DV]
            qb = qh.astype(BF16)
            vb = vh.astype(BF16)
            c_prev = cs_ref[s, h]
            n_prev = n_all[h:h + 1, :]
            sc = _dot_nt(qb, kh.astype(BF16)) * w
            num = _dot(sc.astype(BF16), vb) + inter * _dot_nt(qb, c_prev.astype(BF16))
            den = jnp.sum(sc, axis=-1, keepdims=True) + inter * jnp.sum(qh * n_prev, axis=-1, keepdims=True)
            hh = num / jnp.maximum(jnp.abs(den), jnp.exp(-m))
            hh = hh * lax.rsqrt(jnp.mean(hh * hh, axis=-1, keepdims=True) + EPS)
            og = jax.nn.sigmoid(o_ref[r0:r0 + lc, h * DV:(h + 1) * DV])
            y_ref[r0:r0 + lc, h * DV:(h + 1) * DV] = hh * ng_ref[:, h * DV:(h + 1) * DV] * og

            b_last = b_col[lc - 1:lc, :]
            m_new = m[lc - 1:lc, :]
            decay = jnp.exp(b_last + m0 - m_new)
            kw = kh * jnp.exp(b_last - b_col + i_col - m_new)
            co_ref[s, h] = decay * c_prev + _dot_tn(vb, kw.astype(BF16))
            n_new = decay * n_prev + jnp.sum(kw, axis=0, keepdims=True)
            n_new_all = jnp.where(head_row == h, n_new, n_new_all)
            m_new_all = jnp.where(head_lane == h, m_new, m_new_all)
        no_ref[s] = n_new_all
        mo_ref[s] = m_new_all


def _mlstm(z, zif, cw, cb, bif, ng, buf0, c0, n0, m0, *, nseq, seqlen, nb, lc):
    tok = nseq * seqlen
    nchunk = seqlen // lc
    rows = nb * lc
    tok_map = lambda col: (lambda i, c: (i * nchunk + c, col))
    const2 = lambda i, c: (0, 0)
    st3 = lambda i, c: (i, 0, 0)
    st4 = lambda i, c: (i, 0, 0, 0)
    kern = functools.partial(_mlstm_kernel, nb=nb, lc=lc, carry=nchunk > 1)
    return pl.pallas_call(
        kern,
        grid=(nseq // nb, nchunk),
        in_specs=[
            pl.BlockSpec((rows, QK_WIDTH), tok_map(1)),
            pl.BlockSpec((rows, V_WIDTH), tok_map(2)),
            pl.BlockSpec((rows, V_WIDTH), tok_map(3)),
            pl.BlockSpec((rows, LANES), tok_map(0)),
            pl.BlockSpec((CONV_W, QK_WIDTH), const2),
            pl.BlockSpec((1, QK_WIDTH), const2),
            pl.BlockSpec((1, LANES), const2),
            pl.BlockSpec((1, V_WIDTH), const2),
            pl.BlockSpec((nb, CONV_W - 1, QK_WIDTH), st3),
            pl.BlockSpec((nb, HEADS, DV, DQK), st4),
            pl.BlockSpec((nb, HEADS, DQK), st3),
            pl.BlockSpec((nb, 1, HEADS), st3),
        ],
        out_specs=[
            pl.BlockSpec((rows, V_WIDTH), tok_map(0)),
            pl.BlockSpec((nb, CONV_W - 1, QK_WIDTH), st3),
            pl.BlockSpec((nb, HEADS, DV, DQK), st4),
            pl.BlockSpec((nb, HEADS, DQK), st3),
            pl.BlockSpec((nb, 1, HEADS), st3),
        ],
        out_shape=[
            jax.ShapeDtypeStruct((tok, V_WIDTH), F32),
            jax.ShapeDtypeStruct((nseq, CONV_W - 1, QK_WIDTH), F32),
            jax.ShapeDtypeStruct((nseq, HEADS, DV, DQK), F32),
            jax.ShapeDtypeStruct((nseq, HEADS, DQK), F32),
            jax.ShapeDtypeStruct((nseq, 1, HEADS), F32),
        ],
        scratch_shapes=[pltpu.VMEM((nb, lc + 8, QK_WIDTH), F32)],
        compiler_params=_cparams(("parallel", "arbitrary")),
        name="mlstm",
    )(z, z, z, zif, cw, cb, bif, ng, buf0, c0, n0, m0)


def _mlstm_step_kernel(qk_ref, v_ref, o_ref, if_ref, cw_ref, cb_ref, bif_ref, ng_ref,
                       buf0_ref, c0_ref, n0x_ref, m0x_ref,
                       y_ref, bufo_ref, co_ref, nox_ref, mox_ref, xp_ref, conv_ref, *, nb, lc):
    rows = nb * lc
    lg = lc.bit_length() - 1
    i32 = jnp.int32
    row = lax.broadcasted_iota(i32, (rows, rows), 0)
    col = lax.broadcasted_iota(i32, (rows, rows), 1)
    same = (row >> lg) == (col >> lg)
    causal = jnp.logical_and(same, row >= col)
    same_b = jnp.where(same, 1.0, 0.0).astype(BF16)
    tril_b = jnp.where(causal, 1.0, 0.0).astype(BF16)
    lane = lax.broadcasted_iota(i32, (rows, LANES), 1)
    is_i = lane < HEADS
    is_f = jnp.logical_and(lane >= HEADS, lane < 2 * HEADS)

    def seg_dot(mat, x):
        hi, mid, lo = _split3(x)
        return _dot(mat, hi) + _dot(mat, mid) + _dot(mat, lo)

    for s in range(nb):
        xp_ref[s, 5:8, :] = buf0_ref[s]
        xp_ref[s, 8:8 + lc, :] = qk_ref[s * lc:(s + 1) * lc, :]
        conv_ref[s * lc:(s + 1) * lc, :] = cb_ref[...] + sum(
            xp_ref[s, 5 + j:5 + j + lc, :] * cw_ref[j:j + 1, :] for j in range(CONV_W))
        bufo_ref[s] = xp_ref[s, 5 + lc:8 + lc, :]
    conv = conv_ref[...]
    qk = conv * jax.nn.sigmoid(conv)
    q = qk[:, :QK_WIDTH // 2] * (DQK ** -0.5)
    k = qk[:, QK_WIDTH // 2:]

    gate = if_ref[...] + bif_ref[...]
    lf = jnp.where(is_f, jax.nn.log_sigmoid(gate), 0.0)
    bcum = seg_dot(tril_b, lf)
    btot = seg_dot(same_b, lf)
    pc = jnp.where(is_i, gate, bcum)
    if rows < LANES:
        pc_t = jnp.concatenate([pc, jnp.zeros((LANES - rows, LANES), F32)], axis=0).T[:, :rows]
    else:
        pc_t = pc.T
    m0x = m0x_ref[...]
    mox = jnp.zeros((rows, LANES), F32)

    for h in range(HEADS):
        i_col = pc[:, h:h + 1]
        b_col = pc[:, HEADS + h:HEADS + h + 1]
        i_row = pc_t[h:h + 1, :]
        b_row = pc_t[HEADS + h:HEADS + h + 1, :]
        b_last = btot[:, HEADS + h:HEADS + h + 1]
        m0 = m0x[:, h:h + 1]
        lw = b_col - b_row + i_row
        logw = jnp.where(causal, lw, -jnp.inf)
        g = b_col + m0
        m = jnp.maximum(g, jnp.max(logw, axis=-1, keepdims=True))
        w = jnp.exp(logw - m)
        inter = jnp.exp(g - m)
        lw_end = jnp.where(same, b_last - b_row + i_row, -jnp.inf)
        m_new = jnp.maximum(b_last + m0, jnp.max(lw_end, axis=-1, keepdims=True))
        decay = jnp.exp(b_last + m0 - m_new)
        qh = q[:, h * DQK:(h + 1) * DQK]
        kh = k[:, h * DQK:(h + 1) * DQK]
        qb = qh.astype(BF16)
        vb = v_ref[:, h * DV:(h + 1) * DV].astype(BF16)
        n_prev = n0x_ref[:, h * DQK:(h + 1) * DQK]
        sc = _dot_nt(qb, kh.astype(BF16)) * w
        carried = jnp.concatenate(
            [_dot_nt(qb[s * lc:(s + 1) * lc], c0_ref[s, h].astype(BF16)) for s in range(nb)], axis=0)
        num = _dot(sc.astype(BF16), vb) + inter * carried
        den = jnp.sum(sc, axis=-1, keepdims=True) + inter * jnp.sum(qh * n_prev, axis=-1, keepdims=True)
        hh = num / jnp.maximum(jnp.abs(den), jnp.exp(-m))
        hh = hh * lax.rsqrt(jnp.mean(hh * hh, axis=-1, keepdims=True) + EPS)
        og = jax.nn.sigmoid(o_ref[:, h * DV:(h + 1) * DV])
        y_ref[:, h * DV:(h + 1) * DV] = hh * ng_ref[:, h * DV:(h + 1) * DV] * og

        kw = kh * jnp.exp(b_last - b_col + i_col - m_new)
        kwb = kw.astype(BF16)
        for s in range(nb):
            r0 = s * lc
            co_ref[s, h] = decay[r0:r0 + 1, :] * c0_ref[s, h] + _dot_tn(vb[r0:r0 + lc], kwb[r0:r0 + lc])
        nox_ref[:, h * DQK:(h + 1) * DQK] = decay * n_prev + seg_dot(same_b, kw)
        mox = jnp.where(lane == h, m_new, mox)
    mox_ref[...] = mox


def _mlstm_step(z, zif, cw, cb, bif, ng, buf0, c0, n0, m0, *, nseq, seqlen, nb):
    lc = seqlen
    assert lc & (lc - 1) == 0 and lc >= CONV_W - 1
    tok = nseq * seqlen
    rows = nb * lc
    n0x = jnp.repeat(n0.reshape(nseq, HEADS * DQK), lc, axis=0)
    m0x = jnp.repeat(jnp.pad(m0, ((0, 0), (0, LANES - HEADS))), lc, axis=0)
    tok_map = lambda col: (lambda i: (i, col))
    const2 = lambda i: (0, 0)
    st3 = lambda i: (i, 0, 0)
    st4 = lambda i: (i, 0, 0, 0)
    y, buf, c, nox, mox = pl.pallas_call(
        functools.partial(_mlstm_step_kernel, nb=nb, lc=lc),
        grid=(nseq // nb,),
        in_specs=[
            pl.BlockSpec((rows, QK_WIDTH), tok_map(1)),
            pl.BlockSpec((rows, V_WIDTH), tok_map(2)),
            pl.BlockSpec((rows, V_WIDTH), tok_map(3)),
            pl.BlockSpec((rows, LANES), tok_map(0)),
            pl.BlockSpec((CONV_W, QK_WIDTH), const2),
            pl.BlockSpec((1, QK_WIDTH), const2),
            pl.BlockSpec((1, LANES), const2),
            pl.BlockSpec((1, V_WIDTH), const2),
            pl.BlockSpec((nb, CONV_W - 1, QK_WIDTH), st3),
            pl.BlockSpec((nb, HEADS, DV, DQK), st4),
            pl.BlockSpec((rows, HEADS * DQK), tok_map(0)),
            pl.BlockSpec((rows, LANES), tok_map(0)),
        ],
        out_specs=[
            pl.BlockSpec((rows, V_WIDTH), tok_map(0)),
            pl.BlockSpec((nb, CONV_W - 1, QK_WIDTH), st3),
            pl.BlockSpec((nb, HEADS, DV, DQK), st4),
            pl.BlockSpec((rows, HEADS * DQK), tok_map(0)),
            pl.BlockSpec((rows, LANES), tok_map(0)),
        ],
        out_shape=[
            jax.ShapeDtypeStruct((tok, V_WIDTH), F32),
            jax.ShapeDtypeStruct((nseq, CONV_W - 1, QK_WIDTH), F32),
            jax.ShapeDtypeStruct((nseq, HEADS, DV, DQK), F32),
            jax.ShapeDtypeStruct((tok, HEADS * DQK), F32),
            jax.ShapeDtypeStruct((tok, LANES), F32),
        ],
        scratch_shapes=[pltpu.VMEM((nb, lc + 8, QK_WIDTH), F32), pltpu.VMEM((rows, QK_WIDTH), F32)],
        compiler_params=_cparams(("parallel",)),
        name="mlstm_step",
    )(z, z, z, zif, cw, cb, bif, ng, buf0, c0, n0x, m0x)
    n = nox[::lc].reshape(nseq, HEADS, DQK)
    m = mox[::lc, :HEADS]
    return y, buf, c, n, m


def _merge_kernel(h_ref, ys_ref, ym_ref, g1_ref, g2_ref, wglu_ref, wbs_ref, wbm_ref, wo_ref, o_ref):
    ys = ys_ref[...]
    glu = ys * jax.nn.sigmoid(_dot(ys.astype(BF16), wglu_ref[...]))
    a = _dot(glu.astype(BF16), wbs_ref[...])
    b = _dot(ym_ref[...].astype(BF16), wbm_ref[...])
    merged = jax.nn.sigmoid(g1_ref[...]) * a + jax.nn.sigmoid(g2_ref[...]) * b
    o_ref[...] = h_ref[...] + _dot(merged.astype(BF16), wo_ref[...])


def _merge(h, ys, ym, z, wglu, wbs, wbm, wo, *, n, tm):
    row = lambda i: (i, 0)
    const = lambda i: (0, 0)
    resident = lambda shape: pl.BlockSpec(shape, const, pipeline_mode=pl.Buffered(1))
    return pl.pallas_call(
        _merge_kernel,
        grid=(n // tm,),
        in_specs=[
            pl.BlockSpec((tm, D_MODEL), row),
            pl.BlockSpec((tm, S5_WIDTH), row),
            pl.BlockSpec((tm, V_WIDTH), row),
            pl.BlockSpec((tm, D_MODEL), lambda i: (i, 2)),
            pl.BlockSpec((tm, D_MODEL), lambda i: (i, 3)),
            resident((S5_WIDTH, S5_WIDTH)),
            resident((S5_WIDTH, D_MODEL)),
            resident((V_WIDTH, D_MODEL)),
            resident((D_MODEL, D_MODEL)),
        ],
        out_specs=pl.BlockSpec((tm, D_MODEL), row),
        out_shape=jax.ShapeDtypeStruct((n, D_MODEL), F32),
        compiler_params=_cparams(("parallel",)),
        name="merge",
    )(h, ys, ym, z, z, wglu, wbs, wbm, wo)


def kernel(x_prompt, x_sample, state_s5_re, state_s5_im, state_mlstm_C, state_mlstm_n, state_mlstm_m,
           state_mlstm_conv, meta_tokens, ffn1_norm, ffn1_w_gate, ffn1_w_up, ffn1_w_down, mix_norm, w_in,
           s5_A_re, s5_A_im, s5_log_dt, s5_B_re, s5_B_im, s5_C_re, s5_C_im, s5_D, s5_w_glu,
           mlstm_conv_w, mlstm_conv_b, mlstm_b_i, mlstm_b_f, mlstm_norm, w_branch_s5, w_branch_mlstm,
           w_out, ffn2_norm, ffn2_w_gate, ffn2_w_up, ffn2_w_down, final_norm):
    nbatch, seq, _ = x_prompt.shape
    nsamp, sseq, _ = x_sample.shape
    l = 0

    w1g, w1u, w1d = ffn1_w_gate[l], ffn1_w_up[l], ffn1_w_down[l]
    w2g, w2u, w2d = (w[l].astype(BF16) for w in (ffn2_w_gate, ffn2_w_up, ffn2_w_down))
    o_if = S5_WIDTH + QK_WIDTH + 2 * V_WIDTH
    o_gate = o_if + 2 * HEADS
    wt = w_in[l].T
    w2 = _win_halves(wt, o_gate, tr=WIN_CAST_ROWS)
    wglu, wbs, wbm, wo = (w[l].astype(BF16) for w in (s5_w_glu, w_branch_s5, w_branch_mlstm, w_out))
    g1 = ffn1_norm[l][None]
    gm = mix_norm[l][None]
    g2 = ffn2_norm[l][None]
    gf = final_norm[None]
    bif = jnp.pad(jnp.concatenate([mlstm_b_i[l], mlstm_b_f[l]]), (0, LANES - 2 * HEADS))[None]
    cw = mlstm_conv_w[l]
    cb = mlstm_conv_b[l][None]
    ng = mlstm_norm[l][None]
    d_skip = s5_D[l][None]

    fc, ec, cc, lam = _s5_operators(s5_A_re[l], s5_A_im[l], s5_log_dt[l], s5_B_re[l], s5_B_im[l],
                                    s5_C_re[l], s5_C_im[l])
    w_toe, f_bf, e_bf = _s5_prep(fc, ec, cc)

    def front(x, tm):
        n = x.shape[0]
        h1 = _ffn(x, g1, w1g, w1u, w1d, n=n, tm=tm, ksplit=2)
        z, zif = _win(h1, gm, w2, wt, o_if=o_if, ngate=N_BRANCH * D_MODEL, tm=tm)
        return h1, z, zif

    def mixers(z, zif, s5_state, ml_state, *, nseq, seqlen, nb, lc):
        ys, sre, sim = _s5(z, w_toe, f_bf, e_bf, lam, d_skip, s5_state[0], s5_state[1], nseq=nseq, seqlen=seqlen)
        buf0, c0, n0, m0 = ml_state
        if lc == seqlen and nb > 1:
            ym, buf, c, n, m = _mlstm_step(z, zif, cw, cb, bif, ng, buf0, c0, n0, m0,
                                           nseq=nseq, seqlen=seqlen, nb=nb)
        else:
            ym, buf, c, n, m = _mlstm(z, zif, cw, cb, bif, ng, buf0, c0, n0, m0.reshape(nseq, 1, HEADS),
                                      nseq=nseq, seqlen=seqlen, nb=nb, lc=lc)
            m = m.reshape(nseq, HEADS)
        return ys, ym, (sre, sim), (buf, c, n, m)

    def back(h1, ys, ym, z, n):
        h2 = _merge(h1, ys, ym, z, wglu, wbs, wbm, wo, n=n, tm=MERGE_TM)
        return _ffn(h2, g2, w2g, w2u, w2d, gf, n=n, tm=FFN_TM)

    ntok_s = nsamp * sseq
    x_sm = jnp.concatenate([x_sample.reshape(ntok_s, D_MODEL), meta_tokens], axis=0)
    h1_sm, z_sm, zif_sm = front(x_sm, ntok_s + N_META)

    z_m = jnp.tile(z_sm[ntok_s:], (nbatch, 1))
    zif_m = jnp.tile(zif_sm[ntok_s:], (nbatch, 1))
    zeros = lambda *s: jnp.zeros((nbatch,) + s, F32)
    _, _, s5_m, ml_m = mixers(
        z_m, zif_m, (zeros(S5_GROUPS * S5_STATE), zeros(S5_GROUPS * S5_STATE)),
        (zeros(CONV_W - 1, QK_WIDTH), zeros(HEADS, DV, DQK), zeros(HEADS, DQK), zeros(HEADS)),
        nseq=nbatch, seqlen=N_META, nb=1, lc=N_META)

    ntok_p = nbatch * seq
    h1_p, z_p, zif_p = front(x_prompt.reshape(ntok_p, D_MODEL), FFN_TM)
    ys_p, ym_p, s5_p, ml_p = mixers(z_p, zif_p, s5_m, ml_m, nseq=nbatch, seqlen=seq, nb=1, lc=256)
    y_p = back(h1_p, ys_p, ym_p, z_p, ntok_p)

    ys_s, ym_s, s5_s, ml_s = mixers(
        z_sm, zif_sm,
        (state_s5_re[l].reshape(nsamp, -1), state_s5_im[l].reshape(nsamp, -1)),
        (state_mlstm_conv[l], state_mlstm_C[l], state_mlstm_n[l], state_mlstm_m[l]),
        nseq=nsamp, seqlen=sseq, nb=STEP_NB, lc=sseq)
    y_s = back(h1_sm, ys_s, ym_s, z_sm, ntok_s)

    def pack(n, s5_st, ml_st):
        buf, c, nn, m = ml_st
        return (s5_st[0].reshape(1, n, S5_GROUPS, S5_STATE), s5_st[1].reshape(1, n, S5_GROUPS, S5_STATE),
                c[None], nn[None], m[None], buf[None])

    return ((y_p.reshape(nbatch, seq, D_MODEL), y_s.reshape(nsamp, sseq, D_MODEL))
            + pack(nbatch, s5_p, ml_p) + pack(nsamp, s5_s, ml_s))
```

```python
import functools

import jax
import jax.numpy as jnp
from jax import lax
from jax.experimental import pallas as pl
from jax.experimental.pallas import tpu as pltpu

F32 = jnp.float32
BF16 = jnp.bfloat16

D_MODEL = 2048
D_FF = 5632
N_META = 16
S5_WIDTH = 1024
S5_GROUP = 16
S5_GROUPS = 64
S5_STATE = 64
HEADS = 4
DQK = 128
DV = 256
QK_WIDTH = 1024
V_WIDTH = 1024
CONV_W = 4
N_BRANCH = 2
EPS = 1e-6

LANES = 128
S5_T = 8
S5_GPB = LANES // S5_GROUP
S5_NBLK = S5_WIDTH // LANES
S5_SW = S5_GPB * S5_STATE
VMEM_LIMIT = 58 * 1024 * 1024
FFN_TM = 1024
MERGE_TM = 256
WIN_CAST_ROWS = 456
STEP_NB = 16


def _cparams(sem):
    return pltpu.CompilerParams(dimension_semantics=sem, vmem_limit_bytes=VMEM_LIMIT)


def _rmsnorm(x, g):
    ms = jnp.mean(x * x, axis=-1, keepdims=True)
    return (x * lax.rsqrt(ms + EPS)) * g


def _dot(a, b):
    return jnp.dot(a, b, preferred_element_type=F32)


def _dot_nt(a, b):
    return lax.dot_general(a, b, (((1,), (1,)), ((), ())), preferred_element_type=F32)


def _dot_tn(a, b):
    return lax.dot_general(a, b, (((0,), (0,)), ((), ())), preferred_element_type=F32)


def _split3(x):
    hi = x.astype(BF16)
    r = x - hi.astype(F32)
    mid = r.astype(BF16)
    lo = (r - mid.astype(F32)).astype(BF16)
    return hi, mid, lo


def _ffn_kernel(x_ref, g_ref, wg_ref, wu_ref, wd_ref, *rest, final_norm):
    if final_norm:
        fg_ref, o_ref, xn_ref = rest
    else:
        o_ref, xn_ref = rest
    j = pl.program_id(1)

    @pl.when(j == 0)
    def _():
        xn_ref[...] = _rmsnorm(x_ref[...], g_ref[...]).astype(BF16)
        o_ref[...] = jnp.zeros(o_ref.shape, F32)

    xn = xn_ref[...]
    gt = _dot(xn, wg_ref[...].astype(BF16))
    up = _dot(xn, wu_ref[...].astype(BF16))
    act = (gt * jax.nn.sigmoid(gt) * up).astype(BF16)
    o_ref[...] += _dot(act, wd_ref[...].astype(BF16))

    @pl.when(j == pl.num_programs(1) - 1)
    def _():
        h = x_ref[...] + 0.5 * o_ref[...]
        if final_norm:
            h = _rmsnorm(h, fg_ref[...])
        o_ref[...] = h


def _ffn(x, g, wg, wu, wd, final_g=None, *, n, tm, tf=256):
    in_specs = [
        pl.BlockSpec((tm, D_MODEL), lambda i, j: (i, 0)),
        pl.BlockSpec((1, D_MODEL), lambda i, j: (0, 0)),
        pl.BlockSpec((D_MODEL, tf), lambda i, j: (0, j)),
        pl.BlockSpec((D_MODEL, tf), lambda i, j: (0, j)),
        pl.BlockSpec((tf, D_MODEL), lambda i, j: (j, 0)),
    ]
    args = [x, g, wg, wu, wd]
    if final_g is not None:
        in_specs.append(pl.BlockSpec((1, D_MODEL), lambda i, j: (0, 0)))
        args.append(final_g)
    return pl.pallas_call(
        functools.partial(_ffn_kernel, final_norm=final_g is not None),
        grid=(n // tm, D_FF // tf),
        in_specs=in_specs,
        out_specs=pl.BlockSpec((tm, D_MODEL), lambda i, j: (i, 0)),
        out_shape=jax.ShapeDtypeStruct((n, D_MODEL), F32),
        scratch_shapes=[pltpu.VMEM((tm, D_MODEL), BF16)],
        compiler_params=_cparams(("parallel", "arbitrary")),
        name="ffn",
    )(*args)


def _cast_rows_kernel(w_ref, o_ref, *, tail):
    last = pl.num_programs(0) - 1

    @pl.when(pl.program_id(0) < last)
    def _():
        o_ref[0] = w_ref[...].astype(BF16)

    @pl.when(pl.program_id(0) == last)
    def _():
        o_ref[0, :tail] = w_ref[:tail].astype(BF16)
        o_ref[0, tail:] = jnp.zeros((o_ref.shape[1] - tail, o_ref.shape[2]), BF16)


def _win_halves(wt, o_gate, *, tr):
    rows, k = wt.shape
    per = o_gate // tr
    tail = rows - (2 * per - 1) * tr
    assert per * tr == o_gate and tr % 8 == 0 and 0 < tail < tr and tail % 8 == 0
    return pl.pallas_call(
        functools.partial(_cast_rows_kernel, tail=tail),
        grid=(2 * per,),
        in_specs=[pl.BlockSpec((tr, k), lambda i: (i, 0))],
        out_specs=pl.BlockSpec((1, tr, k), lambda i: (i // per, i % per, 0)),
        out_shape=jax.ShapeDtypeStruct((2, o_gate, k), BF16),
        compiler_params=_cparams(("parallel",)),
        name="w_in_cast",
    )(wt)


def _win_kernel(h_ref, g_ref, wa_ref, wb_ref, wif_ref, z_ref, zif_ref, un_ref, *, na):
    j = pl.program_id(1)

    @pl.when(j == 0)
    def _():
        un = _rmsnorm(h_ref[...], g_ref[...]).astype(BF16)
        un_ref[...] = un
        zif_ref[...] = _dot_nt(un, wif_ref[...].astype(BF16))

    @pl.when(j < na)
    def _():
        z_ref[...] = _dot_nt(un_ref[...], wa_ref[0])

    @pl.when(j >= na)
    def _():
        z_ref[...] = _dot_nt(un_ref[...], wb_ref[0])


def _win(h, g, w2, wt, *, o_if, ngate, tm, tn=1024):
    n = h.shape[0]
    na = o_if // tn
    nb = ngate // tn
    return pl.pallas_call(
        functools.partial(_win_kernel, na=na),
        grid=(n // tm, na + nb),
        in_specs=[
            pl.BlockSpec((tm, D_MODEL), lambda i, j: (i, 0)),
            pl.BlockSpec((1, D_MODEL), lambda i, j: (0, 0)),
            pl.BlockSpec((1, tn, D_MODEL), lambda i, j: (0, jnp.minimum(j, na - 1), 0)),
            pl.BlockSpec((1, tn, D_MODEL), lambda i, j: (1, jnp.maximum(j - na, 0), 0)),
            pl.BlockSpec((LANES, D_MODEL), lambda i, j: (o_if // LANES, 0)),
        ],
        out_specs=[
            pl.BlockSpec((tm, tn), lambda i, j: (i, j)),
            pl.BlockSpec((tm, LANES), lambda i, j: (i, 0)),
        ],
        out_shape=[
            jax.ShapeDtypeStruct((n, (na + nb) * tn), F32),
            jax.ShapeDtypeStruct((n, LANES), F32),
        ],
        scratch_shapes=[pltpu.VMEM((tm, D_MODEL), BF16)],
        compiler_params=_cparams(("parallel", "arbitrary")),
        name="w_in",
    )(h, g, w2, w2, wt)


def _s5_prep_kernel(fc_ref, ec_ref, cc_ref, w_ref, f_ref, e_ref):
    n = S5_T * LANES
    sw2 = 2 * S5_SW
    i32 = jnp.int32
    lg_h = S5_GROUP.bit_length() - 1
    lg_p = S5_STATE.bit_length() - 1
    lg_sw = S5_SW.bit_length() - 1

    def tile_mat(k, c, src_col):
        kk = lax.broadcasted_iota(i32, (k, c), 0)
        cc = lax.broadcasted_iota(i32, (k, c), 1)
        return jnp.where(kk == src_col(cc), 1.0, 0.0).astype(BF16)

    def group_mask(r, c, row_group, col_group):
        rr = lax.broadcasted_iota(i32, (r, c), 0)
        cc = lax.broadcasted_iota(i32, (r, c), 1)
        return row_group(rr) == col_group(cc)

    gmask = S5_GPB - 1
    rf = tile_mat(2 * S5_STATE, sw2, lambda c: ((c >> lg_sw) << lg_p) + (c & (S5_STATE - 1)))
    mf = group_mask(n, sw2, lambda r: (r >> lg_h) & gmask, lambda c: (c >> lg_p) & gmask)
    f_hi, f_mid, _ = _split3(fc_ref[0])
    x_hi = jnp.where(mf, _dot(f_hi, rf), 0.0).astype(BF16)
    x_mid = jnp.where(mf, _dot(f_mid, rf), 0.0).astype(BF16)
    f_ref[0] = x_hi
    lg_l = LANES.bit_length() - 1
    re = tile_mat(LANES, n, lambda c: ((c >> lg_l) << lg_h) + (c & (S5_GROUP - 1)))
    me = group_mask(sw2, n, lambda r: (r >> lg_p) & gmask, lambda c: (c >> lg_h) & gmask)
    e_ref[0] = jnp.where(me, _dot(ec_ref[0].astype(BF16), re), 0.0).astype(BF16)
    rc = tile_mat(S5_GROUP, LANES, lambda c: c & (S5_GROUP - 1))
    mc = group_mask(sw2, LANES, lambda r: (r >> lg_p) & gmask, lambda c: c >> lg_h)
    c_hi, c_mid, _ = _split3(cc_ref[0])
    ct_hi = jnp.where(mc, _dot(c_hi, rc), 0.0).astype(BF16)
    ct_mid = jnp.where(mc, _dot(c_mid, rc), 0.0).astype(BF16)
    w_ref[...] = jnp.zeros(w_ref.shape, w_ref.dtype)
    for lag in range(S5_T):
        r0 = (S5_T - 1 - lag) * LANES
        xh = x_hi[r0:r0 + LANES, :]
        xm = x_mid[r0:r0 + LANES, :]
        k = (_dot(xh, ct_hi) + _dot(xm, ct_hi) + _dot(xh, ct_mid)).astype(BF16)
        for s in range(S5_T - lag):
            t = s + lag
            w_ref[0, s * LANES:(s + 1) * LANES, t * LANES:(t + 1) * LANES] = k


def _s5_prep(fc, ec, cc):
    n = S5_T * LANES
    blk = lambda r, c: pl.BlockSpec((1, r, c), lambda j: (j, 0, 0))
    return pl.pallas_call(
        _s5_prep_kernel,
        grid=(S5_NBLK,),
        in_specs=[blk(n, 2 * S5_STATE), blk(2 * S5_SW, LANES), blk(2 * S5_SW, S5_GROUP)],
        out_specs=[blk(n, n), blk(n, 2 * S5_SW), blk(2 * S5_SW, n)],
        out_shape=[
            jax.ShapeDtypeStruct((S5_NBLK, n, n), BF16),
            jax.ShapeDtypeStruct((S5_NBLK, n, 2 * S5_SW), BF16),
            jax.ShapeDtypeStruct((S5_NBLK, 2 * S5_SW, n), BF16),
        ],
        compiler_params=_cparams(("parallel",)),
        name="s5_prep",
    )(fc, ec, cc)


def _s5_kernel(x_ref, w_ref, f_ref, e_ref, lam_ref, d_ref, h0re_ref, h0im_ref,
               y_ref, ore_ref, oim_ref, yi_ref, s_ref, hin_ref, *, nseq, nblk, rt):
    rows = nseq * nblk
    ar = lam_ref[0, 0:1, :]
    ai = lam_ref[0, 1:2, :]

    def load_u(r0):
        return jnp.concatenate(
            [x_ref[pl.ds(r0 * S5_T + s, rt, stride=S5_T), :] for s in range(S5_T)], axis=1)

    for r0 in range(0, rows, rt):
        ub = load_u(r0).astype(BF16)
        yi_ref[r0:r0 + rt, :] = _dot(ub, w_ref[0])
        s_ref[r0:r0 + rt, :] = _dot(ub, f_ref[0])

    if nblk == 1:
        hre = h0re_ref[...]
        him = h0im_ref[...]
        hin_ref[:, :S5_SW] = hre
        hin_ref[:, S5_SW:] = him
        ore_ref[...] = ar * hre - ai * him + s_ref[:, :S5_SW]
        oim_ref[...] = ar * him + ai * hre + s_ref[:, S5_SW:]
    else:
        def body(c, carry):
            new = []
            for b in range(nseq):
                hre, him = carry[2 * b], carry[2 * b + 1]
                row = b * nblk + c
                hin_ref[pl.ds(row, 1), :S5_SW] = hre
                hin_ref[pl.ds(row, 1), S5_SW:] = him
                sre = s_ref[pl.ds(row, 1), :S5_SW]
                sim = s_ref[pl.ds(row, 1), S5_SW:]
                new.append(ar * hre - ai * him + sre)
                new.append(ar * him + ai * hre + sim)
            return tuple(new)

        init = []
        for b in range(nseq):
            init.append(h0re_ref[b:b + 1, :])
            init.append(h0im_ref[b:b + 1, :])
        fin = lax.fori_loop(0, nblk, body, tuple(init))
        for b in range(nseq):
            ore_ref[b:b + 1, :] = fin[2 * b]
            oim_ref[b:b + 1, :] = fin[2 * b + 1]

    dt = jnp.concatenate([d_ref[...]] * S5_T, axis=1)
    for r0 in range(0, rows, rt):
        yo = _dot(hin_ref[r0:r0 + rt, :].astype(BF16), e_ref[0])
        y = jax.nn.gelu(yi_ref[r0:r0 + rt, :] + yo + load_u(r0) * dt)
        for t in range(S5_T):
            y_ref[pl.ds(r0 * S5_T + t, rt, stride=S5_T), :] = y[:, t * LANES:(t + 1) * LANES]


def _s5(z, w, f, e, lam, d, h0re, h0im, *, nseq, seqlen):
    nblk = seqlen // S5_T
    rows = nseq * nblk
    rt = min(rows, 256)
    n = S5_T * LANES
    tok = nseq * seqlen
    kern = functools.partial(_s5_kernel, nseq=nseq, nblk=nblk, rt=rt)
    return pl.pallas_call(
        kern,
        grid=(S5_NBLK,),
        in_specs=[
            pl.BlockSpec((tok, LANES), lambda j: (0, j)),
            pl.BlockSpec((1, n, n), lambda j: (j, 0, 0)),
            pl.BlockSpec((1, n, 2 * S5_SW), lambda j: (j, 0, 0)),
            pl.BlockSpec((1, 2 * S5_SW, n), lambda j: (j, 0, 0)),
            pl.BlockSpec((1, 2, S5_SW), lambda j: (j, 0, 0)),
            pl.BlockSpec((1, LANES), lambda j: (0, j)),
            pl.BlockSpec((nseq, S5_SW), lambda j: (0, j)),
            pl.BlockSpec((nseq, S5_SW), lambda j: (0, j)),
        ],
        out_specs=[
            pl.BlockSpec((tok, LANES), lambda j: (0, j)),
            pl.BlockSpec((nseq, S5_SW), lambda j: (0, j)),
            pl.BlockSpec((nseq, S5_SW), lambda j: (0, j)),
        ],
        out_shape=[
            jax.ShapeDtypeStruct((tok, S5_WIDTH), F32),
            jax.ShapeDtypeStruct((nseq, S5_GROUPS * S5_STATE), F32),
            jax.ShapeDtypeStruct((nseq, S5_GROUPS * S5_STATE), F32),
        ],
        scratch_shapes=[
            pltpu.VMEM((rows, n), F32),
            pltpu.VMEM((rows, 2 * S5_SW), F32),
            pltpu.VMEM((rows, 2 * S5_SW), F32),
        ],
        compiler_params=_cparams(("parallel",)),
        name="s5",
    )(z, w, f, e, lam, d, h0re, h0im)


def _s5_operators(a_re, a_im, log_dt, b_re, b_im, c_re, c_im):
    dt = jnp.exp(log_dt)[:, None]
    mag = jnp.exp(a_re * dt)
    abar_re = mag * jnp.cos(a_im * dt)
    abar_im = mag * jnp.sin(a_im * dt)
    nr, ni = abar_re - 1.0, abar_im
    den = a_re * a_re + a_im * a_im
    q_re = (nr * a_re + ni * a_im) / den
    q_im = (ni * a_re - nr * a_im) / den
    bb_re = q_re[:, :, None] * b_re - q_im[:, :, None] * b_im
    bb_im = q_re[:, :, None] * b_im + q_im[:, :, None] * b_re
    taus = jnp.arange(S5_T + 1, dtype=F32)[:, None, None]
    pmag = jnp.exp(taus * (a_re * dt)[None])
    pw_re = pmag * jnp.cos(taus * (a_im * dt)[None])
    pw_im = pmag * jnp.sin(taus * (a_im * dt)[None])
    n = S5_T * LANES

    lags = (S5_T - 1) - jnp.arange(S5_T, dtype=F32)[:, None, None]
    lmag = jnp.exp(lags * (a_re * dt)[None])
    lr = (lmag * jnp.cos(lags * (a_im * dt)[None]))[:, :, :, None]
    li = (lmag * jnp.sin(lags * (a_im * dt)[None]))[:, :, :, None]
    fb = jnp.stack([lr * bb_re[None] - li * bb_im[None], lr * bb_im[None] + li * bb_re[None]])
    fc = fb.reshape(2, S5_T, S5_NBLK, S5_GPB, S5_STATE, S5_GROUP).transpose(2, 1, 3, 5, 0, 4)
    fc = fc.reshape(S5_NBLK, n, 2 * S5_STATE)

    cp_re = jnp.swapaxes(c_re, -1, -2)[None]
    cp_im = jnp.swapaxes(c_im, -1, -2)[None]
    tr = pw_re[1:][:, :, :, None]
    ti = pw_im[1:][:, :, :, None]
    ce = jnp.stack([cp_re * tr - cp_im * ti, -(cp_re * ti + cp_im * tr)])
    ec = ce.reshape(2, S5_T, S5_NBLK, S5_GPB, S5_STATE, S5_GROUP).transpose(2, 0, 3, 4, 1, 5)
    ec = ec.reshape(S5_NBLK, 2 * S5_SW, LANES)

    cc = jnp.stack([cp_re[0], -cp_im[0]]).reshape(2, S5_NBLK, S5_GPB, S5_STATE, S5_GROUP)
    cc = cc.transpose(1, 0, 2, 3, 4).reshape(S5_NBLK, 2 * S5_SW, S5_GROUP)

    lam = jnp.stack([pw_re[S5_T].reshape(S5_NBLK, S5_SW), pw_im[S5_T].reshape(S5_NBLK, S5_SW)], axis=1)
    return fc, ec, cc, lam


def _mlstm_kernel(qk_ref, v_ref, o_ref, if_ref, cw_ref, cb_ref, bif_ref, ng_ref,
                  buf0_ref, c0_ref, n0_ref, m0_ref,
                  y_ref, bufo_ref, co_ref, no_ref, mo_ref, xp_ref, *, nb, lc, carry):
    if carry:
        @pl.when(pl.program_id(1) == 0)
        def _():
            bufo_ref[...] = buf0_ref[...]
            co_ref[...] = c0_ref[...]
            no_ref[...] = n0_ref[...]
            mo_ref[...] = m0_ref[...]
        bufs_ref, cs_ref, ns_ref, ms_ref = bufo_ref, co_ref, no_ref, mo_ref
    else:
        bufs_ref, cs_ref, ns_ref, ms_ref = buf0_ref, c0_ref, n0_ref, m0_ref

    lp = max(lc, LANES)
    pad = CONV_W - 1
    heads = range(HEADS)
    head_row = lax.broadcasted_iota(jnp.int32, (HEADS, DQK), 0)
    head_lane = lax.broadcasted_iota(jnp.int32, (1, HEADS), 1)
    row = lax.broadcasted_iota(jnp.int32, (lc, lc), 0)
    col = lax.broadcasted_iota(jnp.int32, (lc, lc), 1)
    causal = row >= col
    tril = jnp.where(causal, 1.0, 0.0).astype(BF16)
    lane = lax.broadcasted_iota(jnp.int32, (lc, LANES), 1)
    is_i = lane < HEADS
    is_f = jnp.logical_and(lane >= HEADS, lane < 2 * HEADS)

    for s in range(nb):
        r0 = s * lc
        xp_ref[s, 0:8 - pad, :] = jnp.zeros((8 - pad, QK_WIDTH), F32)
        xp_ref[s, 8 - pad:8, :] = bufs_ref[s]
        xp_ref[s, 8:8 + lc, :] = qk_ref[r0:r0 + lc, :]
        xp = xp_ref[s]
        conv = cb_ref[...] + xp[8:8 + lc] * cw_ref[pad:pad + 1, :]
        for j in range(1, CONV_W):
            conv = conv + pltpu.roll(xp, j, 0)[8:8 + lc] * cw_ref[pad - j:pad - j + 1, :]
        bufo_ref[s] = xp_ref[s, 8 + lc - pad:8 + lc, :]
        n_all = ns_ref[s]
        m_all = ms_ref[s]
        qk = conv * jax.nn.sigmoid(conv)
        q = qk[:, :QK_WIDTH // 2] * (DQK ** -0.5)
        k = qk[:, QK_WIDTH // 2:]

        gate = if_ref[r0:r0 + lc, :] + bif_ref[...]
        lf = jnp.where(is_f, jax.nn.log_sigmoid(gate), 0.0)
        parts = _split3(lf)
        bcum = _dot(tril, parts[0]) + _dot(tril, parts[1]) + _dot(tril, parts[2])
        pc = jnp.where(is_i, gate, bcum)
        if lc < lp:
            pc_t = jnp.concatenate([pc, jnp.zeros((lp - lc, LANES), F32)], axis=0).T[:, :lc]
        else:
            pc_t = pc.T

        i_col = [pc[:, h:h + 1] for h in heads]
        b_col = [pc[:, HEADS + h:HEADS + h + 1] for h in heads]
        m0 = [m_all[:, h:h + 1] for h in heads]
        qh = [q[:, h * DQK:(h + 1) * DQK] for h in heads]
        kh = [k[:, h * DQK:(h + 1) * DQK] for h in heads]
        qb = [x.astype(BF16) for x in qh]
        vb = [v_ref[r0:r0 + lc, h * DV:(h + 1) * DV].astype(BF16) for h in heads]
        c_prev = [cs_ref[s, h] for h in heads]
        n_prev = [n_all[h:h + 1, :] for h in heads]

        qk_t = [_dot_nt(qb[h], kh[h].astype(BF16)) for h in heads]
        carried = [_dot_nt(qb[h], c_prev[h].astype(BF16)) for h in heads]
        logw = [jnp.where(causal, b_col[h] - pc_t[HEADS + h:HEADS + h + 1, :] + pc_t[h:h + 1, :], -jnp.inf)
                for h in heads]
        g = [b_col[h] + m0[h] for h in heads]
        m = [jnp.maximum(g[h], jnp.max(logw[h], axis=-1, keepdims=True)) for h in heads]
        inter = [jnp.exp(g[h] - m[h]) for h in heads]
        sc = [qk_t[h] * jnp.exp(logw[h] - m[h]) for h in heads]
        num = [_dot(sc[h].astype(BF16), vb[h]) + inter[h] * carried[h] for h in heads]
        den = [jnp.sum(sc[h], axis=-1, keepdims=True)
               + inter[h] * jnp.sum(qh[h] * n_prev[h], axis=-1, keepdims=True) for h in heads]
        hh = [num[h] / jnp.maximum(jnp.abs(den[h]), jnp.exp(-m[h])) for h in heads]
        hh = [x * lax.rsqrt(jnp.mean(x * x, axis=-1, keepdims=True) + EPS) for x in hh]
        for h in heads:
            og = jax.nn.sigmoid(o_ref[r0:r0 + lc, h * DV:(h + 1) * DV])
            y_ref[r0:r0 + lc, h * DV:(h + 1) * DV] = hh[h] * ng_ref[:, h * DV:(h + 1) * DV] * og

        b_last = [x[lc - 1:lc, :] for x in b_col]
        m_new = [x[lc - 1:lc, :] for x in m]
        decay = [jnp.exp(b_last[h] + m0[h] - m_new[h]) for h in heads]
        kw = [kh[h] * jnp.exp(b_last[h] - b_col[h] + i_col[h] - m_new[h]) for h in heads]
        upd = [_dot_tn(vb[h], kw[h].astype(BF16)) for h in heads]
        n_new_all = jnp.zeros((HEADS, DQK), F32)
        m_new_all = jnp.zeros((1, HEADS), F32)
        for h in heads:
            co_ref[s, h] = decay[h] * c_prev[h] + upd[h]
            n_new = decay[h] * n_prev[h] + jnp.sum(kw[h], axis=0, keepdims=True)
            n_new_all = jnp.where(head_row == h, n_new, n_new_all)
            m_new_all = jnp.where(head_lane == h, m_new[h], m_new_all)
        no_ref[s] = n_new_all
        mo_ref[s] = m_new_all


def _mlstm(z, zif, cw, cb, bif, ng, buf0, c0, n0, m0, *, nseq, seqlen, nb, lc):
    tok = nseq * seqlen
    nchunk = seqlen // lc
    rows = nb * lc
    tok_map = lambda col: (lambda i, c: (i * nchunk + c, col))
    const2 = lambda i, c: (0, 0)
    st3 = lambda i, c: (i, 0, 0)
    st4 = lambda i, c: (i, 0, 0, 0)
    kern = functools.partial(_mlstm_kernel, nb=nb, lc=lc, carry=nchunk > 1)
    return pl.pallas_call(
        kern,
        grid=(nseq // nb, nchunk),
        in_specs=[
            pl.BlockSpec((rows, QK_WIDTH), tok_map(1)),
            pl.BlockSpec((rows, V_WIDTH), tok_map(2)),
            pl.BlockSpec((rows, V_WIDTH), tok_map(3)),
            pl.BlockSpec((rows, LANES), tok_map(0)),
            pl.BlockSpec((CONV_W, QK_WIDTH), const2),
            pl.BlockSpec((1, QK_WIDTH), const2),
            pl.BlockSpec((1, LANES), const2),
            pl.BlockSpec((1, V_WIDTH), const2),
            pl.BlockSpec((nb, CONV_W - 1, QK_WIDTH), st3),
            pl.BlockSpec((nb, HEADS, DV, DQK), st4),
            pl.BlockSpec((nb, HEADS, DQK), st3),
            pl.BlockSpec((nb, 1, HEADS), st3),
        ],
        out_specs=[
            pl.BlockSpec((rows, V_WIDTH), tok_map(0)),
            pl.BlockSpec((nb, CONV_W - 1, QK_WIDTH), st3),
            pl.BlockSpec((nb, HEADS, DV, DQK), st4),
            pl.BlockSpec((nb, HEADS, DQK), st3),
            pl.BlockSpec((nb, 1, HEADS), st3),
        ],
        out_shape=[
            jax.ShapeDtypeStruct((tok, V_WIDTH), F32),
            jax.ShapeDtypeStruct((nseq, CONV_W - 1, QK_WIDTH), F32),
            jax.ShapeDtypeStruct((nseq, HEADS, DV, DQK), F32),
            jax.ShapeDtypeStruct((nseq, HEADS, DQK), F32),
            jax.ShapeDtypeStruct((nseq, 1, HEADS), F32),
        ],
        scratch_shapes=[pltpu.VMEM((nb, lc + 8, QK_WIDTH), F32)],
        compiler_params=_cparams(("parallel", "arbitrary")),
        name="mlstm",
    )(z, z, z, zif, cw, cb, bif, ng, buf0, c0, n0, m0)


def _mlstm_step_kernel(qk_ref, v_ref, o_ref, if_ref, cw_ref, cb_ref, bif_ref, ng_ref,
                       buf0_ref, c0_ref, n0x_ref, m0x_ref,
                       y_ref, bufo_ref, co_ref, nox_ref, mox_ref, xp_ref, conv_ref, *, nb, lc):
    rows = nb * lc
    lg = lc.bit_length() - 1
    i32 = jnp.int32
    row = lax.broadcasted_iota(i32, (rows, rows), 0)
    col = lax.broadcasted_iota(i32, (rows, rows), 1)
    same = (row >> lg) == (col >> lg)
    causal = jnp.logical_and(same, row >= col)
    same_b = jnp.where(same, 1.0, 0.0).astype(BF16)
    tril_b = jnp.where(causal, 1.0, 0.0).astype(BF16)
    lane = lax.broadcasted_iota(i32, (rows, LANES), 1)
    is_i = lane < HEADS
    is_f = jnp.logical_and(lane >= HEADS, lane < 2 * HEADS)

    def seg_dot(mat, x):
        hi, mid, lo = _split3(x)
        return _dot(mat, hi) + _dot(mat, mid) + _dot(mat, lo)

    for s in range(nb):
        xp_ref[s, 5:8, :] = buf0_ref[s]
        xp_ref[s, 8:8 + lc, :] = qk_ref[s * lc:(s + 1) * lc, :]
        conv_ref[s * lc:(s + 1) * lc, :] = cb_ref[...] + sum(
            xp_ref[s, 5 + j:5 + j + lc, :] * cw_ref[j:j + 1, :] for j in range(CONV_W))
        bufo_ref[s] = xp_ref[s, 5 + lc:8 + lc, :]
    conv = conv_ref[...]
    qk = conv * jax.nn.sigmoid(conv)
    q = qk[:, :QK_WIDTH // 2] * (DQK ** -0.5)
    k = qk[:, QK_WIDTH // 2:]

    gate = if_ref[...] + bif_ref[...]
    lf = jnp.where(is_f, jax.nn.log_sigmoid(gate), 0.0)
    bcum = seg_dot(tril_b, lf)
    btot = seg_dot(same_b, lf)
    pc = jnp.where(is_i, gate, bcum)
    if rows < LANES:
        pc_t = jnp.concatenate([pc, jnp.zeros((LANES - rows, LANES), F32)], axis=0).T[:, :rows]
    else:
        pc_t = pc.T
    m0x = m0x_ref[...]
    mox = jnp.zeros((rows, LANES), F32)

    for h in range(HEADS):
        i_col = pc[:, h:h + 1]
        b_col = pc[:, HEADS + h:HEADS + h + 1]
        i_row = pc_t[h:h + 1, :]
        b_row = pc_t[HEADS + h:HEADS + h + 1, :]
        b_last = btot[:, HEADS + h:HEADS + h + 1]
        m0 = m0x[:, h:h + 1]
        lw = b_col - b_row + i_row
        logw = jnp.where(causal, lw, -jnp.inf)
        g = b_col + m0
        m = jnp.maximum(g, jnp.max(logw, axis=-1, keepdims=True))
        w = jnp.exp(logw - m)
        inter = jnp.exp(g - m)
        lw_end = jnp.where(same, b_last - b_row + i_row, -jnp.inf)
        m_new = jnp.maximum(b_last + m0, jnp.max(lw_end, axis=-1, keepdims=True))
        decay = jnp.exp(b_last + m0 - m_new)
        qh = q[:, h * DQK:(h + 1) * DQK]
        kh = k[:, h * DQK:(h + 1) * DQK]
        qb = qh.astype(BF16)
        vb = v_ref[:, h * DV:(h + 1) * DV].astype(BF16)
        n_prev = n0x_ref[:, h * DQK:(h + 1) * DQK]
        sc = _dot_nt(qb, kh.astype(BF16)) * w
        carried = jnp.concatenate(
            [_dot_nt(qb[s * lc:(s + 1) * lc], c0_ref[s, h].astype(BF16)) for s in range(nb)], axis=0)
        num = _dot(sc.astype(BF16), vb) + inter * carried
        den = jnp.sum(sc, axis=-1, keepdims=True) + inter * jnp.sum(qh * n_prev, axis=-1, keepdims=True)
        hh = num / jnp.maximum(jnp.abs(den), jnp.exp(-m))
        hh = hh * lax.rsqrt(jnp.mean(hh * hh, axis=-1, keepdims=True) + EPS)
        og = jax.nn.sigmoid(o_ref[:, h * DV:(h + 1) * DV])
        y_ref[:, h * DV:(h + 1) * DV] = hh * ng_ref[:, h * DV:(h + 1) * DV] * og

        kw = kh * jnp.exp(b_last - b_col + i_col - m_new)
        kwb = kw.astype(BF16)
        for s in range(nb):
            r0 = s * lc
            co_ref[s, h] = decay[r0:r0 + 1, :] * c0_ref[s, h] + _dot_tn(vb[r0:r0 + lc], kwb[r0:r0 + lc])
        nox_ref[:, h * DQK:(h + 1) * DQK] = decay * n_prev + seg_dot(same_b, kw)
        mox = jnp.where(lane == h, m_new, mox)
    mox_ref[...] = mox


def _mlstm_step(z, zif, cw, cb, bif, ng, buf0, c0, n0, m0, *, nseq, seqlen, nb):
    lc = seqlen
    assert lc & (lc - 1) == 0 and lc >= CONV_W - 1
    tok = nseq * seqlen
    rows = nb * lc
    n0x = jnp.repeat(n0.reshape(nseq, HEADS * DQK), lc, axis=0)
    m0x = jnp.repeat(jnp.pad(m0, ((0, 0), (0, LANES - HEADS))), lc, axis=0)
    tok_map = lambda col: (lambda i: (i, col))
    const2 = lambda i: (0, 0)
    st3 = lambda i: (i, 0, 0)
    st4 = lambda i: (i, 0, 0, 0)
    y, buf, c, nox, mox = pl.pallas_call(
        functools.partial(_mlstm_step_kernel, nb=nb, lc=lc),
        grid=(nseq // nb,),
        in_specs=[
            pl.BlockSpec((rows, QK_WIDTH), tok_map(1)),
            pl.BlockSpec((rows, V_WIDTH), tok_map(2)),
            pl.BlockSpec((rows, V_WIDTH), tok_map(3)),
            pl.BlockSpec((rows, LANES), tok_map(0)),
            pl.BlockSpec((CONV_W, QK_WIDTH), const2),
            pl.BlockSpec((1, QK_WIDTH), const2),
            pl.BlockSpec((1, LANES), const2),
            pl.BlockSpec((1, V_WIDTH), const2),
            pl.BlockSpec((nb, CONV_W - 1, QK_WIDTH), st3),
            pl.BlockSpec((nb, HEADS, DV, DQK), st4),
            pl.BlockSpec((rows, HEADS * DQK), tok_map(0)),
            pl.BlockSpec((rows, LANES), tok_map(0)),
        ],
        out_specs=[
            pl.BlockSpec((rows, V_WIDTH), tok_map(0)),
            pl.BlockSpec((nb, CONV_W - 1, QK_WIDTH), st3),
            pl.BlockSpec((nb, HEADS, DV, DQK), st4),
            pl.BlockSpec((rows, HEADS * DQK), tok_map(0)),
            pl.BlockSpec((rows, LANES), tok_map(0)),
        ],
        out_shape=[
            jax.ShapeDtypeStruct((tok, V_WIDTH), F32),
            jax.ShapeDtypeStruct((nseq, CONV_W - 1, QK_WIDTH), F32),
            jax.ShapeDtypeStruct((nseq, HEADS, DV, DQK), F32),
            jax.ShapeDtypeStruct((tok, HEADS * DQK), F32),
            jax.ShapeDtypeStruct((tok, LANES), F32),
        ],
        scratch_shapes=[pltpu.VMEM((nb, lc + 8, QK_WIDTH), F32), pltpu.VMEM((rows, QK_WIDTH), F32)],
        compiler_params=_cparams(("parallel",)),
        name="mlstm_step",
    )(z, z, z, zif, cw, cb, bif, ng, buf0, c0, n0x, m0x)
    n = nox[::lc].reshape(nseq, HEADS, DQK)
    m = mox[::lc, :HEADS]
    return y, buf, c, n, m


def _merge_kernel(h_ref, ys_ref, ym_ref, g1_ref, g2_ref, wglu_ref, wbs_ref, wbm_ref, wo_ref, o_ref):
    ys = ys_ref[...]
    glu = ys * jax.nn.sigmoid(_dot(ys.astype(BF16), wglu_ref[...]))
    a = _dot(glu.astype(BF16), wbs_ref[...])
    b = _dot(ym_ref[...].astype(BF16), wbm_ref[...])
    merged = jax.nn.sigmoid(g1_ref[...]) * a + jax.nn.sigmoid(g2_ref[...]) * b
    o_ref[...] = h_ref[...] + _dot(merged.astype(BF16), wo_ref[...])


def _merge(h, ys, ym, z, wglu, wbs, wbm, wo, *, n, tm):
    row = lambda i: (i, 0)
    const = lambda i: (0, 0)
    resident = lambda shape: pl.BlockSpec(shape, const, pipeline_mode=pl.Buffered(1))
    return pl.pallas_call(
        _merge_kernel,
        grid=(n // tm,),
        in_specs=[
            pl.BlockSpec((tm, D_MODEL), row),
            pl.BlockSpec((tm, S5_WIDTH), row),
            pl.BlockSpec((tm, V_WIDTH), row),
            pl.BlockSpec((tm, D_MODEL), lambda i: (i, 2)),
            pl.BlockSpec((tm, D_MODEL), lambda i: (i, 3)),
            resident((S5_WIDTH, S5_WIDTH)),
            resident((S5_WIDTH, D_MODEL)),
            resident((V_WIDTH, D_MODEL)),
            resident((D_MODEL, D_MODEL)),
        ],
        out_specs=pl.BlockSpec((tm, D_MODEL), row),
        out_shape=jax.ShapeDtypeStruct((n, D_MODEL), F32),
        compiler_params=_cparams(("parallel",)),
        name="merge",
    )(h, ys, ym, z, z, wglu, wbs, wbm, wo)


def kernel(x_prompt, x_sample, state_s5_re, state_s5_im, state_mlstm_C, state_mlstm_n, state_mlstm_m,
           state_mlstm_conv, meta_tokens, ffn1_norm, ffn1_w_gate, ffn1_w_up, ffn1_w_down, mix_norm, w_in,
           s5_A_re, s5_A_im, s5_log_dt, s5_B_re, s5_B_im, s5_C_re, s5_C_im, s5_D, s5_w_glu,
           mlstm_conv_w, mlstm_conv_b, mlstm_b_i, mlstm_b_f, mlstm_norm, w_branch_s5, w_branch_mlstm,
           w_out, ffn2_norm, ffn2_w_gate, ffn2_w_up, ffn2_w_down, final_norm):
    nbatch, seq, _ = x_prompt.shape
    nsamp, sseq, _ = x_sample.shape
    l = 0

    w1g, w1u, w1d = ffn1_w_gate[l], ffn1_w_up[l], ffn1_w_down[l]
    w2g, w2u, w2d = ffn2_w_gate[l], ffn2_w_up[l], ffn2_w_down[l]
    o_if = S5_WIDTH + QK_WIDTH + 2 * V_WIDTH
    o_gate = o_if + 2 * HEADS
    wt = w_in[l].T
    w2 = _win_halves(wt, o_gate, tr=WIN_CAST_ROWS)
    wglu, wbs, wbm, wo = (w[l].astype(BF16) for w in (s5_w_glu, w_branch_s5, w_branch_mlstm, w_out))
    g1 = ffn1_norm[l][None]
    gm = mix_norm[l][None]
    g2 = ffn2_norm[l][None]
    gf = final_norm[None]
    bif = jnp.pad(jnp.concatenate([mlstm_b_i[l], mlstm_b_f[l]]), (0, LANES - 2 * HEADS))[None]
    cw = mlstm_conv_w[l]
    cb = mlstm_conv_b[l][None]
    ng = mlstm_norm[l][None]
    d_skip = s5_D[l][None]

    fc, ec, cc, lam = _s5_operators(s5_A_re[l], s5_A_im[l], s5_log_dt[l], s5_B_re[l], s5_B_im[l],
                                    s5_C_re[l], s5_C_im[l])
    w_toe, f_bf, e_bf = _s5_prep(fc, ec, cc)

    def front(x, tm):
        n = x.shape[0]
        h1 = _ffn(x, g1, w1g, w1u, w1d, n=n, tm=tm)
        z, zif = _win(h1, gm, w2, wt, o_if=o_if, ngate=N_BRANCH * D_MODEL, tm=tm)
        return h1, z, zif

    def mixers(z, zif, s5_state, ml_state, *, nseq, seqlen, nb, lc):
        ys, sre, sim = _s5(z, w_toe, f_bf, e_bf, lam, d_skip, s5_state[0], s5_state[1], nseq=nseq, seqlen=seqlen)
        buf0, c0, n0, m0 = ml_state
        if lc == seqlen and nb > 1:
            ym, buf, c, n, m = _mlstm_step(z, zif, cw, cb, bif, ng, buf0, c0, n0, m0,
                                           nseq=nseq, seqlen=seqlen, nb=nb)
        else:
            ym, buf, c, n, m = _mlstm(z, zif, cw, cb, bif, ng, buf0, c0, n0, m0.reshape(nseq, 1, HEADS),
                                      nseq=nseq, seqlen=seqlen, nb=nb, lc=lc)
            m = m.reshape(nseq, HEADS)
        return ys, ym, (sre, sim), (buf, c, n, m)

    def back(h1, ys, ym, z, n):
        h2 = _merge(h1, ys, ym, z, wglu, wbs, wbm, wo, n=n, tm=MERGE_TM)
        return _ffn(h2, g2, w2g, w2u, w2d, gf, n=n, tm=FFN_TM)

    ntok_s = nsamp * sseq
    x_sm = jnp.concatenate([x_sample.reshape(ntok_s, D_MODEL), meta_tokens], axis=0)
    h1_sm, z_sm, zif_sm = front(x_sm, ntok_s + N_META)

    z_m = jnp.tile(z_sm[ntok_s:], (nbatch, 1))
    zif_m = jnp.tile(zif_sm[ntok_s:], (nbatch, 1))
    zeros = lambda *s: jnp.zeros((nbatch,) + s, F32)
    _, _, s5_m, ml_m = mixers(
        z_m, zif_m, (zeros(S5_GROUPS * S5_STATE), zeros(S5_GROUPS * S5_STATE)),
        (zeros(CONV_W - 1, QK_WIDTH), zeros(HEADS, DV, DQK), zeros(HEADS, DQK), zeros(HEADS)),
        nseq=nbatch, seqlen=N_META, nb=1, lc=N_META)

    ntok_p = nbatch * seq
    h1_p, z_p, zif_p = front(x_prompt.reshape(ntok_p, D_MODEL), FFN_TM)
    ys_p, ym_p, s5_p, ml_p = mixers(z_p, zif_p, s5_m, ml_m, nseq=nbatch, seqlen=seq, nb=1, lc=256)
    y_p = back(h1_p, ys_p, ym_p, z_p, ntok_p)

    ys_s, ym_s, s5_s, ml_s = mixers(
        z_sm, zif_sm,
        (state_s5_re[l].reshape(nsamp, -1), state_s5_im[l].reshape(nsamp, -1)),
        (state_mlstm_conv[l], state_mlstm_C[l], state_mlstm_n[l], state_mlstm_m[l]),
        nseq=nsamp, seqlen=sseq, nb=STEP_NB, lc=sseq)
    y_s = back(h1_sm, ys_s, ym_s, z_sm, ntok_s)

    def pack(n, s5_st, ml_st):
        buf, c, nn, m = ml_st
        return (s5_st[0].reshape(1, n, S5_GROUPS, S5_STATE), s5_st[1].reshape(1, n, S5_GROUPS, S5_STATE),
                c[None], nn[None], m[None], buf[None])

    return ((y_p.reshape(nbatch, seq, D_MODEL), y_s.reshape(nsamp, sseq, D_MODEL))
            + pack(nbatch, s5_p, ml_p) + pack(nsamp, s5_s, ml_s))
```

```python
import functools

import jax
import jax.numpy as jnp
from jax import lax
from jax.experimental import pallas as pl
from jax.experimental.pallas import tpu as pltpu

F32 = jnp.float32
BF16 = jnp.bfloat16

D_MODEL = 2048
D_FF = 5632
N_META = 16
S5_WIDTH = 1024
S5_GROUP = 16
S5_GROUPS = 64
S5_STATE = 64
HEADS = 4
DQK = 128
DV = 256
QK_WIDTH = 1024
V_WIDTH = 1024
CONV_W = 4
N_BRANCH = 2
EPS = 1e-6

LANES = 128
S5_T = 8
S5_GPB = LANES // S5_GROUP
S5_NBLK = S5_WIDTH // LANES
S5_SW = S5_GPB * S5_STATE
VMEM_LIMIT = 58 * 1024 * 1024
FFN_TM = 1024
MERGE_TM = 256
WIN_CAST_ROWS = 456
STEP_NB = 16


def _cparams(sem):
    return pltpu.CompilerParams(dimension_semantics=sem, vmem_limit_bytes=VMEM_LIMIT)


def _rmsnorm(x, g):
    ms = jnp.mean(x * x, axis=-1, keepdims=True)
    return (x * lax.rsqrt(ms + EPS)) * g


def _dot(a, b):
    return jnp.dot(a, b, preferred_element_type=F32)


def _dot_nt(a, b):
    return lax.dot_general(a, b, (((1,), (1,)), ((), ())), preferred_element_type=F32)


def _dot_tn(a, b):
    return lax.dot_general(a, b, (((0,), (0,)), ((), ())), preferred_element_type=F32)


def _split3(x):
    hi = x.astype(BF16)
    r = x - hi.astype(F32)
    mid = r.astype(BF16)
    lo = (r - mid.astype(F32)).astype(BF16)
    return hi, mid, lo


def _ffn_kernel(x_ref, g_ref, wg_ref, wu_ref, wd_ref, *rest, final_norm):
    if final_norm:
        fg_ref, o_ref, xn_ref = rest
    else:
        o_ref, xn_ref = rest
    j = pl.program_id(1)

    @pl.when(j == 0)
    def _():
        xn_ref[...] = _rmsnorm(x_ref[...], g_ref[...]).astype(BF16)
        o_ref[...] = jnp.zeros(o_ref.shape, F32)

    xn = xn_ref[...]
    gt = _dot(xn, wg_ref[...].astype(BF16))
    up = _dot(xn, wu_ref[...].astype(BF16))
    act = (gt * jax.nn.sigmoid(gt) * up).astype(BF16)
    o_ref[...] += _dot(act, wd_ref[...].astype(BF16))

    @pl.when(j == pl.num_programs(1) - 1)
    def _():
        h = x_ref[...] + 0.5 * o_ref[...]
        if final_norm:
            h = _rmsnorm(h, fg_ref[...])
        o_ref[...] = h


def _ffn(x, g, wg, wu, wd, final_g=None, *, n, tm, tf=256):
    in_specs = [
        pl.BlockSpec((tm, D_MODEL), lambda i, j: (i, 0)),
        pl.BlockSpec((1, D_MODEL), lambda i, j: (0, 0)),
        pl.BlockSpec((D_MODEL, tf), lambda i, j: (0, j)),
        pl.BlockSpec((D_MODEL, tf), lambda i, j: (0, j)),
        pl.BlockSpec((tf, D_MODEL), lambda i, j: (j, 0)),
    ]
    args = [x, g, wg, wu, wd]
    if final_g is not None:
        in_specs.append(pl.BlockSpec((1, D_MODEL), lambda i, j: (0, 0)))
        args.append(final_g)
    return pl.pallas_call(
        functools.partial(_ffn_kernel, final_norm=final_g is not None),
        grid=(n // tm, D_FF // tf),
        in_specs=in_specs,
        out_specs=pl.BlockSpec((tm, D_MODEL), lambda i, j: (i, 0)),
        out_shape=jax.ShapeDtypeStruct((n, D_MODEL), F32),
        scratch_shapes=[pltpu.VMEM((tm, D_MODEL), BF16)],
        compiler_params=_cparams(("parallel", "arbitrary")),
        name="ffn",
    )(*args)


def _cast_rows_kernel(w_ref, o_ref, *, tail):
    last = pl.num_programs(0) - 1

    @pl.when(pl.program_id(0) < last)
    def _():
        o_ref[0] = w_ref[...].astype(BF16)

    @pl.when(pl.program_id(0) == last)
    def _():
        o_ref[0, :tail] = w_ref[:tail].astype(BF16)
        o_ref[0, tail:] = jnp.zeros((o_ref.shape[1] - tail, o_ref.shape[2]), BF16)


def _win_halves(wt, o_gate, *, tr):
    rows, k = wt.shape
    per = o_gate // tr
    tail = rows - (2 * per - 1) * tr
    assert per * tr == o_gate and tr % 8 == 0 and 0 < tail < tr and tail % 8 == 0
    return pl.pallas_call(
        functools.partial(_cast_rows_kernel, tail=tail),
        grid=(2 * per,),
        in_specs=[pl.BlockSpec((tr, k), lambda i: (i, 0))],
        out_specs=pl.BlockSpec((1, tr, k), lambda i: (i // per, i % per, 0)),
        out_shape=jax.ShapeDtypeStruct((2, o_gate, k), BF16),
        compiler_params=_cparams(("parallel",)),
        name="w_in_cast",
    )(wt)


def _win_kernel(h_ref, g_ref, wa_ref, wb_ref, wif_ref, z_ref, zif_ref, un_ref, *, na):
    j = pl.program_id(1)

    @pl.when(j == 0)
    def _():
        un = _rmsnorm(h_ref[...], g_ref[...]).astype(BF16)
        un_ref[...] = un
        zif_ref[...] = _dot_nt(un, wif_ref[...].astype(BF16))

    @pl.when(j < na)
    def _():
        z_ref[...] = _dot_nt(un_ref[...], wa_ref[0])

    @pl.when(j >= na)
    def _():
        z_ref[...] = _dot_nt(un_ref[...], wb_ref[0])


def _win(h, g, w2, wt, *, o_if, ngate, tm, tn=1024):
    n = h.shape[0]
    na = o_if // tn
    nb = ngate // tn
    return pl.pallas_call(
        functools.partial(_win_kernel, na=na),
        grid=(n // tm, na + nb),
        in_specs=[
            pl.BlockSpec((tm, D_MODEL), lambda i, j: (i, 0)),
            pl.BlockSpec((1, D_MODEL), lambda i, j: (0, 0)),
            pl.BlockSpec((1, tn, D_MODEL), lambda i, j: (0, jnp.minimum(j, na - 1), 0)),
            pl.BlockSpec((1, tn, D_MODEL), lambda i, j: (1, jnp.maximum(j - na, 0), 0)),
            pl.BlockSpec((LANES, D_MODEL), lambda i, j: (o_if // LANES, 0)),
        ],
        out_specs=[
            pl.BlockSpec((tm, tn), lambda i, j: (i, j)),
            pl.BlockSpec((tm, LANES), lambda i, j: (i, 0)),
        ],
        out_shape=[
            jax.ShapeDtypeStruct((n, (na + nb) * tn), F32),
            jax.ShapeDtypeStruct((n, LANES), F32),
        ],
        scratch_shapes=[pltpu.VMEM((tm, D_MODEL), BF16)],
        compiler_params=_cparams(("parallel", "arbitrary")),
        name="w_in",
    )(h, g, w2, w2, wt)


def _s5_prep_kernel(bbr_ref, bbi_ref, ptr_ref, pti_ref, pwr_ref, pwi_ref, cnr_ref, cni_ref,
                    w_ref, f_ref, e_ref):
    n = S5_T * LANES
    i32 = jnp.int32
    lg_h = S5_GROUP.bit_length() - 1
    lg_p = S5_STATE.bit_length() - 1
    lg_l = LANES.bit_length() - 1
    gmask = S5_GPB - 1

    def tile_mat(k, c, src_of_col):
        kk = lax.broadcasted_iota(i32, (k, c), 0)
        cc = lax.broadcasted_iota(i32, (k, c), 1)
        return jnp.where(kk == src_of_col(cc), 1.0, 0.0).astype(BF16)

    def group_mask(r, c, row_group, col_group):
        rr = lax.broadcasted_iota(i32, (r, c), 0)
        cc = lax.broadcasted_iota(i32, (r, c), 1)
        return row_group(rr) == col_group(cc)

    def tiled(x, mat):
        hi, mid, _ = _split3(x)
        return _dot(hi, mat) + _dot(mid, mat)

    def lag_power(pt):
        return jnp.concatenate(
            [jnp.broadcast_to(pt[:, S5_T - 1 - s:S5_T - s], (S5_SW, LANES)) for s in range(S5_T)], axis=1)

    def dot_hi(a, b):
        a_hi, a_mid, _ = _split3(a)
        b_hi, b_mid, _ = _split3(b)
        return _dot(a_hi, b_hi) + _dot(a_mid, b_hi) + _dot(a_hi, b_mid)

    sel_h = tile_mat(S5_GROUP, n, lambda c: c & (S5_GROUP - 1))
    mf = group_mask(S5_SW, n, lambda r: r >> lg_p, lambda c: (c >> lg_h) & gmask)
    lpr = lag_power(ptr_ref[...])
    lpi = lag_power(pti_ref[...])
    bxr = tiled(bbr_ref[...], sel_h)
    bxi = tiled(bbi_ref[...], sel_h)
    ftr = jnp.where(mf, lpr * bxr - lpi * bxi, 0.0)
    fti = jnp.where(mf, lpr * bxi + lpi * bxr, 0.0)
    f_ref[0, :S5_SW, :] = ftr.astype(BF16)
    f_ref[0, S5_SW:, :] = fti.astype(BF16)

    sel_p = tile_mat(S5_STATE, S5_SW, lambda c: c & (S5_STATE - 1))
    mc = group_mask(LANES, S5_SW, lambda r: r >> lg_h, lambda c: c >> lg_p)
    cxr = jnp.where(mc, tiled(cnr_ref[...], sel_p), 0.0)
    cxi = jnp.where(mc, tiled(cni_ref[...], sel_p), 0.0)

    for t in range(S5_T):
        pr = pwr_ref[t + 1:t + 2, :]
        pi = pwi_ref[t + 1:t + 2, :]
        e_ref[0, t * LANES:(t + 1) * LANES, :S5_SW] = (cxr * pr - cxi * pi).astype(BF16)
        e_ref[0, t * LANES:(t + 1) * LANES, S5_SW:] = (-(cxr * pi + cxi * pr)).astype(BF16)

    cn = jnp.concatenate([cxr, -cxi], axis=1)
    w_ref[...] = jnp.zeros(w_ref.shape, w_ref.dtype)
    for lag in range(S5_T):
        c0 = (S5_T - 1 - lag) * LANES
        fblk = jnp.concatenate([ftr[:, c0:c0 + LANES], fti[:, c0:c0 + LANES]], axis=0)
        kt = dot_hi(cn, fblk).astype(BF16)
        for s in range(S5_T - lag):
            t = s + lag
            w_ref[0, t * LANES:(t + 1) * LANES, s * LANES:(s + 1) * LANES] = kt


def _s5_prep(bb_re, bb_im, pwt_re, pwt_im, pw_re, pw_im, cn_re, cn_im):
    n = S5_T * LANES
    rows = lambda r, c: pl.BlockSpec((r, c), lambda j: (j, 0))
    cols = lambda r, c: pl.BlockSpec((r, c), lambda j: (0, j))
    out = pl.BlockSpec((1, n, n), lambda j: (j, 0, 0))
    return pl.pallas_call(
        _s5_prep_kernel,
        grid=(S5_NBLK,),
        in_specs=[rows(S5_SW, S5_GROUP)] * 4 + [cols(S5_GROUP, S5_SW)] * 2 + [rows(LANES, S5_STATE)] * 2,
        out_specs=[out, out, out],
        out_shape=[jax.ShapeDtypeStruct((S5_NBLK, n, n), BF16)] * 3,
        compiler_params=_cparams(("parallel",)),
        name="s5_prep",
    )(bb_re, bb_im, pwt_re, pwt_im, pw_re, pw_im, cn_re, cn_im)


def _s5_kernel(x_ref, w_ref, f_ref, e_ref, lam_ref, d_ref, h0re_ref, h0im_ref,
               y_ref, ore_ref, oim_ref, yi_ref, s_ref, hin_ref, *, nseq, nblk, rt):
    rows = nseq * nblk
    ar = lam_ref[0, 0:1, :]
    ai = lam_ref[0, 1:2, :]

    def load_u(r0):
        return jnp.concatenate(
            [x_ref[pl.ds(r0 * S5_T + s, rt, stride=S5_T), :] for s in range(S5_T)], axis=1)

    for r0 in range(0, rows, rt):
        ub = load_u(r0).astype(BF16)
        yi_ref[r0:r0 + rt, :] = _dot_nt(ub, w_ref[0])
        s_ref[r0:r0 + rt, :] = _dot_nt(ub, f_ref[0])

    if nblk == 1:
        hre = h0re_ref[...]
        him = h0im_ref[...]
        hin_ref[:, :S5_SW] = hre
        hin_ref[:, S5_SW:] = him
        ore_ref[...] = ar * hre - ai * him + s_ref[:, :S5_SW]
        oim_ref[...] = ar * him + ai * hre + s_ref[:, S5_SW:]
    else:
        def body(c, carry):
            new = []
            for b in range(nseq):
                hre, him = carry[2 * b], carry[2 * b + 1]
                row = b * nblk + c
                hin_ref[pl.ds(row, 1), :S5_SW] = hre
                hin_ref[pl.ds(row, 1), S5_SW:] = him
                sre = s_ref[pl.ds(row, 1), :S5_SW]
                sim = s_ref[pl.ds(row, 1), S5_SW:]
                new.append(ar * hre - ai * him + sre)
                new.append(ar * him + ai * hre + sim)
            return tuple(new)

        init = []
        for b in range(nseq):
            init.append(h0re_ref[b:b + 1, :])
            init.append(h0im_ref[b:b + 1, :])
        fin = lax.fori_loop(0, nblk, body, tuple(init))
        for b in range(nseq):
            ore_ref[b:b + 1, :] = fin[2 * b]
            oim_ref[b:b + 1, :] = fin[2 * b + 1]

    dt = jnp.concatenate([d_ref[...]] * S5_T, axis=1)
    for r0 in range(0, rows, rt):
        yo = _dot_nt(hin_ref[r0:r0 + rt, :].astype(BF16), e_ref[0])
        y = jax.nn.gelu(yi_ref[r0:r0 + rt, :] + yo + load_u(r0) * dt)
        for t in range(S5_T):
            y_ref[pl.ds(r0 * S5_T + t, rt, stride=S5_T), :] = y[:, t * LANES:(t + 1) * LANES]


def _s5(z, w, f, e, lam, d, h0re, h0im, *, nseq, seqlen):
    nblk = seqlen // S5_T
    rows = nseq * nblk
    rt = min(rows, 256)
    n = S5_T * LANES
    tok = nseq * seqlen
    kern = functools.partial(_s5_kernel, nseq=nseq, nblk=nblk, rt=rt)
    return pl.pallas_call(
        kern,
        grid=(S5_NBLK,),
        in_specs=[
            pl.BlockSpec((tok, LANES), lambda j: (0, j)),
            pl.BlockSpec((1, n, n), lambda j: (j, 0, 0)),
            pl.BlockSpec((1, n, 2 * S5_SW), lambda j: (j, 0, 0)),
            pl.BlockSpec((1, 2 * S5_SW, n), lambda j: (j, 0, 0)),
            pl.BlockSpec((1, 2, S5_SW), lambda j: (j, 0, 0)),
            pl.BlockSpec((1, LANES), lambda j: (0, j)),
            pl.BlockSpec((nseq, S5_SW), lambda j: (0, j)),
            pl.BlockSpec((nseq, S5_SW), lambda j: (0, j)),
        ],
        out_specs=[
            pl.BlockSpec((tok, LANES), lambda j: (0, j)),
            pl.BlockSpec((nseq, S5_SW), lambda j: (0, j)),
            pl.BlockSpec((nseq, S5_SW), lambda j: (0, j)),
        ],
        out_shape=[
            jax.ShapeDtypeStruct((tok, S5_WIDTH), F32),
            jax.ShapeDtypeStruct((nseq, S5_GROUPS * S5_STATE), F32),
            jax.ShapeDtypeStruct((nseq, S5_GROUPS * S5_STATE), F32),
        ],
        scratch_shapes=[
            pltpu.VMEM((rows, n), F32),
            pltpu.VMEM((rows, 2 * S5_SW), F32),
            pltpu.VMEM((rows, 2 * S5_SW), F32),
        ],
        compiler_params=_cparams(("parallel",)),
        name="s5",
    )(z, w, f, e, lam, d, h0re, h0im)


def _s5_operators(a_re, a_im, log_dt, b_re, b_im):
    dt = jnp.exp(log_dt)[:, None]
    ar, ai = a_re * dt, a_im * dt
    taus = jnp.arange(S5_T + 1, dtype=F32)[:, None, None]
    pmag = jnp.exp(taus * ar[None])
    pw_re = pmag * jnp.cos(taus * ai[None])
    pw_im = pmag * jnp.sin(taus * ai[None])
    nr, ni = pw_re[1] - 1.0, pw_im[1]
    den = a_re * a_re + a_im * a_im
    q_re = (nr * a_re + ni * a_im) / den
    q_im = (ni * a_re - nr * a_im) / den
    gp = S5_GROUPS * S5_STATE
    bb_re = (q_re[:, :, None] * b_re - q_im[:, :, None] * b_im).reshape(gp, S5_GROUP)
    bb_im = (q_re[:, :, None] * b_im + q_im[:, :, None] * b_re).reshape(gp, S5_GROUP)
    padrows = ((0, S5_GROUP - (S5_T + 1)), (0, 0))
    pw_re2 = jnp.pad(pw_re.reshape(S5_T + 1, gp), padrows)
    pw_im2 = jnp.pad(pw_im.reshape(S5_T + 1, gp), padrows)
    lam = jnp.stack([pw_re[S5_T].reshape(S5_NBLK, S5_SW), pw_im[S5_T].reshape(S5_NBLK, S5_SW)], axis=1)
    return bb_re, bb_im, pw_re2.T, pw_im2.T, pw_re2, pw_im2, lam


def _mlstm_kernel(qk_ref, v_ref, o_ref, if_ref, cw_ref, cb_ref, bif_ref, ng_ref,
                  buf0_ref, c0_ref, n0_ref, m0_ref,
                  y_ref, bufo_ref, co_ref, no_ref, mo_ref, xp_ref, *, nb, lc, carry):
    if carry:
        @pl.when(pl.program_id(1) == 0)
        def _():
            bufo_ref[...] = buf0_ref[...]
            co_ref[...] = c0_ref[...]
            no_ref[...] = n0_ref[...]
            mo_ref[...] = m0_ref[...]
        bufs_ref, cs_ref, ns_ref, ms_ref = bufo_ref, co_ref, no_ref, mo_ref
    else:
        bufs_ref, cs_ref, ns_ref, ms_ref = buf0_ref, c0_ref, n0_ref, m0_ref

    lp = max(lc, LANES)
    pad = CONV_W - 1
    heads = range(HEADS)
    head_row = lax.broadcasted_iota(jnp.int32, (HEADS, DQK), 0)
    head_lane = lax.broadcasted_iota(jnp.int32, (1, HEADS), 1)
    row = lax.broadcasted_iota(jnp.int32, (lc, lc), 0)
    col = lax.broadcasted_iota(jnp.int32, (lc, lc), 1)
    causal = row >= col
    tril = jnp.where(causal, 1.0, 0.0).astype(BF16)
    lane = lax.broadcasted_iota(jnp.int32, (lc, LANES), 1)
    is_i = lane < HEADS
    is_f = jnp.logical_and(lane >= HEADS, lane < 2 * HEADS)

    for s in range(nb):
        r0 = s * lc
        xp_ref[s, 0:8 - pad, :] = jnp.zeros((8 - pad, QK_WIDTH), F32)
        xp_ref[s, 8 - pad:8, :] = bufs_ref[s]
        xp_ref[s, 8:8 + lc, :] = qk_ref[r0:r0 + lc, :]
        xp = xp_ref[s]
        conv = cb_ref[...] + xp[8:8 + lc] * cw_ref[pad:pad + 1, :]
        for j in range(1, CONV_W):
            conv = conv + pltpu.roll(xp, j, 0)[8:8 + lc] * cw_ref[pad - j:pad - j + 1, :]
        bufo_ref[s] = xp_ref[s, 8 + lc - pad:8 + lc, :]
        n_all = ns_ref[s]
        m_all = ms_ref[s]
        qk = conv * jax.nn.sigmoid(conv)
        q = qk[:, :QK_WIDTH // 2] * (DQK ** -0.5)
        k = qk[:, QK_WIDTH // 2:]

        gate = if_ref[r0:r0 + lc, :] + bif_ref[...]
        lf = jnp.where(is_f, jax.nn.log_sigmoid(gate), 0.0)
        parts = _split3(lf)
        bcum = _dot(tril, parts[0]) + _dot(tril, parts[1]) + _dot(tril, parts[2])
        pc = jnp.where(is_i, gate, bcum)
        if lc < lp:
            pc_t = jnp.concatenate([pc, jnp.zeros((lp - lc, LANES), F32)], axis=0).T[:, :lc]
        else:
            pc_t = pc.T

        i_col = [pc[:, h:h + 1] for h in heads]
        b_col = [pc[:, HEADS + h:HEADS + h + 1] for h in heads]
        m0 = [m_all[:, h:h + 1] for h in heads]
        qh = [q[:, h * DQK:(h + 1) * DQK] for h in heads]
        kh = [k[:, h * DQK:(h + 1) * DQK] for h in heads]
        qb = [x.astype(BF16) for x in qh]
        vb = [v_ref[r0:r0 + lc, h * DV:(h + 1) * DV].astype(BF16) for h in heads]
        c_prev = [cs_ref[s, h] for h in heads]
        n_prev = [n_all[h:h + 1, :] for h in heads]

        qk_t = [_dot_nt(qb[h], kh[h].astype(BF16)) for h in heads]
        carried = [_dot_nt(qb[h], c_prev[h].astype(BF16)) for h in heads]
        logw = [jnp.where(causal, b_col[h] - pc_t[HEADS + h:HEADS + h + 1, :] + pc_t[h:h + 1, :], -jnp.inf)
                for h in heads]
        g = [b_col[h] + m0[h] for h in heads]
        m = [jnp.maximum(g[h], jnp.max(logw[h], axis=-1, keepdims=True)) for h in heads]
        inter = [jnp.exp(g[h] - m[h]) for h in heads]
        sc = [qk_t[h] * jnp.exp(logw[h] - m[h]) for h in heads]
        num = [_dot(sc[h].astype(BF16), vb[h]) + inter[h] * carried[h] for h in heads]
        den = [jnp.sum(sc[h], axis=-1, keepdims=True)
               + inter[h] * jnp.sum(qh[h] * n_prev[h], axis=-1, keepdims=True) for h in heads]
        hh = [num[h] / jnp.maximum(jnp.abs(den[h]), jnp.exp(-m[h])) for h in heads]
        hh = [x * lax.rsqrt(jnp.mean(x * x, axis=-1, keepdims=True) + EPS) for x in hh]
        for h in heads:
            og = jax.nn.sigmoid(o_ref[r0:r0 + lc, h * DV:(h + 1) * DV])
            y_ref[r0:r0 + lc, h * DV:(h + 1) * DV] = hh[h] * ng_ref[:, h * DV:(h + 1) * DV] * og

        b_last = [x[lc - 1:lc, :] for x in b_col]
        m_new = [x[lc - 1:lc, :] for x in m]
        decay = [jnp.exp(b_last[h] + m0[h] - m_new[h]) for h in heads]
        kw = [kh[h] * jnp.exp(b_last[h] - b_col[h] + i_col[h] - m_new[h]) for h in heads]
        upd = [_dot_tn(vb[h], kw[h].astype(BF16)) for h in heads]
        n_new_all = jnp.zeros((HEADS, DQK), F32)
        m_new_all = jnp.zeros((1, HEADS), F32)
        for h in heads:
            co_ref[s, h] = decay[h] * c_prev[h] + upd[h]
            n_new = decay[h] * n_prev[h] + jnp.sum(kw[h], axis=0, keepdims=True)
            n_new_all = jnp.where(head_row == h, n_new, n_new_all)
            m_new_all = jnp.where(head_lane == h, m_new[h], m_new_all)
        no_ref[s] = n_new_all
        mo_ref[s] = m_new_all


def _mlstm(z, zif, cw, cb, bif, ng, buf0, c0, n0, m0, *, nseq, seqlen, nb, lc):
    tok = nseq * seqlen
    nchunk = seqlen // lc
    rows = nb * lc
    tok_map = lambda col: (lambda i, c: (i * nchunk + c, col))
    const2 = lambda i, c: (0, 0)
    st3 = lambda i, c: (i, 0, 0)
    st4 = lambda i, c: (i, 0, 0, 0)
    kern = functools.partial(_mlstm_kernel, nb=nb, lc=lc, carry=nchunk > 1)
    return pl.pallas_call(
        kern,
        grid=(nseq // nb, nchunk),
        in_specs=[
            pl.BlockSpec((rows, QK_WIDTH), tok_map(1)),
            pl.BlockSpec((rows, V_WIDTH), tok_map(2)),
            pl.BlockSpec((rows, V_WIDTH), tok_map(3)),
            pl.BlockSpec((rows, LANES), tok_map(0)),
            pl.BlockSpec((CONV_W, QK_WIDTH), const2),
            pl.BlockSpec((1, QK_WIDTH), const2),
            pl.BlockSpec((1, LANES), const2),
            pl.BlockSpec((1, V_WIDTH), const2),
            pl.BlockSpec((nb, CONV_W - 1, QK_WIDTH), st3),
            pl.BlockSpec((nb, HEADS, DV, DQK), st4),
            pl.BlockSpec((nb, HEADS, DQK), st3),
            pl.BlockSpec((nb, 1, HEADS), st3),
        ],
        out_specs=[
            pl.BlockSpec((rows, V_WIDTH), tok_map(0)),
            pl.BlockSpec((nb, CONV_W - 1, QK_WIDTH), st3),
            pl.BlockSpec((nb, HEADS, DV, DQK), st4),
            pl.BlockSpec((nb, HEADS, DQK), st3),
            pl.BlockSpec((nb, 1, HEADS), st3),
        ],
        out_shape=[
            jax.ShapeDtypeStruct((tok, V_WIDTH), F32),
            jax.ShapeDtypeStruct((nseq, CONV_W - 1, QK_WIDTH), F32),
            jax.ShapeDtypeStruct((nseq, HEADS, DV, DQK), F32),
            jax.ShapeDtypeStruct((nseq, HEADS, DQK), F32),
            jax.ShapeDtypeStruct((nseq, 1, HEADS), F32),
        ],
        scratch_shapes=[pltpu.VMEM((nb, lc + 8, QK_WIDTH), F32)],
        compiler_params=_cparams(("parallel", "arbitrary")),
        name="mlstm",
    )(z, z, z, zif, cw, cb, bif, ng, buf0, c0, n0, m0)


def _mlstm_step_kernel(qk_ref, v_ref, o_ref, if_ref, cw_ref, cb_ref, bif_ref, ng_ref,
                       buf0_ref, c0_ref, n0x_ref, m0x_ref,
                       y_ref, bufo_ref, co_ref, nox_ref, mox_ref, xp_ref, conv_ref, *, nb, lc):
    rows = nb * lc
    lg = lc.bit_length() - 1
    i32 = jnp.int32
    row = lax.broadcasted_iota(i32, (rows, rows), 0)
    col = lax.broadcasted_iota(i32, (rows, rows), 1)
    same = (row >> lg) == (col >> lg)
    causal = jnp.logical_and(same, row >= col)
    same_b = jnp.where(same, 1.0, 0.0).astype(BF16)
    tril_b = jnp.where(causal, 1.0, 0.0).astype(BF16)
    lane = lax.broadcasted_iota(i32, (rows, LANES), 1)
    is_i = lane < HEADS
    is_f = jnp.logical_and(lane >= HEADS, lane < 2 * HEADS)

    def seg_dot(mat, x):
        hi, mid, lo = _split3(x)
        return _dot(mat, hi) + _dot(mat, mid) + _dot(mat, lo)

    for s in range(nb):
        xp_ref[s, 5:8, :] = buf0_ref[s]
        xp_ref[s, 8:8 + lc, :] = qk_ref[s * lc:(s + 1) * lc, :]
        conv_ref[s * lc:(s + 1) * lc, :] = cb_ref[...] + sum(
            xp_ref[s, 5 + j:5 + j + lc, :] * cw_ref[j:j + 1, :] for j in range(CONV_W))
        bufo_ref[s] = xp_ref[s, 5 + lc:8 + lc, :]
    conv = conv_ref[...]
    qk = conv * jax.nn.sigmoid(conv)
    q = qk[:, :QK_WIDTH // 2] * (DQK ** -0.5)
    k = qk[:, QK_WIDTH // 2:]

    gate = if_ref[...] + bif_ref[...]
    lf = jnp.where(is_f, jax.nn.log_sigmoid(gate), 0.0)
    bcum = seg_dot(tril_b, lf)
    btot = seg_dot(same_b, lf)
    pc = jnp.where(is_i, gate, bcum)
    if rows < LANES:
        pc_t = jnp.concatenate([pc, jnp.zeros((LANES - rows, LANES), F32)], axis=0).T[:, :rows]
    else:
        pc_t = pc.T
    m0x = m0x_ref[...]
    mox = jnp.zeros((rows, LANES), F32)

    for h in range(HEADS):
        i_col = pc[:, h:h + 1]
        b_col = pc[:, HEADS + h:HEADS + h + 1]
        i_row = pc_t[h:h + 1, :]
        b_row = pc_t[HEADS + h:HEADS + h + 1, :]
        b_last = btot[:, HEADS + h:HEADS + h + 1]
        m0 = m0x[:, h:h + 1]
        lw = b_col - b_row + i_row
        logw = jnp.where(causal, lw, -jnp.inf)
        g = b_col + m0
        m = jnp.maximum(g, jnp.max(logw, axis=-1, keepdims=True))
        w = jnp.exp(logw - m)
        inter = jnp.exp(g - m)
        lw_end = jnp.where(same, b_last - b_row + i_row, -jnp.inf)
        m_new = jnp.maximum(b_last + m0, jnp.max(lw_end, axis=-1, keepdims=True))
        decay = jnp.exp(b_last + m0 - m_new)
        qh = q[:, h * DQK:(h + 1) * DQK]
        kh = k[:, h * DQK:(h + 1) * DQK]
        qb = qh.astype(BF16)
        vb = v_ref[:, h * DV:(h + 1) * DV].astype(BF16)
        n_prev = n0x_ref[:, h * DQK:(h + 1) * DQK]
        sc = _dot_nt(qb, kh.astype(BF16)) * w
        carried = jnp.concatenate(
            [_dot_nt(qb[s * lc:(s + 1) * lc], c0_ref[s, h].astype(BF16)) for s in range(nb)], axis=0)
        num = _dot(sc.astype(BF16), vb) + inter * carried
        den = jnp.sum(sc, axis=-1, keepdims=True) + inter * jnp.sum(qh * n_prev, axis=-1, keepdims=True)
        hh = num / jnp.maximum(jnp.abs(den), jnp.exp(-m))
        hh = hh * lax.rsqrt(jnp.mean(hh * hh, axis=-1, keepdims=True) + EPS)
        og = jax.nn.sigmoid(o_ref[:, h * DV:(h + 1) * DV])
        y_ref[:, h * DV:(h + 1) * DV] = hh * ng_ref[:, h * DV:(h + 1) * DV] * og

        kw = kh * jnp.exp(b_last - b_col + i_col - m_new)
        kwb = kw.astype(BF16)
        for s in range(nb):
            r0 = s * lc
            co_ref[s, h] = decay[r0:r0 + 1, :] * c0_ref[s, h] + _dot_tn(vb[r0:r0 + lc], kwb[r0:r0 + lc])
        nox_ref[:, h * DQK:(h + 1) * DQK] = decay * n_prev + seg_dot(same_b, kw)
        mox = jnp.where(lane == h, m_new, mox)
    mox_ref[...] = mox


def _mlstm_step(z, zif, cw, cb, bif, ng, buf0, c0, n0, m0, *, nseq, seqlen, nb):
    lc = seqlen
    assert lc & (lc - 1) == 0 and lc >= CONV_W - 1
    tok = nseq * seqlen
    rows = nb * lc
    n0x = jnp.repeat(n0.reshape(nseq, HEADS * DQK), lc, axis=0)
    m0x = jnp.repeat(jnp.pad(m0, ((0, 0), (0, LANES - HEADS))), lc, axis=0)
    tok_map = lambda col: (lambda i: (i, col))
    const2 = lambda i: (0, 0)
    st3 = lambda i: (i, 0, 0)
    st4 = lambda i: (i, 0, 0, 0)
    y, buf, c, nox, mox = pl.pallas_call(
        functools.partial(_mlstm_step_kernel, nb=nb, lc=lc),
        grid=(nseq // nb,),
        in_specs=[
            pl.BlockSpec((rows, QK_WIDTH), tok_map(1)),
            pl.BlockSpec((rows, V_WIDTH), tok_map(2)),
            pl.BlockSpec((rows, V_WIDTH), tok_map(3)),
            pl.BlockSpec((rows, LANES), tok_map(0)),
            pl.BlockSpec((CONV_W, QK_WIDTH), const2),
            pl.BlockSpec((1, QK_WIDTH), const2),
            pl.BlockSpec((1, LANES), const2),
            pl.BlockSpec((1, V_WIDTH), const2),
            pl.BlockSpec((nb, CONV_W - 1, QK_WIDTH), st3),
            pl.BlockSpec((nb, HEADS, DV, DQK), st4),
            pl.BlockSpec((rows, HEADS * DQK), tok_map(0)),
            pl.BlockSpec((rows, LANES), tok_map(0)),
        ],
        out_specs=[
            pl.BlockSpec((rows, V_WIDTH), tok_map(0)),
            pl.BlockSpec((nb, CONV_W - 1, QK_WIDTH), st3),
            pl.BlockSpec((nb, HEADS, DV, DQK), st4),
            pl.BlockSpec((rows, HEADS * DQK), tok_map(0)),
            pl.BlockSpec((rows, LANES), tok_map(0)),
        ],
        out_shape=[
            jax.ShapeDtypeStruct((tok, V_WIDTH), F32),
            jax.ShapeDtypeStruct((nseq, CONV_W - 1, QK_WIDTH), F32),
            jax.ShapeDtypeStruct((nseq, HEADS, DV, DQK), F32),
            jax.ShapeDtypeStruct((tok, HEADS * DQK), F32),
            jax.ShapeDtypeStruct((tok, LANES), F32),
        ],
        scratch_shapes=[pltpu.VMEM((nb, lc + 8, QK_WIDTH), F32), pltpu.VMEM((rows, QK_WIDTH), F32)],
        compiler_params=_cparams(("parallel",)),
        name="mlstm_step",
    )(z, z, z, zif, cw, cb, bif, ng, buf0, c0, n0x, m0x)
    n = nox[::lc].reshape(nseq, HEADS, DQK)
    m = mox[::lc, :HEADS]
    return y, buf, c, n, m


def _merge_kernel(h_ref, ys_ref, ym_ref, g1_ref, g2_ref, wglu_ref, wbs_ref, wbm_ref, wo_ref, o_ref):
    ys = ys_ref[...]
    glu = ys * jax.nn.sigmoid(_dot(ys.astype(BF16), wglu_ref[...]))
    a = _dot(glu.astype(BF16), wbs_ref[...])
    b = _dot(ym_ref[...].astype(BF16), wbm_ref[...])
    merged = jax.nn.sigmoid(g1_ref[...]) * a + jax.nn.sigmoid(g2_ref[...]) * b
    o_ref[...] = h_ref[...] + _dot(merged.astype(BF16), wo_ref[...])


def _merge(h, ys, ym, z, wglu, wbs, wbm, wo, *, n, tm):
    row = lambda i: (i, 0)
    const = lambda i: (0, 0)
    resident = lambda shape: pl.BlockSpec(shape, const, pipeline_mode=pl.Buffered(1))
    return pl.pallas_call(
        _merge_kernel,
        grid=(n // tm,),
        in_specs=[
            pl.BlockSpec((tm, D_MODEL), row),
            pl.BlockSpec((tm, S5_WIDTH), row),
            pl.BlockSpec((tm, V_WIDTH), row),
            pl.BlockSpec((tm, D_MODEL), lambda i: (i, 2)),
            pl.BlockSpec((tm, D_MODEL), lambda i: (i, 3)),
            resident((S5_WIDTH, S5_WIDTH)),
            resident((S5_WIDTH, D_MODEL)),
            resident((V_WIDTH, D_MODEL)),
            resident((D_MODEL, D_MODEL)),
        ],
        out_specs=pl.BlockSpec((tm, D_MODEL), row),
        out_shape=jax.ShapeDtypeStruct((n, D_MODEL), F32),
        compiler_params=_cparams(("parallel",)),
        name="merge",
    )(h, ys, ym, z, z, wglu, wbs, wbm, wo)


def kernel(x_prompt, x_sample, state_s5_re, state_s5_im, state_mlstm_C, state_mlstm_n, state_mlstm_m,
           state_mlstm_conv, meta_tokens, ffn1_norm, ffn1_w_gate, ffn1_w_up, ffn1_w_down, mix_norm, w_in,
           s5_A_re, s5_A_im, s5_log_dt, s5_B_re, s5_B_im, s5_C_re, s5_C_im, s5_D, s5_w_glu,
           mlstm_conv_w, mlstm_conv_b, mlstm_b_i, mlstm_b_f, mlstm_norm, w_branch_s5, w_branch_mlstm,
           w_out, ffn2_norm, ffn2_w_gate, ffn2_w_up, ffn2_w_down, final_norm):
    nbatch, seq, _ = x_prompt.shape
    nsamp, sseq, _ = x_sample.shape
    l = 0

    w1g, w1u, w1d = ffn1_w_gate[l], ffn1_w_up[l], ffn1_w_down[l]
    w2g, w2u, w2d = ffn2_w_gate[l], ffn2_w_up[l], ffn2_w_down[l]
    o_if = S5_WIDTH + QK_WIDTH + 2 * V_WIDTH
    o_gate = o_if + 2 * HEADS
    wt = w_in[l].T
    w2 = _win_halves(wt, o_gate, tr=WIN_CAST_ROWS)
    wglu, wbs, wbm, wo = (w[l].astype(BF16) for w in (s5_w_glu, w_branch_s5, w_branch_mlstm, w_out))
    g1 = ffn1_norm[l][None]
    gm = mix_norm[l][None]
    g2 = ffn2_norm[l][None]
    gf = final_norm[None]
    bif = jnp.pad(jnp.concatenate([mlstm_b_i[l], mlstm_b_f[l]]), (0, LANES - 2 * HEADS))[None]
    cw = mlstm_conv_w[l]
    cb = mlstm_conv_b[l][None]
    ng = mlstm_norm[l][None]
    d_skip = s5_D[l][None]

    *s5_ops, lam = _s5_operators(s5_A_re[l], s5_A_im[l], s5_log_dt[l], s5_B_re[l], s5_B_im[l])
    gh = S5_GROUPS * S5_GROUP
    w_toe, f_bf, e_bf = _s5_prep(*s5_ops, s5_C_re[l].reshape(gh, S5_STATE), s5_C_im[l].reshape(gh, S5_STATE))

    def front(x, tm):
        n = x.shape[0]
        h1 = _ffn(x, g1, w1g, w1u, w1d, n=n, tm=tm)
        z, zif = _win(h1, gm, w2, wt, o_if=o_if, ngate=N_BRANCH * D_MODEL, tm=tm)
        return h1, z, zif

    def mixers(z, zif, s5_state, ml_state, *, nseq, seqlen, nb, lc):
        ys, sre, sim = _s5(z, w_toe, f_bf, e_bf, lam, d_skip, s5_state[0], s5_state[1], nseq=nseq, seqlen=seqlen)
        buf0, c0, n0, m0 = ml_state
        if lc == seqlen and nb > 1:
            ym, buf, c, n, m = _mlstm_step(z, zif, cw, cb, bif, ng, buf0, c0, n0, m0,
                                           nseq=nseq, seqlen=seqlen, nb=nb)
        else:
            ym, buf, c, n, m = _mlstm(z, zif, cw, cb, bif, ng, buf0, c0, n0, m0.reshape(nseq, 1, HEADS),
                                      nseq=nseq, seqlen=seqlen, nb=nb, lc=lc)
            m = m.reshape(nseq, HEADS)
        return ys, ym, (sre, sim), (buf, c, n, m)

    def back(h1, ys, ym, z, n):
        h2 = _merge(h1, ys, ym, z, wglu, wbs, wbm, wo, n=n, tm=MERGE_TM)
        return _ffn(h2, g2, w2g, w2u, w2d, gf, n=n, tm=FFN_TM)

    ntok_s = nsamp * sseq
    x_sm = jnp.concatenate([x_sample.reshape(ntok_s, D_MODEL), meta_tokens], axis=0)
    h1_sm, z_sm, zif_sm = front(x_sm, ntok_s + N_META)

    z_m = jnp.tile(z_sm[ntok_s:], (nbatch, 1))
    zif_m = jnp.tile(zif_sm[ntok_s:], (nbatch, 1))
    zeros = lambda *s: jnp.zeros((nbatch,) + s, F32)
    _, _, s5_m, ml_m = mixers(
        z_m, zif_m, (zeros(S5_GROUPS * S5_STATE), zeros(S5_GROUPS * S5_STATE)),
        (zeros(CONV_W - 1, QK_WIDTH), zeros(HEADS, DV, DQK), zeros(HEADS, DQK), zeros(HEADS)),
        nseq=nbatch, seqlen=N_META, nb=1, lc=N_META)

    ntok_p = nbatch * seq
    h1_p, z_p, zif_p = front(x_prompt.reshape(ntok_p, D_MODEL), FFN_TM)
    ys_p, ym_p, s5_p, ml_p = mixers(z_p, zif_p, s5_m, ml_m, nseq=nbatch, seqlen=seq, nb=1, lc=256)
    y_p = back(h1_p, ys_p, ym_p, z_p, ntok_p)

    ys_s, ym_s, s5_s, ml_s = mixers(
        z_sm, zif_sm,
        (state_s5_re[l].reshape(nsamp, -1), state_s5_im[l].reshape(nsamp, -1)),
        (state_mlstm_conv[l], state_mlstm_C[l], state_mlstm_n[l], state_mlstm_m[l]),
        nseq=nsamp, seqlen=sseq, nb=STEP_NB, lc=sseq)
    y_s = back(h1_sm, ys_s, ym_s, z_sm, ntok_s)

    def pack(n, s5_st, ml_st):
        buf, c, nn, m = ml_st
        return (s5_st[0].reshape(1, n, S5_GROUPS, S5_STATE), s5_st[1].reshape(1, n, S5_GROUPS, S5_STATE),
                c[None], nn[None], m[None], buf[None])

    return ((y_p.reshape(nbatch, seq, D_MODEL), y_s.reshape(nsamp, sseq, D_MODEL))
            + pack(nbatch, s5_p, ml_p) + pack(nsamp, s5_s, ml_s))
```

```python
import functools

import jax
import jax.numpy as jnp
from jax import lax
from jax.experimental import pallas as pl
from jax.experimental.pallas import tpu as pltpu

F32 = jnp.float32
BF16 = jnp.bfloat16

D_MODEL = 2048
D_FF = 5632
N_META = 16
S5_WIDTH = 1024
S5_GROUP = 16
S5_GROUPS = 64
S5_STATE = 64
HEADS = 4
DQK = 128
DV = 256
QK_WIDTH = 1024
V_WIDTH = 1024
CONV_W = 4
N_BRANCH = 2
EPS = 1e-6

LANES = 128
S5_T = 8
S5_GPB = LANES // S5_GROUP
S5_NBLK = S5_WIDTH // LANES
S5_SW = S5_GPB * S5_STATE
VMEM_LIMIT = 58 * 1024 * 1024
FFN_TM = 1024
MERGE_TM = 256
WIN_CAST_ROWS = 456
STEP_NB = 16


def _cparams(sem):
    return pltpu.CompilerParams(dimension_semantics=sem, vmem_limit_bytes=VMEM_LIMIT)


def _rmsnorm(x, g):
    ms = jnp.mean(x * x, axis=-1, keepdims=True)
    return (x * lax.rsqrt(ms + EPS)) * g


def _dot(a, b):
    return jnp.dot(a, b, preferred_element_type=F32)


def _dot_nt(a, b):
    return lax.dot_general(a, b, (((1,), (1,)), ((), ())), preferred_element_type=F32)


def _dot_tn(a, b):
    return lax.dot_general(a, b, (((0,), (0,)), ((), ())), preferred_element_type=F32)


def _split3(x):
    hi = x.astype(BF16)
    r = x - hi.astype(F32)
    mid = r.astype(BF16)
    lo = (r - mid.astype(F32)).astype(BF16)
    return hi, mid, lo


def _ffn_kernel(x_ref, g_ref, wg_ref, wu_ref, wd_ref, *rest, final_norm):
    if final_norm:
        fg_ref, o_ref, xn_ref = rest
    else:
        o_ref, xn_ref = rest
    j = pl.program_id(1)

    @pl.when(j == 0)
    def _():
        xn_ref[...] = _rmsnorm(x_ref[...], g_ref[...]).astype(BF16)
        o_ref[...] = jnp.zeros(o_ref.shape, F32)

    xn = xn_ref[...]
    gt = _dot(xn, wg_ref[...].astype(BF16))
    up = _dot(xn, wu_ref[...].astype(BF16))
    act = (gt * jax.nn.sigmoid(gt) * up).astype(BF16)
    o_ref[...] += _dot(act, wd_ref[...].astype(BF16))

    @pl.when(j == pl.num_programs(1) - 1)
    def _():
        h = x_ref[...] + 0.5 * o_ref[...]
        if final_norm:
            h = _rmsnorm(h, fg_ref[...])
        o_ref[...] = h


def _ffn(x, g, wg, wu, wd, final_g=None, *, n, tm, tf=256):
    in_specs = [
        pl.BlockSpec((tm, D_MODEL), lambda i, j: (i, 0)),
        pl.BlockSpec((1, D_MODEL), lambda i, j: (0, 0)),
        pl.BlockSpec((D_MODEL, tf), lambda i, j: (0, j)),
        pl.BlockSpec((D_MODEL, tf), lambda i, j: (0, j)),
        pl.BlockSpec((tf, D_MODEL), lambda i, j: (j, 0)),
    ]
    args = [x, g, wg, wu, wd]
    if final_g is not None:
        in_specs.append(pl.BlockSpec((1, D_MODEL), lambda i, j: (0, 0)))
        args.append(final_g)
    return pl.pallas_call(
        functools.partial(_ffn_kernel, final_norm=final_g is not None),
        grid=(n // tm, D_FF // tf),
        in_specs=in_specs,
        out_specs=pl.BlockSpec((tm, D_MODEL), lambda i, j: (i, 0)),
        out_shape=jax.ShapeDtypeStruct((n, D_MODEL), F32),
        scratch_shapes=[pltpu.VMEM((tm, D_MODEL), BF16)],
        compiler_params=_cparams(("parallel", "arbitrary")),
        name="ffn",
    )(*args)


def _cast_rows_kernel(w_ref, o_ref, *, tail):
    last = pl.num_programs(0) - 1

    @pl.when(pl.program_id(0) < last)
    def _():
        o_ref[0] = w_ref[...].astype(BF16)

    @pl.when(pl.program_id(0) == last)
    def _():
        o_ref[0, :tail] = w_ref[:tail].astype(BF16)
        o_ref[0, tail:] = jnp.zeros((o_ref.shape[1] - tail, o_ref.shape[2]), BF16)


def _win_halves(wt, o_gate, *, tr):
    rows, k = wt.shape
    per = o_gate // tr
    tail = rows - (2 * per - 1) * tr
    assert per * tr == o_gate and tr % 8 == 0 and 0 < tail < tr and tail % 8 == 0
    return pl.pallas_call(
        functools.partial(_cast_rows_kernel, tail=tail),
        grid=(2 * per,),
        in_specs=[pl.BlockSpec((tr, k), lambda i: (i, 0))],
        out_specs=pl.BlockSpec((1, tr, k), lambda i: (i // per, i % per, 0)),
        out_shape=jax.ShapeDtypeStruct((2, o_gate, k), BF16),
        compiler_params=_cparams(("parallel",)),
        name="w_in_cast",
    )(wt)


def _win_kernel(h_ref, g_ref, wa_ref, wb_ref, wif_ref, z_ref, zif_ref, un_ref, *, na):
    j = pl.program_id(1)

    @pl.when(j == 0)
    def _():
        un = _rmsnorm(h_ref[...], g_ref[...]).astype(BF16)
        un_ref[...] = un
        zif_ref[...] = _dot_nt(un, wif_ref[...].astype(BF16))

    @pl.when(j < na)
    def _():
        z_ref[...] = _dot_nt(un_ref[...], wa_ref[0])

    @pl.when(j >= na)
    def _():
        z_ref[...] = _dot_nt(un_ref[...], wb_ref[0])


def _win(h, g, w2, wt, *, o_if, ngate, tm, tn=1024):
    n = h.shape[0]
    na = o_if // tn
    nb = ngate // tn
    return pl.pallas_call(
        functools.partial(_win_kernel, na=na),
        grid=(n // tm, na + nb),
        in_specs=[
            pl.BlockSpec((tm, D_MODEL), lambda i, j: (i, 0)),
            pl.BlockSpec((1, D_MODEL), lambda i, j: (0, 0)),
            pl.BlockSpec((1, tn, D_MODEL), lambda i, j: (0, jnp.minimum(j, na - 1), 0)),
            pl.BlockSpec((1, tn, D_MODEL), lambda i, j: (1, jnp.maximum(j - na, 0), 0)),
            pl.BlockSpec((LANES, D_MODEL), lambda i, j: (o_if // LANES, 0)),
        ],
        out_specs=[
            pl.BlockSpec((tm, tn), lambda i, j: (i, j)),
            pl.BlockSpec((tm, LANES), lambda i, j: (i, 0)),
        ],
        out_shape=[
            jax.ShapeDtypeStruct((n, (na + nb) * tn), F32),
            jax.ShapeDtypeStruct((n, LANES), F32),
        ],
        scratch_shapes=[pltpu.VMEM((tm, D_MODEL), BF16)],
        compiler_params=_cparams(("parallel", "arbitrary")),
        name="w_in",
    )(h, g, w2, w2, wt)


def _s5_prep_kernel(bbr_ref, bbi_ref, ptr_ref, pti_ref, pwr_ref, pwi_ref, cnr_ref, cni_ref,
                    w_ref, f_ref, e_ref):
    n = S5_T * LANES
    i32 = jnp.int32
    lg_h = S5_GROUP.bit_length() - 1
    lg_p = S5_STATE.bit_length() - 1
    lg_l = LANES.bit_length() - 1
    gmask = S5_GPB - 1

    def tile_mat(k, c, src_of_col):
        kk = lax.broadcasted_iota(i32, (k, c), 0)
        cc = lax.broadcasted_iota(i32, (k, c), 1)
        return jnp.where(kk == src_of_col(cc), 1.0, 0.0).astype(BF16)

    def group_mask(r, c, row_group, col_group):
        rr = lax.broadcasted_iota(i32, (r, c), 0)
        cc = lax.broadcasted_iota(i32, (r, c), 1)
        return row_group(rr) == col_group(cc)

    def tiled(x, mat):
        hi, mid, _ = _split3(x)
        return _dot(hi, mat) + _dot(mid, mat)

    def lag_power(pt):
        return jnp.concatenate(
            [jnp.broadcast_to(pt[:, S5_T - 1 - s:S5_T - s], (S5_SW, LANES)) for s in range(S5_T)], axis=1)

    def dot_hi(a, b):
        a_hi, a_mid, _ = _split3(a)
        b_hi, b_mid, _ = _split3(b)
        return _dot(a_hi, b_hi) + _dot(a_mid, b_hi) + _dot(a_hi, b_mid)

    sel_h = tile_mat(S5_GROUP, n, lambda c: c & (S5_GROUP - 1))
    mf = group_mask(S5_SW, n, lambda r: r >> lg_p, lambda c: (c >> lg_h) & gmask)
    lpr = lag_power(ptr_ref[...])
    lpi = lag_power(pti_ref[...])
    bxr = tiled(bbr_ref[...], sel_h)
    bxi = tiled(bbi_ref[...], sel_h)
    ftr = jnp.where(mf, lpr * bxr - lpi * bxi, 0.0)
    fti = jnp.where(mf, lpr * bxi + lpi * bxr, 0.0)
    f_ref[0, :S5_SW, :] = ftr.astype(BF16)
    f_ref[0, S5_SW:, :] = fti.astype(BF16)

    sel_p = tile_mat(S5_STATE, S5_SW, lambda c: c & (S5_STATE - 1))
    mc = group_mask(LANES, S5_SW, lambda r: r >> lg_h, lambda c: c >> lg_p)
    cxr = jnp.where(mc, tiled(cnr_ref[...], sel_p), 0.0)
    cxi = jnp.where(mc, tiled(cni_ref[...], sel_p), 0.0)

    for t in range(S5_T):
        pr = pwr_ref[t + 1:t + 2, :]
        pi = pwi_ref[t + 1:t + 2, :]
        e_ref[0, t * LANES:(t + 1) * LANES, :S5_SW] = (cxr * pr - cxi * pi).astype(BF16)
        e_ref[0, t * LANES:(t + 1) * LANES, S5_SW:] = (-(cxr * pi + cxi * pr)).astype(BF16)

    cn = jnp.concatenate([cxr, -cxi], axis=1)
    w_ref[...] = jnp.zeros(w_ref.shape, w_ref.dtype)
    for lag in range(S5_T):
        c0 = (S5_T - 1 - lag) * LANES
        fblk = jnp.concatenate([ftr[:, c0:c0 + LANES], fti[:, c0:c0 + LANES]], axis=0)
        kt = dot_hi(cn, fblk).astype(BF16)
        for s in range(S5_T - lag):
            t = s + lag
            w_ref[0, t * LANES:(t + 1) * LANES, s * LANES:(s + 1) * LANES] = kt


def _s5_prep(bb_re, bb_im, pwt_re, pwt_im, pw_re, pw_im, cn_re, cn_im):
    n = S5_T * LANES
    rows = lambda r, c: pl.BlockSpec((r, c), lambda j: (j, 0))
    cols = lambda r, c: pl.BlockSpec((r, c), lambda j: (0, j))
    out = pl.BlockSpec((1, n, n), lambda j: (j, 0, 0))
    return pl.pallas_call(
        _s5_prep_kernel,
        grid=(S5_NBLK,),
        in_specs=[rows(S5_SW, S5_GROUP)] * 4 + [cols(S5_GROUP, S5_SW)] * 2 + [rows(LANES, S5_STATE)] * 2,
        out_specs=[out, out, out],
        out_shape=[jax.ShapeDtypeStruct((S5_NBLK, n, n), BF16)] * 3,
        compiler_params=_cparams(("parallel",)),
        name="s5_prep",
    )(bb_re, bb_im, pwt_re, pwt_im, pw_re, pw_im, cn_re, cn_im)


def _s5_group(x_ref, y_ref, w, f, e, ar, ai, dt, h0, yi_ref, s_ref, hin_ref, *, nseq, nblk):
    rows = nseq * nblk
    rt = min(rows, 256)

    def load_u(r0):
        return jnp.concatenate(
            [x_ref[pl.ds(r0 * S5_T + s, rt, stride=S5_T), :] for s in range(S5_T)], axis=1)

    for r0 in range(0, rows, rt):
        ub = load_u(r0).astype(BF16)
        if y_ref is not None:
            yi_ref[r0:r0 + rt, :] = _dot_nt(ub, w)
        s_ref[r0:r0 + rt, :] = _dot_nt(ub, f)

    if nblk == 1:
        hre, him = h0
        hin_ref[0:rows, :S5_SW] = hre
        hin_ref[0:rows, S5_SW:] = him
        fin = (ar * hre - ai * him + s_ref[0:rows, :S5_SW], ar * him + ai * hre + s_ref[0:rows, S5_SW:])
    else:
        def body(c, carry):
            new = []
            for b in range(nseq):
                hre, him = carry[2 * b], carry[2 * b + 1]
                row = b * nblk + c
                hin_ref[pl.ds(row, 1), :S5_SW] = hre
                hin_ref[pl.ds(row, 1), S5_SW:] = him
                sre = s_ref[pl.ds(row, 1), :S5_SW]
                sim = s_ref[pl.ds(row, 1), S5_SW:]
                new.append(ar * hre - ai * him + sre)
                new.append(ar * him + ai * hre + sim)
            return tuple(new)

        flat = lax.fori_loop(0, nblk, body, tuple(v for pair in h0 for v in pair))
        fin = [(flat[2 * b], flat[2 * b + 1]) for b in range(nseq)]

    if y_ref is not None:
        for r0 in range(0, rows, rt):
            yo = _dot_nt(hin_ref[r0:r0 + rt, :].astype(BF16), e)
            y = jax.nn.gelu(yi_ref[r0:r0 + rt, :] + yo + load_u(r0) * dt)
            for t in range(S5_T):
                y_ref[pl.ds(r0 * S5_T + t, rt, stride=S5_T), :] = y[:, t * LANES:(t + 1) * LANES]
    return fin


def _s5_kernel(xm_ref, xp_ref, xs_ref, w_ref, f_ref, e_ref, lam_ref, d_ref, h0re_ref, h0im_ref,
               yp_ref, ys_ref, pre_ref, pim_ref, sre_ref, sim_ref, yi_ref, s_ref, hin_ref,
               *, nbatch, nblk_m, nblk_p, nsamp, nblk_s):
    ar = lam_ref[0, 0:1, :]
    ai = lam_ref[0, 1:2, :]
    dt = jnp.concatenate([d_ref[...]] * S5_T, axis=1)
    ops = (w_ref[0], f_ref[0], e_ref[0], ar, ai, dt)
    scratch = (yi_ref, s_ref, hin_ref)
    zero = jnp.zeros((1, S5_SW), F32)
    st = _s5_group(xm_ref, None, *ops, [(zero, zero)] * nbatch, *scratch, nseq=nbatch, nblk=nblk_m)
    st = _s5_group(xp_ref, yp_ref, *ops, st, *scratch, nseq=nbatch, nblk=nblk_p)
    for b in range(nbatch):
        pre_ref[b:b + 1, :] = st[b][0]
        pim_ref[b:b + 1, :] = st[b][1]
    fin = _s5_group(xs_ref, ys_ref, *ops, (h0re_ref[...], h0im_ref[...]), *scratch, nseq=nsamp, nblk=nblk_s)
    sre_ref[...] = fin[0]
    sim_ref[...] = fin[1]


def _s5(z_m, z_p, z_s, w, f, e, lam, d, h0re, h0im, *, nbatch, len_m, len_p, nsamp, len_s):
    assert len_s == S5_T
    n = S5_T * LANES
    tok_m, tok_p, tok_s = nbatch * len_m, nbatch * len_p, nsamp * len_s
    rows = max(nbatch * len_p, nsamp * len_s, nbatch * len_m) // S5_T
    col = lambda r: pl.BlockSpec((r, LANES), lambda j: (0, j))
    op = pl.BlockSpec((1, n, n), lambda j: (j, 0, 0))
    st = lambda r: pl.BlockSpec((r, S5_SW), lambda j: (0, j))
    gp = S5_GROUPS * S5_STATE
    kern = functools.partial(_s5_kernel, nbatch=nbatch, nblk_m=len_m // S5_T, nblk_p=len_p // S5_T,
                             nsamp=nsamp, nblk_s=len_s // S5_T)
    return pl.pallas_call(
        kern,
        grid=(S5_NBLK,),
        in_specs=[col(tok_m), col(tok_p), col(tok_s), op, op, op,
                  pl.BlockSpec((1, 2, S5_SW), lambda j: (j, 0, 0)), col(1), st(nsamp), st(nsamp)],
        out_specs=[col(tok_p), col(tok_s), st(nbatch), st(nbatch), st(nsamp), st(nsamp)],
        out_shape=[
            jax.ShapeDtypeStruct((tok_p, S5_WIDTH), F32),
            jax.ShapeDtypeStruct((tok_s, S5_WIDTH), F32),
            jax.ShapeDtypeStruct((nbatch, gp), F32),
            jax.ShapeDtypeStruct((nbatch, gp), F32),
            jax.ShapeDtypeStruct((nsamp, gp), F32),
            jax.ShapeDtypeStruct((nsamp, gp), F32),
        ],
        scratch_shapes=[pltpu.VMEM((rows, n), F32)] * 3,
        compiler_params=_cparams(("parallel",)),
        name="s5",
    )(z_m, z_p, z_s, w, f, e, lam, d, h0re, h0im)


def _s5_operators(a_re, a_im, log_dt, b_re, b_im):
    dt = jnp.exp(log_dt)[:, None]
    ar, ai = a_re * dt, a_im * dt
    taus = jnp.arange(S5_T + 1, dtype=F32)[:, None, None]
    pmag = jnp.exp(taus * ar[None])
    pw_re = pmag * jnp.cos(taus * ai[None])
    pw_im = pmag * jnp.sin(taus * ai[None])
    nr, ni = pw_re[1] - 1.0, pw_im[1]
    den = a_re * a_re + a_im * a_im
    q_re = (nr * a_re + ni * a_im) / den
    q_im = (ni * a_re - nr * a_im) / den
    gp = S5_GROUPS * S5_STATE
    bb_re = (q_re[:, :, None] * b_re - q_im[:, :, None] * b_im).reshape(gp, S5_GROUP)
    bb_im = (q_re[:, :, None] * b_im + q_im[:, :, None] * b_re).reshape(gp, S5_GROUP)
    padrows = ((0, S5_GROUP - (S5_T + 1)), (0, 0))
    pw_re2 = jnp.pad(pw_re.reshape(S5_T + 1, gp), padrows)
    pw_im2 = jnp.pad(pw_im.reshape(S5_T + 1, gp), padrows)
    lam = jnp.stack([pw_re[S5_T].reshape(S5_NBLK, S5_SW), pw_im[S5_T].reshape(S5_NBLK, S5_SW)], axis=1)
    return bb_re, bb_im, pw_re2.T, pw_im2.T, pw_re2, pw_im2, lam


def _mlstm_kernel(qk_ref, v_ref, o_ref, if_ref, cw_ref, cb_ref, bif_ref, ng_ref,
                  buf0_ref, c0_ref, n0_ref, m0_ref,
                  y_ref, bufo_ref, co_ref, no_ref, mo_ref, xp_ref, *, nb, lc, carry):
    if carry:
        @pl.when(pl.program_id(1) == 0)
        def _():
            bufo_ref[...] = buf0_ref[...]
            co_ref[...] = c0_ref[...]
            no_ref[...] = n0_ref[...]
            mo_ref[...] = m0_ref[...]
        bufs_ref, cs_ref, ns_ref, ms_ref = bufo_ref, co_ref, no_ref, mo_ref
    else:
        bufs_ref, cs_ref, ns_ref, ms_ref = buf0_ref, c0_ref, n0_ref, m0_ref

    lp = max(lc, LANES)
    pad = CONV_W - 1
    heads = range(HEADS)
    head_row = lax.broadcasted_iota(jnp.int32, (HEADS, DQK), 0)
    head_lane = lax.broadcasted_iota(jnp.int32, (1, HEADS), 1)
    row = lax.broadcasted_iota(jnp.int32, (lc, lc), 0)
    col = lax.broadcasted_iota(jnp.int32, (lc, lc), 1)
    causal = row >= col
    tril = jnp.where(causal, 1.0, 0.0).astype(BF16)
    lane = lax.broadcasted_iota(jnp.int32, (lc, LANES), 1)
    is_i = lane < HEADS
    is_f = jnp.logical_and(lane >= HEADS, lane < 2 * HEADS)

    for s in range(nb):
        r0 = s * lc
        xp_ref[s, 0:8 - pad, :] = jnp.zeros((8 - pad, QK_WIDTH), F32)
        xp_ref[s, 8 - pad:8, :] = bufs_ref[s]
        xp_ref[s, 8:8 + lc, :] = qk_ref[r0:r0 + lc, :]
        xp = xp_ref[s]
        conv = cb_ref[...] + xp[8:8 + lc] * cw_ref[pad:pad + 1, :]
        for j in range(1, CONV_W):
            conv = conv + pltpu.roll(xp, j, 0)[8:8 + lc] * cw_ref[pad - j:pad - j + 1, :]
        bufo_ref[s] = xp_ref[s, 8 + lc - pad:8 + lc, :]
        n_all = ns_ref[s]
        m_all = ms_ref[s]
        qk = conv * jax.nn.sigmoid(conv)
        q = qk[:, :QK_WIDTH // 2] * (DQK ** -0.5)
        k = qk[:, QK_WIDTH // 2:]

        gate = if_ref[r0:r0 + lc, :] + bif_ref[...]
        lf = jnp.where(is_f, jax.nn.log_sigmoid(gate), 0.0)
        parts = _split3(lf)
        bcum = _dot(tril, parts[0]) + _dot(tril, parts[1]) + _dot(tril, parts[2])
        pc = jnp.where(is_i, gate, bcum)
        if lc < lp:
            pc_t = jnp.concatenate([pc, jnp.zeros((lp - lc, LANES), F32)], axis=0).T[:, :lc]
        else:
            pc_t = pc.T

        i_col = [pc[:, h:h + 1] for h in heads]
        b_col = [pc[:, HEADS + h:HEADS + h + 1] for h in heads]
        m0 = [m_all[:, h:h + 1] for h in heads]
        qh = [q[:, h * DQK:(h + 1) * DQK] for h in heads]
        kh = [k[:, h * DQK:(h + 1) * DQK] for h in heads]
        qb = [x.astype(BF16) for x in qh]
        vb = [v_ref[r0:r0 + lc, h * DV:(h + 1) * DV].astype(BF16) for h in heads]
        c_prev = [cs_ref[s, h] for h in heads]
        n_prev = [n_all[h:h + 1, :] for h in heads]

        qk_t = [_dot_nt(qb[h], kh[h].astype(BF16)) for h in heads]
        carried = [_dot_nt(qb[h], c_prev[h].astype(BF16)) for h in heads]
        logw = [jnp.where(causal, b_col[h] - pc_t[HEADS + h:HEADS + h + 1, :] + pc_t[h:h + 1, :], -jnp.inf)
                for h in heads]
        g = [b_col[h] + m0[h] for h in heads]
        m = [jnp.maximum(g[h], jnp.max(logw[h], axis=-1, keepdims=True)) for h in heads]
        inter = [jnp.exp(g[h] - m[h]) for h in heads]
        sc = [qk_t[h] * jnp.exp(logw[h] - m[h]) for h in heads]
        num = [_dot(sc[h].astype(BF16), vb[h]) + inter[h] * carried[h] for h in heads]
        den = [jnp.sum(sc[h], axis=-1, keepdims=True)
               + inter[h] * jnp.sum(qh[h] * n_prev[h], axis=-1, keepdims=True) for h in heads]
        hh = [num[h] / jnp.maximum(jnp.abs(den[h]), jnp.exp(-m[h])) for h in heads]
        hh = [x * lax.rsqrt(jnp.mean(x * x, axis=-1, keepdims=True) + EPS) for x in hh]
        for h in heads:
            og = jax.nn.sigmoid(o_ref[r0:r0 + lc, h * DV:(h + 1) * DV])
            y_ref[r0:r0 + lc, h * DV:(h + 1) * DV] = hh[h] * ng_ref[:, h * DV:(h + 1) * DV] * og

        b_last = [x[lc - 1:lc, :] for x in b_col]
        m_new = [x[lc - 1:lc, :] for x in m]
        decay = [jnp.exp(b_last[h] + m0[h] - m_new[h]) for h in heads]
        kw = [kh[h] * jnp.exp(b_last[h] - b_col[h] + i_col[h] - m_new[h]) for h in heads]
        upd = [_dot_tn(vb[h], kw[h].astype(BF16)) for h in heads]
        n_new_all = jnp.zeros((HEADS, DQK), F32)
        m_new_all = jnp.zeros((1, HEADS), F32)
        for h in heads:
            co_ref[s, h] = decay[h] * c_prev[h] + upd[h]
            n_new = decay[h] * n_prev[h] + jnp.sum(kw[h], axis=0, keepdims=True)
            n_new_all = jnp.where(head_row == h, n_new, n_new_all)
            m_new_all = jnp.where(head_lane == h, m_new[h], m_new_all)
        no_ref[s] = n_new_all
        mo_ref[s] = m_new_all


def _mlstm(z, zif, cw, cb, bif, ng, buf0, c0, n0, m0, *, nseq, seqlen, nb, lc):
    tok = nseq * seqlen
    nchunk = seqlen // lc
    rows = nb * lc
    tok_map = lambda col: (lambda i, c: (i * nchunk + c, col))
    const2 = lambda i, c: (0, 0)
    st3 = lambda i, c: (i, 0, 0)
    st4 = lambda i, c: (i, 0, 0, 0)
    kern = functools.partial(_mlstm_kernel, nb=nb, lc=lc, carry=nchunk > 1)
    return pl.pallas_call(
        kern,
        grid=(nseq // nb, nchunk),
        in_specs=[
            pl.BlockSpec((rows, QK_WIDTH), tok_map(1)),
            pl.BlockSpec((rows, V_WIDTH), tok_map(2)),
            pl.BlockSpec((rows, V_WIDTH), tok_map(3)),
            pl.BlockSpec((rows, LANES), tok_map(0)),
            pl.BlockSpec((CONV_W, QK_WIDTH), const2),
            pl.BlockSpec((1, QK_WIDTH), const2),
            pl.BlockSpec((1, LANES), const2),
            pl.BlockSpec((1, V_WIDTH), const2),
            pl.BlockSpec((nb, CONV_W - 1, QK_WIDTH), st3),
            pl.BlockSpec((nb, HEADS, DV, DQK), st4),
            pl.BlockSpec((nb, HEADS, DQK), st3),
            pl.BlockSpec((nb, 1, HEADS), st3),
        ],
        out_specs=[
            pl.BlockSpec((rows, V_WIDTH), tok_map(0)),
            pl.BlockSpec((nb, CONV_W - 1, QK_WIDTH), st3),
            pl.BlockSpec((nb, HEADS, DV, DQK), st4),
            pl.BlockSpec((nb, HEADS, DQK), st3),
            pl.BlockSpec((nb, 1, HEADS), st3),
        ],
        out_shape=[
            jax.ShapeDtypeStruct((tok, V_WIDTH), F32),
            jax.ShapeDtypeStruct((nseq, CONV_W - 1, QK_WIDTH), F32),
            jax.ShapeDtypeStruct((nseq, HEADS, DV, DQK), F32),
            jax.ShapeDtypeStruct((nseq, HEADS, DQK), F32),
            jax.ShapeDtypeStruct((nseq, 1, HEADS), F32),
        ],
        scratch_shapes=[pltpu.VMEM((nb, lc + 8, QK_WIDTH), F32)],
        compiler_params=_cparams(("parallel", "arbitrary")),
        name="mlstm",
    )(z, z, z, zif, cw, cb, bif, ng, buf0, c0, n0, m0)


def _mlstm_step_kernel(qk_ref, v_ref, o_ref, if_ref, cw_ref, cb_ref, bif_ref, ng_ref,
                       buf0_ref, c0_ref, n0x_ref, m0x_ref,
                       y_ref, bufo_ref, co_ref, nox_ref, mox_ref, xp_ref, conv_ref, *, nb, lc):
    rows = nb * lc
    lg = lc.bit_length() - 1
    i32 = jnp.int32
    row = lax.broadcasted_iota(i32, (rows, rows), 0)
    col = lax.broadcasted_iota(i32, (rows, rows), 1)
    same = (row >> lg) == (col >> lg)
    causal = jnp.logical_and(same, row >= col)
    same_b = jnp.where(same, 1.0, 0.0).astype(BF16)
    tril_b = jnp.where(causal, 1.0, 0.0).astype(BF16)
    lane = lax.broadcasted_iota(i32, (rows, LANES), 1)
    is_i = lane < HEADS
    is_f = jnp.logical_and(lane >= HEADS, lane < 2 * HEADS)

    def seg_dot(mat, x):
        hi, mid, lo = _split3(x)
        return _dot(mat, hi) + _dot(mat, mid) + _dot(mat, lo)

    for s in range(nb):
        xp_ref[s, 5:8, :] = buf0_ref[s]
        xp_ref[s, 8:8 + lc, :] = qk_ref[s * lc:(s + 1) * lc, :]
        conv_ref[s * lc:(s + 1) * lc, :] = cb_ref[...] + sum(
            xp_ref[s, 5 + j:5 + j + lc, :] * cw_ref[j:j + 1, :] for j in range(CONV_W))
        bufo_ref[s] = xp_ref[s, 5 + lc:8 + lc, :]
    conv = conv_ref[...]
    qk = conv * jax.nn.sigmoid(conv)
    q = qk[:, :QK_WIDTH // 2] * (DQK ** -0.5)
    k = qk[:, QK_WIDTH // 2:]

    gate = if_ref[...] + bif_ref[...]
    lf = jnp.where(is_f, jax.nn.log_sigmoid(gate), 0.0)
    bcum = seg_dot(tril_b, lf)
    btot = seg_dot(same_b, lf)
    pc = jnp.where(is_i, gate, bcum)
    if rows < LANES:
        pc_t = jnp.concatenate([pc, jnp.zeros((LANES - rows, LANES), F32)], axis=0).T[:, :rows]
    else:
        pc_t = pc.T
    m0x = m0x_ref[...]
    mox = jnp.zeros((rows, LANES), F32)

    for h in range(HEADS):
        i_col = pc[:, h:h + 1]
        b_col = pc[:, HEADS + h:HEADS + h + 1]
        i_row = pc_t[h:h + 1, :]
        b_row = pc_t[HEADS + h:HEADS + h + 1, :]
        b_last = btot[:, HEADS + h:HEADS + h + 1]
        m0 = m0x[:, h:h + 1]
        lw = b_col - b_row + i_row
        logw = jnp.where(causal, lw, -jnp.inf)
        g = b_col + m0
        m = jnp.maximum(g, jnp.max(logw, axis=-1, keepdims=True))
        w = jnp.exp(logw - m)
        inter = jnp.exp(g - m)
        lw_end = jnp.where(same, b_last - b_row + i_row, -jnp.inf)
        m_new = jnp.maximum(b_last + m0, jnp.max(lw_end, axis=-1, keepdims=True))
        decay = jnp.exp(b_last + m0 - m_new)
        qh = q[:, h * DQK:(h + 1) * DQK]
        kh = k[:, h * DQK:(h + 1) * DQK]
        qb = qh.astype(BF16)
        vb = v_ref[:, h * DV:(h + 1) * DV].astype(BF16)
        n_prev = n0x_ref[:, h * DQK:(h + 1) * DQK]
        sc = _dot_nt(qb, kh.astype(BF16)) * w
        carried = jnp.concatenate(
            [_dot_nt(qb[s * lc:(s + 1) * lc], c0_ref[s, h].astype(BF16)) for s in range(nb)], axis=0)
        num = _dot(sc.astype(BF16), vb) + inter * carried
        den = jnp.sum(sc, axis=-1, keepdims=True) + inter * jnp.sum(qh * n_prev, axis=-1, keepdims=True)
        hh = num / jnp.maximum(jnp.abs(den), jnp.exp(-m))
        hh = hh * lax.rsqrt(jnp.mean(hh * hh, axis=-1, keepdims=True) + EPS)
        og = jax.nn.sigmoid(o_ref[:, h * DV:(h + 1) * DV])
        y_ref[:, h * DV:(h + 1) * DV] = hh * ng_ref[:, h * DV:(h + 1) * DV] * og

        kw = kh * jnp.exp(b_last - b_col + i_col - m_new)
        kwb = kw.astype(BF16)
        for s in range(nb):
            r0 = s * lc
            co_ref[s, h] = decay[r0:r0 + 1, :] * c0_ref[s, h] + _dot_tn(vb[r0:r0 + lc], kwb[r0:r0 + lc])
        nox_ref[:, h * DQK:(h + 1) * DQK] = decay * n_prev + seg_dot(same_b, kw)
        mox = jnp.where(lane == h, m_new, mox)
    mox_ref[...] = mox


def _mlstm_step(z, zif, cw, cb, bif, ng, buf0, c0, n0, m0, *, nseq, seqlen, nb):
    lc = seqlen
    assert lc & (lc - 1) == 0 and lc >= CONV_W - 1
    tok = nseq * seqlen
    rows = nb * lc
    n0x = jnp.repeat(n0.reshape(nseq, HEADS * DQK), lc, axis=0)
    m0x = jnp.repeat(jnp.pad(m0, ((0, 0), (0, LANES - HEADS))), lc, axis=0)
    tok_map = lambda col: (lambda i: (i, col))
    const2 = lambda i: (0, 0)
    st3 = lambda i: (i, 0, 0)
    st4 = lambda i: (i, 0, 0, 0)
    y, buf, c, nox, mox = pl.pallas_call(
        functools.partial(_mlstm_step_kernel, nb=nb, lc=lc),
        grid=(nseq // nb,),
        in_specs=[
            pl.BlockSpec((rows, QK_WIDTH), tok_map(1)),
            pl.BlockSpec((rows, V_WIDTH), tok_map(2)),
            pl.BlockSpec((rows, V_WIDTH), tok_map(3)),
            pl.BlockSpec((rows, LANES), tok_map(0)),
            pl.BlockSpec((CONV_W, QK_WIDTH), const2),
            pl.BlockSpec((1, QK_WIDTH), const2),
            pl.BlockSpec((1, LANES), const2),
            pl.BlockSpec((1, V_WIDTH), const2),
            pl.BlockSpec((nb, CONV_W - 1, QK_WIDTH), st3),
            pl.BlockSpec((nb, HEADS, DV, DQK), st4),
            pl.BlockSpec((rows, HEADS * DQK), tok_map(0)),
            pl.BlockSpec((rows, LANES), tok_map(0)),
        ],
        out_specs=[
            pl.BlockSpec((rows, V_WIDTH), tok_map(0)),
            pl.BlockSpec((nb, CONV_W - 1, QK_WIDTH), st3),
            pl.BlockSpec((nb, HEADS, DV, DQK), st4),
            pl.BlockSpec((rows, HEADS * DQK), tok_map(0)),
            pl.BlockSpec((rows, LANES), tok_map(0)),
        ],
        out_shape=[
            jax.ShapeDtypeStruct((tok, V_WIDTH), F32),
            jax.ShapeDtypeStruct((nseq, CONV_W - 1, QK_WIDTH), F32),
            jax.ShapeDtypeStruct((nseq, HEADS, DV, DQK), F32),
            jax.ShapeDtypeStruct((tok, HEADS * DQK), F32),
            jax.ShapeDtypeStruct((tok, LANES), F32),
        ],
        scratch_shapes=[pltpu.VMEM((nb, lc + 8, QK_WIDTH), F32), pltpu.VMEM((rows, QK_WIDTH), F32)],
        compiler_params=_cparams(("parallel",)),
        name="mlstm_step",
    )(z, z, z, zif, cw, cb, bif, ng, buf0, c0, n0x, m0x)
    n = nox[::lc].reshape(nseq, HEADS, DQK)
    m = mox[::lc, :HEADS]
    return y, buf, c, n, m


def _merge_kernel(h_ref, ys_ref, ym_ref, g1_ref, g2_ref, wglu_ref, wbs_ref, wbm_ref, wo_ref, o_ref):
    ys = ys_ref[...]
    glu = ys * jax.nn.sigmoid(_dot(ys.astype(BF16), wglu_ref[...]))
    a = _dot(glu.astype(BF16), wbs_ref[...])
    b = _dot(ym_ref[...].astype(BF16), wbm_ref[...])
    merged = jax.nn.sigmoid(g1_ref[...]) * a + jax.nn.sigmoid(g2_ref[...]) * b
    o_ref[...] = h_ref[...] + _dot(merged.astype(BF16), wo_ref[...])


def _merge(h, ys, ym, z, wglu, wbs, wbm, wo, *, n, tm):
    row = lambda i: (i, 0)
    const = lambda i: (0, 0)
    resident = lambda shape: pl.BlockSpec(shape, const, pipeline_mode=pl.Buffered(1))
    return pl.pallas_call(
        _merge_kernel,
        grid=(n // tm,),
        in_specs=[
            pl.BlockSpec((tm, D_MODEL), row),
            pl.BlockSpec((tm, S5_WIDTH), row),
            pl.BlockSpec((tm, V_WIDTH), row),
            pl.BlockSpec((tm, D_MODEL), lambda i: (i, 2)),
            pl.BlockSpec((tm, D_MODEL), lambda i: (i, 3)),
            resident((S5_WIDTH, S5_WIDTH)),
            resident((S5_WIDTH, D_MODEL)),
            resident((V_WIDTH, D_MODEL)),
            resident((D_MODEL, D_MODEL)),
        ],
        out_specs=pl.BlockSpec((tm, D_MODEL), row),
        out_shape=jax.ShapeDtypeStruct((n, D_MODEL), F32),
        compiler_params=_cparams(("parallel",)),
        name="merge",
    )(h, ys, ym, z, z, wglu, wbs, wbm, wo)


def kernel(x_prompt, x_sample, state_s5_re, state_s5_im, state_mlstm_C, state_mlstm_n, state_mlstm_m,
           state_mlstm_conv, meta_tokens, ffn1_norm, ffn1_w_gate, ffn1_w_up, ffn1_w_down, mix_norm, w_in,
           s5_A_re, s5_A_im, s5_log_dt, s5_B_re, s5_B_im, s5_C_re, s5_C_im, s5_D, s5_w_glu,
           mlstm_conv_w, mlstm_conv_b, mlstm_b_i, mlstm_b_f, mlstm_norm, w_branch_s5, w_branch_mlstm,
           w_out, ffn2_norm, ffn2_w_gate, ffn2_w_up, ffn2_w_down, final_norm):
    nbatch, seq, _ = x_prompt.shape
    nsamp, sseq, _ = x_sample.shape
    l = 0

    w1g, w1u, w1d = ffn1_w_gate[l], ffn1_w_up[l], ffn1_w_down[l]
    w2g, w2u, w2d = ffn2_w_gate[l], ffn2_w_up[l], ffn2_w_down[l]
    o_if = S5_WIDTH + QK_WIDTH + 2 * V_WIDTH
    o_gate = o_if + 2 * HEADS
    wt = w_in[l].T
    w2 = _win_halves(wt, o_gate, tr=WIN_CAST_ROWS)
    wglu, wbs, wbm, wo = (w[l].astype(BF16) for w in (s5_w_glu, w_branch_s5, w_branch_mlstm, w_out))
    g1 = ffn1_norm[l][None]
    gm = mix_norm[l][None]
    g2 = ffn2_norm[l][None]
    gf = final_norm[None]
    bif = jnp.pad(jnp.concatenate([mlstm_b_i[l], mlstm_b_f[l]]), (0, LANES - 2 * HEADS))[None]
    cw = mlstm_conv_w[l]
    cb = mlstm_conv_b[l][None]
    ng = mlstm_norm[l][None]
    d_skip = s5_D[l][None]

    *s5_ops, lam = _s5_operators(s5_A_re[l], s5_A_im[l], s5_log_dt[l], s5_B_re[l], s5_B_im[l])
    gh = S5_GROUPS * S5_GROUP
    w_toe, f_bf, e_bf = _s5_prep(*s5_ops, s5_C_re[l].reshape(gh, S5_STATE), s5_C_im[l].reshape(gh, S5_STATE))

    def front(x, tm):
        n = x.shape[0]
        h1 = _ffn(x, g1, w1g, w1u, w1d, n=n, tm=tm)
        z, zif = _win(h1, gm, w2, wt, o_if=o_if, ngate=N_BRANCH * D_MODEL, tm=tm)
        return h1, z, zif

    def mlstm(z, zif, ml_state, *, nseq, seqlen, nb, lc):
        buf0, c0, n0, m0 = ml_state
        if lc == seqlen and nb > 1:
            return _mlstm_step(z, zif, cw, cb, bif, ng, buf0, c0, n0, m0, nseq=nseq, seqlen=seqlen, nb=nb)
        ym, buf, c, n, m = _mlstm(z, zif, cw, cb, bif, ng, buf0, c0, n0, m0.reshape(nseq, 1, HEADS),
                                  nseq=nseq, seqlen=seqlen, nb=nb, lc=lc)
        return ym, buf, c, n, m.reshape(nseq, HEADS)

    def back(h1, ys, ym, z, n):
        h2 = _merge(h1, ys, ym, z, wglu, wbs, wbm, wo, n=n, tm=MERGE_TM)
        return _ffn(h2, g2, w2g, w2u, w2d, gf, n=n, tm=FFN_TM)

    ntok_s = nsamp * sseq
    ntok_p = nbatch * seq
    x_sm = jnp.concatenate([x_sample.reshape(ntok_s, D_MODEL), meta_tokens], axis=0)
    h1_sm, z_sm, zif_sm = front(x_sm, ntok_s + N_META)
    h1_p, z_p, zif_p = front(x_prompt.reshape(ntok_p, D_MODEL), FFN_TM)

    z_m = jnp.tile(z_sm[ntok_s:], (nbatch, 1))
    zif_m = jnp.tile(zif_sm[ntok_s:], (nbatch, 1))
    zeros = lambda *s: jnp.zeros((nbatch,) + s, F32)

    ys_p, ys_s, *s5_st = _s5(z_m, z_p, z_sm, w_toe, f_bf, e_bf, lam, d_skip,
                             state_s5_re[l].reshape(nsamp, -1), state_s5_im[l].reshape(nsamp, -1),
                             nbatch=nbatch, len_m=N_META, len_p=seq, nsamp=nsamp, len_s=sseq)
    s5_p, s5_s = s5_st[:2], s5_st[2:]

    _, *ml_m = mlstm(z_m, zif_m, (zeros(CONV_W - 1, QK_WIDTH), zeros(HEADS, DV, DQK), zeros(HEADS, DQK), zeros(HEADS)),
                     nseq=nbatch, seqlen=N_META, nb=1, lc=N_META)
    ym_p, *ml_p = mlstm(z_p, zif_p, ml_m, nseq=nbatch, seqlen=seq, nb=1, lc=256)
    ym_s, *ml_s = mlstm(z_sm, zif_sm,
                        (state_mlstm_conv[l], state_mlstm_C[l], state_mlstm_n[l], state_mlstm_m[l]),
                        nseq=nsamp, seqlen=sseq, nb=STEP_NB, lc=sseq)
    y_p = back(h1_p, ys_p, ym_p, z_p, ntok_p)
    y_s = back(h1_sm, ys_s, ym_s, z_sm, ntok_s)

    def pack(n, s5_st, ml_st):
        buf, c, nn, m = ml_st
        return (s5_st[0].reshape(1, n, S5_GROUPS, S5_STATE), s5_st[1].reshape(1, n, S5_GROUPS, S5_STATE),
                c[None], nn[None], m[None], buf[None])

    return ((y_p.reshape(nbatch, seq, D_MODEL), y_s.reshape(nsamp, sseq, D_MODEL))
            + pack(nbatch, s5_p, ml_p) + pack(nsamp, s5_s, ml_s))
```

```python
import functools

import jax
import jax.numpy as jnp
from jax import lax
from jax.experimental import pallas as pl
from jax.experimental.pallas import tpu as pltpu

F32 = jnp.float32
BF16 = jnp.bfloat16

D_MODEL = 2048
D_FF = 5632
N_META = 16
S5_WIDTH = 1024
S5_GROUP = 16
S5_GROUPS = 64
S5_STATE = 64
HEADS = 4
DQK = 128
DV = 256
QK_WIDTH = 1024
V_WIDTH = 1024
CONV_W = 4
N_BRANCH = 2
EPS = 1e-6

LANES = 128
SUBLANES = 8
S5_T = 8
S5_GPB = LANES // S5_GROUP
S5_NBLK = S5_WIDTH // LANES
S5_SW = S5_GPB * S5_STATE
VMEM_LIMIT = 58 * 1024 * 1024
FFN_TM = 1024
MERGE_TM = 256
WIN_CAST_ROWS = 456
S5_ROW_TILE = 256
MLSTM_CHUNK = 256
STEP_NB = 16


def _cparams(sem):
    return pltpu.CompilerParams(dimension_semantics=sem, vmem_limit_bytes=VMEM_LIMIT)


def _rmsnorm(x, g):
    ms = jnp.mean(x * x, axis=-1, keepdims=True)
    return (x * lax.rsqrt(ms + EPS)) * g


def _dot(a, b):
    return jnp.dot(a, b, preferred_element_type=F32)


def _dot_nt(a, b):
    return lax.dot_general(a, b, (((1,), (1,)), ((), ())), preferred_element_type=F32)


def _dot_tn(a, b):
    return lax.dot_general(a, b, (((0,), (0,)), ((), ())), preferred_element_type=F32)


def _split3(x):
    hi = x.astype(BF16)
    r = x - hi.astype(F32)
    mid = r.astype(BF16)
    lo = (r - mid.astype(F32)).astype(BF16)
    return hi, mid, lo


def _ffn_kernel(x_ref, g_ref, wg_ref, wu_ref, wd_ref, *rest, final_norm):
    if final_norm:
        fg_ref, o_ref, xn_ref = rest
    else:
        o_ref, xn_ref = rest
    j = pl.program_id(1)

    @pl.when(j == 0)
    def _():
        xn_ref[...] = _rmsnorm(x_ref[...], g_ref[...]).astype(BF16)
        o_ref[...] = jnp.zeros(o_ref.shape, F32)

    xn = xn_ref[...]
    gt = _dot(xn, wg_ref[...].astype(BF16))
    up = _dot(xn, wu_ref[...].astype(BF16))
    act = (gt * jax.nn.sigmoid(gt) * up).astype(BF16)
    o_ref[...] += _dot(act, wd_ref[...].astype(BF16))

    @pl.when(j == pl.num_programs(1) - 1)
    def _():
        h = x_ref[...] + 0.5 * o_ref[...]
        if final_norm:
            h = _rmsnorm(h, fg_ref[...])
        o_ref[...] = h


def _ffn(x, g, wg, wu, wd, final_g=None, *, n, tm, tf=256):
    in_specs = [
        pl.BlockSpec((tm, D_MODEL), lambda i, j: (i, 0)),
        pl.BlockSpec((1, D_MODEL), lambda i, j: (0, 0)),
        pl.BlockSpec((D_MODEL, tf), lambda i, j: (0, j)),
        pl.BlockSpec((D_MODEL, tf), lambda i, j: (0, j)),
        pl.BlockSpec((tf, D_MODEL), lambda i, j: (j, 0)),
    ]
    args = [x, g, wg, wu, wd]
    if final_g is not None:
        in_specs.append(pl.BlockSpec((1, D_MODEL), lambda i, j: (0, 0)))
        args.append(final_g)
    return pl.pallas_call(
        functools.partial(_ffn_kernel, final_norm=final_g is not None),
        grid=(n // tm, D_FF // tf),
        in_specs=in_specs,
        out_specs=pl.BlockSpec((tm, D_MODEL), lambda i, j: (i, 0)),
        out_shape=jax.ShapeDtypeStruct((n, D_MODEL), F32),
        scratch_shapes=[pltpu.VMEM((tm, D_MODEL), BF16)],
        compiler_params=_cparams(("parallel", "arbitrary")),
        name="ffn",
    )(*args)


def _cast_rows_kernel(w_ref, o_ref, *, tail):
    last = pl.num_programs(0) - 1

    @pl.when(pl.program_id(0) < last)
    def _():
        o_ref[0] = w_ref[...].astype(BF16)

    @pl.when(pl.program_id(0) == last)
    def _():
        o_ref[0, :tail] = w_ref[:tail].astype(BF16)
        o_ref[0, tail:] = jnp.zeros((o_ref.shape[1] - tail, o_ref.shape[2]), BF16)


def _win_halves(wt, o_gate, *, tr):
    rows, k = wt.shape
    per = o_gate // tr
    tail = rows - (2 * per - 1) * tr
    assert per * tr == o_gate and tr % 8 == 0 and 0 < tail < tr and tail % 8 == 0
    return pl.pallas_call(
        functools.partial(_cast_rows_kernel, tail=tail),
        grid=(2 * per,),
        in_specs=[pl.BlockSpec((tr, k), lambda i: (i, 0))],
        out_specs=pl.BlockSpec((1, tr, k), lambda i: (i // per, i % per, 0)),
        out_shape=jax.ShapeDtypeStruct((2, o_gate, k), BF16),
        compiler_params=_cparams(("parallel",)),
        name="w_in_cast",
    )(wt)


def _win_kernel(h_ref, g_ref, wa_ref, wb_ref, wif_ref, z_ref, zif_ref, un_ref, *, na):
    j = pl.program_id(1)

    @pl.when(j == 0)
    def _():
        un = _rmsnorm(h_ref[...], g_ref[...]).astype(BF16)
        un_ref[...] = un
        zif_ref[...] = _dot_nt(un, wif_ref[...].astype(BF16))

    @pl.when(j < na)
    def _():
        z_ref[...] = _dot_nt(un_ref[...], wa_ref[0])

    @pl.when(j >= na)
    def _():
        z_ref[...] = _dot_nt(un_ref[...], wb_ref[0])


def _win(h, g, w2, wt, *, o_if, ngate, tm, tn=1024):
    n = h.shape[0]
    na = o_if // tn
    nb = ngate // tn
    return pl.pallas_call(
        functools.partial(_win_kernel, na=na),
        grid=(n // tm, na + nb),
        in_specs=[
            pl.BlockSpec((tm, D_MODEL), lambda i, j: (i, 0)),
            pl.BlockSpec((1, D_MODEL), lambda i, j: (0, 0)),
            pl.BlockSpec((1, tn, D_MODEL), lambda i, j: (0, jnp.minimum(j, na - 1), 0)),
            pl.BlockSpec((1, tn, D_MODEL), lambda i, j: (1, jnp.maximum(j - na, 0), 0)),
            pl.BlockSpec((LANES, D_MODEL), lambda i, j: (o_if // LANES, 0)),
        ],
        out_specs=[
            pl.BlockSpec((tm, tn), lambda i, j: (i, j)),
            pl.BlockSpec((tm, LANES), lambda i, j: (i, 0)),
        ],
        out_shape=[
            jax.ShapeDtypeStruct((n, (na + nb) * tn), F32),
            jax.ShapeDtypeStruct((n, LANES), F32),
        ],
        scratch_shapes=[pltpu.VMEM((tm, D_MODEL), BF16)],
        compiler_params=_cparams(("parallel", "arbitrary")),
        name="w_in",
    )(h, g, w2, w2, wt)


def _s5_prep_kernel(bbr_ref, bbi_ref, ptr_ref, pti_ref, pwr_ref, pwi_ref, cnr_ref, cni_ref,
                    w_ref, f_ref, e_ref):
    n = S5_T * LANES
    i32 = jnp.int32
    lg_h = S5_GROUP.bit_length() - 1
    lg_p = S5_STATE.bit_length() - 1
    lg_l = LANES.bit_length() - 1
    gmask = S5_GPB - 1

    def tile_mat(k, c, src_of_col):
        kk = lax.broadcasted_iota(i32, (k, c), 0)
        cc = lax.broadcasted_iota(i32, (k, c), 1)
        return jnp.where(kk == src_of_col(cc), 1.0, 0.0).astype(BF16)

    def group_mask(r, c, row_group, col_group):
        rr = lax.broadcasted_iota(i32, (r, c), 0)
        cc = lax.broadcasted_iota(i32, (r, c), 1)
        return row_group(rr) == col_group(cc)

    def tiled(x, mat):
        hi, mid, _ = _split3(x)
        return _dot(hi, mat) + _dot(mid, mat)

    def lag_power(pt):
        return jnp.concatenate(
            [jnp.broadcast_to(pt[:, S5_T - 1 - s:S5_T - s], (S5_SW, LANES)) for s in range(S5_T)], axis=1)

    def dot_hi(a, b):
        a_hi, a_mid, _ = _split3(a)
        b_hi, b_mid, _ = _split3(b)
        return _dot(a_hi, b_hi) + _dot(a_mid, b_hi) + _dot(a_hi, b_mid)

    sel_h = tile_mat(S5_GROUP, n, lambda c: c & (S5_GROUP - 1))
    mf = group_mask(S5_SW, n, lambda r: r >> lg_p, lambda c: (c >> lg_h) & gmask)
    lpr = lag_power(ptr_ref[...])
    lpi = lag_power(pti_ref[...])
    bxr = tiled(bbr_ref[...], sel_h)
    bxi = tiled(bbi_ref[...], sel_h)
    ftr = jnp.where(mf, lpr * bxr - lpi * bxi, 0.0)
    fti = jnp.where(mf, lpr * bxi + lpi * bxr, 0.0)
    f_ref[0, :S5_SW, :] = ftr.astype(BF16)
    f_ref[0, S5_SW:, :] = fti.astype(BF16)

    sel_p = tile_mat(S5_STATE, S5_SW, lambda c: c & (S5_STATE - 1))
    mc = group_mask(LANES, S5_SW, lambda r: r >> lg_h, lambda c: c >> lg_p)
    cxr = jnp.where(mc, tiled(cnr_ref[...], sel_p), 0.0)
    cxi = jnp.where(mc, tiled(cni_ref[...], sel_p), 0.0)

    for t in range(S5_T):
        pr = pwr_ref[t + 1:t + 2, :]
        pi = pwi_ref[t + 1:t + 2, :]
        e_ref[0, t * LANES:(t + 1) * LANES, :S5_SW] = (cxr * pr - cxi * pi).astype(BF16)
        e_ref[0, t * LANES:(t + 1) * LANES, S5_SW:] = (-(cxr * pi + cxi * pr)).astype(BF16)

    cn = jnp.concatenate([cxr, -cxi], axis=1)
    w_ref[...] = jnp.zeros(w_ref.shape, w_ref.dtype)
    for lag in range(S5_T):
        c0 = (S5_T - 1 - lag) * LANES
        fblk = jnp.concatenate([ftr[:, c0:c0 + LANES], fti[:, c0:c0 + LANES]], axis=0)
        kt = dot_hi(cn, fblk).astype(BF16)
        for s in range(S5_T - lag):
            t = s + lag
            w_ref[0, t * LANES:(t + 1) * LANES, s * LANES:(s + 1) * LANES] = kt


def _s5_prep(bb_re, bb_im, pwt_re, pwt_im, pw_re, pw_im, cn_re, cn_im):
    n = S5_T * LANES
    rows = lambda r, c: pl.BlockSpec((r, c), lambda j: (j, 0))
    cols = lambda r, c: pl.BlockSpec((r, c), lambda j: (0, j))
    out = pl.BlockSpec((1, n, n), lambda j: (j, 0, 0))
    return pl.pallas_call(
        _s5_prep_kernel,
        grid=(S5_NBLK,),
        in_specs=[rows(S5_SW, S5_GROUP)] * 4 + [cols(S5_GROUP, S5_SW)] * 2 + [rows(LANES, S5_STATE)] * 2,
        out_specs=[out, out, out],
        out_shape=[jax.ShapeDtypeStruct((S5_NBLK, n, n), BF16)] * 3,
        compiler_params=_cparams(("parallel",)),
        name="s5_prep",
    )(bb_re, bb_im, pwt_re, pwt_im, pw_re, pw_im, cn_re, cn_im)


def _s5_group(x_ref, y_ref, w, f, e, ar, ai, dt, h0, yi_ref, s_ref, hin_ref, *, nseq, nblk):
    rows = nseq * nblk
    rt = min(rows, S5_ROW_TILE)

    def load_u(r0):
        return jnp.concatenate(
            [x_ref[pl.ds(r0 * S5_T + s, rt, stride=S5_T), :] for s in range(S5_T)], axis=1)

    for r0 in range(0, rows, rt):
        ub = load_u(r0).astype(BF16)
        if y_ref is not None:
            yi_ref[r0:r0 + rt, :] = _dot_nt(ub, w)
        s_ref[r0:r0 + rt, :] = _dot_nt(ub, f)

    if nblk == 1:
        hre, him = h0
        hin_ref[0:rows, :S5_SW] = hre
        hin_ref[0:rows, S5_SW:] = him
        fin = (ar * hre - ai * him + s_ref[0:rows, :S5_SW], ar * him + ai * hre + s_ref[0:rows, S5_SW:])
    else:
        def body(c, carry):
            new = []
            for b in range(nseq):
                hre, him = carry[2 * b], carry[2 * b + 1]
                row = b * nblk + c
                hin_ref[pl.ds(row, 1), :S5_SW] = hre
                hin_ref[pl.ds(row, 1), S5_SW:] = him
                sre = s_ref[pl.ds(row, 1), :S5_SW]
                sim = s_ref[pl.ds(row, 1), S5_SW:]
                new.append(ar * hre - ai * him + sre)
                new.append(ar * him + ai * hre + sim)
            return tuple(new)

        flat = lax.fori_loop(0, nblk, body, tuple(v for pair in h0 for v in pair))
        fin = [(flat[2 * b], flat[2 * b + 1]) for b in range(nseq)]

    if y_ref is not None:
        for r0 in range(0, rows, rt):
            yo = _dot_nt(hin_ref[r0:r0 + rt, :].astype(BF16), e)
            y = jax.nn.gelu(yi_ref[r0:r0 + rt, :] + yo + load_u(r0) * dt)
            for t in range(S5_T):
                y_ref[pl.ds(r0 * S5_T + t, rt, stride=S5_T), :] = y[:, t * LANES:(t + 1) * LANES]
    return fin


def _s5_kernel(xm_ref, xp_ref, xs_ref, w_ref, f_ref, e_ref, lam_ref, d_ref, h0re_ref, h0im_ref,
               yp_ref, ys_ref, pre_ref, pim_ref, sre_ref, sim_ref, yi_ref, s_ref, hin_ref,
               *, nbatch, nblk_m, nblk_p, nsamp, nblk_s):
    ar = lam_ref[0, 0:1, :]
    ai = lam_ref[0, 1:2, :]
    dt = jnp.concatenate([d_ref[...]] * S5_T, axis=1)
    ops = (w_ref[0], f_ref[0], e_ref[0], ar, ai, dt)
    scratch = (yi_ref, s_ref, hin_ref)
    zero = jnp.zeros((1, S5_SW), F32)
    st = _s5_group(xm_ref, None, *ops, [(zero, zero)] * nbatch, *scratch, nseq=nbatch, nblk=nblk_m)
    st = _s5_group(xp_ref, yp_ref, *ops, st, *scratch, nseq=nbatch, nblk=nblk_p)
    for b in range(nbatch):
        pre_ref[b:b + 1, :] = st[b][0]
        pim_ref[b:b + 1, :] = st[b][1]
    fin = _s5_group(xs_ref, ys_ref, *ops, (h0re_ref[...], h0im_ref[...]), *scratch, nseq=nsamp, nblk=nblk_s)
    sre_ref[...] = fin[0]
    sim_ref[...] = fin[1]


def _s5(z_m, z_p, z_s, w, f, e, lam, d, h0re, h0im, *, nbatch, len_m, len_p, nsamp, len_s):
    assert len_s == S5_T
    n = S5_T * LANES
    tok_m, tok_p, tok_s = nbatch * len_m, nbatch * len_p, nsamp * len_s
    rows = max(nbatch * len_p, nsamp * len_s, nbatch * len_m) // S5_T
    col = lambda r: pl.BlockSpec((r, LANES), lambda j: (0, j))
    op = pl.BlockSpec((1, n, n), lambda j: (j, 0, 0))
    st = lambda r: pl.BlockSpec((r, S5_SW), lambda j: (0, j))
    gp = S5_GROUPS * S5_STATE
    kern = functools.partial(_s5_kernel, nbatch=nbatch, nblk_m=len_m // S5_T, nblk_p=len_p // S5_T,
                             nsamp=nsamp, nblk_s=len_s // S5_T)
    return pl.pallas_call(
        kern,
        grid=(S5_NBLK,),
        in_specs=[col(tok_m), col(tok_p), col(tok_s), op, op, op,
                  pl.BlockSpec((1, 2, S5_SW), lambda j: (j, 0, 0)), col(1), st(nsamp), st(nsamp)],
        out_specs=[col(tok_p), col(tok_s), st(nbatch), st(nbatch), st(nsamp), st(nsamp)],
        out_shape=[
            jax.ShapeDtypeStruct((tok_p, S5_WIDTH), F32),
            jax.ShapeDtypeStruct((tok_s, S5_WIDTH), F32),
            jax.ShapeDtypeStruct((nbatch, gp), F32),
            jax.ShapeDtypeStruct((nbatch, gp), F32),
            jax.ShapeDtypeStruct((nsamp, gp), F32),
            jax.ShapeDtypeStruct((nsamp, gp), F32),
        ],
        scratch_shapes=[pltpu.VMEM((rows, n), F32)] * 3,
        compiler_params=_cparams(("parallel",)),
        name="s5",
    )(z_m, z_p, z_s, w, f, e, lam, d, h0re, h0im)


def _s5_operators(a_re, a_im, log_dt, b_re, b_im):
    dt = jnp.exp(log_dt)[:, None]
    ar, ai = a_re * dt, a_im * dt
    taus = jnp.arange(S5_T + 1, dtype=F32)[:, None, None]
    pmag = jnp.exp(taus * ar[None])
    pw_re = pmag * jnp.cos(taus * ai[None])
    pw_im = pmag * jnp.sin(taus * ai[None])
    nr, ni = pw_re[1] - 1.0, pw_im[1]
    den = a_re * a_re + a_im * a_im
    q_re = (nr * a_re + ni * a_im) / den
    q_im = (ni * a_re - nr * a_im) / den
    gp = S5_GROUPS * S5_STATE
    bb_re = (q_re[:, :, None] * b_re - q_im[:, :, None] * b_im).reshape(gp, S5_GROUP)
    bb_im = (q_re[:, :, None] * b_im + q_im[:, :, None] * b_re).reshape(gp, S5_GROUP)
    padrows = ((0, S5_GROUP - (S5_T + 1)), (0, 0))
    pw_re2 = jnp.pad(pw_re.reshape(S5_T + 1, gp), padrows)
    pw_im2 = jnp.pad(pw_im.reshape(S5_T + 1, gp), padrows)
    lam = jnp.stack([pw_re[S5_T].reshape(S5_NBLK, S5_SW), pw_im[S5_T].reshape(S5_NBLK, S5_SW)], axis=1)
    return bb_re, bb_im, pw_re2.T, pw_im2.T, pw_re2, pw_im2, lam


def _mlstm_kernel(qk_ref, v_ref, o_ref, if_ref, cw_ref, cb_ref, bif_ref, ng_ref,
                  buf0_ref, c0_ref, n0_ref, m0_ref,
                  y_ref, bufo_ref, co_ref, no_ref, mo_ref, xp_ref, *, nb, lc, carry):
    if carry:
        @pl.when(pl.program_id(1) == 0)
        def _():
            bufo_ref[...] = buf0_ref[...]
            co_ref[...] = c0_ref[...]
            no_ref[...] = n0_ref[...]
            mo_ref[...] = m0_ref[...]
        bufs_ref, cs_ref, ns_ref, ms_ref = bufo_ref, co_ref, no_ref, mo_ref
    else:
        bufs_ref, cs_ref, ns_ref, ms_ref = buf0_ref, c0_ref, n0_ref, m0_ref

    lp = max(lc, LANES)
    pad = CONV_W - 1
    heads = range(HEADS)
    head_row = lax.broadcasted_iota(jnp.int32, (HEADS, DQK), 0)
    head_lane = lax.broadcasted_iota(jnp.int32, (1, HEADS), 1)
    row = lax.broadcasted_iota(jnp.int32, (lc, lc), 0)
    col = lax.broadcasted_iota(jnp.int32, (lc, lc), 1)
    causal = row >= col
    tril = jnp.where(causal, 1.0, 0.0).astype(BF16)
    lane = lax.broadcasted_iota(jnp.int32, (lc, LANES), 1)
    is_i = lane < HEADS
    is_f = jnp.logical_and(lane >= HEADS, lane < 2 * HEADS)

    for s in range(nb):
        r0 = s * lc
        x1 = SUBLANES
        xp_ref[s, 0:x1 - pad, :] = jnp.zeros((x1 - pad, QK_WIDTH), F32)
        xp_ref[s, x1 - pad:x1, :] = bufs_ref[s]
        xp_ref[s, x1:x1 + lc, :] = qk_ref[r0:r0 + lc, :]
        xp = xp_ref[s]
        conv = cb_ref[...] + xp[x1:x1 + lc] * cw_ref[pad:pad + 1, :]
        for j in range(1, CONV_W):
            conv = conv + pltpu.roll(xp, j, 0)[x1:x1 + lc] * cw_ref[pad - j:pad - j + 1, :]
        bufo_ref[s] = xp_ref[s, x1 + lc - pad:x1 + lc, :]
        n_all = ns_ref[s]
        m_all = ms_ref[s]
        qk = conv * jax.nn.sigmoid(conv)
        q = qk[:, :QK_WIDTH // 2] * (DQK ** -0.5)
        k = qk[:, QK_WIDTH // 2:]

        gate = if_ref[r0:r0 + lc, :] + bif_ref[...]
        lf = jnp.where(is_f, jax.nn.log_sigmoid(gate), 0.0)
        parts = _split3(lf)
        bcum = _dot(tril, parts[0]) + _dot(tril, parts[1]) + _dot(tril, parts[2])
        pc = jnp.where(is_i, gate, bcum)
        if lc < lp:
            pc_t = jnp.concatenate([pc, jnp.zeros((lp - lc, LANES), F32)], axis=0).T[:, :lc]
        else:
            pc_t = pc.T

        i_col = [pc[:, h:h + 1] for h in heads]
        b_col = [pc[:, HEADS + h:HEADS + h + 1] for h in heads]
        m0 = [m_all[:, h:h + 1] for h in heads]
        qh = [q[:, h * DQK:(h + 1) * DQK] for h in heads]
        kh = [k[:, h * DQK:(h + 1) * DQK] for h in heads]
        qb = [x.astype(BF16) for x in qh]
        vb = [v_ref[r0:r0 + lc, h * DV:(h + 1) * DV].astype(BF16) for h in heads]
        c_prev = [cs_ref[s, h] for h in heads]
        n_prev = [n_all[h:h + 1, :] for h in heads]

        qk_t = [_dot_nt(qb[h], kh[h].astype(BF16)) for h in heads]
        carried = [_dot_nt(qb[h], c_prev[h].astype(BF16)) for h in heads]
        logw = [jnp.where(causal, b_col[h] - pc_t[HEADS + h:HEADS + h + 1, :] + pc_t[h:h + 1, :], -jnp.inf)
                for h in heads]
        g = [b_col[h] + m0[h] for h in heads]
        m = [jnp.maximum(g[h], jnp.max(logw[h], axis=-1, keepdims=True)) for h in heads]
        inter = [jnp.exp(g[h] - m[h]) for h in heads]
        sc = [qk_t[h] * jnp.exp(logw[h] - m[h]) for h in heads]
        num = [_dot(sc[h].astype(BF16), vb[h]) + inter[h] * carried[h] for h in heads]
        den = [jnp.sum(sc[h], axis=-1, keepdims=True)
               + inter[h] * jnp.sum(qh[h] * n_prev[h], axis=-1, keepdims=True) for h in heads]
        hh = [num[h] / jnp.maximum(jnp.abs(den[h]), jnp.exp(-m[h])) for h in heads]
        hh = [x * lax.rsqrt(jnp.mean(x * x, axis=-1, keepdims=True) + EPS) for x in hh]
        for h in heads:
            og = jax.nn.sigmoid(o_ref[r0:r0 + lc, h * DV:(h + 1) * DV])
            y_ref[r0:r0 + lc, h * DV:(h + 1) * DV] = hh[h] * ng_ref[:, h * DV:(h + 1) * DV] * og

        b_last = [x[lc - 1:lc, :] for x in b_col]
        m_new = [x[lc - 1:lc, :] for x in m]
        decay = [jnp.exp(b_last[h] + m0[h] - m_new[h]) for h in heads]
        kw = [kh[h] * jnp.exp(b_last[h] - b_col[h] + i_col[h] - m_new[h]) for h in heads]
        upd = [_dot_tn(vb[h], kw[h].astype(BF16)) for h in heads]
        n_new_all = jnp.zeros((HEADS, DQK), F32)
        m_new_all = jnp.zeros((1, HEADS), F32)
        for h in heads:
            co_ref[s, h] = decay[h] * c_prev[h] + upd[h]
            n_new = decay[h] * n_prev[h] + jnp.sum(kw[h], axis=0, keepdims=True)
            n_new_all = jnp.where(head_row == h, n_new, n_new_all)
            m_new_all = jnp.where(head_lane == h, m_new[h], m_new_all)
        no_ref[s] = n_new_all
        mo_ref[s] = m_new_all


def _mlstm(z, zif, cw, cb, bif, ng, buf0, c0, n0, m0, *, nseq, seqlen, nb, lc):
    tok = nseq * seqlen
    nchunk = seqlen // lc
    rows = nb * lc
    tok_map = lambda col: (lambda i, c: (i * nchunk + c, col))
    const2 = lambda i, c: (0, 0)
    st3 = lambda i, c: (i, 0, 0)
    st4 = lambda i, c: (i, 0, 0, 0)
    kern = functools.partial(_mlstm_kernel, nb=nb, lc=lc, carry=nchunk > 1)
    return pl.pallas_call(
        kern,
        grid=(nseq // nb, nchunk),
        in_specs=[
            pl.BlockSpec((rows, QK_WIDTH), tok_map(1)),
            pl.BlockSpec((rows, V_WIDTH), tok_map(2)),
            pl.BlockSpec((rows, V_WIDTH), tok_map(3)),
            pl.BlockSpec((rows, LANES), tok_map(0)),
            pl.BlockSpec((CONV_W, QK_WIDTH), const2),
            pl.BlockSpec((1, QK_WIDTH), const2),
            pl.BlockSpec((1, LANES), const2),
            pl.BlockSpec((1, V_WIDTH), const2),
            pl.BlockSpec((nb, CONV_W - 1, QK_WIDTH), st3),
            pl.BlockSpec((nb, HEADS, DV, DQK), st4),
            pl.BlockSpec((nb, HEADS, DQK), st3),
            pl.BlockSpec((nb, 1, HEADS), st3),
        ],
        out_specs=[
            pl.BlockSpec((rows, V_WIDTH), tok_map(0)),
            pl.BlockSpec((nb, CONV_W - 1, QK_WIDTH), st3),
            pl.BlockSpec((nb, HEADS, DV, DQK), st4),
            pl.BlockSpec((nb, HEADS, DQK), st3),
            pl.BlockSpec((nb, 1, HEADS), st3),
        ],
        out_shape=[
            jax.ShapeDtypeStruct((tok, V_WIDTH), F32),
            jax.ShapeDtypeStruct((nseq, CONV_W - 1, QK_WIDTH), F32),
            jax.ShapeDtypeStruct((nseq, HEADS, DV, DQK), F32),
            jax.ShapeDtypeStruct((nseq, HEADS, DQK), F32),
            jax.ShapeDtypeStruct((nseq, 1, HEADS), F32),
        ],
        scratch_shapes=[pltpu.VMEM((nb, lc + SUBLANES, QK_WIDTH), F32)],
        compiler_params=_cparams(("parallel", "arbitrary")),
        name="mlstm",
    )(z, z, z, zif, cw, cb, bif, ng, buf0, c0, n0, m0)


def _mlstm_step_kernel(qk_ref, v_ref, o_ref, if_ref, cw_ref, cb_ref, bif_ref, ng_ref,
                       buf0_ref, c0_ref, n0x_ref, m0x_ref,
                       y_ref, bufo_ref, co_ref, nox_ref, mox_ref, xp_ref, conv_ref, *, nb, lc):
    rows = nb * lc
    lg = lc.bit_length() - 1
    i32 = jnp.int32
    row = lax.broadcasted_iota(i32, (rows, rows), 0)
    col = lax.broadcasted_iota(i32, (rows, rows), 1)
    same = (row >> lg) == (col >> lg)
    causal = jnp.logical_and(same, row >= col)
    same_b = jnp.where(same, 1.0, 0.0).astype(BF16)
    tril_b = jnp.where(causal, 1.0, 0.0).astype(BF16)
    lane = lax.broadcasted_iota(i32, (rows, LANES), 1)
    is_i = lane < HEADS
    is_f = jnp.logical_and(lane >= HEADS, lane < 2 * HEADS)

    def seg_dot(mat, x):
        hi, mid, lo = _split3(x)
        return _dot(mat, hi) + _dot(mat, mid) + _dot(mat, lo)

    x0 = SUBLANES - (CONV_W - 1)
    for s in range(nb):
        xp_ref[s, x0:SUBLANES, :] = buf0_ref[s]
        xp_ref[s, SUBLANES:SUBLANES + lc, :] = qk_ref[s * lc:(s + 1) * lc, :]
        conv_ref[s * lc:(s + 1) * lc, :] = cb_ref[...] + sum(
            xp_ref[s, x0 + j:x0 + j + lc, :] * cw_ref[j:j + 1, :] for j in range(CONV_W))
        bufo_ref[s] = xp_ref[s, x0 + lc:SUBLANES + lc, :]
    conv = conv_ref[...]
    qk = conv * jax.nn.sigmoid(conv)
    q = qk[:, :QK_WIDTH // 2] * (DQK ** -0.5)
    k = qk[:, QK_WIDTH // 2:]

    gate = if_ref[...] + bif_ref[...]
    lf = jnp.where(is_f, jax.nn.log_sigmoid(gate), 0.0)
    bcum = seg_dot(tril_b, lf)
    btot = seg_dot(same_b, lf)
    pc = jnp.where(is_i, gate, bcum)
    if rows < LANES:
        pc_t = jnp.concatenate([pc, jnp.zeros((LANES - rows, LANES), F32)], axis=0).T[:, :rows]
    else:
        pc_t = pc.T
    m0x = m0x_ref[...]
    heads = range(HEADS)
    seqs = range(nb)

    i_col = [pc[:, h:h + 1] for h in heads]
    b_col = [pc[:, HEADS + h:HEADS + h + 1] for h in heads]
    i_row = [pc_t[h:h + 1, :] for h in heads]
    b_row = [pc_t[HEADS + h:HEADS + h + 1, :] for h in heads]
    b_last = [btot[:, HEADS + h:HEADS + h + 1] for h in heads]
    m0 = [m0x[:, h:h + 1] for h in heads]
    qh = [q[:, h * DQK:(h + 1) * DQK] for h in heads]
    kh = [k[:, h * DQK:(h + 1) * DQK] for h in heads]
    qb = [x.astype(BF16) for x in qh]
    vb = [v_ref[:, h * DV:(h + 1) * DV].astype(BF16) for h in heads]
    n_prev = [n0x_ref[:, h * DQK:(h + 1) * DQK] for h in heads]

    qk_t = [_dot_nt(qb[h], kh[h].astype(BF16)) for h in heads]
    carried = [jnp.concatenate([_dot_nt(qb[h][s * lc:(s + 1) * lc], c0_ref[s, h].astype(BF16)) for s in seqs],
                               axis=0) for h in heads]
    logw = [jnp.where(causal, b_col[h] - b_row[h] + i_row[h], -jnp.inf) for h in heads]
    g = [b_col[h] + m0[h] for h in heads]
    m = [jnp.maximum(g[h], jnp.max(logw[h], axis=-1, keepdims=True)) for h in heads]
    inter = [jnp.exp(g[h] - m[h]) for h in heads]
    lw_end = [jnp.where(same, b_last[h] - b_row[h] + i_row[h], -jnp.inf) for h in heads]
    m_new = [jnp.maximum(b_last[h] + m0[h], jnp.max(lw_end[h], axis=-1, keepdims=True)) for h in heads]
    decay = [jnp.exp(b_last[h] + m0[h] - m_new[h]) for h in heads]
    sc = [qk_t[h] * jnp.exp(logw[h] - m[h]) for h in heads]
    num = [_dot(sc[h].astype(BF16), vb[h]) + inter[h] * carried[h] for h in heads]
    den = [jnp.sum(sc[h], axis=-1, keepdims=True)
           + inter[h] * jnp.sum(qh[h] * n_prev[h], axis=-1, keepdims=True) for h in heads]
    hh = [num[h] / jnp.maximum(jnp.abs(den[h]), jnp.exp(-m[h])) for h in heads]
    hh = [x * lax.rsqrt(jnp.mean(x * x, axis=-1, keepdims=True) + EPS) for x in hh]
    for h in heads:
        og = jax.nn.sigmoid(o_ref[:, h * DV:(h + 1) * DV])
        y_ref[:, h * DV:(h + 1) * DV] = hh[h] * ng_ref[:, h * DV:(h + 1) * DV] * og

    kw = [kh[h] * jnp.exp(b_last[h] - b_col[h] + i_col[h] - m_new[h]) for h in heads]
    kwb = [x.astype(BF16) for x in kw]
    upd = [[_dot_tn(vb[h][s * lc:(s + 1) * lc], kwb[h][s * lc:(s + 1) * lc]) for s in seqs] for h in heads]
    mox = jnp.zeros((rows, LANES), F32)
    for h in heads:
        for s in seqs:
            co_ref[s, h] = decay[h][s * lc:s * lc + 1, :] * c0_ref[s, h] + upd[h][s]
        nox_ref[:, h * DQK:(h + 1) * DQK] = decay[h] * n_prev[h] + seg_dot(same_b, kw[h])
        mox = jnp.where(lane == h, m_new[h], mox)
    mox_ref[...] = mox


def _mlstm_step(z, zif, cw, cb, bif, ng, buf0, c0, n0, m0, *, nseq, seqlen, nb):
    lc = seqlen
    assert lc & (lc - 1) == 0 and lc >= CONV_W - 1
    tok = nseq * seqlen
    rows = nb * lc
    n0x = jnp.repeat(n0.reshape(nseq, HEADS * DQK), lc, axis=0)
    m0x = jnp.repeat(jnp.pad(m0, ((0, 0), (0, LANES - HEADS))), lc, axis=0)
    tok_map = lambda col: (lambda i: (i, col))
    const2 = lambda i: (0, 0)
    st3 = lambda i: (i, 0, 0)
    st4 = lambda i: (i, 0, 0, 0)
    y, buf, c, nox, mox = pl.pallas_call(
        functools.partial(_mlstm_step_kernel, nb=nb, lc=lc),
        grid=(nseq // nb,),
        in_specs=[
            pl.BlockSpec((rows, QK_WIDTH), tok_map(1)),
            pl.BlockSpec((rows, V_WIDTH), tok_map(2)),
            pl.BlockSpec((rows, V_WIDTH), tok_map(3)),
            pl.BlockSpec((rows, LANES), tok_map(0)),
            pl.BlockSpec((CONV_W, QK_WIDTH), const2),
            pl.BlockSpec((1, QK_WIDTH), const2),
            pl.BlockSpec((1, LANES), const2),
            pl.BlockSpec((1, V_WIDTH), const2),
            pl.BlockSpec((nb, CONV_W - 1, QK_WIDTH), st3),
            pl.BlockSpec((nb, HEADS, DV, DQK), st4),
            pl.BlockSpec((rows, HEADS * DQK), tok_map(0)),
            pl.BlockSpec((rows, LANES), tok_map(0)),
        ],
        out_specs=[
            pl.BlockSpec((rows, V_WIDTH), tok_map(0)),
            pl.BlockSpec((nb, CONV_W - 1, QK_WIDTH), st3),
            pl.BlockSpec((nb, HEADS, DV, DQK), st4),
            pl.BlockSpec((rows, HEADS * DQK), tok_map(0)),
            pl.BlockSpec((rows, LANES), tok_map(0)),
        ],
        out_shape=[
            jax.ShapeDtypeStruct((tok, V_WIDTH), F32),
            jax.ShapeDtypeStruct((nseq, CONV_W - 1, QK_WIDTH), F32),
            jax.ShapeDtypeStruct((nseq, HEADS, DV, DQK), F32),
            jax.ShapeDtypeStruct((tok, HEADS * DQK), F32),
            jax.ShapeDtypeStruct((tok, LANES), F32),
        ],
        scratch_shapes=[pltpu.VMEM((nb, lc + SUBLANES, QK_WIDTH), F32), pltpu.VMEM((rows, QK_WIDTH), F32)],
        compiler_params=_cparams(("parallel",)),
        name="mlstm_step",
    )(z, z, z, zif, cw, cb, bif, ng, buf0, c0, n0x, m0x)
    n = nox[::lc].reshape(nseq, HEADS, DQK)
    m = mox[::lc, :HEADS]
    return y, buf, c, n, m


def _merge_kernel(h_ref, ys_ref, ym_ref, g1_ref, g2_ref, wglu_ref, wbs_ref, wbm_ref, wo_ref, o_ref):
    ys = ys_ref[...]
    glu = ys * jax.nn.sigmoid(_dot(ys.astype(BF16), wglu_ref[...]))
    a = _dot(glu.astype(BF16), wbs_ref[...])
    b = _dot(ym_ref[...].astype(BF16), wbm_ref[...])
    merged = jax.nn.sigmoid(g1_ref[...]) * a + jax.nn.sigmoid(g2_ref[...]) * b
    o_ref[...] = h_ref[...] + _dot(merged.astype(BF16), wo_ref[...])


def _merge(h, ys, ym, z, wglu, wbs, wbm, wo, *, n, tm):
    row = lambda i: (i, 0)
    const = lambda i: (0, 0)
    resident = lambda shape: pl.BlockSpec(shape, const, pipeline_mode=pl.Buffered(1))
    return pl.pallas_call(
        _merge_kernel,
        grid=(n // tm,),
        in_specs=[
            pl.BlockSpec((tm, D_MODEL), row),
            pl.BlockSpec((tm, S5_WIDTH), row),
            pl.BlockSpec((tm, V_WIDTH), row),
            pl.BlockSpec((tm, D_MODEL), lambda i: (i, 2)),
            pl.BlockSpec((tm, D_MODEL), lambda i: (i, 3)),
            resident((S5_WIDTH, S5_WIDTH)),
            resident((S5_WIDTH, D_MODEL)),
            resident((V_WIDTH, D_MODEL)),
            resident((D_MODEL, D_MODEL)),
        ],
        out_specs=pl.BlockSpec((tm, D_MODEL), row),
        out_shape=jax.ShapeDtypeStruct((n, D_MODEL), F32),
        compiler_params=_cparams(("parallel",)),
        name="merge",
    )(h, ys, ym, z, z, wglu, wbs, wbm, wo)


def kernel(x_prompt, x_sample, state_s5_re, state_s5_im, state_mlstm_C, state_mlstm_n, state_mlstm_m,
           state_mlstm_conv, meta_tokens, ffn1_norm, ffn1_w_gate, ffn1_w_up, ffn1_w_down, mix_norm, w_in,
           s5_A_re, s5_A_im, s5_log_dt, s5_B_re, s5_B_im, s5_C_re, s5_C_im, s5_D, s5_w_glu,
           mlstm_conv_w, mlstm_conv_b, mlstm_b_i, mlstm_b_f, mlstm_norm, w_branch_s5, w_branch_mlstm,
           w_out, ffn2_norm, ffn2_w_gate, ffn2_w_up, ffn2_w_down, final_norm):
    nbatch, seq, _ = x_prompt.shape
    nsamp, sseq, _ = x_sample.shape
    l = 0

    w1g, w1u, w1d = ffn1_w_gate[l], ffn1_w_up[l], ffn1_w_down[l]
    w2g, w2u, w2d = ffn2_w_gate[l], ffn2_w_up[l], ffn2_w_down[l]
    o_if = S5_WIDTH + QK_WIDTH + 2 * V_WIDTH
    o_gate = o_if + 2 * HEADS
    wt = w_in[l].T
    w2 = _win_halves(wt, o_gate, tr=WIN_CAST_ROWS)
    wglu, wbs, wbm, wo = (w[l].astype(BF16) for w in (s5_w_glu, w_branch_s5, w_branch_mlstm, w_out))
    g1 = ffn1_norm[l][None]
    gm = mix_norm[l][None]
    g2 = ffn2_norm[l][None]
    gf = final_norm[None]
    bif = jnp.pad(jnp.concatenate([mlstm_b_i[l], mlstm_b_f[l]]), (0, LANES - 2 * HEADS))[None]
    cw = mlstm_conv_w[l]
    cb = mlstm_conv_b[l][None]
    ng = mlstm_norm[l][None]
    d_skip = s5_D[l][None]

    *s5_ops, lam = _s5_operators(s5_A_re[l], s5_A_im[l], s5_log_dt[l], s5_B_re[l], s5_B_im[l])
    gh = S5_GROUPS * S5_GROUP
    w_toe, f_bf, e_bf = _s5_prep(*s5_ops, s5_C_re[l].reshape(gh, S5_STATE), s5_C_im[l].reshape(gh, S5_STATE))

    def front(x, tm):
        n = x.shape[0]
        h1 = _ffn(x, g1, w1g, w1u, w1d, n=n, tm=tm)
        z, zif = _win(h1, gm, w2, wt, o_if=o_if, ngate=N_BRANCH * D_MODEL, tm=tm)
        return h1, z, zif

    def mlstm(z, zif, ml_state, *, nseq, seqlen, nb, lc):
        buf0, c0, n0, m0 = ml_state
        if lc == seqlen and nb > 1:
            return _mlstm_step(z, zif, cw, cb, bif, ng, buf0, c0, n0, m0, nseq=nseq, seqlen=seqlen, nb=nb)
        ym, buf, c, n, m = _mlstm(z, zif, cw, cb, bif, ng, buf0, c0, n0, m0.reshape(nseq, 1, HEADS),
                                  nseq=nseq, seqlen=seqlen, nb=nb, lc=lc)
        return ym, buf, c, n, m.reshape(nseq, HEADS)

    def back(h1, ys, ym, z, n):
        h2 = _merge(h1, ys, ym, z, wglu, wbs, wbm, wo, n=n, tm=MERGE_TM)
        return _ffn(h2, g2, w2g, w2u, w2d, gf, n=n, tm=FFN_TM)

    ntok_s = nsamp * sseq
    ntok_p = nbatch * seq
    x_sm = jnp.concatenate([x_sample.reshape(ntok_s, D_MODEL), meta_tokens], axis=0)
    h1_sm, z_sm, zif_sm = front(x_sm, ntok_s + N_META)
    h1_p, z_p, zif_p = front(x_prompt.reshape(ntok_p, D_MODEL), FFN_TM)

    z_m = jnp.tile(z_sm[ntok_s:], (nbatch, 1))
    zif_m = jnp.tile(zif_sm[ntok_s:], (nbatch, 1))
    zeros = lambda *s: jnp.zeros((nbatch,) + s, F32)

    ys_p, ys_s, *s5_st = _s5(z_m, z_p, z_sm, w_toe, f_bf, e_bf, lam, d_skip,
                             state_s5_re[l].reshape(nsamp, -1), state_s5_im[l].reshape(nsamp, -1),
                             nbatch=nbatch, len_m=N_META, len_p=seq, nsamp=nsamp, len_s=sseq)
    s5_p, s5_s = s5_st[:2], s5_st[2:]

    _, *ml_m = mlstm(z_m, zif_m, (zeros(CONV_W - 1, QK_WIDTH), zeros(HEADS, DV, DQK), zeros(HEADS, DQK), zeros(HEADS)),
                     nseq=nbatch, seqlen=N_META, nb=1, lc=N_META)
    ym_p, *ml_p = mlstm(z_p, zif_p, ml_m, nseq=nbatch, seqlen=seq, nb=1, lc=MLSTM_CHUNK)
    ym_s, *ml_s = mlstm(z_sm, zif_sm,
                        (state_mlstm_conv[l], state_mlstm_C[l], state_mlstm_n[l], state_mlstm_m[l]),
                        nseq=nsamp, seqlen=sseq, nb=STEP_NB, lc=sseq)
    y_p = back(h1_p, ys_p, ym_p, z_p, ntok_p)
    y_s = back(h1_sm, ys_s, ym_s, z_sm, ntok_s)

    def pack(n, s5_st, ml_st):
        buf, c, nn, m = ml_st
        return (s5_st[0].reshape(1, n, S5_GROUPS, S5_STATE), s5_st[1].reshape(1, n, S5_GROUPS, S5_STATE),
                c[None], nn[None], m[None], buf[None])

    return ((y_p.reshape(nbatch, seq, D_MODEL), y_s.reshape(nsamp, sseq, D_MODEL))
            + pack(nbatch, s5_p, ml_p) + pack(nsamp, s5_s, ml_s))
```

```python
import functools
from typing import Callable, NamedTuple

import jax
import jax.numpy as jnp
from jax import lax
from jax.experimental import pallas as pl
from jax.experimental.pallas import tpu as pltpu

F32 = jnp.float32
BF16 = jnp.bfloat16

D_MODEL = 2048
D_FF = 5632
N_META = 16
S5_WIDTH = 1024
S5_GROUP = 16
S5_GROUPS = 64
S5_STATE = 64
HEADS = 4
DQK = 128
DV = 256
QK_WIDTH = 1024
V_WIDTH = 1024
CONV_W = 4
N_BRANCH = 2
EPS = 1e-6

LANES = 128
SUBLANES = 8
S5_T = 8
S5_GPB = LANES // S5_GROUP
S5_NBLK = S5_WIDTH // LANES
S5_SW = S5_GPB * S5_STATE
VMEM_LIMIT = 58 * 1024 * 1024
FFN_TM = 1024
MERGE_TM = 256
WIN_CAST_ROWS = 456
S5_ROW_TILE = 256
MLSTM_CHUNK = 256


def _cparams(sem):
    return pltpu.CompilerParams(dimension_semantics=sem, vmem_limit_bytes=VMEM_LIMIT)


def _rmsnorm(x, g):
    ms = jnp.mean(x * x, axis=-1, keepdims=True)
    return (x * lax.rsqrt(ms + EPS)) * g


def _dot(a, b):
    return jnp.dot(a, b, preferred_element_type=F32)


def _dot_nt(a, b):
    return lax.dot_general(a, b, (((1,), (1,)), ((), ())), preferred_element_type=F32)


def _dot_tn(a, b):
    return lax.dot_general(a, b, (((0,), (0,)), ((), ())), preferred_element_type=F32)


def _split3(x):
    hi = x.astype(BF16)
    r = x - hi.astype(F32)
    mid = r.astype(BF16)
    lo = (r - mid.astype(F32)).astype(BF16)
    return hi, mid, lo


class _Part(NamedTuple):
    kernel: Callable
    in_specs: list
    args: list
    out_specs: list
    out_shape: list
    scratch: list


def _ffn_kernel(x_ref, g_ref, wg_ref, wu_ref, wd_ref, *rest, final_norm):
    if final_norm:
        fg_ref, o_ref, xn_ref = rest
    else:
        o_ref, xn_ref = rest
    j = pl.program_id(1)

    @pl.when(j == 0)
    def _():
        xn_ref[...] = _rmsnorm(x_ref[...], g_ref[...]).astype(BF16)
        o_ref[...] = jnp.zeros(o_ref.shape, F32)

    xn = xn_ref[...]
    gt = _dot(xn, wg_ref[...].astype(BF16))
    up = _dot(xn, wu_ref[...].astype(BF16))
    act = (gt * jax.nn.sigmoid(gt) * up).astype(BF16)
    o_ref[...] += _dot(act, wd_ref[...].astype(BF16))

    @pl.when(j == pl.num_programs(1) - 1)
    def _():
        h = x_ref[...] + 0.5 * o_ref[...]
        if final_norm:
            h = _rmsnorm(h, fg_ref[...])
        o_ref[...] = h


def _ffn(x, g, wg, wu, wd, final_g=None, *, n, tm, tf=256):
    in_specs = [
        pl.BlockSpec((tm, D_MODEL), lambda i, j: (i, 0)),
        pl.BlockSpec((1, D_MODEL), lambda i, j: (0, 0)),
        pl.BlockSpec((D_MODEL, tf), lambda i, j: (0, j)),
        pl.BlockSpec((D_MODEL, tf), lambda i, j: (0, j)),
        pl.BlockSpec((tf, D_MODEL), lambda i, j: (j, 0)),
    ]
    args = [x, g, wg, wu, wd]
    if final_g is not None:
        in_specs.append(pl.BlockSpec((1, D_MODEL), lambda i, j: (0, 0)))
        args.append(final_g)
    return pl.pallas_call(
        functools.partial(_ffn_kernel, final_norm=final_g is not None),
        grid=(n // tm, D_FF // tf),
        in_specs=in_specs,
        out_specs=pl.BlockSpec((tm, D_MODEL), lambda i, j: (i, 0)),
        out_shape=jax.ShapeDtypeStruct((n, D_MODEL), F32),
        scratch_shapes=[pltpu.VMEM((tm, D_MODEL), BF16)],
        compiler_params=_cparams(("parallel", "arbitrary")),
        name="ffn",
    )(*args)


def _cast_rows_kernel(w_ref, o_ref, *, tail):
    last = pl.num_programs(0) - 1

    @pl.when(pl.program_id(0) < last)
    def _():
        o_ref[0] = w_ref[...].astype(BF16)

    @pl.when(pl.program_id(0) == last)
    def _():
        o_ref[0, :tail] = w_ref[:tail].astype(BF16)
        o_ref[0, tail:] = jnp.zeros((o_ref.shape[1] - tail, o_ref.shape[2]), BF16)


def _win_halves(wt, o_gate, *, tr):
    rows, k = wt.shape
    per = o_gate // tr
    tail = rows - (2 * per - 1) * tr
    assert per * tr == o_gate and tr % 8 == 0 and 0 < tail < tr and tail % 8 == 0
    return pl.pallas_call(
        functools.partial(_cast_rows_kernel, tail=tail),
        grid=(2 * per,),
        in_specs=[pl.BlockSpec((tr, k), lambda i: (i, 0))],
        out_specs=pl.BlockSpec((1, tr, k), lambda i: (i // per, i % per, 0)),
        out_shape=jax.ShapeDtypeStruct((2, o_gate, k), BF16),
        compiler_params=_cparams(("parallel",)),
        name="w_in_cast",
    )(wt)


def _win_kernel(h_ref, g_ref, wa_ref, wb_ref, wif_ref, z_ref, zif_ref, un_ref, *, na):
    j = pl.program_id(1)

    @pl.when(j == 0)
    def _():
        un = _rmsnorm(h_ref[...], g_ref[...]).astype(BF16)
        un_ref[...] = un
        zif_ref[...] = _dot_nt(un, wif_ref[...].astype(BF16))

    @pl.when(j < na)
    def _():
        z_ref[...] = _dot_nt(un_ref[...], wa_ref[0])

    @pl.when(j >= na)
    def _():
        z_ref[...] = _dot_nt(un_ref[...], wb_ref[0])


def _win(h, g, w2, wt, *, o_if, ngate, tm, tn=1024):
    n = h.shape[0]
    na = o_if // tn
    nb = ngate // tn
    return pl.pallas_call(
        functools.partial(_win_kernel, na=na),
        grid=(n // tm, na + nb),
        in_specs=[
            pl.BlockSpec((tm, D_MODEL), lambda i, j: (i, 0)),
            pl.BlockSpec((1, D_MODEL), lambda i, j: (0, 0)),
            pl.BlockSpec((1, tn, D_MODEL), lambda i, j: (0, jnp.minimum(j, na - 1), 0)),
            pl.BlockSpec((1, tn, D_MODEL), lambda i, j: (1, jnp.maximum(j - na, 0), 0)),
            pl.BlockSpec((LANES, D_MODEL), lambda i, j: (o_if // LANES, 0)),
        ],
        out_specs=[
            pl.BlockSpec((tm, tn), lambda i, j: (i, j)),
            pl.BlockSpec((tm, LANES), lambda i, j: (i, 0)),
        ],
        out_shape=[
            jax.ShapeDtypeStruct((n, (na + nb) * tn), F32),
            jax.ShapeDtypeStruct((n, LANES), F32),
        ],
        scratch_shapes=[pltpu.VMEM((tm, D_MODEL), BF16)],
        compiler_params=_cparams(("parallel", "arbitrary")),
        name="w_in",
    )(h, g, w2, w2, wt)


def _s5_prep_kernel(bbr_ref, bbi_ref, ptr_ref, pti_ref, pwr_ref, pwi_ref, cnr_ref, cni_ref,
                    w_ref, f_ref, e_ref):
    n = S5_T * LANES
    i32 = jnp.int32
    lg_h = S5_GROUP.bit_length() - 1
    lg_p = S5_STATE.bit_length() - 1
    lg_l = LANES.bit_length() - 1
    gmask = S5_GPB - 1

    def tile_mat(k, c, src_of_col):
        kk = lax.broadcasted_iota(i32, (k, c), 0)
        cc = lax.broadcasted_iota(i32, (k, c), 1)
        return jnp.where(kk == src_of_col(cc), 1.0, 0.0).astype(BF16)

    def group_mask(r, c, row_group, col_group):
        rr = lax.broadcasted_iota(i32, (r, c), 0)
        cc = lax.broadcasted_iota(i32, (r, c), 1)
        return row_group(rr) == col_group(cc)

    def tiled(x, mat):
        hi, mid, _ = _split3(x)
        return _dot(hi, mat) + _dot(mid, mat)

    def lag_power(pt):
        return jnp.concatenate(
            [jnp.broadcast_to(pt[:, S5_T - 1 - s:S5_T - s], (S5_SW, LANES)) for s in range(S5_T)], axis=1)

    def dot_hi(a, b):
        a_hi, a_mid, _ = _split3(a)
        b_hi, b_mid, _ = _split3(b)
        return _dot(a_hi, b_hi) + _dot(a_mid, b_hi) + _dot(a_hi, b_mid)

    sel_h = tile_mat(S5_GROUP, n, lambda c: c & (S5_GROUP - 1))
    mf = group_mask(S5_SW, n, lambda r: r >> lg_p, lambda c: (c >> lg_h) & gmask)
    lpr = lag_power(ptr_ref[...])
    lpi = lag_power(pti_ref[...])
    bxr = tiled(bbr_ref[...], sel_h)
    bxi = tiled(bbi_ref[...], sel_h)
    ftr = jnp.where(mf, lpr * bxr - lpi * bxi, 0.0)
    fti = jnp.where(mf, lpr * bxi + lpi * bxr, 0.0)
    f_ref[0, :S5_SW, :] = ftr.astype(BF16)
    f_ref[0, S5_SW:, :] = fti.astype(BF16)

    sel_p = tile_mat(S5_STATE, S5_SW, lambda c: c & (S5_STATE - 1))
    mc = group_mask(LANES, S5_SW, lambda r: r >> lg_h, lambda c: c >> lg_p)
    cxr = jnp.where(mc, tiled(cnr_ref[...], sel_p), 0.0)
    cxi = jnp.where(mc, tiled(cni_ref[...], sel_p), 0.0)

    for t in range(S5_T):
        pr = pwr_ref[t + 1:t + 2, :]
        pi = pwi_ref[t + 1:t + 2, :]
        e_ref[0, t * LANES:(t + 1) * LANES, :S5_SW] = (cxr * pr - cxi * pi).astype(BF16)
        e_ref[0, t * LANES:(t + 1) * LANES, S5_SW:] = (-(cxr * pi + cxi * pr)).astype(BF16)

    cn = jnp.concatenate([cxr, -cxi], axis=1)
    w_ref[...] = jnp.zeros(w_ref.shape, w_ref.dtype)
    for lag in range(S5_T):
        c0 = (S5_T - 1 - lag) * LANES
        fblk = jnp.concatenate([ftr[:, c0:c0 + LANES], fti[:, c0:c0 + LANES]], axis=0)
        kt = dot_hi(cn, fblk).astype(BF16)
        for s in range(S5_T - lag):
            t = s + lag
            w_ref[0, t * LANES:(t + 1) * LANES, s * LANES:(s + 1) * LANES] = kt


def _s5_prep(bb_re, bb_im, pwt_re, pwt_im, pw_re, pw_im, cn_re, cn_im):
    n = S5_T * LANES
    rows = lambda r, c: pl.BlockSpec((r, c), lambda j: (j, 0))
    cols = lambda r, c: pl.BlockSpec((r, c), lambda j: (0, j))
    out = pl.BlockSpec((1, n, n), lambda j: (j, 0, 0))
    return pl.pallas_call(
        _s5_prep_kernel,
        grid=(S5_NBLK,),
        in_specs=[rows(S5_SW, S5_GROUP)] * 4 + [cols(S5_GROUP, S5_SW)] * 2 + [rows(LANES, S5_STATE)] * 2,
        out_specs=[out, out, out],
        out_shape=[jax.ShapeDtypeStruct((S5_NBLK, n, n), BF16)] * 3,
        compiler_params=_cparams(("parallel",)),
        name="s5_prep",
    )(bb_re, bb_im, pwt_re, pwt_im, pw_re, pw_im, cn_re, cn_im)


def _s5_group(x_ref, y_ref, w, f, e, ar, ai, dt, h0, yi_ref, s_ref, hin_ref, *, nseq, nblk):
    rows = nseq * nblk
    rt = min(rows, S5_ROW_TILE)

    def load_u(r0):
        return jnp.concatenate(
            [x_ref[pl.ds(r0 * S5_T + s, rt, stride=S5_T), :] for s in range(S5_T)], axis=1)

    for r0 in range(0, rows, rt):
        ub = load_u(r0).astype(BF16)
        if y_ref is not None:
            yi_ref[r0:r0 + rt, :] = _dot_nt(ub, w)
        s_ref[r0:r0 + rt, :] = _dot_nt(ub, f)

    if nblk == 1:
        hre, him = h0
        hin_ref[0:rows, :S5_SW] = hre
        hin_ref[0:rows, S5_SW:] = him
        fin = (ar * hre - ai * him + s_ref[0:rows, :S5_SW], ar * him + ai * hre + s_ref[0:rows, S5_SW:])
    else:
        def body(c, carry):
            new = []
            for b in range(nseq):
                hre, him = carry[2 * b], carry[2 * b + 1]
                row = b * nblk + c
                hin_ref[pl.ds(row, 1), :S5_SW] = hre
                hin_ref[pl.ds(row, 1), S5_SW:] = him
                sre = s_ref[pl.ds(row, 1), :S5_SW]
                sim = s_ref[pl.ds(row, 1), S5_SW:]
                new.append(ar * hre - ai * him + sre)
                new.append(ar * him + ai * hre + sim)
            return tuple(new)

        flat = lax.fori_loop(0, nblk, body, tuple(v for pair in h0 for v in pair))
        fin = [(flat[2 * b], flat[2 * b + 1]) for b in range(nseq)]

    if y_ref is not None:
        for r0 in range(0, rows, rt):
            yo = _dot_nt(hin_ref[r0:r0 + rt, :].astype(BF16), e)
            y = jax.nn.gelu(yi_ref[r0:r0 + rt, :] + yo + load_u(r0) * dt)
            for t in range(S5_T):
                y_ref[pl.ds(r0 * S5_T + t, rt, stride=S5_T), :] = y[:, t * LANES:(t + 1) * LANES]
    return fin


def _s5_kernel(xm_ref, xp_ref, xs_ref, w_ref, f_ref, e_ref, lam_ref, d_ref, h0re_ref, h0im_ref,
               yp_ref, ys_ref, pre_ref, pim_ref, sre_ref, sim_ref, yi_ref, s_ref, hin_ref,
               *, nbatch, nblk_m, nblk_p, nsamp, nblk_s):
    ar = lam_ref[0, 0:1, :]
    ai = lam_ref[0, 1:2, :]
    dt = jnp.concatenate([d_ref[...]] * S5_T, axis=1)
    ops = (w_ref[0], f_ref[0], e_ref[0], ar, ai, dt)
    scratch = (yi_ref, s_ref, hin_ref)
    zero = jnp.zeros((1, S5_SW), F32)
    st = _s5_group(xm_ref, None, *ops, [(zero, zero)] * nbatch, *scratch, nseq=nbatch, nblk=nblk_m)
    st = _s5_group(xp_ref, yp_ref, *ops, st, *scratch, nseq=nbatch, nblk=nblk_p)
    for b in range(nbatch):
        pre_ref[b:b + 1, :] = st[b][0]
        pim_ref[b:b + 1, :] = st[b][1]
    fin = _s5_group(xs_ref, ys_ref, *ops, (h0re_ref[...], h0im_ref[...]), *scratch, nseq=nsamp, nblk=nblk_s)
    sre_ref[...] = fin[0]
    sim_ref[...] = fin[1]


def _s5(z_m, z_p, z_s, w, f, e, lam, d, h0re, h0im, *, nbatch, len_m, len_p, nsamp, len_s):
    assert len_s == S5_T
    n = S5_T * LANES
    tok_m, tok_p, tok_s = nbatch * len_m, nbatch * len_p, nsamp * len_s
    rows = max(nbatch * len_p, nsamp * len_s, nbatch * len_m) // S5_T
    col = lambda r: pl.BlockSpec((r, LANES), lambda j: (0, j))
    op = pl.BlockSpec((1, n, n), lambda j: (j, 0, 0))
    st = lambda r: pl.BlockSpec((r, S5_SW), lambda j: (0, j))
    gp = S5_GROUPS * S5_STATE
    kern = functools.partial(_s5_kernel, nbatch=nbatch, nblk_m=len_m // S5_T, nblk_p=len_p // S5_T,
                             nsamp=nsamp, nblk_s=len_s // S5_T)
    return pl.pallas_call(
        kern,
        grid=(S5_NBLK,),
        in_specs=[col(tok_m), col(tok_p), col(tok_s), op, op, op,
                  pl.BlockSpec((1, 2, S5_SW), lambda j: (j, 0, 0)), col(1), st(nsamp), st(nsamp)],
        out_specs=[col(tok_p), col(tok_s), st(nbatch), st(nbatch), st(nsamp), st(nsamp)],
        out_shape=[
            jax.ShapeDtypeStruct((tok_p, S5_WIDTH), F32),
            jax.ShapeDtypeStruct((tok_s, S5_WIDTH), F32),
            jax.ShapeDtypeStruct((nbatch, gp), F32),
            jax.ShapeDtypeStruct((nbatch, gp), F32),
            jax.ShapeDtypeStruct((nsamp, gp), F32),
            jax.ShapeDtypeStruct((nsamp, gp), F32),
        ],
        scratch_shapes=[pltpu.VMEM((rows, n), F32)] * 3,
        compiler_params=_cparams(("parallel",)),
        name="s5",
    )(z_m, z_p, z_s, w, f, e, lam, d, h0re, h0im)


def _s5_operators(a_re, a_im, log_dt, b_re, b_im):
    dt = jnp.exp(log_dt)[:, None]
    ar, ai = a_re * dt, a_im * dt
    taus = jnp.arange(S5_T + 1, dtype=F32)[:, None, None]
    pmag = jnp.exp(taus * ar[None])
    pw_re = pmag * jnp.cos(taus * ai[None])
    pw_im = pmag * jnp.sin(taus * ai[None])
    nr, ni = pw_re[1] - 1.0, pw_im[1]
    den = a_re * a_re + a_im * a_im
    q_re = (nr * a_re + ni * a_im) / den
    q_im = (ni * a_re - nr * a_im) / den
    gp = S5_GROUPS * S5_STATE
    bb_re = (q_re[:, :, None] * b_re - q_im[:, :, None] * b_im).reshape(gp, S5_GROUP)
    bb_im = (q_re[:, :, None] * b_im + q_im[:, :, None] * b_re).reshape(gp, S5_GROUP)
    padrows = ((0, S5_GROUP - (S5_T + 1)), (0, 0))
    pw_re2 = jnp.pad(pw_re.reshape(S5_T + 1, gp), padrows)
    pw_im2 = jnp.pad(pw_im.reshape(S5_T + 1, gp), padrows)
    lam = jnp.stack([pw_re[S5_T].reshape(S5_NBLK, S5_SW), pw_im[S5_T].reshape(S5_NBLK, S5_SW)], axis=1)
    return bb_re, bb_im, pw_re2.T, pw_im2.T, pw_re2, pw_im2, lam


def _mlstm_kernel(qk_ref, v_ref, o_ref, if_ref, cw_ref, cb_ref, bif_ref, ng_ref,
                  buf0_ref, c0_ref, n0_ref, m0_ref,
                  y_ref, bufo_ref, co_ref, no_ref, mo_ref, xp_ref, *, nb, lc, carry):
    if carry:
        @pl.when(pl.program_id(1) == 0)
        def _():
            bufo_ref[...] = buf0_ref[...]
            co_ref[...] = c0_ref[...]
            no_ref[...] = n0_ref[...]
            mo_ref[...] = m0_ref[...]
        bufs_ref, cs_ref, ns_ref, ms_ref = bufo_ref, co_ref, no_ref, mo_ref
    else:
        bufs_ref, cs_ref, ns_ref, ms_ref = buf0_ref, c0_ref, n0_ref, m0_ref

    lp = max(lc, LANES)
    pad = CONV_W - 1
    heads = range(HEADS)
    head_row = lax.broadcasted_iota(jnp.int32, (HEADS, DQK), 0)
    head_lane = lax.broadcasted_iota(jnp.int32, (1, HEADS), 1)
    row = lax.broadcasted_iota(jnp.int32, (lc, lc), 0)
    col = lax.broadcasted_iota(jnp.int32, (lc, lc), 1)
    causal = row >= col
    tril = jnp.where(causal, 1.0, 0.0).astype(BF16)
    lane = lax.broadcasted_iota(jnp.int32, (lc, LANES), 1)
    is_i = lane < HEADS
    is_f = jnp.logical_and(lane >= HEADS, lane < 2 * HEADS)

    for s in range(nb):
        r0 = s * lc
        x1 = SUBLANES
        xp_ref[s, 0:x1 - pad, :] = jnp.zeros((x1 - pad, QK_WIDTH), F32)
        xp_ref[s, x1 - pad:x1, :] = bufs_ref[s]
        xp_ref[s, x1:x1 + lc, :] = qk_ref[r0:r0 + lc, :]
        xp = xp_ref[s]
        conv = cb_ref[...] + xp[x1:x1 + lc] * cw_ref[pad:pad + 1, :]
        for j in range(1, CONV_W):
            conv = conv + pltpu.roll(xp, j, 0)[x1:x1 + lc] * cw_ref[pad - j:pad - j + 1, :]
        bufo_ref[s] = xp_ref[s, x1 + lc - pad:x1 + lc, :]
        n_all = ns_ref[s]
        m_all = ms_ref[s]
        qk = conv * jax.nn.sigmoid(conv)
        q = qk[:, :QK_WIDTH // 2] * (DQK ** -0.5)
        k = qk[:, QK_WIDTH // 2:]

        gate = if_ref[r0:r0 + lc, :] + bif_ref[...]
        lf = jnp.where(is_f, jax.nn.log_sigmoid(gate), 0.0)
        parts = _split3(lf)
        bcum = _dot(tril, parts[0]) + _dot(tril, parts[1]) + _dot(tril, parts[2])
        pc = jnp.where(is_i, gate, bcum)
        if lc < lp:
            pc_t = jnp.concatenate([pc, jnp.zeros((lp - lc, LANES), F32)], axis=0).T[:, :lc]
        else:
            pc_t = pc.T

        i_col = [pc[:, h:h + 1] for h in heads]
        b_col = [pc[:, HEADS + h:HEADS + h + 1] for h in heads]
        m0 = [m_all[:, h:h + 1] for h in heads]
        qh = [q[:, h * DQK:(h + 1) * DQK] for h in heads]
        kh = [k[:, h * DQK:(h + 1) * DQK] for h in heads]
        qb = [x.astype(BF16) for x in qh]
        vb = [v_ref[r0:r0 + lc, h * DV:(h + 1) * DV].astype(BF16) for h in heads]
        c_prev = [cs_ref[s, h] for h in heads]
        n_prev = [n_all[h:h + 1, :] for h in heads]

        qk_t = [_dot_nt(qb[h], kh[h].astype(BF16)) for h in heads]
        carried = [_dot_nt(qb[h], c_prev[h].astype(BF16)) for h in heads]
        logw = [jnp.where(causal, b_col[h] - pc_t[HEADS + h:HEADS + h + 1, :] + pc_t[h:h + 1, :], -jnp.inf)
                for h in heads]
        g = [b_col[h] + m0[h] for h in heads]
        m = [jnp.maximum(g[h], jnp.max(logw[h], axis=-1, keepdims=True)) for h in heads]
        inter = [jnp.exp(g[h] - m[h]) for h in heads]
        sc = [qk_t[h] * jnp.exp(logw[h] - m[h]) for h in heads]
        num = [_dot(sc[h].astype(BF16), vb[h]) + inter[h] * carried[h] for h in heads]
        den = [jnp.sum(sc[h], axis=-1, keepdims=True)
               + inter[h] * jnp.sum(qh[h] * n_prev[h], axis=-1, keepdims=True) for h in heads]
        hh = [num[h] / jnp.maximum(jnp.abs(den[h]), jnp.exp(-m[h])) for h in heads]
        hh = [x * lax.rsqrt(jnp.mean(x * x, axis=-1, keepdims=True) + EPS) for x in hh]
        for h in heads:
            og = jax.nn.sigmoid(o_ref[r0:r0 + lc, h * DV:(h + 1) * DV])
            y_ref[r0:r0 + lc, h * DV:(h + 1) * DV] = hh[h] * ng_ref[:, h * DV:(h + 1) * DV] * og

        b_last = [x[lc - 1:lc, :] for x in b_col]
        m_new = [x[lc - 1:lc, :] for x in m]
        decay = [jnp.exp(b_last[h] + m0[h] - m_new[h]) for h in heads]
        kw = [kh[h] * jnp.exp(b_last[h] - b_col[h] + i_col[h] - m_new[h]) for h in heads]
        upd = [_dot_tn(vb[h], kw[h].astype(BF16)) for h in heads]
        n_new_all = jnp.zeros((HEADS, DQK), F32)
        m_new_all = jnp.zeros((1, HEADS), F32)
        for h in heads:
            co_ref[s, h] = decay[h] * c_prev[h] + upd[h]
            n_new = decay[h] * n_prev[h] + jnp.sum(kw[h], axis=0, keepdims=True)
            n_new_all = jnp.where(head_row == h, n_new, n_new_all)
            m_new_all = jnp.where(head_lane == h, m_new[h], m_new_all)
        no_ref[s] = n_new_all
        mo_ref[s] = m_new_all


def _mlstm(z, zif, cw, cb, bif, ng, buf0, c0, n0, m0, *, nseq, seqlen, nb, lc):
    tok = nseq * seqlen
    nchunk = seqlen // lc
    rows = nb * lc
    tok_map = lambda col: (lambda i, c: (i * nchunk + c, col))
    const2 = lambda i, c: (0, 0)
    st3 = lambda i, c: (i, 0, 0)
    st4 = lambda i, c: (i, 0, 0, 0)
    kern = functools.partial(_mlstm_kernel, nb=nb, lc=lc, carry=nchunk > 1)
    return pl.pallas_call(
        kern,
        grid=(nseq // nb, nchunk),
        in_specs=[
            pl.BlockSpec((rows, QK_WIDTH), tok_map(1)),
            pl.BlockSpec((rows, V_WIDTH), tok_map(2)),
            pl.BlockSpec((rows, V_WIDTH), tok_map(3)),
            pl.BlockSpec((rows, LANES), tok_map(0)),
            pl.BlockSpec((CONV_W, QK_WIDTH), const2),
            pl.BlockSpec((1, QK_WIDTH), const2),
            pl.BlockSpec((1, LANES), const2),
            pl.BlockSpec((1, V_WIDTH), const2),
            pl.BlockSpec((nb, CONV_W - 1, QK_WIDTH), st3),
            pl.BlockSpec((nb, HEADS, DV, DQK), st4),
            pl.BlockSpec((nb, HEADS, DQK), st3),
            pl.BlockSpec((nb, 1, HEADS), st3),
        ],
        out_specs=[
            pl.BlockSpec((rows, V_WIDTH), tok_map(0)),
            pl.BlockSpec((nb, CONV_W - 1, QK_WIDTH), st3),
            pl.BlockSpec((nb, HEADS, DV, DQK), st4),
            pl.BlockSpec((nb, HEADS, DQK), st3),
            pl.BlockSpec((nb, 1, HEADS), st3),
        ],
        out_shape=[
            jax.ShapeDtypeStruct((tok, V_WIDTH), F32),
            jax.ShapeDtypeStruct((nseq, CONV_W - 1, QK_WIDTH), F32),
            jax.ShapeDtypeStruct((nseq, HEADS, DV, DQK), F32),
            jax.ShapeDtypeStruct((nseq, HEADS, DQK), F32),
            jax.ShapeDtypeStruct((nseq, 1, HEADS), F32),
        ],
        scratch_shapes=[pltpu.VMEM((nb, lc + SUBLANES, QK_WIDTH), F32)],
        compiler_params=_cparams(("parallel", "arbitrary")),
        name="mlstm",
    )(z, z, z, zif, cw, cb, bif, ng, buf0, c0, n0, m0)


def _mlstm_step_kernel(qk_ref, v_ref, o_ref, if_ref, cw_ref, cb_ref, bif_ref, ng_ref,
                       buf0_ref, c0_ref, n0x_ref, m0x_ref,
                       y_ref, bufo_ref, co_ref, nox_ref, mox_ref, xp_ref, conv_ref, *, nb, lc):
    rows = nb * lc
    lg = lc.bit_length() - 1
    i32 = jnp.int32
    row = lax.broadcasted_iota(i32, (rows, rows), 0)
    col = lax.broadcasted_iota(i32, (rows, rows), 1)
    same = (row >> lg) == (col >> lg)
    causal = jnp.logical_and(same, row >= col)
    same_b = jnp.where(same, 1.0, 0.0).astype(BF16)
    tril_b = jnp.where(causal, 1.0, 0.0).astype(BF16)
    lane = lax.broadcasted_iota(i32, (rows, LANES), 1)
    is_i = lane < HEADS
    is_f = jnp.logical_and(lane >= HEADS, lane < 2 * HEADS)

    def seg_dot(mat, x):
        hi, mid, lo = _split3(x)
        return _dot(mat, hi) + _dot(mat, mid) + _dot(mat, lo)

    x0 = SUBLANES - (CONV_W - 1)
    for s in range(nb):
        xp_ref[s, x0:SUBLANES, :] = buf0_ref[s]
        xp_ref[s, SUBLANES:SUBLANES + lc, :] = qk_ref[s * lc:(s + 1) * lc, :]
        conv_ref[s * lc:(s + 1) * lc, :] = cb_ref[...] + sum(
            xp_ref[s, x0 + j:x0 + j + lc, :] * cw_ref[j:j + 1, :] for j in range(CONV_W))
        bufo_ref[s] = xp_ref[s, x0 + lc:SUBLANES + lc, :]
    conv = conv_ref[...]
    qk = conv * jax.nn.sigmoid(conv)
    q = qk[:, :QK_WIDTH // 2] * (DQK ** -0.5)
    k = qk[:, QK_WIDTH // 2:]

    gate = if_ref[...] + bif_ref[...]
    lf = jnp.where(is_f, jax.nn.log_sigmoid(gate), 0.0)
    bcum = seg_dot(tril_b, lf)
    btot = seg_dot(same_b, lf)
    pc = jnp.where(is_i, gate, bcum)
    if rows < LANES:
        pc_t = jnp.concatenate([pc, jnp.zeros((LANES - rows, LANES), F32)], axis=0).T[:, :rows]
    else:
        pc_t = pc.T
    m0x = m0x_ref[...]
    heads = range(HEADS)
    seqs = range(nb)

    i_col = [pc[:, h:h + 1] for h in heads]
    b_col = [pc[:, HEADS + h:HEADS + h + 1] for h in heads]
    i_row = [pc_t[h:h + 1, :] for h in heads]
    b_row = [pc_t[HEADS + h:HEADS + h + 1, :] for h in heads]
    b_last = [btot[:, HEADS + h:HEADS + h + 1] for h in heads]
    m0 = [m0x[:, h:h + 1] for h in heads]
    qh = [q[:, h * DQK:(h + 1) * DQK] for h in heads]
    kh = [k[:, h * DQK:(h + 1) * DQK] for h in heads]
    qb = [x.astype(BF16) for x in qh]
    vb = [v_ref[:, h * DV:(h + 1) * DV].astype(BF16) for h in heads]
    n_prev = [n0x_ref[:, h * DQK:(h + 1) * DQK] for h in heads]

    qk_t = [_dot_nt(qb[h], kh[h].astype(BF16)) for h in heads]
    carried = [jnp.concatenate([_dot_nt(qb[h][s * lc:(s + 1) * lc], c0_ref[s, h].astype(BF16)) for s in seqs],
                               axis=0) for h in heads]
    logw = [jnp.where(causal, b_col[h] - b_row[h] + i_row[h], -jnp.inf) for h in heads]
    g = [b_col[h] + m0[h] for h in heads]
    m = [jnp.maximum(g[h], jnp.max(logw[h], axis=-1, keepdims=True)) for h in heads]
    inter = [jnp.exp(g[h] - m[h]) for h in heads]
    lw_end = [jnp.where(same, b_last[h] - b_row[h] + i_row[h], -jnp.inf) for h in heads]
    m_new = [jnp.maximum(b_last[h] + m0[h], jnp.max(lw_end[h], axis=-1, keepdims=True)) for h in heads]
    decay = [jnp.exp(b_last[h] + m0[h] - m_new[h]) for h in heads]
    sc = [qk_t[h] * jnp.exp(logw[h] - m[h]) for h in heads]
    num = [_dot(sc[h].astype(BF16), vb[h]) + inter[h] * carried[h] for h in heads]
    den = [jnp.sum(sc[h], axis=-1, keepdims=True)
           + inter[h] * jnp.sum(qh[h] * n_prev[h], axis=-1, keepdims=True) for h in heads]
    hh = [num[h] / jnp.maximum(jnp.abs(den[h]), jnp.exp(-m[h])) for h in heads]
    hh = [x * lax.rsqrt(jnp.mean(x * x, axis=-1, keepdims=True) + EPS) for x in hh]
    for h in heads:
        og = jax.nn.sigmoid(o_ref[:, h * DV:(h + 1) * DV])
        y_ref[:, h * DV:(h + 1) * DV] = hh[h] * ng_ref[:, h * DV:(h + 1) * DV] * og

    kw = [kh[h] * jnp.exp(b_last[h] - b_col[h] + i_col[h] - m_new[h]) for h in heads]
    kwb = [x.astype(BF16) for x in kw]
    upd = [[_dot_tn(vb[h][s * lc:(s + 1) * lc], kwb[h][s * lc:(s + 1) * lc]) for s in seqs] for h in heads]
    mox = jnp.zeros((rows, LANES), F32)
    for h in heads:
        for s in seqs:
            co_ref[s, h] = decay[h][s * lc:s * lc + 1, :] * c0_ref[s, h] + upd[h][s]
        nox_ref[:, h * DQK:(h + 1) * DQK] = decay[h] * n_prev[h] + seg_dot(same_b, kw[h])
        mox = jnp.where(lane == h, m_new[h], mox)
    mox_ref[...] = mox


def _mlstm_step_part(z, zif, cw, cb, bif, ng, buf0, c0, n0, m0, *, nseq, seqlen, nb):
    lc = seqlen
    assert lc & (lc - 1) == 0 and lc >= CONV_W - 1 and nseq % nb == 0 and (nb * lc) % SUBLANES == 0
    tok = nseq * seqlen
    rows = nb * lc
    n0x = jnp.repeat(n0.reshape(nseq, HEADS * DQK), lc, axis=0)
    m0x = jnp.repeat(jnp.pad(m0, ((0, 0), (0, LANES - HEADS))), lc, axis=0)
    tok_map = lambda col: (lambda i: (i, col))
    const2 = lambda i: (0, 0)
    st3 = lambda i: (i, 0, 0)
    st4 = lambda i: (i, 0, 0, 0)
    part = _Part(
        kernel=functools.partial(_mlstm_step_kernel, nb=nb, lc=lc),
        in_specs=[
            pl.BlockSpec((rows, QK_WIDTH), tok_map(1)),
            pl.BlockSpec((rows, V_WIDTH), tok_map(2)),
            pl.BlockSpec((rows, V_WIDTH), tok_map(3)),
            pl.BlockSpec((rows, LANES), tok_map(0)),
            pl.BlockSpec((CONV_W, QK_WIDTH), const2),
            pl.BlockSpec((1, QK_WIDTH), const2),
            pl.BlockSpec((1, LANES), const2),
            pl.BlockSpec((1, V_WIDTH), const2),
            pl.BlockSpec((nb, CONV_W - 1, QK_WIDTH), st3),
            pl.BlockSpec((nb, HEADS, DV, DQK), st4),
            pl.BlockSpec((rows, HEADS * DQK), tok_map(0)),
            pl.BlockSpec((rows, LANES), tok_map(0)),
        ],
        args=[z, z, z, zif, cw, cb, bif, ng, buf0, c0, n0x, m0x],
        out_specs=[
            pl.BlockSpec((rows, V_WIDTH), tok_map(0)),
            pl.BlockSpec((nb, CONV_W - 1, QK_WIDTH), st3),
            pl.BlockSpec((nb, HEADS, DV, DQK), st4),
            pl.BlockSpec((rows, HEADS * DQK), tok_map(0)),
            pl.BlockSpec((rows, LANES), tok_map(0)),
        ],
        out_shape=[
            jax.ShapeDtypeStruct((tok, V_WIDTH), F32),
            jax.ShapeDtypeStruct((nseq, CONV_W - 1, QK_WIDTH), F32),
            jax.ShapeDtypeStruct((nseq, HEADS, DV, DQK), F32),
            jax.ShapeDtypeStruct((tok, HEADS * DQK), F32),
            jax.ShapeDtypeStruct((tok, LANES), F32),
        ],
        scratch=[pltpu.VMEM((nb, lc + SUBLANES, QK_WIDTH), F32), pltpu.VMEM((rows, QK_WIDTH), F32)],
    )

    def finish(y, buf, c, nox, mox):
        return y, buf, c, nox[::lc].reshape(nseq, HEADS, DQK), mox[::lc, :HEADS]

    return part, finish


def _merge_kernel(h_ref, ys_ref, ym_ref, g1_ref, g2_ref, wglu_ref, wbs_ref, wbm_ref, wo_ref, o_ref):
    ys = ys_ref[...]
    glu = ys * jax.nn.sigmoid(_dot(ys.astype(BF16), wglu_ref[...]))
    a = _dot(glu.astype(BF16), wbs_ref[...])
    b = _dot(ym_ref[...].astype(BF16), wbm_ref[...])
    merged = jax.nn.sigmoid(g1_ref[...]) * a + jax.nn.sigmoid(g2_ref[...]) * b
    o_ref[...] = h_ref[...] + _dot(merged.astype(BF16), wo_ref[...])


def _merge_part(h, ys, ym, z, wglu, wbs, wbm, wo, *, n, tm):
    row = lambda i: (i, 0)
    const = lambda i: (0, 0)
    resident = lambda shape: pl.BlockSpec(shape, const, pipeline_mode=pl.Buffered(1))
    return _Part(
        kernel=_merge_kernel,
        in_specs=[
            pl.BlockSpec((tm, D_MODEL), row),
            pl.BlockSpec((tm, S5_WIDTH), row),
            pl.BlockSpec((tm, V_WIDTH), row),
            pl.BlockSpec((tm, D_MODEL), lambda i: (i, 2)),
            pl.BlockSpec((tm, D_MODEL), lambda i: (i, 3)),
            resident((S5_WIDTH, S5_WIDTH)),
            resident((S5_WIDTH, D_MODEL)),
            resident((V_WIDTH, D_MODEL)),
            resident((D_MODEL, D_MODEL)),
        ],
        args=[h, ys, ym, z, z, wglu, wbs, wbm, wo],
        out_specs=[pl.BlockSpec((tm, D_MODEL), row)],
        out_shape=[jax.ShapeDtypeStruct((n, D_MODEL), F32)],
        scratch=[],
    )


def _run_parts(parts, steps, name):
    n_in = [len(p.in_specs) for p in parts]
    n_out = [len(p.out_specs) for p in parts]
    n_scr = [len(p.scratch) for p in parts]

    def body(*refs):
        ins, outs, scr = refs[:sum(n_in)], refs[sum(n_in):sum(n_in) + sum(n_out)], refs[sum(n_in) + sum(n_out):]
        i = o = c = 0
        for p, ni, no, nc in zip(parts, n_in, n_out, n_scr):
            p.kernel(*ins[i:i + ni], *outs[o:o + no], *scr[c:c + nc])
            i, o, c = i + ni, o + no, c + nc

    flat = pl.pallas_call(
        body,
        grid=(steps,),
        in_specs=[sp for p in parts for sp in p.in_specs],
        out_specs=[sp for p in parts for sp in p.out_specs],
        out_shape=[sh for p in parts for sh in p.out_shape],
        scratch_shapes=[sc for p in parts for sc in p.scratch],
        compiler_params=_cparams(("parallel",)),
        name=name,
    )(*[a for p in parts for a in p.args])
    res, o = [], 0
    for no in n_out:
        res.append(list(flat[o:o + no]))
        o += no
    return res


def kernel(x_prompt, x_sample, state_s5_re, state_s5_im, state_mlstm_C, state_mlstm_n, state_mlstm_m,
           state_mlstm_conv, meta_tokens, ffn1_norm, ffn1_w_gate, ffn1_w_up, ffn1_w_down, mix_norm, w_in,
           s5_A_re, s5_A_im, s5_log_dt, s5_B_re, s5_B_im, s5_C_re, s5_C_im, s5_D, s5_w_glu,
           mlstm_conv_w, mlstm_conv_b, mlstm_b_i, mlstm_b_f, mlstm_norm, w_branch_s5, w_branch_mlstm,
           w_out, ffn2_norm, ffn2_w_gate, ffn2_w_up, ffn2_w_down, final_norm):
    nbatch, seq, _ = x_prompt.shape
    nsamp, sseq, _ = x_sample.shape
    l = 0

    w1g, w1u, w1d = ffn1_w_gate[l], ffn1_w_up[l], ffn1_w_down[l]
    w2g, w2u, w2d = ffn2_w_gate[l], ffn2_w_up[l], ffn2_w_down[l]
    o_if = S5_WIDTH + QK_WIDTH + 2 * V_WIDTH
    o_gate = o_if + 2 * HEADS
    wt = w_in[l].T
    w2 = _win_halves(wt, o_gate, tr=WIN_CAST_ROWS)
    wglu, wbs, wbm, wo = (w[l].astype(BF16) for w in (s5_w_glu, w_branch_s5, w_branch_mlstm, w_out))
    g1 = ffn1_norm[l][None]
    gm = mix_norm[l][None]
    g2 = ffn2_norm[l][None]
    gf = final_norm[None]
    bif = jnp.pad(jnp.concatenate([mlstm_b_i[l], mlstm_b_f[l]]), (0, LANES - 2 * HEADS))[None]
    cw = mlstm_conv_w[l]
    cb = mlstm_conv_b[l][None]
    ng = mlstm_norm[l][None]
    d_skip = s5_D[l][None]

    *s5_ops, lam = _s5_operators(s5_A_re[l], s5_A_im[l], s5_log_dt[l], s5_B_re[l], s5_B_im[l])
    gh = S5_GROUPS * S5_GROUP
    w_toe, f_bf, e_bf = _s5_prep(*s5_ops, s5_C_re[l].reshape(gh, S5_STATE), s5_C_im[l].reshape(gh, S5_STATE))

    def front(x, tm):
        n = x.shape[0]
        h1 = _ffn(x, g1, w1g, w1u, w1d, n=n, tm=tm)
        z, zif = _win(h1, gm, w2, wt, o_if=o_if, ngate=N_BRANCH * D_MODEL, tm=tm)
        return h1, z, zif

    def mlstm(z, zif, ml_state, *, nseq, seqlen, lc):
        buf0, c0, n0, m0 = ml_state
        ym, buf, c, n, m = _mlstm(z, zif, cw, cb, bif, ng, buf0, c0, n0, m0.reshape(nseq, 1, HEADS),
                                  nseq=nseq, seqlen=seqlen, nb=1, lc=lc)
        return ym, buf, c, n, m.reshape(nseq, HEADS)

    def merge_part(h1, ys, ym, z, n):
        return _merge_part(h1, ys, ym, z, wglu, wbs, wbm, wo, n=n, tm=MERGE_TM)

    def ffn2(h2, n):
        return _ffn(h2, g2, w2g, w2u, w2d, gf, n=n, tm=FFN_TM)

    ntok_s = nsamp * sseq
    ntok_p = nbatch * seq
    x_sm = jnp.concatenate([x_sample.reshape(ntok_s, D_MODEL), meta_tokens], axis=0)
    h1_sm, z_sm, zif_sm = front(x_sm, ntok_s + N_META)
    h1_p, z_p, zif_p = front(x_prompt.reshape(ntok_p, D_MODEL), FFN_TM)

    z_m = jnp.tile(z_sm[ntok_s:], (nbatch, 1))
    zif_m = jnp.tile(zif_sm[ntok_s:], (nbatch, 1))
    zeros = lambda *s: jnp.zeros((nbatch,) + s, F32)

    ys_p, ys_s, *s5_st = _s5(z_m, z_p, z_sm, w_toe, f_bf, e_bf, lam, d_skip,
                             state_s5_re[l].reshape(nsamp, -1), state_s5_im[l].reshape(nsamp, -1),
                             nbatch=nbatch, len_m=N_META, len_p=seq, nsamp=nsamp, len_s=sseq)
    s5_p, s5_s = s5_st[:2], s5_st[2:]

    _, *ml_m = mlstm(z_m, zif_m, (zeros(CONV_W - 1, QK_WIDTH), zeros(HEADS, DV, DQK), zeros(HEADS, DQK), zeros(HEADS)),
                     nseq=nbatch, seqlen=N_META, lc=N_META)
    ym_p, *ml_p = mlstm(z_p, zif_p, ml_m, nseq=nbatch, seqlen=seq, lc=MLSTM_CHUNK)

    steps_p = ntok_p // MERGE_TM
    step_part, step_finish = _mlstm_step_part(
        z_sm, zif_sm, cw, cb, bif, ng, state_mlstm_conv[l], state_mlstm_C[l], state_mlstm_n[l], state_mlstm_m[l],
        nseq=nsamp, seqlen=sseq, nb=nsamp // steps_p)
    (h2_p,), step_out = _run_parts([merge_part(h1_p, ys_p, ym_p, z_p, ntok_p), step_part], steps_p, "merge_mlstm_step")
    ym_s, *ml_s = step_finish(*step_out)
    (h2_s,), = _run_parts([merge_part(h1_sm, ys_s, ym_s, z_sm, ntok_s)], ntok_s // MERGE_TM, "merge")
    y_p = ffn2(h2_p, ntok_p)
    y_s = ffn2(h2_s, ntok_s)

    def pack(n, s5_st, ml_st):
        buf, c, nn, m = ml_st
        return (s5_st[0].reshape(1, n, S5_GROUPS, S5_STATE), s5_st[1].reshape(1, n, S5_GROUPS, S5_STATE),
                c[None], nn[None], m[None], buf[None])

    return ((y_p.reshape(nbatch, seq, D_MODEL), y_s.reshape(nsamp, sseq, D_MODEL))
            + pack(nbatch, s5_p, ml_p) + pack(nsamp, s5_s, ml_s))
```

```python
import functools
from typing import Callable, NamedTuple

import jax
import jax.numpy as jnp
from jax import lax
from jax.experimental import pallas as pl
from jax.experimental.pallas import tpu as pltpu

F32 = jnp.float32
BF16 = jnp.bfloat16

D_MODEL = 2048
D_FF = 5632
N_META = 16
S5_WIDTH = 1024
S5_GROUP = 16
S5_GROUPS = 64
S5_STATE = 64
HEADS = 4
DQK = 128
DV = 256
QK_WIDTH = 1024
V_WIDTH = 1024
CONV_W = 4
N_BRANCH = 2
EPS = 1e-6

LANES = 128
SUBLANES = 8
MXU_DIM = 256
S5_T = 8
S5_GPB = LANES // S5_GROUP
S5_NBLK = S5_WIDTH // LANES
S5_SW = S5_GPB * S5_STATE
VMEM_LIMIT = 58 * 1024 * 1024
FFN_TM = 1024
MERGE_TM = 256
WIN_CAST_ROWS = 456
S5_ROW_TILE = 256
MLSTM_CHUNK = 256


def _cparams(sem):
    return pltpu.CompilerParams(dimension_semantics=sem, vmem_limit_bytes=VMEM_LIMIT)


def _rmsnorm(x, g):
    ms = jnp.mean(x * x, axis=-1, keepdims=True)
    return (x * lax.rsqrt(ms + EPS)) * g


def _dot(a, b):
    return jnp.dot(a, b, preferred_element_type=F32)


def _dot_nt(a, b):
    return lax.dot_general(a, b, (((1,), (1,)), ((), ())), preferred_element_type=F32)


def _dot_tn(a, b):
    return lax.dot_general(a, b, (((0,), (0,)), ((), ())), preferred_element_type=F32)


def _split3(x):
    hi = x.astype(BF16)
    r = x - hi.astype(F32)
    mid = r.astype(BF16)
    lo = (r - mid.astype(F32)).astype(BF16)
    return hi, mid, lo


class _Part(NamedTuple):
    kernel: Callable
    in_specs: list
    args: list
    out_specs: list
    out_shape: list
    scratch: list


def _ffn_kernel(x_ref, g_ref, wg_ref, wu_ref, wd_ref, *rest, final_norm):
    if final_norm:
        fg_ref, o_ref, xn_ref = rest
    else:
        o_ref, xn_ref = rest
    j = pl.program_id(1)

    @pl.when(j == 0)
    def _():
        xn_ref[...] = _rmsnorm(x_ref[...], g_ref[...]).astype(BF16)
        o_ref[...] = jnp.zeros(o_ref.shape, F32)

    xn = xn_ref[...]
    gt = _dot(xn, wg_ref[...].astype(BF16))
    up = _dot(xn, wu_ref[...].astype(BF16))
    act = (gt * jax.nn.sigmoid(gt) * up).astype(BF16)
    o_ref[...] += _dot(act, wd_ref[...].astype(BF16))

    @pl.when(j == pl.num_programs(1) - 1)
    def _():
        h = x_ref[...] + 0.5 * o_ref[...]
        if final_norm:
            h = _rmsnorm(h, fg_ref[...])
        o_ref[...] = h


def _ffn(x, g, wg, wu, wd, final_g=None, *, n, tm, tf=256):
    in_specs = [
        pl.BlockSpec((tm, D_MODEL), lambda i, j: (i, 0)),
        pl.BlockSpec((1, D_MODEL), lambda i, j: (0, 0)),
        pl.BlockSpec((D_MODEL, tf), lambda i, j: (0, j)),
        pl.BlockSpec((D_MODEL, tf), lambda i, j: (0, j)),
        pl.BlockSpec((tf, D_MODEL), lambda i, j: (j, 0)),
    ]
    args = [x, g, wg, wu, wd]
    if final_g is not None:
        in_specs.append(pl.BlockSpec((1, D_MODEL), lambda i, j: (0, 0)))
        args.append(final_g)
    return pl.pallas_call(
        functools.partial(_ffn_kernel, final_norm=final_g is not None),
        grid=(n // tm, D_FF // tf),
        in_specs=in_specs,
        out_specs=pl.BlockSpec((tm, D_MODEL), lambda i, j: (i, 0)),
        out_shape=jax.ShapeDtypeStruct((n, D_MODEL), F32),
        scratch_shapes=[pltpu.VMEM((tm, D_MODEL), BF16)],
        compiler_params=_cparams(("parallel", "arbitrary")),
        name="ffn",
    )(*args)


def _cast_rows_kernel(w_ref, o_ref, *, tail):
    last = pl.num_programs(0) - 1

    @pl.when(pl.program_id(0) < last)
    def _():
        o_ref[0] = w_ref[...].astype(BF16)

    @pl.when(pl.program_id(0) == last)
    def _():
        o_ref[0, :tail] = w_ref[:tail].astype(BF16)
        o_ref[0, tail:] = jnp.zeros((o_ref.shape[1] - tail, o_ref.shape[2]), BF16)


def _win_halves(wt, o_gate, *, tr):
    rows, k = wt.shape
    per = o_gate // tr
    tail = rows - (2 * per - 1) * tr
    assert per * tr == o_gate and tr % 8 == 0 and 0 < tail < tr and tail % 8 == 0
    return pl.pallas_call(
        functools.partial(_cast_rows_kernel, tail=tail),
        grid=(2 * per,),
        in_specs=[pl.BlockSpec((tr, k), lambda i: (i, 0))],
        out_specs=pl.BlockSpec((1, tr, k), lambda i: (i // per, i % per, 0)),
        out_shape=jax.ShapeDtypeStruct((2, o_gate, k), BF16),
        compiler_params=_cparams(("parallel",)),
        name="w_in_cast",
    )(wt)


def _win_kernel(h_ref, g_ref, wa_ref, wb_ref, wif_ref, z_ref, zif_ref, un_ref, *, na):
    j = pl.program_id(1)

    @pl.when(j == 0)
    def _():
        un = _rmsnorm(h_ref[...], g_ref[...]).astype(BF16)
        un_ref[...] = un
        zif_ref[...] = _dot_nt(un, wif_ref[...].astype(BF16))

    @pl.when(j < na)
    def _():
        z_ref[...] = _dot_nt(un_ref[...], wa_ref[0])

    @pl.when(j >= na)
    def _():
        z_ref[...] = _dot_nt(un_ref[...], wb_ref[0])


def _win(h, g, w2, wt, *, o_if, ngate, tm, tn=1024):
    n = h.shape[0]
    na = o_if // tn
    nb = ngate // tn
    return pl.pallas_call(
        functools.partial(_win_kernel, na=na),
        grid=(n // tm, na + nb),
        in_specs=[
            pl.BlockSpec((tm, D_MODEL), lambda i, j: (i, 0)),
            pl.BlockSpec((1, D_MODEL), lambda i, j: (0, 0)),
            pl.BlockSpec((1, tn, D_MODEL), lambda i, j: (0, jnp.minimum(j, na - 1), 0)),
            pl.BlockSpec((1, tn, D_MODEL), lambda i, j: (1, jnp.maximum(j - na, 0), 0)),
            pl.BlockSpec((LANES, D_MODEL), lambda i, j: (o_if // LANES, 0)),
        ],
        out_specs=[
            pl.BlockSpec((tm, tn), lambda i, j: (i, j)),
            pl.BlockSpec((tm, LANES), lambda i, j: (i, 0)),
        ],
        out_shape=[
            jax.ShapeDtypeStruct((n, (na + nb) * tn), F32),
            jax.ShapeDtypeStruct((n, LANES), F32),
        ],
        scratch_shapes=[pltpu.VMEM((tm, D_MODEL), BF16)],
        compiler_params=_cparams(("parallel", "arbitrary")),
        name="w_in",
    )(h, g, w2, w2, wt)


def _s5_prep_kernel(bbr_ref, bbi_ref, ptr_ref, pti_ref, pwr_ref, pwi_ref, cnr_ref, cni_ref,
                    w_ref, f_ref, e_ref):
    n = S5_T * LANES
    i32 = jnp.int32
    lg_h = S5_GROUP.bit_length() - 1
    lg_p = S5_STATE.bit_length() - 1
    lg_l = LANES.bit_length() - 1
    gmask = S5_GPB - 1

    def tile_mat(k, c, src_of_col):
        kk = lax.broadcasted_iota(i32, (k, c), 0)
        cc = lax.broadcasted_iota(i32, (k, c), 1)
        return jnp.where(kk == src_of_col(cc), 1.0, 0.0).astype(BF16)

    def group_mask(r, c, row_group, col_group):
        rr = lax.broadcasted_iota(i32, (r, c), 0)
        cc = lax.broadcasted_iota(i32, (r, c), 1)
        return row_group(rr) == col_group(cc)

    def tiled(x, mat):
        hi, mid, _ = _split3(x)
        return _dot(hi, mat) + _dot(mid, mat)

    def lag_power(pt):
        return jnp.concatenate(
            [jnp.broadcast_to(pt[:, S5_T - 1 - s:S5_T - s], (S5_SW, LANES)) for s in range(S5_T)], axis=1)

    def dot_hi(a, b):
        a_hi, a_mid, _ = _split3(a)
        b_hi, b_mid, _ = _split3(b)
        return _dot(a_hi, b_hi) + _dot(a_mid, b_hi) + _dot(a_hi, b_mid)

    sel_h = tile_mat(S5_GROUP, n, lambda c: c & (S5_GROUP - 1))
    mf = group_mask(S5_SW, n, lambda r: r >> lg_p, lambda c: (c >> lg_h) & gmask)
    lpr = lag_power(ptr_ref[...])
    lpi = lag_power(pti_ref[...])
    bxr = tiled(bbr_ref[...], sel_h)
    bxi = tiled(bbi_ref[...], sel_h)
    ftr = jnp.where(mf, lpr * bxr - lpi * bxi, 0.0)
    fti = jnp.where(mf, lpr * bxi + lpi * bxr, 0.0)
    f_ref[0, :S5_SW, :] = ftr.astype(BF16)
    f_ref[0, S5_SW:, :] = fti.astype(BF16)

    sel_p = tile_mat(S5_STATE, S5_SW, lambda c: c & (S5_STATE - 1))
    mc = group_mask(LANES, S5_SW, lambda r: r >> lg_h, lambda c: c >> lg_p)
    cxr = jnp.where(mc, tiled(cnr_ref[...], sel_p), 0.0)
    cxi = jnp.where(mc, tiled(cni_ref[...], sel_p), 0.0)

    for t in range(S5_T):
        pr = pwr_ref[t + 1:t + 2, :]
        pi = pwi_ref[t + 1:t + 2, :]
        e_ref[0, t * LANES:(t + 1) * LANES, :S5_SW] = (cxr * pr - cxi * pi).astype(BF16)
        e_ref[0, t * LANES:(t + 1) * LANES, S5_SW:] = (-(cxr * pi + cxi * pr)).astype(BF16)

    cn = jnp.concatenate([cxr, -cxi], axis=1)
    w_ref[...] = jnp.zeros(w_ref.shape, w_ref.dtype)
    k_all = dot_hi(cn, jnp.concatenate([ftr, fti], axis=0)).astype(BF16)
    for lag in range(S5_T):
        c0 = (S5_T - 1 - lag) * LANES
        kt = k_all[:, c0:c0 + LANES]
        for s in range(S5_T - lag):
            t = s + lag
            w_ref[0, t * LANES:(t + 1) * LANES, s * LANES:(s + 1) * LANES] = kt


def _s5_prep(bb_re, bb_im, pwt_re, pwt_im, pw_re, pw_im, cn_re, cn_im):
    n = S5_T * LANES
    rows = lambda r, c: pl.BlockSpec((r, c), lambda j: (j, 0))
    cols = lambda r, c: pl.BlockSpec((r, c), lambda j: (0, j))
    out = pl.BlockSpec((1, n, n), lambda j: (j, 0, 0))
    return pl.pallas_call(
        _s5_prep_kernel,
        grid=(S5_NBLK,),
        in_specs=[rows(S5_SW, S5_GROUP)] * 4 + [cols(S5_GROUP, S5_SW)] * 2 + [rows(LANES, S5_STATE)] * 2,
        out_specs=[out, out, out],
        out_shape=[jax.ShapeDtypeStruct((S5_NBLK, n, n), BF16)] * 3,
        compiler_params=_cparams(("parallel",)),
        name="s5_prep",
    )(bb_re, bb_im, pwt_re, pwt_im, pw_re, pw_im, cn_re, cn_im)


def _s5_group(x_ref, y_ref, w, f, e, ar, ai, dt, h0, yi_ref, s_ref, hin_ref, *, nseq, nblk):
    rows = nseq * nblk
    rt = min(rows, S5_ROW_TILE)

    def load_u(r0):
        return jnp.concatenate(
            [x_ref[pl.ds(r0 * S5_T + s, rt, stride=S5_T), :] for s in range(S5_T)], axis=1)

    for r0 in range(0, rows, rt):
        ub = load_u(r0).astype(BF16)
        if y_ref is not None:
            for c0 in range(0, S5_T * LANES, MXU_DIM):
                c1 = c0 + MXU_DIM
                yi_ref[r0:r0 + rt, c0:c1] = _dot_nt(ub[:, :c1], w[c0:c1, :c1])
        s_ref[r0:r0 + rt, :] = _dot_nt(ub, f)

    if nblk == 1:
        hre, him = h0
        hin_ref[0:rows, :S5_SW] = hre
        hin_ref[0:rows, S5_SW:] = him
        fin = (ar * hre - ai * him + s_ref[0:rows, :S5_SW], ar * him + ai * hre + s_ref[0:rows, S5_SW:])
    else:
        def body(c, carry):
            new = []
            for b in range(nseq):
                hre, him = carry[2 * b], carry[2 * b + 1]
                row = b * nblk + c
                hin_ref[pl.ds(row, 1), :S5_SW] = hre
                hin_ref[pl.ds(row, 1), S5_SW:] = him
                sre = s_ref[pl.ds(row, 1), :S5_SW]
                sim = s_ref[pl.ds(row, 1), S5_SW:]
                new.append(ar * hre - ai * him + sre)
                new.append(ar * him + ai * hre + sim)
            return tuple(new)

        flat = lax.fori_loop(0, nblk, body, tuple(v for pair in h0 for v in pair))
        fin = [(flat[2 * b], flat[2 * b + 1]) for b in range(nseq)]

    if y_ref is not None:
        for r0 in range(0, rows, rt):
            yo = _dot_nt(hin_ref[r0:r0 + rt, :].astype(BF16), e)
            y = jax.nn.gelu(yi_ref[r0:r0 + rt, :] + yo + load_u(r0) * dt)
            for t in range(S5_T):
                y_ref[pl.ds(r0 * S5_T + t, rt, stride=S5_T), :] = y[:, t * LANES:(t + 1) * LANES]
    return fin


def _s5_kernel(xm_ref, xp_ref, xs_ref, w_ref, f_ref, e_ref, lam_ref, d_ref, h0re_ref, h0im_ref,
               yp_ref, ys_ref, pre_ref, pim_ref, sre_ref, sim_ref, yi_ref, s_ref, hin_ref,
               *, nbatch, nblk_m, nblk_p, nsamp, nblk_s):
    ar = lam_ref[0, 0:1, :]
    ai = lam_ref[0, 1:2, :]
    dt = jnp.concatenate([d_ref[...]] * S5_T, axis=1)
    ops = (w_ref[0], f_ref[0], e_ref[0], ar, ai, dt)
    scratch = (yi_ref, s_ref, hin_ref)
    zero = jnp.zeros((1, S5_SW), F32)
    st = _s5_group(xm_ref, None, *ops, [(zero, zero)] * nbatch, *scratch, nseq=nbatch, nblk=nblk_m)
    st = _s5_group(xp_ref, yp_ref, *ops, st, *scratch, nseq=nbatch, nblk=nblk_p)
    for b in range(nbatch):
        pre_ref[b:b + 1, :] = st[b][0]
        pim_ref[b:b + 1, :] = st[b][1]
    fin = _s5_group(xs_ref, ys_ref, *ops, (h0re_ref[...], h0im_ref[...]), *scratch, nseq=nsamp, nblk=nblk_s)
    sre_ref[...] = fin[0]
    sim_ref[...] = fin[1]


def _s5(z_m, z_p, z_s, w, f, e, lam, d, h0re, h0im, *, nbatch, len_m, len_p, nsamp, len_s):
    assert len_s == S5_T
    n = S5_T * LANES
    tok_m, tok_p, tok_s = nbatch * len_m, nbatch * len_p, nsamp * len_s
    rows = max(nbatch * len_p, nsamp * len_s, nbatch * len_m) // S5_T
    col = lambda r: pl.BlockSpec((r, LANES), lambda j: (0, j))
    op = pl.BlockSpec((1, n, n), lambda j: (j, 0, 0))
    st = lambda r: pl.BlockSpec((r, S5_SW), lambda j: (0, j))
    gp = S5_GROUPS * S5_STATE
    kern = functools.partial(_s5_kernel, nbatch=nbatch, nblk_m=len_m // S5_T, nblk_p=len_p // S5_T,
                             nsamp=nsamp, nblk_s=len_s // S5_T)
    return pl.pallas_call(
        kern,
        grid=(S5_NBLK,),
        in_specs=[col(tok_m), col(tok_p), col(tok_s), op, op, op,
                  pl.BlockSpec((1, 2, S5_SW), lambda j: (j, 0, 0)), col(1), st(nsamp), st(nsamp)],
        out_specs=[col(tok_p), col(tok_s), st(nbatch), st(nbatch), st(nsamp), st(nsamp)],
        out_shape=[
            jax.ShapeDtypeStruct((tok_p, S5_WIDTH), F32),
            jax.ShapeDtypeStruct((tok_s, S5_WIDTH), F32),
            jax.ShapeDtypeStruct((nbatch, gp), F32),
            jax.ShapeDtypeStruct((nbatch, gp), F32),
            jax.ShapeDtypeStruct((nsamp, gp), F32),
            jax.ShapeDtypeStruct((nsamp, gp), F32),
        ],
        scratch_shapes=[pltpu.VMEM((rows, n), F32)] * 3,
        compiler_params=_cparams(("parallel",)),
        name="s5",
    )(z_m, z_p, z_s, w, f, e, lam, d, h0re, h0im)


def _s5_operators(a_re, a_im, log_dt, b_re, b_im):
    dt = jnp.exp(log_dt)[:, None]
    ar, ai = a_re * dt, a_im * dt
    taus = jnp.arange(S5_T + 1, dtype=F32)[:, None, None]
    pmag = jnp.exp(taus * ar[None])
    pw_re = pmag * jnp.cos(taus * ai[None])
    pw_im = pmag * jnp.sin(taus * ai[None])
    nr, ni = pw_re[1] - 1.0, pw_im[1]
    den = a_re * a_re + a_im * a_im
    q_re = (nr * a_re + ni * a_im) / den
    q_im = (ni * a_re - nr * a_im) / den
    gp = S5_GROUPS * S5_STATE
    bb_re = (q_re[:, :, None] * b_re - q_im[:, :, None] * b_im).reshape(gp, S5_GROUP)
    bb_im = (q_re[:, :, None] * b_im + q_im[:, :, None] * b_re).reshape(gp, S5_GROUP)
    padrows = ((0, S5_GROUP - (S5_T + 1)), (0, 0))
    pw_re2 = jnp.pad(pw_re.reshape(S5_T + 1, gp), padrows)
    pw_im2 = jnp.pad(pw_im.reshape(S5_T + 1, gp), padrows)
    lam = jnp.stack([pw_re[S5_T].reshape(S5_NBLK, S5_SW), pw_im[S5_T].reshape(S5_NBLK, S5_SW)], axis=1)
    return bb_re, bb_im, pw_re2.T, pw_im2.T, pw_re2, pw_im2, lam


def _mlstm_kernel(qk_ref, v_ref, o_ref, if_ref, cw_ref, cb_ref, bif_ref, ng_ref,
                  buf0_ref, c0_ref, n0_ref, m0_ref,
                  y_ref, bufo_ref, co_ref, no_ref, mo_ref, xp_ref, *, nb, lc, carry):
    if carry:
        @pl.when(pl.program_id(1) == 0)
        def _():
            bufo_ref[...] = buf0_ref[...]
            co_ref[...] = c0_ref[...]
            no_ref[...] = n0_ref[...]
            mo_ref[...] = m0_ref[...]
        bufs_ref, cs_ref, ns_ref, ms_ref = bufo_ref, co_ref, no_ref, mo_ref
    else:
        bufs_ref, cs_ref, ns_ref, ms_ref = buf0_ref, c0_ref, n0_ref, m0_ref

    lp = max(lc, LANES)
    pad = CONV_W - 1
    heads = range(HEADS)
    head_row = lax.broadcasted_iota(jnp.int32, (HEADS, DQK), 0)
    head_lane = lax.broadcasted_iota(jnp.int32, (1, HEADS), 1)
    row = lax.broadcasted_iota(jnp.int32, (lc, lc), 0)
    col = lax.broadcasted_iota(jnp.int32, (lc, lc), 1)
    causal = row >= col
    tril = jnp.where(causal, 1.0, 0.0).astype(BF16)
    lane = lax.broadcasted_iota(jnp.int32, (lc, LANES), 1)
    is_i = lane < HEADS
    is_f = jnp.logical_and(lane >= HEADS, lane < 2 * HEADS)

    for s in range(nb):
        r0 = s * lc
        x1 = SUBLANES
        xp_ref[s, 0:x1 - pad, :] = jnp.zeros((x1 - pad, QK_WIDTH), F32)
        xp_ref[s, x1 - pad:x1, :] = bufs_ref[s]
        xp_ref[s, x1:x1 + lc, :] = qk_ref[r0:r0 + lc, :]
        xp = xp_ref[s]
        conv = cb_ref[...] + xp[x1:x1 + lc] * cw_ref[pad:pad + 1, :]
        for j in range(1, CONV_W):
            conv = conv + pltpu.roll(xp, j, 0)[x1:x1 + lc] * cw_ref[pad - j:pad - j + 1, :]
        bufo_ref[s] = xp_ref[s, x1 + lc - pad:x1 + lc, :]
        n_all = ns_ref[s]
        m_all = ms_ref[s]
        qk = conv * jax.nn.sigmoid(conv)
        q = qk[:, :QK_WIDTH // 2] * (DQK ** -0.5)
        k = qk[:, QK_WIDTH // 2:]

        gate = if_ref[r0:r0 + lc, :] + bif_ref[...]
        lf = jnp.where(is_f, jax.nn.log_sigmoid(gate), 0.0)
        parts = _split3(lf)
        bcum = _dot(tril, parts[0]) + _dot(tril, parts[1]) + _dot(tril, parts[2])
        pc = jnp.where(is_i, gate, bcum)
        if lc < lp:
            pc_t = jnp.concatenate([pc, jnp.zeros((lp - lc, LANES), F32)], axis=0).T[:, :lc]
        else:
            pc_t = pc.T

        i_col = [pc[:, h:h + 1] for h in heads]
        b_col = [pc[:, HEADS + h:HEADS + h + 1] for h in heads]
        m0 = [m_all[:, h:h + 1] for h in heads]
        qh = [q[:, h * DQK:(h + 1) * DQK] for h in heads]
        kh = [k[:, h * DQK:(h + 1) * DQK] for h in heads]
        qb = [x.astype(BF16) for x in qh]
        vb = [v_ref[r0:r0 + lc, h * DV:(h + 1) * DV].astype(BF16) for h in heads]
        c_prev = [cs_ref[s, h] for h in heads]
        n_prev = [n_all[h:h + 1, :] for h in heads]

        qk_t = [_dot_nt(qb[h], kh[h].astype(BF16)) for h in heads]
        carried = [_dot_nt(qb[h], c_prev[h].astype(BF16)) for h in heads]
        logw = [jnp.where(causal, b_col[h] - pc_t[HEADS + h:HEADS + h + 1, :] + pc_t[h:h + 1, :], -jnp.inf)
                for h in heads]
        g = [b_col[h] + m0[h] for h in heads]
        m = [jnp.maximum(g[h], jnp.max(logw[h], axis=-1, keepdims=True)) for h in heads]
        inter = [jnp.exp(g[h] - m[h]) for h in heads]
        sc = [qk_t[h] * jnp.exp(logw[h] - m[h]) for h in heads]
        num = [_dot(sc[h].astype(BF16), vb[h]) + inter[h] * carried[h] for h in heads]
        den = [jnp.sum(sc[h], axis=-1, keepdims=True)
               + inter[h] * jnp.sum(qh[h] * n_prev[h], axis=-1, keepdims=True) for h in heads]
        hh = [num[h] / jnp.maximum(jnp.abs(den[h]), jnp.exp(-m[h])) for h in heads]
        hh = [x * lax.rsqrt(jnp.mean(x * x, axis=-1, keepdims=True) + EPS) for x in hh]
        for h in heads:
            og = jax.nn.sigmoid(o_ref[r0:r0 + lc, h * DV:(h + 1) * DV])
            y_ref[r0:r0 + lc, h * DV:(h + 1) * DV] = hh[h] * ng_ref[:, h * DV:(h + 1) * DV] * og

        b_last = [x[lc - 1:lc, :] for x in b_col]
        m_new = [x[lc - 1:lc, :] for x in m]
        decay = [jnp.exp(b_last[h] + m0[h] - m_new[h]) for h in heads]
        kw = [kh[h] * jnp.exp(b_last[h] - b_col[h] + i_col[h] - m_new[h]) for h in heads]
        upd = [_dot_tn(vb[h], kw[h].astype(BF16)) for h in heads]
        n_new_all = jnp.zeros((HEADS, DQK), F32)
        m_new_all = jnp.zeros((1, HEADS), F32)
        for h in heads:
            co_ref[s, h] = decay[h] * c_prev[h] + upd[h]
            n_new = decay[h] * n_prev[h] + jnp.sum(kw[h], axis=0, keepdims=True)
            n_new_all = jnp.where(head_row == h, n_new, n_new_all)
            m_new_all = jnp.where(head_lane == h, m_new[h], m_new_all)
        no_ref[s] = n_new_all
        mo_ref[s] = m_new_all


def _mlstm(z, zif, cw, cb, bif, ng, buf0, c0, n0, m0, *, nseq, seqlen, nb, lc):
    tok = nseq * seqlen
    nchunk = seqlen // lc
    rows = nb * lc
    tok_map = lambda col: (lambda i, c: (i * nchunk + c, col))
    const2 = lambda i, c: (0, 0)
    st3 = lambda i, c: (i, 0, 0)
    st4 = lambda i, c: (i, 0, 0, 0)
    kern = functools.partial(_mlstm_kernel, nb=nb, lc=lc, carry=nchunk > 1)
    return pl.pallas_call(
        kern,
        grid=(nseq // nb, nchunk),
        in_specs=[
            pl.BlockSpec((rows, QK_WIDTH), tok_map(1)),
            pl.BlockSpec((rows, V_WIDTH), tok_map(2)),
            pl.BlockSpec((rows, V_WIDTH), tok_map(3)),
            pl.BlockSpec((rows, LANES), tok_map(0)),
            pl.BlockSpec((CONV_W, QK_WIDTH), const2),
            pl.BlockSpec((1, QK_WIDTH), const2),
            pl.BlockSpec((1, LANES), const2),
            pl.BlockSpec((1, V_WIDTH), const2),
            pl.BlockSpec((nb, CONV_W - 1, QK_WIDTH), st3),
            pl.BlockSpec((nb, HEADS, DV, DQK), st4),
            pl.BlockSpec((nb, HEADS, DQK), st3),
            pl.BlockSpec((nb, 1, HEADS), st3),
        ],
        out_specs=[
            pl.BlockSpec((rows, V_WIDTH), tok_map(0)),
            pl.BlockSpec((nb, CONV_W - 1, QK_WIDTH), st3),
            pl.BlockSpec((nb, HEADS, DV, DQK), st4),
            pl.BlockSpec((nb, HEADS, DQK), st3),
            pl.BlockSpec((nb, 1, HEADS), st3),
        ],
        out_shape=[
            jax.ShapeDtypeStruct((tok, V_WIDTH), F32),
            jax.ShapeDtypeStruct((nseq, CONV_W - 1, QK_WIDTH), F32),
            jax.ShapeDtypeStruct((nseq, HEADS, DV, DQK), F32),
            jax.ShapeDtypeStruct((nseq, HEADS, DQK), F32),
            jax.ShapeDtypeStruct((nseq, 1, HEADS), F32),
        ],
        scratch_shapes=[pltpu.VMEM((nb, lc + SUBLANES, QK_WIDTH), F32)],
        compiler_params=_cparams(("parallel", "arbitrary")),
        name="mlstm",
    )(z, z, z, zif, cw, cb, bif, ng, buf0, c0, n0, m0)


def _mlstm_step_kernel(qk_ref, v_ref, o_ref, if_ref, cw_ref, cb_ref, bif_ref, ng_ref,
                       buf0_ref, c0_ref, n0x_ref, m0x_ref,
                       y_ref, bufo_ref, co_ref, nox_ref, mox_ref, xp_ref, conv_ref, *, nb, lc):
    rows = nb * lc
    lg = lc.bit_length() - 1
    i32 = jnp.int32
    row = lax.broadcasted_iota(i32, (rows, rows), 0)
    col = lax.broadcasted_iota(i32, (rows, rows), 1)
    same = (row >> lg) == (col >> lg)
    causal = jnp.logical_and(same, row >= col)
    same_b = jnp.where(same, 1.0, 0.0).astype(BF16)
    tril_b = jnp.where(causal, 1.0, 0.0).astype(BF16)
    lane = lax.broadcasted_iota(i32, (rows, LANES), 1)
    is_i = lane < HEADS
    is_f = jnp.logical_and(lane >= HEADS, lane < 2 * HEADS)

    def seg_dot(mat, x):
        hi, mid, lo = _split3(x)
        return _dot(mat, hi) + _dot(mat, mid) + _dot(mat, lo)

    x0 = SUBLANES - (CONV_W - 1)
    for s in range(nb):
        xp_ref[s, x0:SUBLANES, :] = buf0_ref[s]
        xp_ref[s, SUBLANES:SUBLANES + lc, :] = qk_ref[s * lc:(s + 1) * lc, :]
        conv_ref[s * lc:(s + 1) * lc, :] = cb_ref[...] + sum(
            xp_ref[s, x0 + j:x0 + j + lc, :] * cw_ref[j:j + 1, :] for j in range(CONV_W))
        bufo_ref[s] = xp_ref[s, x0 + lc:SUBLANES + lc, :]
    conv = conv_ref[...]
    qk = conv * jax.nn.sigmoid(conv)
    q = qk[:, :QK_WIDTH // 2] * (DQK ** -0.5)
    k = qk[:, QK_WIDTH // 2:]

    gate = if_ref[...] + bif_ref[...]
    lf = jnp.where(is_f, jax.nn.log_sigmoid(gate), 0.0)
    bcum = seg_dot(tril_b, lf)
    btot = seg_dot(same_b, lf)
    pc = jnp.where(is_i, gate, bcum)
    if rows < LANES:
        pc_t = jnp.concatenate([pc, jnp.zeros((LANES - rows, LANES), F32)], axis=0).T[:, :rows]
    else:
        pc_t = pc.T
    m0x = m0x_ref[...]
    heads = range(HEADS)
    seqs = range(nb)

    i_col = [pc[:, h:h + 1] for h in heads]
    b_col = [pc[:, HEADS + h:HEADS + h + 1] for h in heads]
    i_row = [pc_t[h:h + 1, :] for h in heads]
    b_row = [pc_t[HEADS + h:HEADS + h + 1, :] for h in heads]
    b_last = [btot[:, HEADS + h:HEADS + h + 1] for h in heads]
    m0 = [m0x[:, h:h + 1] for h in heads]
    qh = [q[:, h * DQK:(h + 1) * DQK] for h in heads]
    kh = [k[:, h * DQK:(h + 1) * DQK] for h in heads]
    qb = [x.astype(BF16) for x in qh]
    vb = [v_ref[:, h * DV:(h + 1) * DV].astype(BF16) for h in heads]
    n_prev = [n0x_ref[:, h * DQK:(h + 1) * DQK] for h in heads]

    qk_t = [_dot_nt(qb[h], kh[h].astype(BF16)) for h in heads]
    carried = [jnp.concatenate([_dot_nt(qb[h][s * lc:(s + 1) * lc], c0_ref[s, h].astype(BF16)) for s in seqs],
                               axis=0) for h in heads]
    logw = [jnp.where(causal, b_col[h] - b_row[h] + i_row[h], -jnp.inf) for h in heads]
    g = [b_col[h] + m0[h] for h in heads]
    m = [jnp.maximum(g[h], jnp.max(logw[h], axis=-1, keepdims=True)) for h in heads]
    inter = [jnp.exp(g[h] - m[h]) for h in heads]
    lw_end = [jnp.where(same, b_last[h] - b_row[h] + i_row[h], -jnp.inf) for h in heads]
    m_new = [jnp.maximum(b_last[h] + m0[h], jnp.max(lw_end[h], axis=-1, keepdims=True)) for h in heads]
    decay = [jnp.exp(b_last[h] + m0[h] - m_new[h]) for h in heads]
    sc = [qk_t[h] * jnp.exp(logw[h] - m[h]) for h in heads]
    num = [_dot(sc[h].astype(BF16), vb[h]) + inter[h] * carried[h] for h in heads]
    den = [jnp.sum(sc[h], axis=-1, keepdims=True)
           + inter[h] * jnp.sum(qh[h] * n_prev[h], axis=-1, keepdims=True) for h in heads]
    hh = [num[h] / jnp.maximum(jnp.abs(den[h]), jnp.exp(-m[h])) for h in heads]
    hh = [x * lax.rsqrt(jnp.mean(x * x, axis=-1, keepdims=True) + EPS) for x in hh]
    for h in heads:
        og = jax.nn.sigmoid(o_ref[:, h * DV:(h + 1) * DV])
        y_ref[:, h * DV:(h + 1) * DV] = hh[h] * ng_ref[:, h * DV:(h + 1) * DV] * og

    kw = [kh[h] * jnp.exp(b_last[h] - b_col[h] + i_col[h] - m_new[h]) for h in heads]
    kwb = [x.astype(BF16) for x in kw]
    upd = [[_dot_tn(vb[h][s * lc:(s + 1) * lc], kwb[h][s * lc:(s + 1) * lc]) for s in seqs] for h in heads]
    mox = jnp.zeros((rows, LANES), F32)
    for h in heads:
        for s in seqs:
            co_ref[s, h] = decay[h][s * lc:s * lc + 1, :] * c0_ref[s, h] + upd[h][s]
        nox_ref[:, h * DQK:(h + 1) * DQK] = decay[h] * n_prev[h] + seg_dot(same_b, kw[h])
        mox = jnp.where(lane == h, m_new[h], mox)
    mox_ref[...] = mox


def _mlstm_step_part(z, zif, cw, cb, bif, ng, buf0, c0, n0, m0, *, nseq, seqlen, nb):
    lc = seqlen
    assert lc & (lc - 1) == 0 and lc >= CONV_W - 1 and nseq % nb == 0 and (nb * lc) % SUBLANES == 0
    tok = nseq * seqlen
    rows = nb * lc
    n0x = jnp.repeat(n0.reshape(nseq, HEADS * DQK), lc, axis=0)
    m0x = jnp.repeat(jnp.pad(m0, ((0, 0), (0, LANES - HEADS))), lc, axis=0)
    tok_map = lambda col: (lambda i: (i, col))
    const2 = lambda i: (0, 0)
    st3 = lambda i: (i, 0, 0)
    st4 = lambda i: (i, 0, 0, 0)
    part = _Part(
        kernel=functools.partial(_mlstm_step_kernel, nb=nb, lc=lc),
        in_specs=[
            pl.BlockSpec((rows, QK_WIDTH), tok_map(1)),
            pl.BlockSpec((rows, V_WIDTH), tok_map(2)),
            pl.BlockSpec((rows, V_WIDTH), tok_map(3)),
            pl.BlockSpec((rows, LANES), tok_map(0)),
            pl.BlockSpec((CONV_W, QK_WIDTH), const2),
            pl.BlockSpec((1, QK_WIDTH), const2),
            pl.BlockSpec((1, LANES), const2),
            pl.BlockSpec((1, V_WIDTH), const2),
            pl.BlockSpec((nb, CONV_W - 1, QK_WIDTH), st3),
            pl.BlockSpec((nb, HEADS, DV, DQK), st4),
            pl.BlockSpec((rows, HEADS * DQK), tok_map(0)),
            pl.BlockSpec((rows, LANES), tok_map(0)),
        ],
        args=[z, z, z, zif, cw, cb, bif, ng, buf0, c0, n0x, m0x],
        out_specs=[
            pl.BlockSpec((rows, V_WIDTH), tok_map(0)),
            pl.BlockSpec((nb, CONV_W - 1, QK_WIDTH), st3),
            pl.BlockSpec((nb, HEADS, DV, DQK), st4),
            pl.BlockSpec((rows, HEADS * DQK), tok_map(0)),
            pl.BlockSpec((rows, LANES), tok_map(0)),
        ],
        out_shape=[
            jax.ShapeDtypeStruct((tok, V_WIDTH), F32),
            jax.ShapeDtypeStruct((nseq, CONV_W - 1, QK_WIDTH), F32),
            jax.ShapeDtypeStruct((nseq, HEADS, DV, DQK), F32),
            jax.ShapeDtypeStruct((tok, HEADS * DQK), F32),
            jax.ShapeDtypeStruct((tok, LANES), F32),
        ],
        scratch=[pltpu.VMEM((nb, lc + SUBLANES, QK_WIDTH), F32), pltpu.VMEM((rows, QK_WIDTH), F32)],
    )

    def finish(y, buf, c, nox, mox):
        return y, buf, c, nox[::lc].reshape(nseq, HEADS, DQK), mox[::lc, :HEADS]

    return part, finish


def _merge_kernel(h_ref, ys_ref, ym_ref, g1_ref, g2_ref, wglu_ref, wbs_ref, wbm_ref, wo_ref, o_ref):
    ys = ys_ref[...]
    glu = ys * jax.nn.sigmoid(_dot(ys.astype(BF16), wglu_ref[...]))
    a = _dot(glu.astype(BF16), wbs_ref[...])
    b = _dot(ym_ref[...].astype(BF16), wbm_ref[...])
    merged = jax.nn.sigmoid(g1_ref[...]) * a + jax.nn.sigmoid(g2_ref[...]) * b
    o_ref[...] = h_ref[...] + _dot(merged.astype(BF16), wo_ref[...])


def _merge_part(h, ys, ym, z, wglu, wbs, wbm, wo, *, n, tm):
    row = lambda i: (i, 0)
    const = lambda i: (0, 0)
    resident = lambda shape: pl.BlockSpec(shape, const, pipeline_mode=pl.Buffered(1))
    return _Part(
        kernel=_merge_kernel,
        in_specs=[
            pl.BlockSpec((tm, D_MODEL), row),
            pl.BlockSpec((tm, S5_WIDTH), row),
            pl.BlockSpec((tm, V_WIDTH), row),
            pl.BlockSpec((tm, D_MODEL), lambda i: (i, 2)),
            pl.BlockSpec((tm, D_MODEL), lambda i: (i, 3)),
            resident((S5_WIDTH, S5_WIDTH)),
            resident((S5_WIDTH, D_MODEL)),
            resident((V_WIDTH, D_MODEL)),
            resident((D_MODEL, D_MODEL)),
        ],
        args=[h, ys, ym, z, z, wglu, wbs, wbm, wo],
        out_specs=[pl.BlockSpec((tm, D_MODEL), row)],
        out_shape=[jax.ShapeDtypeStruct((n, D_MODEL), F32)],
        scratch=[],
    )


def _run_parts(parts, steps, name):
    n_in = [len(p.in_specs) for p in parts]
    n_out = [len(p.out_specs) for p in parts]
    n_scr = [len(p.scratch) for p in parts]

    def body(*refs):
        ins, outs, scr = refs[:sum(n_in)], refs[sum(n_in):sum(n_in) + sum(n_out)], refs[sum(n_in) + sum(n_out):]
        i = o = c = 0
        for p, ni, no, nc in zip(parts, n_in, n_out, n_scr):
            p.kernel(*ins[i:i + ni], *outs[o:o + no], *scr[c:c + nc])
            i, o, c = i + ni, o + no, c + nc

    flat = pl.pallas_call(
        body,
        grid=(steps,),
        in_specs=[sp for p in parts for sp in p.in_specs],
        out_specs=[sp for p in parts for sp in p.out_specs],
        out_shape=[sh for p in parts for sh in p.out_shape],
        scratch_shapes=[sc for p in parts for sc in p.scratch],
        compiler_params=_cparams(("parallel",)),
        name=name,
    )(*[a for p in parts for a in p.args])
    res, o = [], 0
    for no in n_out:
        res.append(list(flat[o:o + no]))
        o += no
    return res


def kernel(x_prompt, x_sample, state_s5_re, state_s5_im, state_mlstm_C, state_mlstm_n, state_mlstm_m,
           state_mlstm_conv, meta_tokens, ffn1_norm, ffn1_w_gate, ffn1_w_up, ffn1_w_down, mix_norm, w_in,
           s5_A_re, s5_A_im, s5_log_dt, s5_B_re, s5_B_im, s5_C_re, s5_C_im, s5_D, s5_w_glu,
           mlstm_conv_w, mlstm_conv_b, mlstm_b_i, mlstm_b_f, mlstm_norm, w_branch_s5, w_branch_mlstm,
           w_out, ffn2_norm, ffn2_w_gate, ffn2_w_up, ffn2_w_down, final_norm):
    nbatch, seq, _ = x_prompt.shape
    nsamp, sseq, _ = x_sample.shape
    assert w_in.shape[0] == 1, "one layer per call"
    l = 0

    w1g, w1u, w1d = ffn1_w_gate[l], ffn1_w_up[l], ffn1_w_down[l]
    w2g, w2u, w2d = ffn2_w_gate[l], ffn2_w_up[l], ffn2_w_down[l]
    o_if = S5_WIDTH + QK_WIDTH + 2 * V_WIDTH
    o_gate = o_if + 2 * HEADS
    wt = w_in[l].T
    w2 = _win_halves(wt, o_gate, tr=WIN_CAST_ROWS)
    wglu, wbs, wbm, wo = (w[l].astype(BF16) for w in (s5_w_glu, w_branch_s5, w_branch_mlstm, w_out))
    g1 = ffn1_norm[l][None]
    gm = mix_norm[l][None]
    g2 = ffn2_norm[l][None]
    gf = final_norm[None]
    bif = jnp.pad(jnp.concatenate([mlstm_b_i[l], mlstm_b_f[l]]), (0, LANES - 2 * HEADS))[None]
    cw = mlstm_conv_w[l]
    cb = mlstm_conv_b[l][None]
    ng = mlstm_norm[l][None]
    d_skip = s5_D[l][None]

    *s5_ops, lam = _s5_operators(s5_A_re[l], s5_A_im[l], s5_log_dt[l], s5_B_re[l], s5_B_im[l])
    gh = S5_GROUPS * S5_GROUP
    w_toe, f_bf, e_bf = _s5_prep(*s5_ops, s5_C_re[l].reshape(gh, S5_STATE), s5_C_im[l].reshape(gh, S5_STATE))

    def front(x, tm):
        n = x.shape[0]
        h1 = _ffn(x, g1, w1g, w1u, w1d, n=n, tm=tm)
        z, zif = _win(h1, gm, w2, wt, o_if=o_if, ngate=N_BRANCH * D_MODEL, tm=tm)
        return h1, z, zif

    def mlstm(z, zif, ml_state, *, nseq, seqlen, lc):
        buf0, c0, n0, m0 = ml_state
        ym, buf, c, n, m = _mlstm(z, zif, cw, cb, bif, ng, buf0, c0, n0, m0.reshape(nseq, 1, HEADS),
                                  nseq=nseq, seqlen=seqlen, nb=1, lc=lc)
        return ym, buf, c, n, m.reshape(nseq, HEADS)

    def merge_part(h1, ys, ym, z, n):
        return _merge_part(h1, ys, ym, z, wglu, wbs, wbm, wo, n=n, tm=MERGE_TM)

    def ffn2(h2, n):
        return _ffn(h2, g2, w2g, w2u, w2d, gf, n=n, tm=FFN_TM)

    ntok_s = nsamp * sseq
    ntok_p = nbatch * seq
    x_sm = jnp.concatenate([x_sample.reshape(ntok_s, D_MODEL), meta_tokens], axis=0)
    h1_sm, z_sm, zif_sm = front(x_sm, ntok_s + N_META)
    h1_p, z_p, zif_p = front(x_prompt.reshape(ntok_p, D_MODEL), FFN_TM)

    z_m = jnp.tile(z_sm[ntok_s:], (nbatch, 1))
    zif_m = jnp.tile(zif_sm[ntok_s:], (nbatch, 1))
    zeros = lambda *s: jnp.zeros((nbatch,) + s, F32)

    ys_p, ys_s, *s5_st = _s5(z_m, z_p, z_sm, w_toe, f_bf, e_bf, lam, d_skip,
                             state_s5_re[l].reshape(nsamp, -1), state_s5_im[l].reshape(nsamp, -1),
                             nbatch=nbatch, len_m=N_META, len_p=seq, nsamp=nsamp, len_s=sseq)
    s5_p, s5_s = s5_st[:2], s5_st[2:]

    _, *ml_m = mlstm(z_m, zif_m, (zeros(CONV_W - 1, QK_WIDTH), zeros(HEADS, DV, DQK), zeros(HEADS, DQK), zeros(HEADS)),
                     nseq=nbatch, seqlen=N_META, lc=N_META)
    ym_p, *ml_p = mlstm(z_p, zif_p, ml_m, nseq=nbatch, seqlen=seq, lc=MLSTM_CHUNK)

    steps_p = ntok_p // MERGE_TM
    step_part, step_finish = _mlstm_step_part(
        z_sm, zif_sm, cw, cb, bif, ng, state_mlstm_conv[l], state_mlstm_C[l], state_mlstm_n[l], state_mlstm_m[l],
        nseq=nsamp, seqlen=sseq, nb=nsamp // steps_p)
    (h2_p,), step_out = _run_parts([merge_part(h1_p, ys_p, ym_p, z_p, ntok_p), step_part], steps_p, "merge_mlstm_step")
    ym_s, *ml_s = step_finish(*step_out)
    (h2_s,), = _run_parts([merge_part(h1_sm, ys_s, ym_s, z_sm, ntok_s)], ntok_s // MERGE_TM, "merge")
    y_p = ffn2(h2_p, ntok_p)
    y_s = ffn2(h2_s, ntok_s)

    def pack(n, s5_st, ml_st):
        buf, c, nn, m = ml_st
        return (s5_st[0].reshape(1, n, S5_GROUPS, S5_STATE), s5_st[1].reshape(1, n, S5_GROUPS, S5_STATE),
                c[None], nn[None], m[None], buf[None])

    return ((y_p.reshape(nbatch, seq, D_MODEL), y_s.reshape(nsamp, sseq, D_MODEL))
            + pack(nbatch, s5_p, ml_p) + pack(nsamp, s5_s, ml_s))
```

```python
import functools
from typing import Callable, NamedTuple

import jax
import jax.numpy as jnp
from jax import lax
from jax.experimental import pallas as pl
from jax.experimental.pallas import tpu as pltpu

F32 = jnp.float32
BF16 = jnp.bfloat16

D_MODEL = 2048
D_FF = 5632
N_META = 16
S5_WIDTH = 1024
S5_GROUP = 16
S5_GROUPS = 64
S5_STATE = 64
HEADS = 4
DQK = 128
DV = 256
QK_WIDTH = 1024
V_WIDTH = 1024
CONV_W = 4
N_BRANCH = 2
EPS = 1e-6

LANES = 128
SUBLANES = 8
MXU_DIM = 256
S5_T = 8
S5_GPB = LANES // S5_GROUP
S5_NBLK = S5_WIDTH // LANES
S5_SW = S5_GPB * S5_STATE
VMEM_LIMIT = 58 * 1024 * 1024
FFN_TM = 1024
MERGE_TM = 256
WIN_CAST_ROWS = 456
S5_ROW_TILE = 256
MLSTM_CHUNK = 256


def _cparams(sem):
    return pltpu.CompilerParams(dimension_semantics=sem, vmem_limit_bytes=VMEM_LIMIT)


def _rmsnorm(x, g):
    ms = jnp.mean(x * x, axis=-1, keepdims=True)
    return (x * lax.rsqrt(ms + EPS)) * g


def _dot(a, b):
    return jnp.dot(a, b, preferred_element_type=F32)


def _dot_nt(a, b):
    return lax.dot_general(a, b, (((1,), (1,)), ((), ())), preferred_element_type=F32)


def _dot_tn(a, b):
    return lax.dot_general(a, b, (((0,), (0,)), ((), ())), preferred_element_type=F32)


def _split3(x):
    hi = x.astype(BF16)
    r = x - hi.astype(F32)
    mid = r.astype(BF16)
    lo = (r - mid.astype(F32)).astype(BF16)
    return hi, mid, lo


class _Part(NamedTuple):
    kernel: Callable
    in_specs: list
    args: list
    out_specs: list
    out_shape: list
    scratch: list


def _ffn_kernel(x_ref, g_ref, wg_ref, wu_ref, wd_ref, *rest, final_norm, emit_bf16):
    if final_norm:
        fg_ref, rest = rest[0], rest[1:]
    if emit_bf16:
        o_ref, wgb_ref, wub_ref, wdb_ref, xn_ref = rest
    else:
        o_ref, xn_ref = rest
    j = pl.program_id(1)

    @pl.when(j == 0)
    def _():
        xn_ref[...] = _rmsnorm(x_ref[...], g_ref[...]).astype(BF16)
        o_ref[...] = jnp.zeros(o_ref.shape, F32)

    wg, wu, wd = (w[...].astype(BF16) for w in (wg_ref, wu_ref, wd_ref))
    if emit_bf16:
        wgb_ref[...] = wg
        wub_ref[...] = wu
        wdb_ref[...] = wd
    xn = xn_ref[...]
    gt = _dot(xn, wg)
    up = _dot(xn, wu)
    act = (gt * jax.nn.sigmoid(gt) * up).astype(BF16)
    o_ref[...] += _dot(act, wd)

    @pl.when(j == pl.num_programs(1) - 1)
    def _():
        h = x_ref[...] + 0.5 * o_ref[...]
        if final_norm:
            h = _rmsnorm(h, fg_ref[...])
        o_ref[...] = h


def _ffn(x, g, wg, wu, wd, final_g=None, *, n, tm, tf=256, emit_bf16=False):
    assert not emit_bf16 or n == tm
    wspec = [
        pl.BlockSpec((D_MODEL, tf), lambda i, j: (0, j)),
        pl.BlockSpec((D_MODEL, tf), lambda i, j: (0, j)),
        pl.BlockSpec((tf, D_MODEL), lambda i, j: (j, 0)),
    ]
    in_specs = [pl.BlockSpec((tm, D_MODEL), lambda i, j: (i, 0)), pl.BlockSpec((1, D_MODEL), lambda i, j: (0, 0)), *wspec]
    args = [x, g, wg, wu, wd]
    if final_g is not None:
        in_specs.append(pl.BlockSpec((1, D_MODEL), lambda i, j: (0, 0)))
        args.append(final_g)
    out_specs = [pl.BlockSpec((tm, D_MODEL), lambda i, j: (i, 0))]
    out_shape = [jax.ShapeDtypeStruct((n, D_MODEL), F32)]
    if emit_bf16:
        out_specs += wspec
        out_shape += [jax.ShapeDtypeStruct(w.shape, BF16) for w in (wg, wu, wd)]
    res = pl.pallas_call(
        functools.partial(_ffn_kernel, final_norm=final_g is not None, emit_bf16=emit_bf16),
        grid=(n // tm, D_FF // tf),
        in_specs=in_specs,
        out_specs=out_specs,
        out_shape=out_shape,
        scratch_shapes=[pltpu.VMEM((tm, D_MODEL), BF16)],
        compiler_params=_cparams(("parallel", "arbitrary")),
        name="ffn",
    )(*args)
    return res if emit_bf16 else res[0]


def _cast_rows_kernel(w_ref, o_ref, *, tail):
    last = pl.num_programs(0) - 1

    @pl.when(pl.program_id(0) < last)
    def _():
        o_ref[0] = w_ref[...].astype(BF16)

    @pl.when(pl.program_id(0) == last)
    def _():
        o_ref[0, :tail] = w_ref[:tail].astype(BF16)
        o_ref[0, tail:] = jnp.zeros((o_ref.shape[1] - tail, o_ref.shape[2]), BF16)


def _win_halves(wt, o_gate, *, tr):
    rows, k = wt.shape
    per = o_gate // tr
    tail = rows - (2 * per - 1) * tr
    assert per * tr == o_gate and tr % 8 == 0 and 0 < tail < tr and tail % 8 == 0
    return pl.pallas_call(
        functools.partial(_cast_rows_kernel, tail=tail),
        grid=(2 * per,),
        in_specs=[pl.BlockSpec((tr, k), lambda i: (i, 0))],
        out_specs=pl.BlockSpec((1, tr, k), lambda i: (i // per, i % per, 0)),
        out_shape=jax.ShapeDtypeStruct((2, o_gate, k), BF16),
        compiler_params=_cparams(("parallel",)),
        name="w_in_cast",
    )(wt)


def _win_kernel(h_ref, g_ref, wa_ref, wb_ref, wif_ref, z_ref, zif_ref, un_ref, *, na):
    j = pl.program_id(1)

    @pl.when(j == 0)
    def _():
        un = _rmsnorm(h_ref[...], g_ref[...]).astype(BF16)
        un_ref[...] = un
        zif_ref[...] = _dot_nt(un, wif_ref[...].astype(BF16))

    @pl.when(j < na)
    def _():
        z_ref[...] = _dot_nt(un_ref[...], wa_ref[0])

    @pl.when(j >= na)
    def _():
        z_ref[...] = _dot_nt(un_ref[...], wb_ref[0])


def _win(h, g, w2, wt, *, o_if, ngate, tm, tn=1024):
    n = h.shape[0]
    na = o_if // tn
    nb = ngate // tn
    return pl.pallas_call(
        functools.partial(_win_kernel, na=na),
        grid=(n // tm, na + nb),
        in_specs=[
            pl.BlockSpec((tm, D_MODEL), lambda i, j: (i, 0)),
            pl.BlockSpec((1, D_MODEL), lambda i, j: (0, 0)),
            pl.BlockSpec((1, tn, D_MODEL), lambda i, j: (0, jnp.minimum(j, na - 1), 0)),
            pl.BlockSpec((1, tn, D_MODEL), lambda i, j: (1, jnp.maximum(j - na, 0), 0)),
            pl.BlockSpec((LANES, D_MODEL), lambda i, j: (o_if // LANES, 0)),
        ],
        out_specs=[
            pl.BlockSpec((tm, tn), lambda i, j: (i, j)),
            pl.BlockSpec((tm, LANES), lambda i, j: (i, 0)),
        ],
        out_shape=[
            jax.ShapeDtypeStruct((n, (na + nb) * tn), F32),
            jax.ShapeDtypeStruct((n, LANES), F32),
        ],
        scratch_shapes=[pltpu.VMEM((tm, D_MODEL), BF16)],
        compiler_params=_cparams(("parallel", "arbitrary")),
        name="w_in",
    )(h, g, w2, w2, wt)


def _s5_prep_kernel(bbr_ref, bbi_ref, ptr_ref, pti_ref, pwr_ref, pwi_ref, cnr_ref, cni_ref,
                    w_ref, f_ref, e_ref):
    n = S5_T * LANES
    i32 = jnp.int32
    lg_h = S5_GROUP.bit_length() - 1
    lg_p = S5_STATE.bit_length() - 1
    lg_l = LANES.bit_length() - 1
    gmask = S5_GPB - 1

    def tile_mat(k, c, src_of_col):
        kk = lax.broadcasted_iota(i32, (k, c), 0)
        cc = lax.broadcasted_iota(i32, (k, c), 1)
        return jnp.where(kk == src_of_col(cc), 1.0, 0.0).astype(BF16)

    def group_mask(r, c, row_group, col_group):
        rr = lax.broadcasted_iota(i32, (r, c), 0)
        cc = lax.broadcasted_iota(i32, (r, c), 1)
        return row_group(rr) == col_group(cc)

    def tiled(x, mat):
        hi, mid, _ = _split3(x)
        return _dot(hi, mat) + _dot(mid, mat)

    def lag_power(pt):
        return jnp.concatenate(
            [jnp.broadcast_to(pt[:, S5_T - 1 - s:S5_T - s], (S5_SW, LANES)) for s in range(S5_T)], axis=1)

    def dot_hi(a, b):
        a_hi, a_mid, _ = _split3(a)
        b_hi, b_mid, _ = _split3(b)
        return _dot(a_hi, b_hi) + _dot(a_mid, b_hi) + _dot(a_hi, b_mid)

    sel_h = tile_mat(S5_GROUP, n, lambda c: c & (S5_GROUP - 1))
    mf = group_mask(S5_SW, n, lambda r: r >> lg_p, lambda c: (c >> lg_h) & gmask)
    lpr = lag_power(ptr_ref[...])
    lpi = lag_power(pti_ref[...])
    bxr = tiled(bbr_ref[...], sel_h)
    bxi = tiled(bbi_ref[...], sel_h)
    ftr = jnp.where(mf, lpr * bxr - lpi * bxi, 0.0)
    fti = jnp.where(mf, lpr * bxi + lpi * bxr, 0.0)
    f_ref[0, :S5_SW, :] = ftr.astype(BF16)
    f_ref[0, S5_SW:, :] = fti.astype(BF16)

    sel_p = tile_mat(S5_STATE, S5_SW, lambda c: c & (S5_STATE - 1))
    mc = group_mask(LANES, S5_SW, lambda r: r >> lg_h, lambda c: c >> lg_p)
    cxr = jnp.where(mc, tiled(cnr_ref[...], sel_p), 0.0)
    cxi = jnp.where(mc, tiled(cni_ref[...], sel_p), 0.0)

    for t in range(S5_T):
        pr = pwr_ref[t + 1:t + 2, :]
        pi = pwi_ref[t + 1:t + 2, :]
        e_ref[0, t * LANES:(t + 1) * LANES, :S5_SW] = (cxr * pr - cxi * pi).astype(BF16)
        e_ref[0, t * LANES:(t + 1) * LANES, S5_SW:] = (-(cxr * pi + cxi * pr)).astype(BF16)

    cn = jnp.concatenate([cxr, -cxi], axis=1)
    w_ref[...] = jnp.zeros(w_ref.shape, w_ref.dtype)
    k_all = dot_hi(cn, jnp.concatenate([ftr, fti], axis=0)).astype(BF16)
    for lag in range(S5_T):
        c0 = (S5_T - 1 - lag) * LANES
        kt = k_all[:, c0:c0 + LANES]
        for s in range(S5_T - lag):
            t = s + lag
            w_ref[0, t * LANES:(t + 1) * LANES, s * LANES:(s + 1) * LANES] = kt


def _s5_prep(bb_re, bb_im, pwt_re, pwt_im, pw_re, pw_im, cn_re, cn_im):
    n = S5_T * LANES
    rows = lambda r, c: pl.BlockSpec((r, c), lambda j: (j, 0))
    cols = lambda r, c: pl.BlockSpec((r, c), lambda j: (0, j))
    out = pl.BlockSpec((1, n, n), lambda j: (j, 0, 0))
    return pl.pallas_call(
        _s5_prep_kernel,
        grid=(S5_NBLK,),
        in_specs=[rows(S5_SW, S5_GROUP)] * 4 + [cols(S5_GROUP, S5_SW)] * 2 + [rows(LANES, S5_STATE)] * 2,
        out_specs=[out, out, out],
        out_shape=[jax.ShapeDtypeStruct((S5_NBLK, n, n), BF16)] * 3,
        compiler_params=_cparams(("parallel",)),
        name="s5_prep",
    )(bb_re, bb_im, pwt_re, pwt_im, pw_re, pw_im, cn_re, cn_im)


def _s5_group(x_ref, y_ref, w, f, e, ar, ai, dt, h0, yi_ref, s_ref, hin_ref, *, nseq, nblk):
    rows = nseq * nblk
    rt = min(rows, S5_ROW_TILE)

    def load_u(r0):
        return jnp.concatenate(
            [x_ref[pl.ds(r0 * S5_T + s, rt, stride=S5_T), :] for s in range(S5_T)], axis=1)

    for r0 in range(0, rows, rt):
        ub = load_u(r0).astype(BF16)
        if y_ref is not None:
            for c0 in range(0, S5_T * LANES, MXU_DIM):
                c1 = c0 + MXU_DIM
                yi_ref[r0:r0 + rt, c0:c1] = _dot_nt(ub[:, :c1], w[c0:c1, :c1])
        s_ref[r0:r0 + rt, :] = _dot_nt(ub, f)

    if nblk == 1:
        hre, him = h0
        hin_ref[0:rows, :S5_SW] = hre
        hin_ref[0:rows, S5_SW:] = him
        fin = (ar * hre - ai * him + s_ref[0:rows, :S5_SW], ar * him + ai * hre + s_ref[0:rows, S5_SW:])
    else:
        def body(c, carry):
            new = []
            for b in range(nseq):
                hre, him = carry[2 * b], carry[2 * b + 1]
                row = b * nblk + c
                hin_ref[pl.ds(row, 1), :S5_SW] = hre
                hin_ref[pl.ds(row, 1), S5_SW:] = him
                sre = s_ref[pl.ds(row, 1), :S5_SW]
                sim = s_ref[pl.ds(row, 1), S5_SW:]
                new.append(ar * hre - ai * him + sre)
                new.append(ar * him + ai * hre + sim)
            return tuple(new)

        flat = lax.fori_loop(0, nblk, body, tuple(v for pair in h0 for v in pair))
        fin = [(flat[2 * b], flat[2 * b + 1]) for b in range(nseq)]

    if y_ref is not None:
        for r0 in range(0, rows, rt):
            yo = _dot_nt(hin_ref[r0:r0 + rt, :].astype(BF16), e)
            y = jax.nn.gelu(yi_ref[r0:r0 + rt, :] + yo + load_u(r0) * dt)
            for t in range(S5_T):
                y_ref[pl.ds(r0 * S5_T + t, rt, stride=S5_T), :] = y[:, t * LANES:(t + 1) * LANES]
    return fin


def _s5_kernel(xm_ref, xp_ref, xs_ref, w_ref, f_ref, e_ref, lam_ref, d_ref, h0re_ref, h0im_ref,
               yp_ref, ys_ref, pre_ref, pim_ref, sre_ref, sim_ref, yi_ref, s_ref, hin_ref,
               *, nbatch, nblk_m, nblk_p, nsamp, nblk_s):
    ar = lam_ref[0, 0:1, :]
    ai = lam_ref[0, 1:2, :]
    dt = jnp.concatenate([d_ref[...]] * S5_T, axis=1)
    ops = (w_ref[0], f_ref[0], e_ref[0], ar, ai, dt)
    scratch = (yi_ref, s_ref, hin_ref)
    zero = jnp.zeros((1, S5_SW), F32)
    st = _s5_group(xm_ref, None, *ops, [(zero, zero)] * nbatch, *scratch, nseq=nbatch, nblk=nblk_m)
    st = _s5_group(xp_ref, yp_ref, *ops, st, *scratch, nseq=nbatch, nblk=nblk_p)
    for b in range(nbatch):
        pre_ref[b:b + 1, :] = st[b][0]
        pim_ref[b:b + 1, :] = st[b][1]
    fin = _s5_group(xs_ref, ys_ref, *ops, (h0re_ref[...], h0im_ref[...]), *scratch, nseq=nsamp, nblk=nblk_s)
    sre_ref[...] = fin[0]
    sim_ref[...] = fin[1]


def _s5(z_m, z_p, z_s, w, f, e, lam, d, h0re, h0im, *, nbatch, len_m, len_p, nsamp, len_s):
    assert len_s == S5_T
    n = S5_T * LANES
    tok_m, tok_p, tok_s = nbatch * len_m, nbatch * len_p, nsamp * len_s
    rows = max(nbatch * len_p, nsamp * len_s, nbatch * len_m) // S5_T
    col = lambda r: pl.BlockSpec((r, LANES), lambda j: (0, j))
    op = pl.BlockSpec((1, n, n), lambda j: (j, 0, 0))
    st = lambda r: pl.BlockSpec((r, S5_SW), lambda j: (0, j))
    gp = S5_GROUPS * S5_STATE
    kern = functools.partial(_s5_kernel, nbatch=nbatch, nblk_m=len_m // S5_T, nblk_p=len_p // S5_T,
                             nsamp=nsamp, nblk_s=len_s // S5_T)
    return pl.pallas_call(
        kern,
        grid=(S5_NBLK,),
        in_specs=[col(tok_m), col(tok_p), col(tok_s), op, op, op,
                  pl.BlockSpec((1, 2, S5_SW), lambda j: (j, 0, 0)), col(1), st(nsamp), st(nsamp)],
        out_specs=[col(tok_p), col(tok_s), st(nbatch), st(nbatch), st(nsamp), st(nsamp)],
        out_shape=[
            jax.ShapeDtypeStruct((tok_p, S5_WIDTH), F32),
            jax.ShapeDtypeStruct((tok_s, S5_WIDTH), F32),
            jax.ShapeDtypeStruct((nbatch, gp), F32),
            jax.ShapeDtypeStruct((nbatch, gp), F32),
            jax.ShapeDtypeStruct((nsamp, gp), F32),
            jax.ShapeDtypeStruct((nsamp, gp), F32),
        ],
        scratch_shapes=[pltpu.VMEM((rows, n), F32)] * 3,
        compiler_params=_cparams(("parallel",)),
        name="s5",
    )(z_m, z_p, z_s, w, f, e, lam, d, h0re, h0im)


def _s5_operators(a_re, a_im, log_dt, b_re, b_im):
    dt = jnp.exp(log_dt)[:, None]
    ar, ai = a_re * dt, a_im * dt
    taus = jnp.arange(S5_T + 1, dtype=F32)[:, None, None]
    pmag = jnp.exp(taus * ar[None])
    pw_re = pmag * jnp.cos(taus * ai[None])
    pw_im = pmag * jnp.sin(taus * ai[None])
    nr, ni = pw_re[1] - 1.0, pw_im[1]
    den = a_re * a_re + a_im * a_im
    q_re = (nr * a_re + ni * a_im) / den
    q_im = (ni * a_re - nr * a_im) / den
    gp = S5_GROUPS * S5_STATE
    bb_re = (q_re[:, :, None] * b_re - q_im[:, :, None] * b_im).reshape(gp, S5_GROUP)
    bb_im = (q_re[:, :, None] * b_im + q_im[:, :, None] * b_re).reshape(gp, S5_GROUP)
    padrows = ((0, S5_GROUP - (S5_T + 1)), (0, 0))
    pw_re2 = jnp.pad(pw_re.reshape(S5_T + 1, gp), padrows)
    pw_im2 = jnp.pad(pw_im.reshape(S5_T + 1, gp), padrows)
    lam = jnp.stack([pw_re[S5_T].reshape(S5_NBLK, S5_SW), pw_im[S5_T].reshape(S5_NBLK, S5_SW)], axis=1)
    return bb_re, bb_im, pw_re2.T, pw_im2.T, pw_re2, pw_im2, lam


def _mlstm_kernel(qk_ref, v_ref, o_ref, if_ref, cw_ref, cb_ref, bif_ref, ng_ref,
                  buf0_ref, c0_ref, n0_ref, m0_ref,
                  y_ref, bufo_ref, co_ref, no_ref, mo_ref, xp_ref, *, nb, lc, carry):
    if carry:
        @pl.when(pl.program_id(1) == 0)
        def _():
            bufo_ref[...] = buf0_ref[...]
            co_ref[...] = c0_ref[...]
            no_ref[...] = n0_ref[...]
            mo_ref[...] = m0_ref[...]
        bufs_ref, cs_ref, ns_ref, ms_ref = bufo_ref, co_ref, no_ref, mo_ref
    else:
        bufs_ref, cs_ref, ns_ref, ms_ref = buf0_ref, c0_ref, n0_ref, m0_ref

    lp = max(lc, LANES)
    pad = CONV_W - 1
    heads = range(HEADS)
    head_row = lax.broadcasted_iota(jnp.int32, (HEADS, DQK), 0)
    head_lane = lax.broadcasted_iota(jnp.int32, (1, HEADS), 1)
    row = lax.broadcasted_iota(jnp.int32, (lc, lc), 0)
    col = lax.broadcasted_iota(jnp.int32, (lc, lc), 1)
    causal = row >= col
    tril = jnp.where(causal, 1.0, 0.0).astype(BF16)
    lane = lax.broadcasted_iota(jnp.int32, (lc, LANES), 1)
    is_i = lane < HEADS
    is_f = jnp.logical_and(lane >= HEADS, lane < 2 * HEADS)

    for s in range(nb):
        r0 = s * lc
        x1 = SUBLANES
        xp_ref[s, 0:x1 - pad, :] = jnp.zeros((x1 - pad, QK_WIDTH), F32)
        xp_ref[s, x1 - pad:x1, :] = bufs_ref[s]
        xp_ref[s, x1:x1 + lc, :] = qk_ref[r0:r0 + lc, :]
        xp = xp_ref[s]
        conv = cb_ref[...] + xp[x1:x1 + lc] * cw_ref[pad:pad + 1, :]
        for j in range(1, CONV_W):
            conv = conv + pltpu.roll(xp, j, 0)[x1:x1 + lc] * cw_ref[pad - j:pad - j + 1, :]
        bufo_ref[s] = xp_ref[s, x1 + lc - pad:x1 + lc, :]
        n_all = ns_ref[s]
        m_all = ms_ref[s]
        qk = conv * jax.nn.sigmoid(conv)
        q = qk[:, :QK_WIDTH // 2] * (DQK ** -0.5)
        k = qk[:, QK_WIDTH // 2:]

        gate = if_ref[r0:r0 + lc, :] + bif_ref[...]
        lf = jnp.where(is_f, jax.nn.log_sigmoid(gate), 0.0)
        parts = _split3(lf)
        bcum = _dot(tril, parts[0]) + _dot(tril, parts[1]) + _dot(tril, parts[2])
        pc = jnp.where(is_i, gate, bcum)
        if lc < lp:
            pc_t = jnp.concatenate([pc, jnp.zeros((lp - lc, LANES), F32)], axis=0).T[:, :lc]
        else:
            pc_t = pc.T

        i_col = [pc[:, h:h + 1] for h in heads]
        b_col = [pc[:, HEADS + h:HEADS + h + 1] for h in heads]
        m0 = [m_all[:, h:h + 1] for h in heads]
        qh = [q[:, h * DQK:(h + 1) * DQK] for h in heads]
        kh = [k[:, h * DQK:(h + 1) * DQK] for h in heads]
        qb = [x.astype(BF16) for x in qh]
        vb = [v_ref[r0:r0 + lc, h * DV:(h + 1) * DV].astype(BF16) for h in heads]
        c_prev = [cs_ref[s, h] for h in heads]
        n_prev = [n_all[h:h + 1, :] for h in heads]

        qk_t = [_dot_nt(qb[h], kh[h].astype(BF16)) for h in heads]
        carried = [_dot_nt(qb[h], c_prev[h].astype(BF16)) for h in heads]
        logw = [jnp.where(causal, b_col[h] - pc_t[HEADS + h:HEADS + h + 1, :] + pc_t[h:h + 1, :], -jnp.inf)
                for h in heads]
        g = [b_col[h] + m0[h] for h in heads]
        m = [jnp.maximum(g[h], jnp.max(logw[h], axis=-1, keepdims=True)) for h in heads]
        inter = [jnp.exp(g[h] - m[h]) for h in heads]
        sc = [qk_t[h] * jnp.exp(logw[h] - m[h]) for h in heads]
        num = [_dot(sc[h].astype(BF16), vb[h]) + inter[h] * carried[h] for h in heads]
        den = [jnp.sum(sc[h], axis=-1, keepdims=True)
               + inter[h] * jnp.sum(qh[h] * n_prev[h], axis=-1, keepdims=True) for h in heads]
        hh = [num[h] / jnp.maximum(jnp.abs(den[h]), jnp.exp(-m[h])) for h in heads]
        hh = [x * lax.rsqrt(jnp.mean(x * x, axis=-1, keepdims=True) + EPS) for x in hh]
        for h in heads:
            og = jax.nn.sigmoid(o_ref[r0:r0 + lc, h * DV:(h + 1) * DV])
            y_ref[r0:r0 + lc, h * DV:(h + 1) * DV] = hh[h] * ng_ref[:, h * DV:(h + 1) * DV] * og

        b_last = [x[lc - 1:lc, :] for x in b_col]
        m_new = [x[lc - 1:lc, :] for x in m]
        decay = [jnp.exp(b_last[h] + m0[h] - m_new[h]) for h in heads]
        kw = [kh[h] * jnp.exp(b_last[h] - b_col[h] + i_col[h] - m_new[h]) for h in heads]
        upd = [_dot_tn(vb[h], kw[h].astype(BF16)) for h in heads]
        n_new_all = jnp.zeros((HEADS, DQK), F32)
        m_new_all = jnp.zeros((1, HEADS), F32)
        for h in heads:
            co_ref[s, h] = decay[h] * c_prev[h] + upd[h]
            n_new = decay[h] * n_prev[h] + jnp.sum(kw[h], axis=0, keepdims=True)
            n_new_all = jnp.where(head_row == h, n_new, n_new_all)
            m_new_all = jnp.where(head_lane == h, m_new[h], m_new_all)
        no_ref[s] = n_new_all
        mo_ref[s] = m_new_all


def _mlstm(z, zif, cw, cb, bif, ng, buf0, c0, n0, m0, *, nseq, seqlen, nb, lc):
    tok = nseq * seqlen
    nchunk = seqlen // lc
    rows = nb * lc
    tok_map = lambda col: (lambda i, c: (i * nchunk + c, col))
    const2 = lambda i, c: (0, 0)
    st3 = lambda i, c: (i, 0, 0)
    st4 = lambda i, c: (i, 0, 0, 0)
    kern = functools.partial(_mlstm_kernel, nb=nb, lc=lc, carry=nchunk > 1)
    return pl.pallas_call(
        kern,
        grid=(nseq // nb, nchunk),
        in_specs=[
            pl.BlockSpec((rows, QK_WIDTH), tok_map(1)),
            pl.BlockSpec((rows, V_WIDTH), tok_map(2)),
            pl.BlockSpec((rows, V_WIDTH), tok_map(3)),
            pl.BlockSpec((rows, LANES), tok_map(0)),
            pl.BlockSpec((CONV_W, QK_WIDTH), const2),
            pl.BlockSpec((1, QK_WIDTH), const2),
            pl.BlockSpec((1, LANES), const2),
            pl.BlockSpec((1, V_WIDTH), const2),
            pl.BlockSpec((nb, CONV_W - 1, QK_WIDTH), st3),
            pl.BlockSpec((nb, HEADS, DV, DQK), st4),
            pl.BlockSpec((nb, HEADS, DQK), st3),
            pl.BlockSpec((nb, 1, HEADS), st3),
        ],
        out_specs=[
            pl.BlockSpec((rows, V_WIDTH), tok_map(0)),
            pl.BlockSpec((nb, CONV_W - 1, QK_WIDTH), st3),
            pl.BlockSpec((nb, HEADS, DV, DQK), st4),
            pl.BlockSpec((nb, HEADS, DQK), st3),
            pl.BlockSpec((nb, 1, HEADS), st3),
        ],
        out_shape=[
            jax.ShapeDtypeStruct((tok, V_WIDTH), F32),
            jax.ShapeDtypeStruct((nseq, CONV_W - 1, QK_WIDTH), F32),
            jax.ShapeDtypeStruct((nseq, HEADS, DV, DQK), F32),
            jax.ShapeDtypeStruct((nseq, HEADS, DQK), F32),
            jax.ShapeDtypeStruct((nseq, 1, HEADS), F32),
        ],
        scratch_shapes=[pltpu.VMEM((nb, lc + SUBLANES, QK_WIDTH), F32)],
        compiler_params=_cparams(("parallel", "arbitrary")),
        name="mlstm",
    )(z, z, z, zif, cw, cb, bif, ng, buf0, c0, n0, m0)


def _mlstm_step_kernel(qk_ref, v_ref, o_ref, if_ref, cw_ref, cb_ref, bif_ref, ng_ref,
                       buf0_ref, c0_ref, n0x_ref, m0x_ref,
                       y_ref, bufo_ref, co_ref, nox_ref, mox_ref, xp_ref, conv_ref, *, nb, lc):
    rows = nb * lc
    lg = lc.bit_length() - 1
    i32 = jnp.int32
    row = lax.broadcasted_iota(i32, (rows, rows), 0)
    col = lax.broadcasted_iota(i32, (rows, rows), 1)
    same = (row >> lg) == (col >> lg)
    causal = jnp.logical_and(same, row >= col)
    same_b = jnp.where(same, 1.0, 0.0).astype(BF16)
    tril_b = jnp.where(causal, 1.0, 0.0).astype(BF16)
    lane = lax.broadcasted_iota(i32, (rows, LANES), 1)
    is_i = lane < HEADS
    is_f = jnp.logical_and(lane >= HEADS, lane < 2 * HEADS)

    def seg_dot(mat, x):
        hi, mid, lo = _split3(x)
        return _dot(mat, hi) + _dot(mat, mid) + _dot(mat, lo)

    x0 = SUBLANES - (CONV_W - 1)
    for s in range(nb):
        xp_ref[s, x0:SUBLANES, :] = buf0_ref[s]
        xp_ref[s, SUBLANES:SUBLANES + lc, :] = qk_ref[s * lc:(s + 1) * lc, :]
        conv_ref[s * lc:(s + 1) * lc, :] = cb_ref[...] + sum(
            xp_ref[s, x0 + j:x0 + j + lc, :] * cw_ref[j:j + 1, :] for j in range(CONV_W))
        bufo_ref[s] = xp_ref[s, x0 + lc:SUBLANES + lc, :]
    conv = conv_ref[...]
    qk = conv * jax.nn.sigmoid(conv)
    q = qk[:, :QK_WIDTH // 2] * (DQK ** -0.5)
    k = qk[:, QK_WIDTH // 2:]

    gate = if_ref[...] + bif_ref[...]
    lf = jnp.where(is_f, jax.nn.log_sigmoid(gate), 0.0)
    bcum = seg_dot(tril_b, lf)
    btot = seg_dot(same_b, lf)
    pc = jnp.where(is_i, gate, bcum)
    if rows < LANES:
        pc_t = jnp.concatenate([pc, jnp.zeros((LANES - rows, LANES), F32)], axis=0).T[:, :rows]
    else:
        pc_t = pc.T
    m0x = m0x_ref[...]
    heads = range(HEADS)
    seqs = range(nb)

    i_col = [pc[:, h:h + 1] for h in heads]
    b_col = [pc[:, HEADS + h:HEADS + h + 1] for h in heads]
    i_row = [pc_t[h:h + 1, :] for h in heads]
    b_row = [pc_t[HEADS + h:HEADS + h + 1, :] for h in heads]
    b_last = [btot[:, HEADS + h:HEADS + h + 1] for h in heads]
    m0 = [m0x[:, h:h + 1] for h in heads]
    qh = [q[:, h * DQK:(h + 1) * DQK] for h in heads]
    kh = [k[:, h * DQK:(h + 1) * DQK] for h in heads]
    qb = [x.astype(BF16) for x in qh]
    vb = [v_ref[:, h * DV:(h + 1) * DV].astype(BF16) for h in heads]
    n_prev = [n0x_ref[:, h * DQK:(h + 1) * DQK] for h in heads]

    qk_t = [_dot_nt(qb[h], kh[h].astype(BF16)) for h in heads]
    carried = [jnp.concatenate([_dot_nt(qb[h][s * lc:(s + 1) * lc], c0_ref[s, h].astype(BF16)) for s in seqs],
                               axis=0) for h in heads]
    logw = [jnp.where(causal, b_col[h] - b_row[h] + i_row[h], -jnp.inf) for h in heads]
    g = [b_col[h] + m0[h] for h in heads]
    m = [jnp.maximum(g[h], jnp.max(logw[h], axis=-1, keepdims=True)) for h in heads]
    inter = [jnp.exp(g[h] - m[h]) for h in heads]
    lw_end = [jnp.where(same, b_last[h] - b_row[h] + i_row[h], -jnp.inf) for h in heads]
    m_new = [jnp.maximum(b_last[h] + m0[h], jnp.max(lw_end[h], axis=-1, keepdims=True)) for h in heads]
    decay = [jnp.exp(b_last[h] + m0[h] - m_new[h]) for h in heads]
    sc = [qk_t[h] * jnp.exp(logw[h] - m[h]) for h in heads]
    num = [_dot(sc[h].astype(BF16), vb[h]) + inter[h] * carried[h] for h in heads]
    den = [jnp.sum(sc[h], axis=-1, keepdims=True)
           + inter[h] * jnp.sum(qh[h] * n_prev[h], axis=-1, keepdims=True) for h in heads]
    hh = [num[h] / jnp.maximum(jnp.abs(den[h]), jnp.exp(-m[h])) for h in heads]
    hh = [x * lax.rsqrt(jnp.mean(x * x, axis=-1, keepdims=True) + EPS) for x in hh]
    for h in heads:
        og = jax.nn.sigmoid(o_ref[:, h * DV:(h + 1) * DV])
        y_ref[:, h * DV:(h + 1) * DV] = hh[h] * ng_ref[:, h * DV:(h + 1) * DV] * og

    kw = [kh[h] * jnp.exp(b_last[h] - b_col[h] + i_col[h] - m_new[h]) for h in heads]
    kwb = [x.astype(BF16) for x in kw]
    upd = [[_dot_tn(vb[h][s * lc:(s + 1) * lc], kwb[h][s * lc:(s + 1) * lc]) for s in seqs] for h in heads]
    mox = jnp.zeros((rows, LANES), F32)
    for h in heads:
        for s in seqs:
            co_ref[s, h] = decay[h][s * lc:s * lc + 1, :] * c0_ref[s, h] + upd[h][s]
        nox_ref[:, h * DQK:(h + 1) * DQK] = decay[h] * n_prev[h] + seg_dot(same_b, kw[h])
        mox = jnp.where(lane == h, m_new[h], mox)
    mox_ref[...] = mox


def _mlstm_step_part(z, zif, cw, cb, bif, ng, buf0, c0, n0, m0, *, nseq, seqlen, nb):
    lc = seqlen
    assert lc & (lc - 1) == 0 and lc >= CONV_W - 1 and nseq % nb == 0 and (nb * lc) % SUBLANES == 0
    tok = nseq * seqlen
    rows = nb * lc
    n0x = jnp.repeat(n0.reshape(nseq, HEADS * DQK), lc, axis=0)
    m0x = jnp.repeat(jnp.pad(m0, ((0, 0), (0, LANES - HEADS))), lc, axis=0)
    tok_map = lambda col: (lambda i: (i, col))
    const2 = lambda i: (0, 0)
    st3 = lambda i: (i, 0, 0)
    st4 = lambda i: (i, 0, 0, 0)
    part = _Part(
        kernel=functools.partial(_mlstm_step_kernel, nb=nb, lc=lc),
        in_specs=[
            pl.BlockSpec((rows, QK_WIDTH), tok_map(1)),
            pl.BlockSpec((rows, V_WIDTH), tok_map(2)),
            pl.BlockSpec((rows, V_WIDTH), tok_map(3)),
            pl.BlockSpec((rows, LANES), tok_map(0)),
            pl.BlockSpec((CONV_W, QK_WIDTH), const2),
            pl.BlockSpec((1, QK_WIDTH), const2),
            pl.BlockSpec((1, LANES), const2),
            pl.BlockSpec((1, V_WIDTH), const2),
            pl.BlockSpec((nb, CONV_W - 1, QK_WIDTH), st3),
            pl.BlockSpec((nb, HEADS, DV, DQK), st4),
            pl.BlockSpec((rows, HEADS * DQK), tok_map(0)),
            pl.BlockSpec((rows, LANES), tok_map(0)),
        ],
        args=[z, z, z, zif, cw, cb, bif, ng, buf0, c0, n0x, m0x],
        out_specs=[
            pl.BlockSpec((rows, V_WIDTH), tok_map(0)),
            pl.BlockSpec((nb, CONV_W - 1, QK_WIDTH), st3),
            pl.BlockSpec((nb, HEADS, DV, DQK), st4),
            pl.BlockSpec((rows, HEADS * DQK), tok_map(0)),
            pl.BlockSpec((rows, LANES), tok_map(0)),
        ],
        out_shape=[
            jax.ShapeDtypeStruct((tok, V_WIDTH), F32),
            jax.ShapeDtypeStruct((nseq, CONV_W - 1, QK_WIDTH), F32),
            jax.ShapeDtypeStruct((nseq, HEADS, DV, DQK), F32),
            jax.ShapeDtypeStruct((tok, HEADS * DQK), F32),
            jax.ShapeDtypeStruct((tok, LANES), F32),
        ],
        scratch=[pltpu.VMEM((nb, lc + SUBLANES, QK_WIDTH), F32), pltpu.VMEM((rows, QK_WIDTH), F32)],
    )

    def finish(y, buf, c, nox, mox):
        return y, buf, c, nox[::lc].reshape(nseq, HEADS, DQK), mox[::lc, :HEADS]

    return part, finish


def _merge_kernel(h_ref, ys_ref, ym_ref, g1_ref, g2_ref, wglu_ref, wbs_ref, wbm_ref, wo_ref, o_ref):
    ys = ys_ref[...]
    glu = ys * jax.nn.sigmoid(_dot(ys.astype(BF16), wglu_ref[...]))
    a = _dot(glu.astype(BF16), wbs_ref[...])
    b = _dot(ym_ref[...].astype(BF16), wbm_ref[...])
    merged = jax.nn.sigmoid(g1_ref[...]) * a + jax.nn.sigmoid(g2_ref[...]) * b
    o_ref[...] = h_ref[...] + _dot(merged.astype(BF16), wo_ref[...])


def _merge_part(h, ys, ym, z, wglu, wbs, wbm, wo, *, n, tm):
    row = lambda i: (i, 0)
    const = lambda i: (0, 0)
    resident = lambda shape: pl.BlockSpec(shape, const, pipeline_mode=pl.Buffered(1))
    return _Part(
        kernel=_merge_kernel,
        in_specs=[
            pl.BlockSpec((tm, D_MODEL), row),
            pl.BlockSpec((tm, S5_WIDTH), row),
            pl.BlockSpec((tm, V_WIDTH), row),
            pl.BlockSpec((tm, D_MODEL), lambda i: (i, 2)),
            pl.BlockSpec((tm, D_MODEL), lambda i: (i, 3)),
            resident((S5_WIDTH, S5_WIDTH)),
            resident((S5_WIDTH, D_MODEL)),
            resident((V_WIDTH, D_MODEL)),
            resident((D_MODEL, D_MODEL)),
        ],
        args=[h, ys, ym, z, z, wglu, wbs, wbm, wo],
        out_specs=[pl.BlockSpec((tm, D_MODEL), row)],
        out_shape=[jax.ShapeDtypeStruct((n, D_MODEL), F32)],
        scratch=[],
    )


def _run_parts(parts, steps, name):
    n_in = [len(p.in_specs) for p in parts]
    n_out = [len(p.out_specs) for p in parts]
    n_scr = [len(p.scratch) for p in parts]

    def body(*refs):
        ins, outs, scr = refs[:sum(n_in)], refs[sum(n_in):sum(n_in) + sum(n_out)], refs[sum(n_in) + sum(n_out):]
        i = o = c = 0
        for p, ni, no, nc in zip(parts, n_in, n_out, n_scr):
            p.kernel(*ins[i:i + ni], *outs[o:o + no], *scr[c:c + nc])
            i, o, c = i + ni, o + no, c + nc

    flat = pl.pallas_call(
        body,
        grid=(steps,),
        in_specs=[sp for p in parts for sp in p.in_specs],
        out_specs=[sp for p in parts for sp in p.out_specs],
        out_shape=[sh for p in parts for sh in p.out_shape],
        scratch_shapes=[sc for p in parts for sc in p.scratch],
        compiler_params=_cparams(("parallel",)),
        name=name,
    )(*[a for p in parts for a in p.args])
    res, o = [], 0
    for no in n_out:
        res.append(list(flat[o:o + no]))
        o += no
    return res


def kernel(x_prompt, x_sample, state_s5_re, state_s5_im, state_mlstm_C, state_mlstm_n, state_mlstm_m,
           state_mlstm_conv, meta_tokens, ffn1_norm, ffn1_w_gate, ffn1_w_up, ffn1_w_down, mix_norm, w_in,
           s5_A_re, s5_A_im, s5_log_dt, s5_B_re, s5_B_im, s5_C_re, s5_C_im, s5_D, s5_w_glu,
           mlstm_conv_w, mlstm_conv_b, mlstm_b_i, mlstm_b_f, mlstm_norm, w_branch_s5, w_branch_mlstm,
           w_out, ffn2_norm, ffn2_w_gate, ffn2_w_up, ffn2_w_down, final_norm):
    nbatch, seq, _ = x_prompt.shape
    nsamp, sseq, _ = x_sample.shape
    assert w_in.shape[0] == 1, "one layer per call"
    l = 0

    w1g, w1u, w1d = ffn1_w_gate[l], ffn1_w_up[l], ffn1_w_down[l]
    w2g, w2u, w2d = ffn2_w_gate[l], ffn2_w_up[l], ffn2_w_down[l]
    o_if = S5_WIDTH + QK_WIDTH + 2 * V_WIDTH
    o_gate = o_if + 2 * HEADS
    wt = w_in[l].T
    w2 = _win_halves(wt, o_gate, tr=WIN_CAST_ROWS)
    wglu, wbs, wbm, wo = (w[l].astype(BF16) for w in (s5_w_glu, w_branch_s5, w_branch_mlstm, w_out))
    g1 = ffn1_norm[l][None]
    gm = mix_norm[l][None]
    g2 = ffn2_norm[l][None]
    gf = final_norm[None]
    bif = jnp.pad(jnp.concatenate([mlstm_b_i[l], mlstm_b_f[l]]), (0, LANES - 2 * HEADS))[None]
    cw = mlstm_conv_w[l]
    cb = mlstm_conv_b[l][None]
    ng = mlstm_norm[l][None]
    d_skip = s5_D[l][None]

    *s5_ops, lam = _s5_operators(s5_A_re[l], s5_A_im[l], s5_log_dt[l], s5_B_re[l], s5_B_im[l])
    gh = S5_GROUPS * S5_GROUP
    w_toe, f_bf, e_bf = _s5_prep(*s5_ops, s5_C_re[l].reshape(gh, S5_STATE), s5_C_im[l].reshape(gh, S5_STATE))

    def win(h1, tm):
        return _win(h1, gm, w2, wt, o_if=o_if, ngate=N_BRANCH * D_MODEL, tm=tm)

    def mlstm(z, zif, ml_state, *, nseq, seqlen, lc):
        buf0, c0, n0, m0 = ml_state
        ym, buf, c, n, m = _mlstm(z, zif, cw, cb, bif, ng, buf0, c0, n0, m0.reshape(nseq, 1, HEADS),
                                  nseq=nseq, seqlen=seqlen, nb=1, lc=lc)
        return ym, buf, c, n, m.reshape(nseq, HEADS)

    def merge_part(h1, ys, ym, z, n):
        return _merge_part(h1, ys, ym, z, wglu, wbs, wbm, wo, n=n, tm=MERGE_TM)

    ntok_s = nsamp * sseq
    ntok_p = nbatch * seq
    x_sm = jnp.concatenate([x_sample.reshape(ntok_s, D_MODEL), meta_tokens], axis=0)
    ntok_sm = ntok_s + N_META
    h1_sm, *w1_bf = _ffn(x_sm, g1, w1g, w1u, w1d, n=ntok_sm, tm=ntok_sm, emit_bf16=True)
    z_sm, zif_sm = win(h1_sm, ntok_sm)
    h1_p = _ffn(x_prompt.reshape(ntok_p, D_MODEL), g1, *w1_bf, n=ntok_p, tm=FFN_TM)
    z_p, zif_p = win(h1_p, FFN_TM)

    z_m = jnp.tile(z_sm[ntok_s:], (nbatch, 1))
    zif_m = jnp.tile(zif_sm[ntok_s:], (nbatch, 1))
    zeros = lambda *s: jnp.zeros((nbatch,) + s, F32)

    ys_p, ys_s, *s5_st = _s5(z_m, z_p, z_sm, w_toe, f_bf, e_bf, lam, d_skip,
                             state_s5_re[l].reshape(nsamp, -1), state_s5_im[l].reshape(nsamp, -1),
                             nbatch=nbatch, len_m=N_META, len_p=seq, nsamp=nsamp, len_s=sseq)
    s5_p, s5_s = s5_st[:2], s5_st[2:]

    _, *ml_m = mlstm(z_m, zif_m, (zeros(CONV_W - 1, QK_WIDTH), zeros(HEADS, DV, DQK), zeros(HEADS, DQK), zeros(HEADS)),
                     nseq=nbatch, seqlen=N_META, lc=N_META)
    ym_p, *ml_p = mlstm(z_p, zif_p, ml_m, nseq=nbatch, seqlen=seq, lc=MLSTM_CHUNK)

    steps_p = ntok_p // MERGE_TM
    step_part, step_finish = _mlstm_step_part(
        z_sm, zif_sm, cw, cb, bif, ng, state_mlstm_conv[l], state_mlstm_C[l], state_mlstm_n[l], state_mlstm_m[l],
        nseq=nsamp, seqlen=sseq, nb=nsamp // steps_p)
    (h2_p,), step_out = _run_parts([merge_part(h1_p, ys_p, ym_p, z_p, ntok_p), step_part], steps_p, "merge_mlstm_step")
    ym_s, *ml_s = step_finish(*step_out)
    (h2_s,), = _run_parts([merge_part(h1_sm, ys_s, ym_s, z_sm, ntok_s)], ntok_s // MERGE_TM, "merge")
    y_s, *w2_bf = _ffn(h2_s, g2, w2g, w2u, w2d, gf, n=ntok_s, tm=ntok_s, emit_bf16=True)
    y_p = _ffn(h2_p, g2, *w2_bf, gf, n=ntok_p, tm=FFN_TM)

    def pack(n, s5_st, ml_st):
        buf, c, nn, m = ml_st
        return (s5_st[0].reshape(1, n, S5_GROUPS, S5_STATE), s5_st[1].reshape(1, n, S5_GROUPS, S5_STATE),
                c[None], nn[None], m[None], buf[None])

    return ((y_p.reshape(nbatch, seq, D_MODEL), y_s.reshape(nsamp, sseq, D_MODEL))
            + pack(nbatch, s5_p, ml_p) + pack(nsamp, s5_s, ml_s))
```

```python
import functools
from typing import Callable, NamedTuple

import jax
import jax.numpy as jnp
from jax import lax
from jax.experimental import pallas as pl
from jax.experimental.pallas import tpu as pltpu

F32 = jnp.float32
BF16 = jnp.bfloat16

D_MODEL = 2048
D_FF = 5632
N_META = 16
S5_WIDTH = 1024
S5_GROUP = 16
S5_GROUPS = 64
S5_STATE = 64
HEADS = 4
DQK = 128
DV = 256
QK_WIDTH = 1024
V_WIDTH = 1024
CONV_W = 4
N_BRANCH = 2
EPS = 1e-6

LANES = 128
SUBLANES = 8
MXU_DIM = 256
S5_T = 8
S5_GPB = LANES // S5_GROUP
S5_NBLK = S5_WIDTH // LANES
S5_SW = S5_GPB * S5_STATE
VMEM_LIMIT = 58 * 1024 * 1024
FFN_TM = 1024
FFN_TF_BF16 = 512
MERGE_TM = 256
WIN_CAST_ROWS = 456
S5_ROW_TILE = 256
MLSTM_CHUNK = 256


def _cparams(sem):
    return pltpu.CompilerParams(dimension_semantics=sem, vmem_limit_bytes=VMEM_LIMIT)


def _rmsnorm(x, g):
    ms = jnp.mean(x * x, axis=-1, keepdims=True)
    return (x * lax.rsqrt(ms + EPS)) * g


def _dot(a, b):
    return jnp.dot(a, b, preferred_element_type=F32)


def _dot_nt(a, b):
    return lax.dot_general(a, b, (((1,), (1,)), ((), ())), preferred_element_type=F32)


def _dot_tn(a, b):
    return lax.dot_general(a, b, (((0,), (0,)), ((), ())), preferred_element_type=F32)


def _split3(x):
    hi = x.astype(BF16)
    r = x - hi.astype(F32)
    mid = r.astype(BF16)
    lo = (r - mid.astype(F32)).astype(BF16)
    return hi, mid, lo


class _Part(NamedTuple):
    kernel: Callable
    in_specs: list
    args: list
    out_specs: list
    out_shape: list
    scratch: list


def _ffn_kernel(x_ref, g_ref, wg_ref, wu_ref, wd_ref, *rest, final_norm, emit_bf16):
    if final_norm:
        fg_ref, rest = rest[0], rest[1:]
    if emit_bf16:
        o_ref, wgb_ref, wub_ref, wdb_ref, xn_ref = rest
    else:
        o_ref, xn_ref = rest
    j = pl.program_id(1)

    @pl.when(j == 0)
    def _():
        xn_ref[...] = _rmsnorm(x_ref[...], g_ref[...]).astype(BF16)
        o_ref[...] = jnp.zeros(o_ref.shape, F32)

    wg, wu, wd = (w[...].astype(BF16) for w in (wg_ref, wu_ref, wd_ref))
    if emit_bf16:
        wgb_ref[...] = wg
        wub_ref[...] = wu
        wdb_ref[...] = wd
    xn = xn_ref[...]
    gt = _dot(xn, wg)
    up = _dot(xn, wu)
    act = (gt * jax.nn.sigmoid(gt) * up).astype(BF16)
    o_ref[...] += _dot(act, wd)

    @pl.when(j == pl.num_programs(1) - 1)
    def _():
        h = x_ref[...] + 0.5 * o_ref[...]
        if final_norm:
            h = _rmsnorm(h, fg_ref[...])
        o_ref[...] = h


def _ffn(x, g, wg, wu, wd, final_g=None, *, n, tm, tf=256, emit_bf16=False):
    assert not emit_bf16 or n == tm
    wspec = [
        pl.BlockSpec((D_MODEL, tf), lambda i, j: (0, j)),
        pl.BlockSpec((D_MODEL, tf), lambda i, j: (0, j)),
        pl.BlockSpec((tf, D_MODEL), lambda i, j: (j, 0)),
    ]
    in_specs = [pl.BlockSpec((tm, D_MODEL), lambda i, j: (i, 0)), pl.BlockSpec((1, D_MODEL), lambda i, j: (0, 0)), *wspec]
    args = [x, g, wg, wu, wd]
    if final_g is not None:
        in_specs.append(pl.BlockSpec((1, D_MODEL), lambda i, j: (0, 0)))
        args.append(final_g)
    out_specs = [pl.BlockSpec((tm, D_MODEL), lambda i, j: (i, 0))]
    out_shape = [jax.ShapeDtypeStruct((n, D_MODEL), F32)]
    if emit_bf16:
        out_specs += wspec
        out_shape += [jax.ShapeDtypeStruct(w.shape, BF16) for w in (wg, wu, wd)]
    res = pl.pallas_call(
        functools.partial(_ffn_kernel, final_norm=final_g is not None, emit_bf16=emit_bf16),
        grid=(n // tm, D_FF // tf),
        in_specs=in_specs,
        out_specs=out_specs,
        out_shape=out_shape,
        scratch_shapes=[pltpu.VMEM((tm, D_MODEL), BF16)],
        compiler_params=_cparams(("parallel", "arbitrary")),
        name="ffn",
    )(*args)
    return res if emit_bf16 else res[0]


def _cast_rows_kernel(w_ref, o_ref, *, tail):
    last = pl.num_programs(0) - 1

    @pl.when(pl.program_id(0) < last)
    def _():
        o_ref[0] = w_ref[...].astype(BF16)

    @pl.when(pl.program_id(0) == last)
    def _():
        o_ref[0, :tail] = w_ref[:tail].astype(BF16)
        o_ref[0, tail:] = jnp.zeros((o_ref.shape[1] - tail, o_ref.shape[2]), BF16)


def _win_halves(wt, o_gate, *, tr):
    rows, k = wt.shape
    per = o_gate // tr
    tail = rows - (2 * per - 1) * tr
    assert per * tr == o_gate and tr % 8 == 0 and 0 < tail < tr and tail % 8 == 0
    return pl.pallas_call(
        functools.partial(_cast_rows_kernel, tail=tail),
        grid=(2 * per,),
        in_specs=[pl.BlockSpec((tr, k), lambda i: (i, 0))],
        out_specs=pl.BlockSpec((1, tr, k), lambda i: (i // per, i % per, 0)),
        out_shape=jax.ShapeDtypeStruct((2, o_gate, k), BF16),
        compiler_params=_cparams(("parallel",)),
        name="w_in_cast",
    )(wt)


def _win_kernel(h_ref, g_ref, wa_ref, wb_ref, wif_ref, z_ref, zif_ref, un_ref, *, na):
    j = pl.program_id(1)

    @pl.when(j == 0)
    def _():
        un = _rmsnorm(h_ref[...], g_ref[...]).astype(BF16)
        un_ref[...] = un
        zif_ref[...] = _dot_nt(un, wif_ref[...].astype(BF16))

    @pl.when(j < na)
    def _():
        z_ref[...] = _dot_nt(un_ref[...], wa_ref[0])

    @pl.when(j >= na)
    def _():
        z_ref[...] = _dot_nt(un_ref[...], wb_ref[0])


def _win(h, g, w2, wt, *, o_if, ngate, tm, tn=1024):
    n = h.shape[0]
    na = o_if // tn
    nb = ngate // tn
    return pl.pallas_call(
        functools.partial(_win_kernel, na=na),
        grid=(n // tm, na + nb),
        in_specs=[
            pl.BlockSpec((tm, D_MODEL), lambda i, j: (i, 0)),
            pl.BlockSpec((1, D_MODEL), lambda i, j: (0, 0)),
            pl.BlockSpec((1, tn, D_MODEL), lambda i, j: (0, jnp.minimum(j, na - 1), 0)),
            pl.BlockSpec((1, tn, D_MODEL), lambda i, j: (1, jnp.maximum(j - na, 0), 0)),
            pl.BlockSpec((LANES, D_MODEL), lambda i, j: (o_if // LANES, 0)),
        ],
        out_specs=[
            pl.BlockSpec((tm, tn), lambda i, j: (i, j)),
            pl.BlockSpec((tm, LANES), lambda i, j: (i, 0)),
        ],
        out_shape=[
            jax.ShapeDtypeStruct((n, (na + nb) * tn), F32),
            jax.ShapeDtypeStruct((n, LANES), F32),
        ],
        scratch_shapes=[pltpu.VMEM((tm, D_MODEL), BF16)],
        compiler_params=_cparams(("parallel", "arbitrary")),
        name="w_in",
    )(h, g, w2, w2, wt)


def _s5_prep_kernel(bbr_ref, bbi_ref, ptr_ref, pti_ref, pwr_ref, pwi_ref, cnr_ref, cni_ref,
                    w_ref, f_ref, e_ref):
    n = S5_T * LANES
    i32 = jnp.int32
    lg_h = S5_GROUP.bit_length() - 1
    lg_p = S5_STATE.bit_length() - 1
    lg_l = LANES.bit_length() - 1
    gmask = S5_GPB - 1

    def tile_mat(k, c, src_of_col):
        kk = lax.broadcasted_iota(i32, (k, c), 0)
        cc = lax.broadcasted_iota(i32, (k, c), 1)
        return jnp.where(kk == src_of_col(cc), 1.0, 0.0).astype(BF16)

    def group_mask(r, c, row_group, col_group):
        rr = lax.broadcasted_iota(i32, (r, c), 0)
        cc = lax.broadcasted_iota(i32, (r, c), 1)
        return row_group(rr) == col_group(cc)

    def tiled(x, mat):
        hi, mid, _ = _split3(x)
        return _dot(hi, mat) + _dot(mid, mat)

    def lag_power(pt):
        return jnp.concatenate(
            [jnp.broadcast_to(pt[:, S5_T - 1 - s:S5_T - s], (S5_SW, LANES)) for s in range(S5_T)], axis=1)

    def dot_hi(a, b):
        a_hi, a_mid, _ = _split3(a)
        b_hi, b_mid, _ = _split3(b)
        return _dot(a_hi, b_hi) + _dot(a_mid, b_hi) + _dot(a_hi, b_mid)

    sel_h = tile_mat(S5_GROUP, n, lambda c: c & (S5_GROUP - 1))
    mf = group_mask(S5_SW, n, lambda r: r >> lg_p, lambda c: (c >> lg_h) & gmask)
    lpr = lag_power(ptr_ref[...])
    lpi = lag_power(pti_ref[...])
    bxr = tiled(bbr_ref[...], sel_h)
    bxi = tiled(bbi_ref[...], sel_h)
    ftr = jnp.where(mf, lpr * bxr - lpi * bxi, 0.0)
    fti = jnp.where(mf, lpr * bxi + lpi * bxr, 0.0)
    f_ref[0, :S5_SW, :] = ftr.astype(BF16)
    f_ref[0, S5_SW:, :] = fti.astype(BF16)

    sel_p = tile_mat(S5_STATE, S5_SW, lambda c: c & (S5_STATE - 1))
    mc = group_mask(LANES, S5_SW, lambda r: r >> lg_h, lambda c: c >> lg_p)
    cxr = jnp.where(mc, tiled(cnr_ref[...], sel_p), 0.0)
    cxi = jnp.where(mc, tiled(cni_ref[...], sel_p), 0.0)

    for t in range(S5_T):
        pr = pwr_ref[t + 1:t + 2, :]
        pi = pwi_ref[t + 1:t + 2, :]
        e_ref[0, t * LANES:(t + 1) * LANES, :S5_SW] = (cxr * pr - cxi * pi).astype(BF16)
        e_ref[0, t * LANES:(t + 1) * LANES, S5_SW:] = (-(cxr * pi + cxi * pr)).astype(BF16)

    cn = jnp.concatenate([cxr, -cxi], axis=1)
    w_ref[...] = jnp.zeros(w_ref.shape, w_ref.dtype)
    k_all = dot_hi(cn, jnp.concatenate([ftr, fti], axis=0)).astype(BF16)
    for lag in range(S5_T):
        c0 = (S5_T - 1 - lag) * LANES
        kt = k_all[:, c0:c0 + LANES]
        for s in range(S5_T - lag):
            t = s + lag
            w_ref[0, t * LANES:(t + 1) * LANES, s * LANES:(s + 1) * LANES] = kt


def _s5_prep(bb_re, bb_im, pwt_re, pwt_im, pw_re, pw_im, cn_re, cn_im):
    n = S5_T * LANES
    rows = lambda r, c: pl.BlockSpec((r, c), lambda j: (j, 0))
    cols = lambda r, c: pl.BlockSpec((r, c), lambda j: (0, j))
    out = pl.BlockSpec((1, n, n), lambda j: (j, 0, 0))
    return pl.pallas_call(
        _s5_prep_kernel,
        grid=(S5_NBLK,),
        in_specs=[rows(S5_SW, S5_GROUP)] * 4 + [cols(S5_GROUP, S5_SW)] * 2 + [rows(LANES, S5_STATE)] * 2,
        out_specs=[out, out, out],
        out_shape=[jax.ShapeDtypeStruct((S5_NBLK, n, n), BF16)] * 3,
        compiler_params=_cparams(("parallel",)),
        name="s5_prep",
    )(bb_re, bb_im, pwt_re, pwt_im, pw_re, pw_im, cn_re, cn_im)


def _s5_group(x_ref, y_ref, w, f, e, ar, ai, dt, h0, yi_ref, s_ref, hin_ref, *, nseq, nblk):
    rows = nseq * nblk
    rt = min(rows, S5_ROW_TILE)

    def load_u(r0):
        return jnp.concatenate(
            [x_ref[pl.ds(r0 * S5_T + s, rt, stride=S5_T), :] for s in range(S5_T)], axis=1)

    for r0 in range(0, rows, rt):
        ub = load_u(r0).astype(BF16)
        if y_ref is not None:
            for c0 in range(0, S5_T * LANES, MXU_DIM):
                c1 = c0 + MXU_DIM
                yi_ref[r0:r0 + rt, c0:c1] = _dot_nt(ub[:, :c1], w[c0:c1, :c1])
        s_ref[r0:r0 + rt, :] = _dot_nt(ub, f)

    if nblk == 1:
        hre, him = h0
        hin_ref[0:rows, :S5_SW] = hre
        hin_ref[0:rows, S5_SW:] = him
        fin = (ar * hre - ai * him + s_ref[0:rows, :S5_SW], ar * him + ai * hre + s_ref[0:rows, S5_SW:])
    else:
        def body(c, carry):
            new = []
            for b in range(nseq):
                hre, him = carry[2 * b], carry[2 * b + 1]
                row = b * nblk + c
                hin_ref[pl.ds(row, 1), :S5_SW] = hre
                hin_ref[pl.ds(row, 1), S5_SW:] = him
                sre = s_ref[pl.ds(row, 1), :S5_SW]
                sim = s_ref[pl.ds(row, 1), S5_SW:]
                new.append(ar * hre - ai * him + sre)
                new.append(ar * him + ai * hre + sim)
            return tuple(new)

        flat = lax.fori_loop(0, nblk, body, tuple(v for pair in h0 for v in pair))
        fin = [(flat[2 * b], flat[2 * b + 1]) for b in range(nseq)]

    if y_ref is not None:
        for r0 in range(0, rows, rt):
            yo = _dot_nt(hin_ref[r0:r0 + rt, :].astype(BF16), e)
            y = jax.nn.gelu(yi_ref[r0:r0 + rt, :] + yo + load_u(r0) * dt)
            for t in range(S5_T):
                y_ref[pl.ds(r0 * S5_T + t, rt, stride=S5_T), :] = y[:, t * LANES:(t + 1) * LANES]
    return fin


def _s5_kernel(xm_ref, xp_ref, xs_ref, w_ref, f_ref, e_ref, lam_ref, d_ref, h0re_ref, h0im_ref,
               yp_ref, ys_ref, pre_ref, pim_ref, sre_ref, sim_ref, yi_ref, s_ref, hin_ref,
               *, nbatch, nblk_m, nblk_p, nsamp, nblk_s):
    ar = lam_ref[0, 0:1, :]
    ai = lam_ref[0, 1:2, :]
    dt = jnp.concatenate([d_ref[...]] * S5_T, axis=1)
    ops = (w_ref[0], f_ref[0], e_ref[0], ar, ai, dt)
    scratch = (yi_ref, s_ref, hin_ref)
    zero = jnp.zeros((1, S5_SW), F32)
    st = _s5_group(xm_ref, None, *ops, [(zero, zero)] * nbatch, *scratch, nseq=nbatch, nblk=nblk_m)
    st = _s5_group(xp_ref, yp_ref, *ops, st, *scratch, nseq=nbatch, nblk=nblk_p)
    for b in range(nbatch):
        pre_ref[b:b + 1, :] = st[b][0]
        pim_ref[b:b + 1, :] = st[b][1]
    fin = _s5_group(xs_ref, ys_ref, *ops, (h0re_ref[...], h0im_ref[...]), *scratch, nseq=nsamp, nblk=nblk_s)
    sre_ref[...] = fin[0]
    sim_ref[...] = fin[1]


def _s5(z_m, z_p, z_s, w, f, e, lam, d, h0re, h0im, *, nbatch, len_m, len_p, nsamp, len_s):
    assert len_s == S5_T
    n = S5_T * LANES
    tok_m, tok_p, tok_s = nbatch * len_m, nbatch * len_p, nsamp * len_s
    rows = max(nbatch * len_p, nsamp * len_s, nbatch * len_m) // S5_T
    col = lambda r: pl.BlockSpec((r, LANES), lambda j: (0, j))
    op = pl.BlockSpec((1, n, n), lambda j: (j, 0, 0))
    st = lambda r: pl.BlockSpec((r, S5_SW), lambda j: (0, j))
    gp = S5_GROUPS * S5_STATE
    kern = functools.partial(_s5_kernel, nbatch=nbatch, nblk_m=len_m // S5_T, nblk_p=len_p // S5_T,
                             nsamp=nsamp, nblk_s=len_s // S5_T)
    return pl.pallas_call(
        kern,
        grid=(S5_NBLK,),
        in_specs=[col(tok_m), col(tok_p), col(tok_s), op, op, op,
                  pl.BlockSpec((1, 2, S5_SW), lambda j: (j, 0, 0)), col(1), st(nsamp), st(nsamp)],
        out_specs=[col(tok_p), col(tok_s), st(nbatch), st(nbatch), st(nsamp), st(nsamp)],
        out_shape=[
            jax.ShapeDtypeStruct((tok_p, S5_WIDTH), F32),
            jax.ShapeDtypeStruct((tok_s, S5_WIDTH), F32),
            jax.ShapeDtypeStruct((nbatch, gp), F32),
            jax.ShapeDtypeStruct((nbatch, gp), F32),
            jax.ShapeDtypeStruct((nsamp, gp), F32),
            jax.ShapeDtypeStruct((nsamp, gp), F32),
        ],
        scratch_shapes=[pltpu.VMEM((rows, n), F32)] * 3,
        compiler_params=_cparams(("parallel",)),
        name="s5",
    )(z_m, z_p, z_s, w, f, e, lam, d, h0re, h0im)


def _s5_operators(a_re, a_im, log_dt, b_re, b_im):
    dt = jnp.exp(log_dt)[:, None]
    ar, ai = a_re * dt, a_im * dt
    taus = jnp.arange(S5_T + 1, dtype=F32)[:, None, None]
    pmag = jnp.exp(taus * ar[None])
    pw_re = pmag * jnp.cos(taus * ai[None])
    pw_im = pmag * jnp.sin(taus * ai[None])
    nr, ni = pw_re[1] - 1.0, pw_im[1]
    den = a_re * a_re + a_im * a_im
    q_re = (nr * a_re + ni * a_im) / den
    q_im = (ni * a_re - nr * a_im) / den
    gp = S5_GROUPS * S5_STATE
    bb_re = (q_re[:, :, None] * b_re - q_im[:, :, None] * b_im).reshape(gp, S5_GROUP)
    bb_im = (q_re[:, :, None] * b_im + q_im[:, :, None] * b_re).reshape(gp, S5_GROUP)
    padrows = ((0, S5_GROUP - (S5_T + 1)), (0, 0))
    pw_re2 = jnp.pad(pw_re.reshape(S5_T + 1, gp), padrows)
    pw_im2 = jnp.pad(pw_im.reshape(S5_T + 1, gp), padrows)
    lam = jnp.stack([pw_re[S5_T].reshape(S5_NBLK, S5_SW), pw_im[S5_T].reshape(S5_NBLK, S5_SW)], axis=1)
    return bb_re, bb_im, pw_re2.T, pw_im2.T, pw_re2, pw_im2, lam


def _mlstm_kernel(qk_ref, v_ref, o_ref, if_ref, cw_ref, cb_ref, bif_ref, ng_ref,
                  buf0_ref, c0_ref, n0_ref, m0_ref,
                  y_ref, bufo_ref, co_ref, no_ref, mo_ref, xp_ref, *, nb, lc, carry):
    if carry:
        @pl.when(pl.program_id(1) == 0)
        def _():
            bufo_ref[...] = buf0_ref[...]
            co_ref[...] = c0_ref[...]
            no_ref[...] = n0_ref[...]
            mo_ref[...] = m0_ref[...]
        bufs_ref, cs_ref, ns_ref, ms_ref = bufo_ref, co_ref, no_ref, mo_ref
    else:
        bufs_ref, cs_ref, ns_ref, ms_ref = buf0_ref, c0_ref, n0_ref, m0_ref

    lp = max(lc, LANES)
    pad = CONV_W - 1
    heads = range(HEADS)
    head_row = lax.broadcasted_iota(jnp.int32, (HEADS, DQK), 0)
    head_lane = lax.broadcasted_iota(jnp.int32, (1, HEADS), 1)
    row = lax.broadcasted_iota(jnp.int32, (lc, lc), 0)
    col = lax.broadcasted_iota(jnp.int32, (lc, lc), 1)
    causal = row >= col
    tril = jnp.where(causal, 1.0, 0.0).astype(BF16)
    lane = lax.broadcasted_iota(jnp.int32, (lc, LANES), 1)
    is_i = lane < HEADS
    is_f = jnp.logical_and(lane >= HEADS, lane < 2 * HEADS)

    for s in range(nb):
        r0 = s * lc
        x1 = SUBLANES
        xp_ref[s, 0:x1 - pad, :] = jnp.zeros((x1 - pad, QK_WIDTH), F32)
        xp_ref[s, x1 - pad:x1, :] = bufs_ref[s]
        xp_ref[s, x1:x1 + lc, :] = qk_ref[r0:r0 + lc, :]
        xp = xp_ref[s]
        conv = cb_ref[...] + xp[x1:x1 + lc] * cw_ref[pad:pad + 1, :]
        for j in range(1, CONV_W):
            conv = conv + pltpu.roll(xp, j, 0)[x1:x1 + lc] * cw_ref[pad - j:pad - j + 1, :]
        bufo_ref[s] = xp_ref[s, x1 + lc - pad:x1 + lc, :]
        n_all = ns_ref[s]
        m_all = ms_ref[s]
        qk = conv * jax.nn.sigmoid(conv)
        q = qk[:, :QK_WIDTH // 2] * (DQK ** -0.5)
        k = qk[:, QK_WIDTH // 2:]

        gate = if_ref[r0:r0 + lc, :] + bif_ref[...]
        lf = jnp.where(is_f, jax.nn.log_sigmoid(gate), 0.0)
        parts = _split3(lf)
        bcum = _dot(tril, parts[0]) + _dot(tril, parts[1]) + _dot(tril, parts[2])
        pc = jnp.where(is_i, gate, bcum)
        if lc < lp:
            pc_t = jnp.concatenate([pc, jnp.zeros((lp - lc, LANES), F32)], axis=0).T[:, :lc]
        else:
            pc_t = pc.T

        i_col = [pc[:, h:h + 1] for h in heads]
        b_col = [pc[:, HEADS + h:HEADS + h + 1] for h in heads]
        m0 = [m_all[:, h:h + 1] for h in heads]
        qh = [q[:, h * DQK:(h + 1) * DQK] for h in heads]
        kh = [k[:, h * DQK:(h + 1) * DQK] for h in heads]
        qb = [x.astype(BF16) for x in qh]
        vb = [v_ref[r0:r0 + lc, h * DV:(h + 1) * DV].astype(BF16) for h in heads]
        c_prev = [cs_ref[s, h] for h in heads]
        n_prev = [n_all[h:h + 1, :] for h in heads]

        qk_t = [_dot_nt(qb[h], kh[h].astype(BF16)) for h in heads]
        carried = [_dot_nt(qb[h], c_prev[h].astype(BF16)) for h in heads]
        logw = [jnp.where(causal, b_col[h] - pc_t[HEADS + h:HEADS + h + 1, :] + pc_t[h:h + 1, :], -jnp.inf)
                for h in heads]
        g = [b_col[h] + m0[h] for h in heads]
        m = [jnp.maximum(g[h], jnp.max(logw[h], axis=-1, keepdims=True)) for h in heads]
        inter = [jnp.exp(g[h] - m[h]) for h in heads]
        sc = [qk_t[h] * jnp.exp(logw[h] - m[h]) for h in heads]
        num = [_dot(sc[h].astype(BF16), vb[h]) + inter[h] * carried[h] for h in heads]
        den = [jnp.sum(sc[h], axis=-1, keepdims=True)
               + inter[h] * jnp.sum(qh[h] * n_prev[h], axis=-1, keepdims=True) for h in heads]
        hh = [num[h] / jnp.maximum(jnp.abs(den[h]), jnp.exp(-m[h])) for h in heads]
        hh = [x * lax.rsqrt(jnp.mean(x * x, axis=-1, keepdims=True) + EPS) for x in hh]
        for h in heads:
            og = jax.nn.sigmoid(o_ref[r0:r0 + lc, h * DV:(h + 1) * DV])
            y_ref[r0:r0 + lc, h * DV:(h + 1) * DV] = hh[h] * ng_ref[:, h * DV:(h + 1) * DV] * og

        b_last = [x[lc - 1:lc, :] for x in b_col]
        m_new = [x[lc - 1:lc, :] for x in m]
        decay = [jnp.exp(b_last[h] + m0[h] - m_new[h]) for h in heads]
        kw = [kh[h] * jnp.exp(b_last[h] - b_col[h] + i_col[h] - m_new[h]) for h in heads]
        upd = [_dot_tn(vb[h], kw[h].astype(BF16)) for h in heads]
        n_new_all = jnp.zeros((HEADS, DQK), F32)
        m_new_all = jnp.zeros((1, HEADS), F32)
        for h in heads:
            co_ref[s, h] = decay[h] * c_prev[h] + upd[h]
            n_new = decay[h] * n_prev[h] + jnp.sum(kw[h], axis=0, keepdims=True)
            n_new_all = jnp.where(head_row == h, n_new, n_new_all)
            m_new_all = jnp.where(head_lane == h, m_new[h], m_new_all)
        no_ref[s] = n_new_all
        mo_ref[s] = m_new_all


def _mlstm(z, zif, cw, cb, bif, ng, buf0, c0, n0, m0, *, nseq, seqlen, nb, lc):
    tok = nseq * seqlen
    nchunk = seqlen // lc
    rows = nb * lc
    tok_map = lambda col: (lambda i, c: (i * nchunk + c, col))
    const2 = lambda i, c: (0, 0)
    st3 = lambda i, c: (i, 0, 0)
    st4 = lambda i, c: (i, 0, 0, 0)
    kern = functools.partial(_mlstm_kernel, nb=nb, lc=lc, carry=nchunk > 1)
    return pl.pallas_call(
        kern,
        grid=(nseq // nb, nchunk),
        in_specs=[
            pl.BlockSpec((rows, QK_WIDTH), tok_map(1)),
            pl.BlockSpec((rows, V_WIDTH), tok_map(2)),
            pl.BlockSpec((rows, V_WIDTH), tok_map(3)),
            pl.BlockSpec((rows, LANES), tok_map(0)),
            pl.BlockSpec((CONV_W, QK_WIDTH), const2),
            pl.BlockSpec((1, QK_WIDTH), const2),
            pl.BlockSpec((1, LANES), const2),
            pl.BlockSpec((1, V_WIDTH), const2),
            pl.BlockSpec((nb, CONV_W - 1, QK_WIDTH), st3),
            pl.BlockSpec((nb, HEADS, DV, DQK), st4),
            pl.BlockSpec((nb, HEADS, DQK), st3),
            pl.BlockSpec((nb, 1, HEADS), st3),
        ],
        out_specs=[
            pl.BlockSpec((rows, V_WIDTH), tok_map(0)),
            pl.BlockSpec((nb, CONV_W - 1, QK_WIDTH), st3),
            pl.BlockSpec((nb, HEADS, DV, DQK), st4),
            pl.BlockSpec((nb, HEADS, DQK), st3),
            pl.BlockSpec((nb, 1, HEADS), st3),
        ],
        out_shape=[
            jax.ShapeDtypeStruct((tok, V_WIDTH), F32),
            jax.ShapeDtypeStruct((nseq, CONV_W - 1, QK_WIDTH), F32),
            jax.ShapeDtypeStruct((nseq, HEADS, DV, DQK), F32),
            jax.ShapeDtypeStruct((nseq, HEADS, DQK), F32),
            jax.ShapeDtypeStruct((nseq, 1, HEADS), F32),
        ],
        scratch_shapes=[pltpu.VMEM((nb, lc + SUBLANES, QK_WIDTH), F32)],
        compiler_params=_cparams(("parallel", "arbitrary")),
        name="mlstm",
    )(z, z, z, zif, cw, cb, bif, ng, buf0, c0, n0, m0)


def _mlstm_step_kernel(qk_ref, v_ref, o_ref, if_ref, cw_ref, cb_ref, bif_ref, ng_ref,
                       buf0_ref, c0_ref, n0x_ref, m0x_ref,
                       y_ref, bufo_ref, co_ref, nox_ref, mox_ref, xp_ref, conv_ref, *, nb, lc):
    rows = nb * lc
    lg = lc.bit_length() - 1
    i32 = jnp.int32
    row = lax.broadcasted_iota(i32, (rows, rows), 0)
    col = lax.broadcasted_iota(i32, (rows, rows), 1)
    same = (row >> lg) == (col >> lg)
    causal = jnp.logical_and(same, row >= col)
    same_b = jnp.where(same, 1.0, 0.0).astype(BF16)
    tril_b = jnp.where(causal, 1.0, 0.0).astype(BF16)
    lane = lax.broadcasted_iota(i32, (rows, LANES), 1)
    is_i = lane < HEADS
    is_f = jnp.logical_and(lane >= HEADS, lane < 2 * HEADS)

    def seg_dot(mat, x):
        hi, mid, lo = _split3(x)
        return _dot(mat, hi) + _dot(mat, mid) + _dot(mat, lo)

    x0 = SUBLANES - (CONV_W - 1)
    for s in range(nb):
        xp_ref[s, x0:SUBLANES, :] = buf0_ref[s]
        xp_ref[s, SUBLANES:SUBLANES + lc, :] = qk_ref[s * lc:(s + 1) * lc, :]
        conv_ref[s * lc:(s + 1) * lc, :] = cb_ref[...] + sum(
            xp_ref[s, x0 + j:x0 + j + lc, :] * cw_ref[j:j + 1, :] for j in range(CONV_W))
        bufo_ref[s] = xp_ref[s, x0 + lc:SUBLANES + lc, :]
    conv = conv_ref[...]
    qk = conv * jax.nn.sigmoid(conv)
    q = qk[:, :QK_WIDTH // 2] * (DQK ** -0.5)
    k = qk[:, QK_WIDTH // 2:]

    gate = if_ref[...] + bif_ref[...]
    lf = jnp.where(is_f, jax.nn.log_sigmoid(gate), 0.0)
    bcum = seg_dot(tril_b, lf)
    btot = seg_dot(same_b, lf)
    pc = jnp.where(is_i, gate, bcum)
    if rows < LANES:
        pc_t = jnp.concatenate([pc, jnp.zeros((LANES - rows, LANES), F32)], axis=0).T[:, :rows]
    else:
        pc_t = pc.T
    m0x = m0x_ref[...]
    heads = range(HEADS)
    seqs = range(nb)

    i_col = [pc[:, h:h + 1] for h in heads]
    b_col = [pc[:, HEADS + h:HEADS + h + 1] for h in heads]
    i_row = [pc_t[h:h + 1, :] for h in heads]
    b_row = [pc_t[HEADS + h:HEADS + h + 1, :] for h in heads]
    b_last = [btot[:, HEADS + h:HEADS + h + 1] for h in heads]
    m0 = [m0x[:, h:h + 1] for h in heads]
    qh = [q[:, h * DQK:(h + 1) * DQK] for h in heads]
    kh = [k[:, h * DQK:(h + 1) * DQK] for h in heads]
    qb = [x.astype(BF16) for x in qh]
    vb = [v_ref[:, h * DV:(h + 1) * DV].astype(BF16) for h in heads]
    n_prev = [n0x_ref[:, h * DQK:(h + 1) * DQK] for h in heads]

    qk_t = [_dot_nt(qb[h], kh[h].astype(BF16)) for h in heads]
    carried = [jnp.concatenate([_dot_nt(qb[h][s * lc:(s + 1) * lc], c0_ref[s, h].astype(BF16)) for s in seqs],
                               axis=0) for h in heads]
    logw = [jnp.where(causal, b_col[h] - b_row[h] + i_row[h], -jnp.inf) for h in heads]
    g = [b_col[h] + m0[h] for h in heads]
    m = [jnp.maximum(g[h], jnp.max(logw[h], axis=-1, keepdims=True)) for h in heads]
    inter = [jnp.exp(g[h] - m[h]) for h in heads]
    lw_end = [jnp.where(same, b_last[h] - b_row[h] + i_row[h], -jnp.inf) for h in heads]
    m_new = [jnp.maximum(b_last[h] + m0[h], jnp.max(lw_end[h], axis=-1, keepdims=True)) for h in heads]
    decay = [jnp.exp(b_last[h] + m0[h] - m_new[h]) for h in heads]
    sc = [qk_t[h] * jnp.exp(logw[h] - m[h]) for h in heads]
    num = [_dot(sc[h].astype(BF16), vb[h]) + inter[h] * carried[h] for h in heads]
    den = [jnp.sum(sc[h], axis=-1, keepdims=True)
           + inter[h] * jnp.sum(qh[h] * n_prev[h], axis=-1, keepdims=True) for h in heads]
    hh = [num[h] / jnp.maximum(jnp.abs(den[h]), jnp.exp(-m[h])) for h in heads]
    hh = [x * lax.rsqrt(jnp.mean(x * x, axis=-1, keepdims=True) + EPS) for x in hh]
    for h in heads:
        og = jax.nn.sigmoid(o_ref[:, h * DV:(h + 1) * DV])
        y_ref[:, h * DV:(h + 1) * DV] = hh[h] * ng_ref[:, h * DV:(h + 1) * DV] * og

    kw = [kh[h] * jnp.exp(b_last[h] - b_col[h] + i_col[h] - m_new[h]) for h in heads]
    kwb = [x.astype(BF16) for x in kw]
    upd = [[_dot_tn(vb[h][s * lc:(s + 1) * lc], kwb[h][s * lc:(s + 1) * lc]) for s in seqs] for h in heads]
    mox = jnp.zeros((rows, LANES), F32)
    for h in heads:
        for s in seqs:
            co_ref[s, h] = decay[h][s * lc:s * lc + 1, :] * c0_ref[s, h] + upd[h][s]
        nox_ref[:, h * DQK:(h + 1) * DQK] = decay[h] * n_prev[h] + seg_dot(same_b, kw[h])
        mox = jnp.where(lane == h, m_new[h], mox)
    mox_ref[...] = mox


def _mlstm_step_part(z, zif, cw, cb, bif, ng, buf0, c0, n0, m0, *, nseq, seqlen, nb):
    lc = seqlen
    assert lc & (lc - 1) == 0 and lc >= CONV_W - 1 and nseq % nb == 0 and (nb * lc) % SUBLANES == 0
    tok = nseq * seqlen
    rows = nb * lc
    n0x = jnp.repeat(n0.reshape(nseq, HEADS * DQK), lc, axis=0)
    m0x = jnp.repeat(jnp.pad(m0, ((0, 0), (0, LANES - HEADS))), lc, axis=0)
    tok_map = lambda col: (lambda i: (i, col))
    const2 = lambda i: (0, 0)
    st3 = lambda i: (i, 0, 0)
    st4 = lambda i: (i, 0, 0, 0)
    part = _Part(
        kernel=functools.partial(_mlstm_step_kernel, nb=nb, lc=lc),
        in_specs=[
            pl.BlockSpec((rows, QK_WIDTH), tok_map(1)),
            pl.BlockSpec((rows, V_WIDTH), tok_map(2)),
            pl.BlockSpec((rows, V_WIDTH), tok_map(3)),
            pl.BlockSpec((rows, LANES), tok_map(0)),
            pl.BlockSpec((CONV_W, QK_WIDTH), const2),
            pl.BlockSpec((1, QK_WIDTH), const2),
            pl.BlockSpec((1, LANES), const2),
            pl.BlockSpec((1, V_WIDTH), const2),
            pl.BlockSpec((nb, CONV_W - 1, QK_WIDTH), st3),
            pl.BlockSpec((nb, HEADS, DV, DQK), st4),
            pl.BlockSpec((rows, HEADS * DQK), tok_map(0)),
            pl.BlockSpec((rows, LANES), tok_map(0)),
        ],
        args=[z, z, z, zif, cw, cb, bif, ng, buf0, c0, n0x, m0x],
        out_specs=[
            pl.BlockSpec((rows, V_WIDTH), tok_map(0)),
            pl.BlockSpec((nb, CONV_W - 1, QK_WIDTH), st3),
            pl.BlockSpec((nb, HEADS, DV, DQK), st4),
            pl.BlockSpec((rows, HEADS * DQK), tok_map(0)),
            pl.BlockSpec((rows, LANES), tok_map(0)),
        ],
        out_shape=[
            jax.ShapeDtypeStruct((tok, V_WIDTH), F32),
            jax.ShapeDtypeStruct((nseq, CONV_W - 1, QK_WIDTH), F32),
            jax.ShapeDtypeStruct((nseq, HEADS, DV, DQK), F32),
            jax.ShapeDtypeStruct((tok, HEADS * DQK), F32),
            jax.ShapeDtypeStruct((tok, LANES), F32),
        ],
        scratch=[pltpu.VMEM((nb, lc + SUBLANES, QK_WIDTH), F32), pltpu.VMEM((rows, QK_WIDTH), F32)],
    )

    def finish(y, buf, c, nox, mox):
        return y, buf, c, nox[::lc].reshape(nseq, HEADS, DQK), mox[::lc, :HEADS]

    return part, finish


def _merge_kernel(h_ref, ys_ref, ym_ref, g1_ref, g2_ref, wglu_ref, wbs_ref, wbm_ref, wo_ref, o_ref):
    ys = ys_ref[...]
    glu = ys * jax.nn.sigmoid(_dot(ys.astype(BF16), wglu_ref[...]))
    a = _dot(glu.astype(BF16), wbs_ref[...])
    b = _dot(ym_ref[...].astype(BF16), wbm_ref[...])
    merged = jax.nn.sigmoid(g1_ref[...]) * a + jax.nn.sigmoid(g2_ref[...]) * b
    o_ref[...] = h_ref[...] + _dot(merged.astype(BF16), wo_ref[...])


def _merge_part(h, ys, ym, z, wglu, wbs, wbm, wo, *, n, tm):
    row = lambda i: (i, 0)
    const = lambda i: (0, 0)
    resident = lambda shape: pl.BlockSpec(shape, const, pipeline_mode=pl.Buffered(1))
    return _Part(
        kernel=_merge_kernel,
        in_specs=[
            pl.BlockSpec((tm, D_MODEL), row),
            pl.BlockSpec((tm, S5_WIDTH), row),
            pl.BlockSpec((tm, V_WIDTH), row),
            pl.BlockSpec((tm, D_MODEL), lambda i: (i, 2)),
            pl.BlockSpec((tm, D_MODEL), lambda i: (i, 3)),
            resident((S5_WIDTH, S5_WIDTH)),
            resident((S5_WIDTH, D_MODEL)),
            resident((V_WIDTH, D_MODEL)),
            resident((D_MODEL, D_MODEL)),
        ],
        args=[h, ys, ym, z, z, wglu, wbs, wbm, wo],
        out_specs=[pl.BlockSpec((tm, D_MODEL), row)],
        out_shape=[jax.ShapeDtypeStruct((n, D_MODEL), F32)],
        scratch=[],
    )


def _run_parts(parts, steps, name):
    n_in = [len(p.in_specs) for p in parts]
    n_out = [len(p.out_specs) for p in parts]
    n_scr = [len(p.scratch) for p in parts]

    def body(*refs):
        ins, outs, scr = refs[:sum(n_in)], refs[sum(n_in):sum(n_in) + sum(n_out)], refs[sum(n_in) + sum(n_out):]
        i = o = c = 0
        for p, ni, no, nc in zip(parts, n_in, n_out, n_scr):
            p.kernel(*ins[i:i + ni], *outs[o:o + no], *scr[c:c + nc])
            i, o, c = i + ni, o + no, c + nc

    flat = pl.pallas_call(
        body,
        grid=(steps,),
        in_specs=[sp for p in parts for sp in p.in_specs],
        out_specs=[sp for p in parts for sp in p.out_specs],
        out_shape=[sh for p in parts for sh in p.out_shape],
        scratch_shapes=[sc for p in parts for sc in p.scratch],
        compiler_params=_cparams(("parallel",)),
        name=name,
    )(*[a for p in parts for a in p.args])
    res, o = [], 0
    for no in n_out:
        res.append(list(flat[o:o + no]))
        o += no
    return res


def kernel(x_prompt, x_sample, state_s5_re, state_s5_im, state_mlstm_C, state_mlstm_n, state_mlstm_m,
           state_mlstm_conv, meta_tokens, ffn1_norm, ffn1_w_gate, ffn1_w_up, ffn1_w_down, mix_norm, w_in,
           s5_A_re, s5_A_im, s5_log_dt, s5_B_re, s5_B_im, s5_C_re, s5_C_im, s5_D, s5_w_glu,
           mlstm_conv_w, mlstm_conv_b, mlstm_b_i, mlstm_b_f, mlstm_norm, w_branch_s5, w_branch_mlstm,
           w_out, ffn2_norm, ffn2_w_gate, ffn2_w_up, ffn2_w_down, final_norm):
    nbatch, seq, _ = x_prompt.shape
    nsamp, sseq, _ = x_sample.shape
    assert w_in.shape[0] == 1, "one layer per call"
    l = 0

    w1g, w1u, w1d = ffn1_w_gate[l], ffn1_w_up[l], ffn1_w_down[l]
    w2g, w2u, w2d = ffn2_w_gate[l], ffn2_w_up[l], ffn2_w_down[l]
    o_if = S5_WIDTH + QK_WIDTH + 2 * V_WIDTH
    o_gate = o_if + 2 * HEADS
    wt = w_in[l].T
    w2 = _win_halves(wt, o_gate, tr=WIN_CAST_ROWS)
    wglu, wbs, wbm, wo = (w[l].astype(BF16) for w in (s5_w_glu, w_branch_s5, w_branch_mlstm, w_out))
    g1 = ffn1_norm[l][None]
    gm = mix_norm[l][None]
    g2 = ffn2_norm[l][None]
    gf = final_norm[None]
    bif = jnp.pad(jnp.concatenate([mlstm_b_i[l], mlstm_b_f[l]]), (0, LANES - 2 * HEADS))[None]
    cw = mlstm_conv_w[l]
    cb = mlstm_conv_b[l][None]
    ng = mlstm_norm[l][None]
    d_skip = s5_D[l][None]

    *s5_ops, lam = _s5_operators(s5_A_re[l], s5_A_im[l], s5_log_dt[l], s5_B_re[l], s5_B_im[l])
    gh = S5_GROUPS * S5_GROUP
    w_toe, f_bf, e_bf = _s5_prep(*s5_ops, s5_C_re[l].reshape(gh, S5_STATE), s5_C_im[l].reshape(gh, S5_STATE))

    def win(h1, tm):
        return _win(h1, gm, w2, wt, o_if=o_if, ngate=N_BRANCH * D_MODEL, tm=tm)

    def mlstm(z, zif, ml_state, *, nseq, seqlen, lc):
        buf0, c0, n0, m0 = ml_state
        ym, buf, c, n, m = _mlstm(z, zif, cw, cb, bif, ng, buf0, c0, n0, m0.reshape(nseq, 1, HEADS),
                                  nseq=nseq, seqlen=seqlen, nb=1, lc=lc)
        return ym, buf, c, n, m.reshape(nseq, HEADS)

    def merge_part(h1, ys, ym, z, n):
        return _merge_part(h1, ys, ym, z, wglu, wbs, wbm, wo, n=n, tm=MERGE_TM)

    ntok_s = nsamp * sseq
    ntok_p = nbatch * seq
    x_sm = jnp.concatenate([x_sample.reshape(ntok_s, D_MODEL), meta_tokens], axis=0)
    ntok_sm = ntok_s + N_META
    h1_sm, *w1_bf = _ffn(x_sm, g1, w1g, w1u, w1d, n=ntok_sm, tm=ntok_sm, emit_bf16=True)
    z_sm, zif_sm = win(h1_sm, ntok_sm)
    h1_p = _ffn(x_prompt.reshape(ntok_p, D_MODEL), g1, *w1_bf, n=ntok_p, tm=FFN_TM, tf=FFN_TF_BF16)
    z_p, zif_p = win(h1_p, FFN_TM)

    z_m = jnp.tile(z_sm[ntok_s:], (nbatch, 1))
    zif_m = jnp.tile(zif_sm[ntok_s:], (nbatch, 1))
    zeros = lambda *s: jnp.zeros((nbatch,) + s, F32)

    ys_p, ys_s, *s5_st = _s5(z_m, z_p, z_sm, w_toe, f_bf, e_bf, lam, d_skip,
                             state_s5_re[l].reshape(nsamp, -1), state_s5_im[l].reshape(nsamp, -1),
                             nbatch=nbatch, len_m=N_META, len_p=seq, nsamp=nsamp, len_s=sseq)
    s5_p, s5_s = s5_st[:2], s5_st[2:]

    _, *ml_m = mlstm(z_m, zif_m, (zeros(CONV_W - 1, QK_WIDTH), zeros(HEADS, DV, DQK), zeros(HEADS, DQK), zeros(HEADS)),
                     nseq=nbatch, seqlen=N_META, lc=N_META)
    ym_p, *ml_p = mlstm(z_p, zif_p, ml_m, nseq=nbatch, seqlen=seq, lc=MLSTM_CHUNK)

    steps_p = ntok_p // MERGE_TM
    step_part, step_finish = _mlstm_step_part(
        z_sm, zif_sm, cw, cb, bif, ng, state_mlstm_conv[l], state_mlstm_C[l], state_mlstm_n[l], state_mlstm_m[l],
        nseq=nsamp, seqlen=sseq, nb=nsamp // steps_p)
    (h2_p,), step_out = _run_parts([merge_part(h1_p, ys_p, ym_p, z_p, ntok_p), step_part], steps_p, "merge_mlstm_step")
    ym_s, *ml_s = step_finish(*step_out)
    (h2_s,), = _run_parts([merge_part(h1_sm, ys_s, ym_s, z_sm, ntok_s)], ntok_s // MERGE_TM, "merge")
    y_s, *w2_bf = _ffn(h2_s, g2, w2g, w2u, w2d, gf, n=ntok_s, tm=ntok_s, emit_bf16=True)
    y_p = _ffn(h2_p, g2, *w2_bf, gf, n=ntok_p, tm=FFN_TM)

    def pack(n, s5_st, ml_st):
        buf, c, nn, m = ml_st
        return (s5_st[0].reshape(1, n, S5_GROUPS, S5_STATE), s5_st[1].reshape(1, n, S5_GROUPS, S5_STATE),
                c[None], nn[None], m[None], buf[None])

    return ((y_p.reshape(nbatch, seq, D_MODEL), y_s.reshape(nsamp, sseq, D_MODEL))
            + pack(nbatch, s5_p, ml_p) + pack(nsamp, s5_s, ml_s))
```
